```python
import jax, jax.numpy as jnp
from jax import lax
import numpy as np

D_MODEL = 2048
BATCH = 8
SEQ = 8192
DEPTH = 2

RMS_EPS = 1e-6
ROPE_THETA = 10000.0
D_FF = 5632
Q_BLOCK = 128

MLA_HEADS = 8
MLA_Q_LORA = 512
MLA_KV_LORA = 256
MLA_NOPE_DIM = 128
MLA_ROPE_DIM = 64
MLA_V_DIM = 128

SWA_HEADS = 8
SWA_KV_HEADS = 2
SWA_HEAD_DIM = 64
WINDOW = 128

FOX_HEADS = 8
FOX_HEAD_DIM = 64

IN_SPLITS = [
    MLA_Q_LORA,
    MLA_KV_LORA,
    MLA_ROPE_DIM,
    SWA_HEADS * SWA_HEAD_DIM,
    SWA_KV_HEADS * SWA_HEAD_DIM,
    SWA_KV_HEADS * SWA_HEAD_DIM,
    FOX_HEADS * FOX_HEAD_DIM,
    FOX_HEADS * FOX_HEAD_DIM,
    FOX_HEADS * FOX_HEAD_DIM,
    FOX_HEADS,
]
IN_COLS = int(sum(IN_SPLITS))
IN_OFFSETS = [int(o) for o in np.cumsum(IN_SPLITS)[:-1]]
MIX_WIDTH = MLA_HEADS * MLA_V_DIM + SWA_HEADS * SWA_HEAD_DIM + FOX_HEADS * FOX_HEAD_DIM

kernel_name = "hybrid_mla_swa_sink_fox_macaron"


def rmsnorm(x, g):
    x32 = x.astype(jnp.float32)
    y = x32 * lax.rsqrt(jnp.mean(x32 * x32, axis=-1, keepdims=True) + RMS_EPS)
    return (y * g.astype(jnp.float32)).astype(x.dtype)


def swiglu(h, w_gate, w_up, w_down):
    return (jax.nn.silu(h @ w_gate) * (h @ w_up)) @ w_down


def rope_tables(positions, dim):
    inv_freq = ROPE_THETA ** (-jnp.arange(0, dim, 2, dtype=jnp.float32) / dim)
    ang = positions.astype(jnp.float32)[..., None] * inv_freq
    return jnp.cos(ang), jnp.sin(ang)


def apply_rope(x, cos, sin):
    half = x.shape[-1] // 2
    x1, x2 = x[..., :half], x[..., half:]
    c, s = cos[:, :, None, :], sin[:, :, None, :]
    return jnp.concatenate([x1 * c - x2 * s, x2 * c + x1 * s], axis=-1).astype(x.dtype)


def causal_block_attention(q, k, v, scale, log_f_cum=None):
    B, S, H, dq = q.shape
    dv = v.shape[-1]
    n = S // Q_BLOCK
    qb = q.reshape(B, n, Q_BLOCK, H, dq).transpose(1, 0, 2, 3, 4)
    key_pos = jnp.arange(S)
    idx = jnp.arange(n)

    def scores_for(i, qi):
        s = jnp.einsum('bqhd,bkhd->bhqk', qi, k).astype(jnp.float32) * scale
        q_pos = i * Q_BLOCK + jnp.arange(Q_BLOCK)
        return s, key_pos[None, :] <= q_pos[:, None]

    def finish(s, mask):
        s = jnp.where(mask[None, None], s, -jnp.inf)
        p = jax.nn.softmax(s, axis=-1).astype(v.dtype)
        return jnp.einsum('bhqk,bkhd->bqhd', p, v)

    if log_f_cum is None:
        def step(args):
            i, qi = args
            s, mask = scores_for(i, qi)
            return finish(s, mask)
        out = lax.map(step, (idx, qb))
    else:
        c_all = log_f_cum.transpose(0, 2, 1)
        cb = log_f_cum.reshape(B, n, Q_BLOCK, H).transpose(1, 0, 3, 2)

        def step(args):
            i, qi, ci = args
            s, mask = scores_for(i, qi)
            s = s + (ci[..., :, None] - c_all[:, :, None, :])
            return finish(s, mask)
        out = lax.map(step, (idx, qb, cb))
    return out.transpose(1, 0, 2, 3, 4).reshape(B, S, H, dv)


def sliding_window_sink_attention(q, k, v, sinks):
    B, S, H, d = q.shape
    Hkv = k.shape[2]
    G = H // Hkv
    n = S // WINDOW
    qb = q.reshape(B, n, WINDOW, Hkv, G, d)
    pad = jnp.zeros((B, WINDOW, Hkv, d), k.dtype)
    kp = jnp.concatenate([pad, k], axis=1).reshape(B, n + 1, WINDOW, Hkv, d)
    vp = jnp.concatenate([pad.astype(v.dtype), v], axis=1).reshape(B, n + 1, WINDOW, Hkv, d)
    kw = jnp.concatenate([kp[:, :-1], kp[:, 1:]], axis=2)
    vw = jnp.concatenate([vp[:, :-1], vp[:, 1:]], axis=2)
    s = jnp.einsum('bnqhgd,bnkhd->bnhgqk', qb, kw).astype(jnp.float32) * (d ** -0.5)
    blk = jnp.arange(n)[:, None, None]
    q_pos = blk * WINDOW + jnp.arange(WINDOW)[None, :, None]
    k_pos = (blk - 1) * WINDOW + jnp.arange(2 * WINDOW)[None, None, :]
    mask = (k_pos <= q_pos) & (k_pos > q_pos - WINDOW) & (k_pos >= 0)
    s = jnp.where(mask[None, :, None, None], s, -jnp.inf)
    sink = jnp.broadcast_to(sinks.astype(jnp.float32).reshape(Hkv, G)[None, None, :, :, None, None],
                            s.shape[:-1] + (1,))
    p = jax.nn.softmax(jnp.concatenate([s, sink], axis=-1), axis=-1)[..., :-1]
    o = jnp.einsum('bnhgqk,bnkhd->bnqhgd', p.astype(v.dtype), vw)
    return o.reshape(B, S, H * d)


def hybrid_mixer(h, cos_m, sin_m, cos_s, sin_s, w_in, q_norm, w_q_b, kv_norm, w_kv_b,
                 sinks, forget_bias, w_out):
    B, S, _ = h.shape
    proj = h @ w_in
    (c_q, c_kv, k_rope, q_s, k_s, v_s, q_f, k_f, v_f, f_logit) = jnp.split(proj, IN_OFFSETS, axis=-1)

    q = (rmsnorm(c_q, q_norm) @ w_q_b).reshape(B, S, MLA_HEADS, MLA_NOPE_DIM + MLA_ROPE_DIM)
    q_nope, q_pe = q[..., :MLA_NOPE_DIM], apply_rope(q[..., MLA_NOPE_DIM:], cos_m, sin_m)
    kv = (rmsnorm(c_kv, kv_norm) @ w_kv_b).reshape(B, S, MLA_HEADS, MLA_NOPE_DIM + MLA_V_DIM)
    k_nope, v_m = kv[..., :MLA_NOPE_DIM], kv[..., MLA_NOPE_DIM:]
    k_pe = apply_rope(k_rope[:, :, None, :], cos_m, sin_m)
    q_m = jnp.concatenate([q_nope, q_pe], axis=-1)
    k_m = jnp.concatenate([k_nope, jnp.broadcast_to(k_pe, (B, S, MLA_HEADS, MLA_ROPE_DIM))], axis=-1)
    o_mla = causal_block_attention(q_m, k_m, v_m, (MLA_NOPE_DIM + MLA_ROPE_DIM) ** -0.5)

    q_s = apply_rope(q_s.reshape(B, S, SWA_HEADS, SWA_HEAD_DIM), cos_s, sin_s)
    k_s = apply_rope(k_s.reshape(B, S, SWA_KV_HEADS, SWA_HEAD_DIM), cos_s, sin_s)
    v_s = v_s.reshape(B, S, SWA_KV_HEADS, SWA_HEAD_DIM)
    o_swa = sliding_window_sink_attention(q_s, k_s, v_s, sinks)

    log_f = jax.nn.log_sigmoid(f_logit.astype(jnp.float32) + forget_bias.astype(jnp.float32))
    c = jnp.cumsum(log_f, axis=1)
    o_fox = causal_block_attention(q_f.reshape(B, S, FOX_HEADS, FOX_HEAD_DIM),
                                   k_f.reshape(B, S, FOX_HEADS, FOX_HEAD_DIM),
                                   v_f.reshape(B, S, FOX_HEADS, FOX_HEAD_DIM),
                                   FOX_HEAD_DIM ** -0.5, c)

    mixed = jnp.concatenate([o_mla.reshape(B, S, -1), o_swa, o_fox.reshape(B, S, -1)], axis=-1)
    return mixed @ w_out


def _fwd_setup_inputs(seed: int = 0) -> dict:
    key = jax.random.key(seed)
    ks = jax.random.split(key, 24)
    f32 = jnp.float32

    def w(k, fan_in, fan_out):
        return jax.random.normal(k, (DEPTH, fan_in, fan_out), f32) * fan_in ** -0.5

    def gain(k, dim):
        return 1.0 + 0.1 * jax.random.normal(k, (DEPTH, dim), f32)

    return {
        "x": jax.random.normal(ks[0], (BATCH, SEQ, D_MODEL), f32),
        "positions": jnp.broadcast_to(jnp.arange(SEQ, dtype=jnp.int32), (BATCH, SEQ)),
        "ffn1_norm": gain(ks[1], D_MODEL),
        "ffn1_w_gate": w(ks[2], D_MODEL, D_FF),
        "ffn1_w_up": w(ks[3], D_MODEL, D_FF),
        "ffn1_w_down": w(ks[4], D_FF, D_MODEL),
        "mix_norm": gain(ks[5], D_MODEL),
        "w_in": w(ks[6], D_MODEL, IN_COLS),
        "mla_q_norm": gain(ks[7], MLA_Q_LORA),
        "mla_w_q_b": w(ks[8], MLA_Q_LORA, MLA_HEADS * (MLA_NOPE_DIM + MLA_ROPE_DIM)),
        "mla_kv_norm": gain(ks[9], MLA_KV_LORA),
        "mla_w_kv_b": w(ks[10], MLA_KV_LORA, MLA_HEADS * (MLA_NOPE_DIM + MLA_V_DIM)),
        "swa_sinks": 0.5 * jax.random.normal(ks[11], (DEPTH, SWA_HEADS), f32),
        "fox_forget_bias": 0.1 * jax.random.normal(ks[12], (DEPTH, FOX_HEADS), f32),
        "w_out": w(ks[13], MIX_WIDTH, D_MODEL),
        "ffn2_norm": gain(ks[14], D_MODEL),
        "ffn2_w_gate": w(ks[15], D_MODEL, D_FF),
        "ffn2_w_up": w(ks[16], D_MODEL, D_FF),
        "ffn2_w_down": w(ks[17], D_FF, D_MODEL),
        "final_norm": 1.0 + 0.1 * jax.random.normal(ks[18], (D_MODEL,), f32),
    }


def _fwd_reference(x, positions, ffn1_norm, ffn1_w_gate, ffn1_w_up, ffn1_w_down, mix_norm, w_in,
              mla_q_norm, mla_w_q_b, mla_kv_norm, mla_w_kv_b, swa_sinks, fox_forget_bias, w_out,
              ffn2_norm, ffn2_w_gate, ffn2_w_up, ffn2_w_down, final_norm):
    cos_m, sin_m = rope_tables(positions, MLA_ROPE_DIM)
    cos_s, sin_s = rope_tables(positions, SWA_HEAD_DIM)
    for l in range(DEPTH):
        x = x + 0.5 * swiglu(rmsnorm(x, ffn1_norm[l]), ffn1_w_gate[l], ffn1_w_up[l], ffn1_w_down[l])
        x = x + hybrid_mixer(rmsnorm(x, mix_norm[l]), cos_m, sin_m, cos_s, sin_s, w_in[l],
                             mla_q_norm[l], mla_w_q_b[l], mla_kv_norm[l], mla_w_kv_b[l],
                             swa_sinks[l], fox_forget_bias[l], w_out[l])
        x = x + 0.5 * swiglu(rmsnorm(x, ffn2_norm[l]), ffn2_w_gate[l], ffn2_w_up[l], ffn2_w_down[l])
    return rmsnorm(x, final_norm)


import jax as _jax
import jax.numpy as _jnp

TWIN_FORMAT = 'train_step'
FWD_PARAMS = ['x', 'positions', 'ffn1_norm', 'ffn1_w_gate', 'ffn1_w_up', 'ffn1_w_down', 'mix_norm', 'w_in', 'mla_q_norm', 'mla_w_q_b', 'mla_kv_norm', 'mla_w_kv_b', 'swa_sinks', 'fox_forget_bias', 'w_out', 'ffn2_norm', 'ffn2_w_gate', 'ffn2_w_up', 'ffn2_w_down', 'final_norm']
TWIN_WEIGHTS = ['ffn1_norm', 'ffn1_w_gate', 'ffn1_w_up', 'ffn1_w_down', 'mix_norm', 'w_in', 'mla_q_norm', 'mla_w_q_b', 'mla_kv_norm', 'mla_w_kv_b', 'swa_sinks', 'fox_forget_bias', 'w_out', 'ffn2_norm', 'ffn2_w_gate', 'ffn2_w_up', 'ffn2_w_down', 'final_norm']
TWIN_DIFF_INPUT = 'x'
TWIN_INPUTS = ['x', 'positions', 'ffn1_norm', 'ffn1_w_gate', 'ffn1_w_up', 'ffn1_w_down', 'mix_norm', 'w_in', 'mla_q_norm', 'mla_w_q_b', 'mla_kv_norm', 'mla_w_kv_b', 'swa_sinks', 'fox_forget_bias', 'w_out', 'ffn2_norm', 'ffn2_w_gate', 'ffn2_w_up', 'ffn2_w_down', 'final_norm', 'loss_target', 'm_ffn1_norm', 'm_ffn1_w_gate', 'm_ffn1_w_up', 'm_ffn1_w_down', 'm_mix_norm', 'm_w_in', 'm_mla_q_norm', 'm_mla_w_q_b', 'm_mla_kv_norm', 'm_mla_w_kv_b', 'm_swa_sinks', 'm_fox_forget_bias', 'm_w_out', 'm_ffn2_norm', 'm_ffn2_w_gate', 'm_ffn2_w_up', 'm_ffn2_w_down', 'm_final_norm', 'v_ffn1_norm', 'v_ffn1_w_gate', 'v_ffn1_w_up', 'v_ffn1_w_down', 'v_mix_norm', 'v_w_in', 'v_mla_q_norm', 'v_mla_w_q_b', 'v_mla_kv_norm', 'v_mla_w_kv_b', 'v_swa_sinks', 'v_fox_forget_bias', 'v_w_out', 'v_ffn2_norm', 'v_ffn2_w_gate', 'v_ffn2_w_up', 'v_ffn2_w_down', 'v_final_norm']
TWIN_OUTPUTS = ['loss', 'grad_x', 'grad_ffn1_norm', 'grad_ffn1_w_gate', 'grad_ffn1_w_up', 'grad_ffn1_w_down', 'grad_mix_norm', 'grad_w_in', 'grad_mla_q_norm', 'grad_mla_w_q_b', 'grad_mla_kv_norm', 'grad_mla_w_kv_b', 'grad_swa_sinks', 'grad_fox_forget_bias', 'grad_w_out', 'grad_ffn2_norm', 'grad_ffn2_w_gate', 'grad_ffn2_w_up', 'grad_ffn2_w_down', 'grad_final_norm', 'delta_ffn1_norm', 'delta_ffn1_w_gate', 'delta_ffn1_w_up', 'delta_ffn1_w_down', 'delta_mix_norm', 'delta_w_in', 'delta_mla_q_norm', 'delta_mla_w_q_b', 'delta_mla_kv_norm', 'delta_mla_w_kv_b', 'delta_swa_sinks', 'delta_fox_forget_bias', 'delta_w_out', 'delta_ffn2_norm', 'delta_ffn2_w_gate', 'delta_ffn2_w_up', 'delta_ffn2_w_down', 'delta_final_norm', 'new_m_ffn1_norm', 'new_m_ffn1_w_gate', 'new_m_ffn1_w_up', 'new_m_ffn1_w_down', 'new_m_mix_norm', 'new_m_w_in', 'new_m_mla_q_norm', 'new_m_mla_w_q_b', 'new_m_mla_kv_norm', 'new_m_mla_w_kv_b', 'new_m_swa_sinks', 'new_m_fox_forget_bias', 'new_m_w_out', 'new_m_ffn2_norm', 'new_m_ffn2_w_gate', 'new_m_ffn2_w_up', 'new_m_ffn2_w_down', 'new_m_final_norm', 'new_v_ffn1_norm', 'new_v_ffn1_w_gate', 'new_v_ffn1_w_up', 'new_v_ffn1_w_down', 'new_v_mix_norm', 'new_v_w_in', 'new_v_mla_q_norm', 'new_v_mla_w_q_b', 'new_v_mla_kv_norm', 'new_v_mla_w_kv_b', 'new_v_swa_sinks', 'new_v_fox_forget_bias', 'new_v_w_out', 'new_v_ffn2_norm', 'new_v_ffn2_w_gate', 'new_v_ffn2_w_up', 'new_v_ffn2_w_down', 'new_v_final_norm']
TWIN_LEAF_KINDS = {'loss': 'loss', 'grad_x': 'grad_x', 'grad_ffn1_norm': 'grad_w', 'grad_ffn1_w_gate': 'grad_w', 'grad_ffn1_w_up': 'grad_w', 'grad_ffn1_w_down': 'grad_w', 'grad_mix_norm': 'grad_w', 'grad_w_in': 'grad_w', 'grad_mla_q_norm': 'grad_w', 'grad_mla_w_q_b': 'grad_w', 'grad_mla_kv_norm': 'grad_w', 'grad_mla_w_kv_b': 'grad_w', 'grad_swa_sinks': 'grad_w', 'grad_fox_forget_bias': 'grad_w', 'grad_w_out': 'grad_w', 'grad_ffn2_norm': 'grad_w', 'grad_ffn2_w_gate': 'grad_w', 'grad_ffn2_w_up': 'grad_w', 'grad_ffn2_w_down': 'grad_w', 'grad_final_norm': 'grad_w', 'delta_ffn1_norm': 'delta_w', 'delta_ffn1_w_gate': 'delta_w', 'delta_ffn1_w_up': 'delta_w', 'delta_ffn1_w_down': 'delta_w', 'delta_mix_norm': 'delta_w', 'delta_w_in': 'delta_w', 'delta_mla_q_norm': 'delta_w', 'delta_mla_w_q_b': 'delta_w', 'delta_mla_kv_norm': 'delta_w', 'delta_mla_w_kv_b': 'delta_w', 'delta_swa_sinks': 'delta_w', 'delta_fox_forget_bias': 'delta_w', 'delta_w_out': 'delta_w', 'delta_ffn2_norm': 'delta_w', 'delta_ffn2_w_gate': 'delta_w', 'delta_ffn2_w_up': 'delta_w', 'delta_ffn2_w_down': 'delta_w', 'delta_final_norm': 'delta_w', 'new_m_ffn1_norm': 'new_m', 'new_m_ffn1_w_gate': 'new_m', 'new_m_ffn1_w_up': 'new_m', 'new_m_ffn1_w_down': 'new_m', 'new_m_mix_norm': 'new_m', 'new_m_w_in': 'new_m', 'new_m_mla_q_norm': 'new_m', 'new_m_mla_w_q_b': 'new_m', 'new_m_mla_kv_norm': 'new_m', 'new_m_mla_w_kv_b': 'new_m', 'new_m_swa_sinks': 'new_m', 'new_m_fox_forget_bias': 'new_m', 'new_m_w_out': 'new_m', 'new_m_ffn2_norm': 'new_m', 'new_m_ffn2_w_gate': 'new_m', 'new_m_ffn2_w_up': 'new_m', 'new_m_ffn2_w_down': 'new_m', 'new_m_final_norm': 'new_m', 'new_v_ffn1_norm': 'new_v', 'new_v_ffn1_w_gate': 'new_v', 'new_v_ffn1_w_up': 'new_v', 'new_v_ffn1_w_down': 'new_v', 'new_v_mix_norm': 'new_v', 'new_v_w_in': 'new_v', 'new_v_mla_q_norm': 'new_v', 'new_v_mla_w_q_b': 'new_v', 'new_v_mla_kv_norm': 'new_v', 'new_v_mla_w_kv_b': 'new_v', 'new_v_swa_sinks': 'new_v', 'new_v_fox_forget_bias': 'new_v', 'new_v_w_out': 'new_v', 'new_v_ffn2_norm': 'new_v', 'new_v_ffn2_w_gate': 'new_v', 'new_v_ffn2_w_up': 'new_v', 'new_v_ffn2_w_down': 'new_v', 'new_v_final_norm': 'new_v'}


def _forward(args):
    return _fwd_reference(*[args[k] for k in FWD_PARAMS])


def _output_shape():
    def fwd():
        inp = _fwd_setup_inputs(0)
        return _fwd_reference(*[inp[k] for k in FWD_PARAMS])
    out = _jax.eval_shape(fwd)
    return out.shape, out.dtype

N_MICROBATCH = 1
ADAM_LR = 0.001
ADAM_B1 = 0.9
ADAM_B2 = 0.999
ADAM_EPS = 1e-08
ADAM_WD = 0.01
ADAM_STEP = 10
PER_EXAMPLE_BATCH_AXIS = {'x': 0, 'positions': 0, 'loss_target': 0}
SHARED_INPUTS = []
_WEIGHT_DTYPES = {'ffn1_norm': _jnp.float32, 'ffn1_w_gate': _jnp.float32, 'ffn1_w_up': _jnp.float32, 'ffn1_w_down': _jnp.float32, 'mix_norm': _jnp.float32, 'w_in': _jnp.float32, 'mla_q_norm': _jnp.float32, 'mla_w_q_b': _jnp.float32, 'mla_kv_norm': _jnp.float32, 'mla_w_kv_b': _jnp.float32, 'swa_sinks': _jnp.float32, 'fox_forget_bias': _jnp.float32, 'w_out': _jnp.float32, 'ffn2_norm': _jnp.float32, 'ffn2_w_gate': _jnp.float32, 'ffn2_w_up': _jnp.float32, 'ffn2_w_down': _jnp.float32, 'final_norm': _jnp.float32}
MOMENT_SCALE = {'ffn1_norm': 5.783482e-02, 'ffn1_w_gate': 2.479591e-02, 'ffn1_w_up': 2.405064e-02, 'ffn1_w_down': 3.986378e-02, 'mix_norm': 6.318347e-02, 'w_in': 5.089112e-02, 'mla_q_norm': 3.072477e-02, 'mla_w_q_b': 1.743487e-02, 'mla_kv_norm': 7.059493e-02, 'mla_w_kv_b': 2.382955e-02, 'swa_sinks': 2.211983e-02, 'fox_forget_bias': 6.416510e-01, 'w_out': 4.774859e-02, 'ffn2_norm': 4.951873e-02, 'ffn2_w_gate': 2.139300e-02, 'ffn2_w_up': 2.105022e-02, 'ffn2_w_down': 3.499282e-02, 'final_norm': 3.221509e+01}


def _to_microbatches(a, axis):
    t = _jnp.moveaxis(a, axis, 0)
    t = t.reshape((N_MICROBATCH, t.shape[0] // N_MICROBATCH) + t.shape[1:])
    return _jnp.moveaxis(t, 1, axis + 1)


def setup_inputs(seed: int = 0) -> dict:
    inp = _fwd_setup_inputs(seed)
    key = _jax.random.fold_in(_jax.random.key(seed), 7919)
    shape, _ = _output_shape()
    out = dict(inp)
    out["loss_target"] = _jax.random.normal(_jax.random.fold_in(key, 0), shape, _jnp.float32)
    for i, name in enumerate(TWIN_WEIGHTS):
        w = inp[name].astype(_jnp.float32)
        if MOMENT_SCALE is None:
            s = _jnp.sqrt(_jnp.mean(_jnp.square(w)) + 1e-30)
        else:
            s = MOMENT_SCALE[name]
        km, kv = _jax.random.split(_jax.random.fold_in(key, i + 1))
        out[name] = w
        out["m_" + name] = s * _jax.random.normal(km, w.shape, _jnp.float32)
        out["v_" + name] = (s * s) * _jax.random.uniform(kv, w.shape, _jnp.float32, 0.5, 1.5)
    if N_MICROBATCH > 1:
        for name, axis in PER_EXAMPLE_BATCH_AXIS.items():
            out[name] = _to_microbatches(out[name], axis)
    return {'x': out['x'], 'positions': out['positions'], 'ffn1_norm': out['ffn1_norm'], 'ffn1_w_gate': out['ffn1_w_gate'], 'ffn1_w_up': out['ffn1_w_up'], 'ffn1_w_down': out['ffn1_w_down'], 'mix_norm': out['mix_norm'], 'w_in': out['w_in'], 'mla_q_norm': out['mla_q_norm'], 'mla_w_q_b': out['mla_w_q_b'], 'mla_kv_norm': out['mla_kv_norm'], 'mla_w_kv_b': out['mla_w_kv_b'], 'swa_sinks': out['swa_sinks'], 'fox_forget_bias': out['fox_forget_bias'], 'w_out': out['w_out'], 'ffn2_norm': out['ffn2_norm'], 'ffn2_w_gate': out['ffn2_w_gate'], 'ffn2_w_up': out['ffn2_w_up'], 'ffn2_w_down': out['ffn2_w_down'], 'final_norm': out['final_norm'], 'loss_target': out['loss_target'], 'm_ffn1_norm': out['m_ffn1_norm'], 'm_ffn1_w_gate': out['m_ffn1_w_gate'], 'm_ffn1_w_up': out['m_ffn1_w_up'], 'm_ffn1_w_down': out['m_ffn1_w_down'], 'm_mix_norm': out['m_mix_norm'], 'm_w_in': out['m_w_in'], 'm_mla_q_norm': out['m_mla_q_norm'], 'm_mla_w_q_b': out['m_mla_w_q_b'], 'm_mla_kv_norm': out['m_mla_kv_norm'], 'm_mla_w_kv_b': out['m_mla_w_kv_b'], 'm_swa_sinks': out['m_swa_sinks'], 'm_fox_forget_bias': out['m_fox_forget_bias'], 'm_w_out': out['m_w_out'], 'm_ffn2_norm': out['m_ffn2_norm'], 'm_ffn2_w_gate': out['m_ffn2_w_gate'], 'm_ffn2_w_up': out['m_ffn2_w_up'], 'm_ffn2_w_down': out['m_ffn2_w_down'], 'm_final_norm': out['m_final_norm'], 'v_ffn1_norm': out['v_ffn1_norm'], 'v_ffn1_w_gate': out['v_ffn1_w_gate'], 'v_ffn1_w_up': out['v_ffn1_w_up'], 'v_ffn1_w_down': out['v_ffn1_w_down'], 'v_mix_norm': out['v_mix_norm'], 'v_w_in': out['v_w_in'], 'v_mla_q_norm': out['v_mla_q_norm'], 'v_mla_w_q_b': out['v_mla_w_q_b'], 'v_mla_kv_norm': out['v_mla_kv_norm'], 'v_mla_w_kv_b': out['v_mla_w_kv_b'], 'v_swa_sinks': out['v_swa_sinks'], 'v_fox_forget_bias': out['v_fox_forget_bias'], 'v_w_out': out['v_w_out'], 'v_ffn2_norm': out['v_ffn2_norm'], 'v_ffn2_w_gate': out['v_ffn2_w_gate'], 'v_ffn2_w_up': out['v_ffn2_w_up'], 'v_ffn2_w_down': out['v_ffn2_w_down'], 'v_final_norm': out['v_final_norm']}


def _loss(weights, diff, rest, loss_target):
    with _jax.named_scope("forward"):
        args = {**rest, TWIN_DIFF_INPUT: diff, **{k: w.astype(_WEIGHT_DTYPES[k]) for k, w in weights.items()}}
        y = _forward(args)
    with _jax.named_scope("loss_head"):
        err = _jnp.square(y.astype(_jnp.float32) - loss_target)
        return 0.5 * _jnp.sum(_jnp.mean(err, axis=-1)) if err.ndim else 0.5 * err


def _adamw(w, g, m, v):
    m = ADAM_B1 * m + (1.0 - ADAM_B1) * g
    v = ADAM_B2 * v + (1.0 - ADAM_B2) * _jnp.square(g)
    m_hat = m / (1.0 - ADAM_B1 ** ADAM_STEP)
    v_hat = v / (1.0 - ADAM_B2 ** ADAM_STEP)
    delta = -ADAM_LR * (m_hat / (_jnp.sqrt(v_hat) + ADAM_EPS) + ADAM_WD * w)
    return delta, m, v


def reference(x, positions, ffn1_norm, ffn1_w_gate, ffn1_w_up, ffn1_w_down, mix_norm, w_in, mla_q_norm, mla_w_q_b, mla_kv_norm, mla_w_kv_b, swa_sinks, fox_forget_bias, w_out, ffn2_norm, ffn2_w_gate, ffn2_w_up, ffn2_w_down, final_norm, loss_target, m_ffn1_norm, m_ffn1_w_gate, m_ffn1_w_up, m_ffn1_w_down, m_mix_norm, m_w_in, m_mla_q_norm, m_mla_w_q_b, m_mla_kv_norm, m_mla_w_kv_b, m_swa_sinks, m_fox_forget_bias, m_w_out, m_ffn2_norm, m_ffn2_w_gate, m_ffn2_w_up, m_ffn2_w_down, m_final_norm, v_ffn1_norm, v_ffn1_w_gate, v_ffn1_w_up, v_ffn1_w_down, v_mix_norm, v_w_in, v_mla_q_norm, v_mla_w_q_b, v_mla_kv_norm, v_mla_w_kv_b, v_swa_sinks, v_fox_forget_bias, v_w_out, v_ffn2_norm, v_ffn2_w_gate, v_ffn2_w_up, v_ffn2_w_down, v_final_norm):
    given = dict(x=x, positions=positions, ffn1_norm=ffn1_norm, ffn1_w_gate=ffn1_w_gate, ffn1_w_up=ffn1_w_up, ffn1_w_down=ffn1_w_down, mix_norm=mix_norm, w_in=w_in, mla_q_norm=mla_q_norm, mla_w_q_b=mla_w_q_b, mla_kv_norm=mla_kv_norm, mla_w_kv_b=mla_w_kv_b, swa_sinks=swa_sinks, fox_forget_bias=fox_forget_bias, w_out=w_out, ffn2_norm=ffn2_norm, ffn2_w_gate=ffn2_w_gate, ffn2_w_up=ffn2_w_up, ffn2_w_down=ffn2_w_down, final_norm=final_norm, loss_target=loss_target, m_ffn1_norm=m_ffn1_norm, m_ffn1_w_gate=m_ffn1_w_gate, m_ffn1_w_up=m_ffn1_w_up, m_ffn1_w_down=m_ffn1_w_down, m_mix_norm=m_mix_norm, m_w_in=m_w_in, m_mla_q_norm=m_mla_q_norm, m_mla_w_q_b=m_mla_w_q_b, m_mla_kv_norm=m_mla_kv_norm, m_mla_w_kv_b=m_mla_w_kv_b, m_swa_sinks=m_swa_sinks, m_fox_forget_bias=m_fox_forget_bias, m_w_out=m_w_out, m_ffn2_norm=m_ffn2_norm, m_ffn2_w_gate=m_ffn2_w_gate, m_ffn2_w_up=m_ffn2_w_up, m_ffn2_w_down=m_ffn2_w_down, m_final_norm=m_final_norm, v_ffn1_norm=v_ffn1_norm, v_ffn1_w_gate=v_ffn1_w_gate, v_ffn1_w_up=v_ffn1_w_up, v_ffn1_w_down=v_ffn1_w_down, v_mix_norm=v_mix_norm, v_w_in=v_w_in, v_mla_q_norm=v_mla_q_norm, v_mla_w_q_b=v_mla_w_q_b, v_mla_kv_norm=v_mla_kv_norm, v_mla_w_kv_b=v_mla_w_kv_b, v_swa_sinks=v_swa_sinks, v_fox_forget_bias=v_fox_forget_bias, v_w_out=v_w_out, v_ffn2_norm=v_ffn2_norm, v_ffn2_w_gate=v_ffn2_w_gate, v_ffn2_w_up=v_ffn2_w_up, v_ffn2_w_down=v_ffn2_w_down, v_final_norm=v_final_norm)
    weights = {n: given[n] for n in TWIN_WEIGHTS}
    shared = {n: given[n] for n in SHARED_INPUTS}
    per_example = {n: given[n] for n in ['x', 'positions']}
    grad_fn = _jax.value_and_grad(_loss, argnums=(0, 1))

    def one_microbatch(ex, loss_target):
        ex = dict(ex)
        diff = ex.pop(TWIN_DIFF_INPUT)
        return grad_fn(weights, diff, {**shared, **ex}, loss_target)

    if N_MICROBATCH == 1:
        loss, (grad_w, grad_x) = one_microbatch(per_example, given["loss_target"])
    else:
        def body(carry, xs):
            loss_sum, grad_sum = carry
            l_k, (gw_k, gx_k) = one_microbatch(xs[0], xs[1])
            with _jax.named_scope("update"):
                return (loss_sum + l_k, _jax.tree.map(_jnp.add, grad_sum, gw_k)), gx_k

        init = (_jnp.zeros((), _jnp.float32), _jax.tree.map(_jnp.zeros_like, weights))
        (loss, grad_w), grad_x = _jax.lax.scan(body, init, (per_example, given["loss_target"]))
    with _jax.named_scope("update"):
        delta_w, new_m, new_v = {}, {}, {}
        for n in TWIN_WEIGHTS:
            delta_w[n], new_m[n], new_v[n] = _adamw(weights[n], grad_w[n], given["m_" + n], given["v_" + n])
    return (loss, grad_x, *[grad_w[n] for n in TWIN_WEIGHTS], *[delta_w[n] for n in TWIN_WEIGHTS],
            *[new_m[n] for n in TWIN_WEIGHTS], *[new_v[n] for n in TWIN_WEIGHTS])
```

```python
import jax
import jax.numpy as jnp
from jax import lax
from jax.experimental import pallas as pl
from jax.experimental.pallas import tpu as pltpu

F32, BF16 = jnp.float32, jnp.bfloat16
MESH = pl.DeviceIdType.MESH

RMS_EPS = 1e-6
ROPE_THETA = 10000.0
N_HEADS = 8
Q_LORA, KV_LORA = 512, 256
NOPE, ROPE, VDIM = 128, 64, 128
QK_MLA = NOPE + ROPE
SWA_KV, HD, WINDOW = 2, 64, 128
P_CQ, P_CKV, P_QS, P_KS, P_VS, P_QF, P_KF, P_VF, P_KR, P_F, P_COLS = (
    0, 512, 768, 1280, 1408, 1536, 2048, 2560, 3072, 3136, 3200)
IN_COLS = 3144
IN_KR = 768
ADAM_LR, ADAM_B1, ADAM_B2, ADAM_EPS, ADAM_WD, ADAM_STEP = 0.001, 0.9, 0.999, 1e-08, 0.01, 10
NEG = -1e30
VMEM_LIMIT = 56 * 1024 * 1024
N_CHIPS = 4
N_DEV = 8
SMALL_ROWS = 128


def _tile(n, pref):
    return pref if n % pref == 0 else n


def _cparams(*sem):
    return pltpu.CompilerParams(dimension_semantics=sem, vmem_limit_bytes=VMEM_LIMIT)


def _sigmoid(x):
    return 1.0 / (1.0 + jnp.exp(-x))


def _dot(a, b):
    return jnp.dot(a, b, preferred_element_type=F32)


def _dot_nt(a, b):
    return lax.dot_general(a, b, (((1,), (1,)), ((), ())), preferred_element_type=F32)


def _dot_tn(a, b):
    return lax.dot_general(a, b, (((0,), (0,)), ((), ())), preferred_element_type=F32)


def _dot_tn_f32(a, b):
    return lax.dot_general(a, b, (((0,), (0,)), ((), ())), preferred_element_type=F32,
                           precision=lax.Precision.HIGHEST)


def rms_fwd(x, gain):
    S, D = x.shape
    tm = _tile(S, 512)

    def body(x_ref, g_ref, h_ref):
        xv = x_ref[...]
        r = lax.rsqrt(jnp.mean(xv * xv, axis=-1, keepdims=True) + RMS_EPS)
        h_ref[...] = (xv * r * g_ref[...]).astype(BF16)

    return pl.pallas_call(
        body, name="rms_fwd", grid=(S // tm,),
        in_specs=[pl.BlockSpec((tm, D), lambda i: (i, 0)), pl.BlockSpec((1, D), lambda i: (0, 0))],
        out_specs=pl.BlockSpec((tm, D), lambda i: (i, 0)),
        out_shape=jax.ShapeDtypeStruct((S, D), BF16), compiler_params=_cparams("arbitrary"))(x, gain)


def rms_bwd(dh, x, gain, resid):
    S, D = x.shape
    tm = _tile(S, 512)

    def body(dh_ref, x_ref, g_ref, r_ref, dx_ref, dg_ref):
        xv, dhv = x_ref[...], dh_ref[...]
        r = lax.rsqrt(jnp.mean(xv * xv, axis=-1, keepdims=True) + RMS_EPS)
        xhat = xv * r
        dhg = dhv * g_ref[...]
        dx_ref[...] = r_ref[...] + r * (dhg - xhat * jnp.mean(dhg * xhat, axis=-1, keepdims=True))

        @pl.when(pl.program_id(0) == 0)
        def _():
            dg_ref[...] = jnp.zeros_like(dg_ref)

        dg_ref[...] += jnp.sum(dhv * xhat, axis=0, keepdims=True)

    row = pl.BlockSpec((tm, D), lambda i: (i, 0))
    vec = pl.BlockSpec((1, D), lambda i: (0, 0))
    return pl.pallas_call(
        body, name="rms_bwd", grid=(S // tm,), in_specs=[row, row, vec, row], out_specs=[row, vec],
        out_shape=[jax.ShapeDtypeStruct((S, D), F32), jax.ShapeDtypeStruct((1, D), F32)],
        compiler_params=_cparams("arbitrary"))(dh, x, gain, resid)


def loss_head(x, gain, target):
    S, D = x.shape
    tm = _tile(S, 512)

    def body(x_ref, g_ref, t_ref, loss_ref, dx_ref, dg_ref):
        xv, g = x_ref[...], g_ref[...]
        r = lax.rsqrt(jnp.mean(xv * xv, axis=-1, keepdims=True) + RMS_EPS)
        xhat = xv * r
        err = xhat * g - t_ref[...]
        dy = err * (1.0 / D)
        dyg = dy * g
        dx_ref[...] = r * (dyg - xhat * jnp.mean(dyg * xhat, axis=-1, keepdims=True))

        @pl.when(pl.program_id(0) == 0)
        def _():
            dg_ref[...] = jnp.zeros_like(dg_ref)
            loss_ref[...] = jnp.zeros_like(loss_ref)

        dg_ref[...] += jnp.sum(dy * xhat, axis=0, keepdims=True)
        loss_ref[...] += 0.5 * jnp.sum(jnp.mean(err * err, axis=-1, keepdims=True), axis=0, keepdims=True)

    row = pl.BlockSpec((tm, D), lambda i: (i, 0))
    vec = pl.BlockSpec((1, D), lambda i: (0, 0))
    return pl.pallas_call(
        body, name="loss_head", grid=(S // tm,), in_specs=[row, vec, row],
        out_specs=[pl.BlockSpec((1, 128), lambda i: (0, 0)), row, vec],
        out_shape=[jax.ShapeDtypeStruct((1, 128), F32), jax.ShapeDtypeStruct((S, D), F32),
                   jax.ShapeDtypeStruct((1, D), F32)],
        compiler_params=_cparams("arbitrary"))(x, gain, target)


def gate_up(h, wg, wu, l):
    S, D = h.shape
    Fs = wg.shape[3]
    tm = _tile(S, 512)

    def body(h_ref, wg_ref, wu_ref, g_ref, u_ref, a_ref):
        hv = h_ref[...]
        g = _dot(hv, wg_ref[...])
        u = _dot(hv, wu_ref[...])
        g_ref[...] = g.astype(BF16)
        u_ref[...] = u.astype(BF16)
        a_ref[...] = (g * _sigmoid(g) * u).astype(BF16)

    w_spec = pl.BlockSpec((None, None, D, Fs), lambda j, i: (l, j, 0, 0))
    o_spec = pl.BlockSpec((tm, Fs), lambda j, i: (i, j))
    o_shape = jax.ShapeDtypeStruct((S, N_CHIPS * Fs), BF16)
    return pl.pallas_call(
        body, name="gate_up", grid=(N_CHIPS, S // tm),
        in_specs=[pl.BlockSpec((tm, D), lambda j, i: (i, 0)), w_spec, w_spec],
        out_specs=[o_spec, o_spec, o_spec], out_shape=[o_shape, o_shape, o_shape],
        compiler_params=_cparams("arbitrary", "arbitrary"))(h, wg, wu)


def down_proj(a, wd, l, x):
    S, F = a.shape
    D = wd.shape[2]
    tm, tk = _tile(S, 512), F // N_CHIPS
    nk = F // tk

    def body(a_ref, w_ref, x_ref, o_ref, acc_ref):
        k = pl.program_id(1)

        @pl.when(k == 0)
        def _():
            acc_ref[...] = jnp.zeros_like(acc_ref)

        acc_ref[...] += _dot(a_ref[...], w_ref[...])

        @pl.when(k == nk - 1)
        def _():
            o_ref[...] = x_ref[...] + 0.5 * acc_ref[...]

    return pl.pallas_call(
        body, name="down_proj", grid=(S // tm, nk),
        in_specs=[pl.BlockSpec((tm, tk), lambda i, k: (i, k)), pl.BlockSpec((None, tk, D), lambda i, k: (l, k, 0)),
                  pl.BlockSpec((tm, D), lambda i, k: (i, 0))],
        out_specs=pl.BlockSpec((tm, D), lambda i, k: (i, 0)), out_shape=jax.ShapeDtypeStruct((S, D), F32),
        scratch_shapes=[pltpu.VMEM((tm, D), F32)],
        compiler_params=_cparams("arbitrary", "arbitrary"))(a, wd, x)


def down_bwd(dy, wd, l, g, u):
    S, D = dy.shape
    F = g.shape[1]
    Fs = F // N_CHIPS
    tm = _tile(S, 512)

    def body(dy_ref, w_ref, g_ref, u_ref, dg_ref, du_ref):
        da = 0.5 * _dot_nt(dy_ref[...].astype(BF16), w_ref[...])
        gv, uv = g_ref[...].astype(F32), u_ref[...].astype(F32)
        sig = _sigmoid(gv)
        du_ref[...] = (da * (gv * sig)).astype(BF16)
        dg_ref[...] = (da * uv * (sig * (1.0 + gv * (1.0 - sig)))).astype(BF16)

    t_spec = pl.BlockSpec((tm, Fs), lambda j, i: (i, j))
    o_shape = jax.ShapeDtypeStruct((S, F), BF16)
    return pl.pallas_call(
        body, name="down_bwd", grid=(N_CHIPS, S // tm),
        in_specs=[pl.BlockSpec((tm, D), lambda j, i: (i, 0)), pl.BlockSpec((None, Fs, D), lambda j, i: (l, j, 0)),
                  t_spec, t_spec],
        out_specs=[t_spec, t_spec], out_shape=[o_shape, o_shape],
        compiler_params=_cparams("arbitrary", "arbitrary"))(dy, wd, g, u)


def gate_up_bwd(dg, du, wg, wu, l):
    S, F = dg.shape
    D, Fs = wg.shape[2], wg.shape[3]
    tm = _tile(S, 512)

    def body(dg_ref, du_ref, wg_ref, wu_ref, o_ref):
        k = pl.program_id(1)

        @pl.when(k == 0)
        def _():
            o_ref[...] = jnp.zeros_like(o_ref)

        o_ref[...] += _dot_nt(dg_ref[...], wg_ref[...]) + _dot_nt(du_ref[...], wu_ref[...])

    t_spec = pl.BlockSpec((tm, Fs), lambda i, k: (i, k))
    w_spec = pl.BlockSpec((None, None, D, Fs), lambda i, k: (l, k, 0, 0))
    return pl.pallas_call(
        body, name="gate_up_bwd", grid=(S // tm, N_CHIPS), in_specs=[t_spec, t_spec, w_spec, w_spec],
        out_specs=pl.BlockSpec((tm, D), lambda i, k: (i, 0)), out_shape=jax.ShapeDtypeStruct((S, D), F32),
        compiler_params=_cparams("arbitrary", "arbitrary"))(dg, du, wg, wu)


def mm_nn(a, b, l, tn, out_dtype, resid=None):
    S, K = a.shape
    N = b.shape[2]
    tm = _tile(S, 512)

    def body(a_ref, b_ref, *rest):
        o_ref = rest[-1]
        acc = _dot(a_ref[...].astype(BF16), b_ref[...])
        if resid is not None:
            acc = rest[0][...] + acc
        o_ref[...] = acc.astype(out_dtype)

    o_spec = pl.BlockSpec((tm, tn), lambda n, i: (i, n))
    in_specs = [pl.BlockSpec((tm, K), lambda n, i: (i, 0)), pl.BlockSpec((None, K, tn), lambda n, i: (l, 0, n))]
    args = [a, b]
    if resid is not None:
        in_specs.append(o_spec)
        args.append(resid)
    return pl.pallas_call(
        body, name="mm_nn", grid=(N // tn, S // tm), in_specs=in_specs, out_specs=o_spec,
        out_shape=jax.ShapeDtypeStruct((S, N), out_dtype),
        compiler_params=_cparams("arbitrary", "arbitrary"))(*args)


def mm_nt(a, b, l, tn, out_dtype):
    S, K = a.shape
    N = b.shape[1]
    tm = _tile(S, 512)

    def body(a_ref, b_ref, o_ref):
        o_ref[...] = _dot_nt(a_ref[...].astype(BF16), b_ref[...]).astype(out_dtype)

    return pl.pallas_call(
        body, name="mm_nt", grid=(N // tn, S // tm),
        in_specs=[pl.BlockSpec((tm, K), lambda n, i: (i, 0)), pl.BlockSpec((None, tn, K), lambda n, i: (l, n, 0))],
        out_specs=pl.BlockSpec((tm, tn), lambda n, i: (i, n)), out_shape=jax.ShapeDtypeStruct((S, N), out_dtype),
        compiler_params=_cparams("arbitrary", "arbitrary"))(a, b)


def mm_tn(a, b, tka, tnb, scale=1.0, blocked=False):
    S, Ka = a.shape
    Nb = b.shape[1]
    ts = _tile(S, 1024)
    ns = S // ts

    def body(a_ref, b_ref, o_ref, acc_ref):
        s = pl.program_id(2)

        @pl.when(s == 0)
        def _():
            acc_ref[...] = jnp.zeros_like(acc_ref)

        acc_ref[...] += _dot_tn(a_ref[...].astype(BF16), b_ref[...].astype(BF16))

        @pl.when(s == ns - 1)
        def _():
            o_ref[...] = (scale * acc_ref[...]).astype(BF16)

    if blocked:
        o_spec = pl.BlockSpec((None, tka, tnb), lambda ka, nb, s: (nb, ka, 0))
        o_shape = jax.ShapeDtypeStruct((Nb // tnb, Ka, tnb), BF16)
    else:
        o_spec = pl.BlockSpec((tka, tnb), lambda ka, nb, s: (ka, nb))
        o_shape = jax.ShapeDtypeStruct((Ka, Nb), BF16)
    return pl.pallas_call(
        body, name="mm_tn", grid=(Ka // tka, Nb // tnb, ns),
        in_specs=[pl.BlockSpec((ts, tka), lambda ka, nb, s: (s, ka)),
                  pl.BlockSpec((ts, tnb), lambda ka, nb, s: (s, nb))],
        out_specs=o_spec, out_shape=o_shape, scratch_shapes=[pltpu.VMEM((tka, tnb), F32)],
        compiler_params=_cparams("arbitrary", "arbitrary", "arbitrary"))(a, b)


def _rope(x, c2, s2):
    half = x.shape[-1] // 2
    rot = jnp.concatenate([x[:, half:], x[:, :half]], axis=-1)
    return x * c2 + rot * s2


def _tri(tm, upper):
    r = lax.broadcasted_iota(jnp.int32, (tm, tm), 0)
    c = lax.broadcasted_iota(jnp.int32, (tm, tm), 1)
    return jnp.where((c >= r) if upper else (c <= r), 1.0, 0.0).astype(F32)


def _log_sigmoid(x):
    return jnp.minimum(x, 0.0) - jnp.log(1.0 + jnp.exp(-jnp.abs(x)))


def _norm_hat(c):
    r = lax.rsqrt(jnp.mean(c * c, axis=-1, keepdims=True) + RMS_EPS)
    return c * r, r


def _tok_spec(tm, width, rev_n=None):
    if rev_n is None:
        return pl.BlockSpec((tm, width), lambda i: (i, 0))
    return pl.BlockSpec((tm, width), lambda i: (rev_n - 1 - i, 0))


def _head_spec(heads, tm, width, rev_n=None):
    if rev_n is None:
        return pl.BlockSpec((heads, tm, width), lambda i: (0, i, 0))
    return pl.BlockSpec((heads, tm, width), lambda i: (0, rev_n - 1 - i, 0))


def _full_spec(shape):
    return pl.BlockSpec(shape, lambda i: (0,) * len(shape))


def mixer_prep(p, c2, s2, q_norm, kv_norm, bias, wqb, wkvb, l):
    S = p.shape[0]
    tm = _tile(S, 256)
    H = N_HEADS

    def body(p_ref, c2_ref, s2_ref, qn_ref, kvn_ref, b_ref, wqb_ref, wkvb_ref,
             cqn_ref, ckvn_ref, qm_ref, km_ref, vm_ref, qs_ref, ks_ref, vs_ref, qf_ref, kf_ref, vf_ref,
             ccol_ref, crow_ref, carry_row, carry_col):
        c2, s2 = c2_ref[...], s2_ref[...]
        cqn = (_norm_hat(p_ref[:, P_CQ:P_CQ + Q_LORA])[0] * qn_ref[...]).astype(BF16)
        ckvn = (_norm_hat(p_ref[:, P_CKV:P_CKV + KV_LORA])[0] * kvn_ref[...]).astype(BF16)
        cqn_ref[...] = cqn
        ckvn_ref[...] = ckvn
        q = _dot(cqn, wqb_ref[...])
        kv = _dot(ckvn, wkvb_ref[...])
        k_pe = _rope(p_ref[:, P_KR:P_KR + ROPE], c2, s2).astype(BF16)
        for h in range(H):
            qm_ref[h, :, 0:NOPE] = q[:, h * QK_MLA:h * QK_MLA + NOPE].astype(BF16)
            qm_ref[h, :, NOPE:QK_MLA] = _rope(q[:, h * QK_MLA + NOPE:(h + 1) * QK_MLA], c2, s2).astype(BF16)
            km_ref[h, :, 0:NOPE] = kv[:, h * 256:h * 256 + NOPE].astype(BF16)
            km_ref[h, :, NOPE:QK_MLA] = k_pe
            vm_ref[h] = kv[:, h * 256 + NOPE:(h + 1) * 256].astype(BF16)
            qs_ref[h] = _rope(p_ref[:, P_QS + h * HD:P_QS + (h + 1) * HD], c2, s2).astype(BF16)
            qf_ref[h] = p_ref[:, P_QF + h * HD:P_QF + (h + 1) * HD].astype(BF16)
            kf_ref[h] = p_ref[:, P_KF + h * HD:P_KF + (h + 1) * HD].astype(BF16)
            vf_ref[h] = p_ref[:, P_VF + h * HD:P_VF + (h + 1) * HD].astype(BF16)
        for h in range(SWA_KV):
            ks_ref[h] = _rope(p_ref[:, P_KS + h * HD:P_KS + (h + 1) * HD], c2, s2).astype(BF16)
            vs_ref[h] = p_ref[:, P_VS + h * HD:P_VS + (h + 1) * HD].astype(BF16)

        @pl.when(pl.program_id(0) == 0)
        def _():
            carry_row[...] = jnp.zeros_like(carry_row)
            carry_col[...] = jnp.zeros_like(carry_col)

        log_f = _log_sigmoid(p_ref[:, P_F:P_F + H] + b_ref[...])
        c_tok = jnp.dot(_tri(tm, upper=False), log_f, preferred_element_type=F32,
                        precision=lax.Precision.HIGHEST) + carry_row[0:1, 0:H]
        c_rows = _dot_tn_f32(log_f, _tri(tm, upper=True)) + carry_col[:, 0:1]
        for h in range(H):
            ccol_ref[h] = c_tok[:, h:h + 1]
            crow_ref[h] = c_rows[h:h + 1, :]
        carry_row[0:1, 0:H] = c_tok[tm - 1:tm, :]
        carry_col[:, 0:1] = c_rows[:, tm - 1:tm]

    bshape = lambda *s: jax.ShapeDtypeStruct(s, BF16)
    out_shape = [bshape(S, Q_LORA), bshape(S, KV_LORA), bshape(H, S, QK_MLA), bshape(H, S, QK_MLA), bshape(H, S, VDIM),
                 bshape(H, S, HD), bshape(SWA_KV, S, HD), bshape(SWA_KV, S, HD), bshape(H, S, HD), bshape(H, S, HD),
                 bshape(H, S, HD), jax.ShapeDtypeStruct((H, S, 1), F32), jax.ShapeDtypeStruct((H, 1, S), F32)]
    out_specs = [_tok_spec(tm, Q_LORA), _tok_spec(tm, KV_LORA), _head_spec(H, tm, QK_MLA), _head_spec(H, tm, QK_MLA),
                 _head_spec(H, tm, VDIM), _head_spec(H, tm, HD), _head_spec(SWA_KV, tm, HD), _head_spec(SWA_KV, tm, HD),
                 _head_spec(H, tm, HD), _head_spec(H, tm, HD), _head_spec(H, tm, HD), _head_spec(H, tm, 1),
                 pl.BlockSpec((H, 1, tm), lambda i: (0, 0, i))]
    in_specs = [_tok_spec(tm, P_COLS), _tok_spec(tm, ROPE), _tok_spec(tm, ROPE), _full_spec((1, Q_LORA)),
                _full_spec((1, KV_LORA)), _full_spec((1, H)),
                pl.BlockSpec((None,) + wqb.shape[1:], lambda i: (l, 0, 0)),
                pl.BlockSpec((None,) + wkvb.shape[1:], lambda i: (l, 0, 0))]
    return pl.pallas_call(
        body, name="mixer_prep", grid=(S // tm,), in_specs=in_specs, out_specs=out_specs, out_shape=out_shape,
        scratch_shapes=[pltpu.VMEM((8, 128), F32), pltpu.VMEM((8, 128), F32)],
        compiler_params=_cparams("arbitrary"))(p, c2, s2, q_norm, kv_norm, bias, wqb, wkvb)


def mixer_prep_bwd(p, c2, s2, q_norm, kv_norm, bias, wqb, wkvb, l, dqm, dkm, dvm, dqs, dks, dvs, dqf, dkf, dvf, dc):
    S = p.shape[0]
    tm = _tile(S, 256)
    nt = S // tm
    H, G = N_HEADS, N_HEADS // SWA_KV

    def body(p_ref, c2_ref, s2_ref, qn_ref, kvn_ref, b_ref, wqb_ref, wkvb_ref,
             dqm_ref, dkm_ref, dvm_ref, dqs_ref, dks_ref, dvs_ref, dqf_ref, dkf_ref, dvf_ref, dc_ref,
             dp_ref, dq_ref, dkv_ref, dqn_ref, dkvn_ref, db_ref, carry):
        c2, s2 = c2_ref[...], -s2_ref[...]

        @pl.when(pl.program_id(0) == 0)
        def _():
            dqn_ref[...] = jnp.zeros_like(dqn_ref)
            dkvn_ref[...] = jnp.zeros_like(dkvn_ref)
            db_ref[...] = jnp.zeros_like(db_ref)
            carry[...] = jnp.zeros_like(carry)

        dk_pe = jnp.zeros((tm, ROPE), F32)
        for h in range(H):
            dq_ref[:, h * QK_MLA:h * QK_MLA + NOPE] = dqm_ref[h, :, 0:NOPE].astype(BF16)
            dq_ref[:, h * QK_MLA + NOPE:(h + 1) * QK_MLA] = _rope(dqm_ref[h, :, NOPE:QK_MLA], c2, s2).astype(BF16)
            dkv_ref[:, h * 256:h * 256 + NOPE] = dkm_ref[h, :, 0:NOPE].astype(BF16)
            dkv_ref[:, h * 256 + NOPE:(h + 1) * 256] = dvm_ref[h].astype(BF16)
            dk_pe = dk_pe + dkm_ref[h, :, NOPE:QK_MLA]

        def through_norm(dcn, c, gain, dgain_ref):
            c_hat, r = _norm_hat(c)
            dhg = dcn * gain
            dgain_ref[...] += jnp.sum(dcn * c_hat, axis=0, keepdims=True)
            return r * (dhg - c_hat * jnp.mean(dhg * c_hat, axis=-1, keepdims=True))

        dcqn = _dot_nt(dq_ref[...], wqb_ref[...])
        dckvn = _dot_nt(dkv_ref[...], wkvb_ref[...])
        dp_ref[:, P_CQ:P_CQ + Q_LORA] = through_norm(dcqn, p_ref[:, P_CQ:P_CQ + Q_LORA], qn_ref[...], dqn_ref).astype(BF16)
        dp_ref[:, P_CKV:P_CKV + KV_LORA] = through_norm(
            dckvn, p_ref[:, P_CKV:P_CKV + KV_LORA], kvn_ref[...], dkvn_ref).astype(BF16)
        for h in range(H):
            dp_ref[:, P_QS + h * HD:P_QS + (h + 1) * HD] = _rope(dqs_ref[h], c2, s2).astype(BF16)
            dp_ref[:, P_QF + h * HD:P_QF + (h + 1) * HD] = dqf_ref[h].astype(BF16)
            dp_ref[:, P_KF + h * HD:P_KF + (h + 1) * HD] = dkf_ref[h].astype(BF16)
            dp_ref[:, P_VF + h * HD:P_VF + (h + 1) * HD] = dvf_ref[h].astype(BF16)
        for kvh in range(SWA_KV):
            dk = dks_ref[kvh * G]
            dv = dvs_ref[kvh * G]
            for g in range(1, G):
                dk = dk + dks_ref[kvh * G + g]
                dv = dv + dvs_ref[kvh * G + g]
            dp_ref[:, P_KS + kvh * HD:P_KS + (kvh + 1) * HD] = _rope(dk, c2, s2).astype(BF16)
            dp_ref[:, P_VS + kvh * HD:P_VS + (kvh + 1) * HD] = dv.astype(BF16)

        dcv = dc_ref[...]
        dlog_f = jnp.dot(_tri(tm, upper=True), dcv, preferred_element_type=F32,
                         precision=lax.Precision.HIGHEST) + carry[0:1, 0:H]
        carry[0:1, 0:H] = dlog_f[0:1, :]
        df = dlog_f * _sigmoid(-(p_ref[:, P_F:P_F + H] + b_ref[...]))
        db_ref[...] += jnp.sum(df, axis=0, keepdims=True)
        dp_ref[:, P_KR:P_COLS] = jnp.zeros((tm, P_COLS - P_KR), BF16)
        dp_ref[:, P_KR:P_KR + ROPE] = _rope(dk_pe, c2, s2).astype(BF16)
        dp_ref[:, P_F:P_F + H] = df.astype(BF16)

    rev = nt
    in_specs = [_tok_spec(tm, P_COLS, rev), _tok_spec(tm, ROPE, rev), _tok_spec(tm, ROPE, rev), _full_spec((1, Q_LORA)),
                _full_spec((1, KV_LORA)), _full_spec((1, H)),
                pl.BlockSpec((None,) + wqb.shape[1:], lambda i: (l, 0, 0)),
                pl.BlockSpec((None,) + wkvb.shape[1:], lambda i: (l, 0, 0)),
                _head_spec(H, tm, QK_MLA, rev), _head_spec(H, tm, QK_MLA, rev), _head_spec(H, tm, VDIM, rev)]
    in_specs += [_head_spec(H, tm, HD, rev)] * 6 + [_tok_spec(tm, H, rev)]
    out_specs = [_tok_spec(tm, P_COLS, rev), _tok_spec(tm, N_HEADS * QK_MLA, rev), _tok_spec(tm, N_HEADS * 256, rev),
                 _full_spec((1, Q_LORA)), _full_spec((1, KV_LORA)), _full_spec((1, H))]
    out_shape = [jax.ShapeDtypeStruct((S, P_COLS), BF16), jax.ShapeDtypeStruct((S, N_HEADS * QK_MLA), BF16),
                 jax.ShapeDtypeStruct((S, N_HEADS * 256), BF16), jax.ShapeDtypeStruct((1, Q_LORA), F32),
                 jax.ShapeDtypeStruct((1, KV_LORA), F32), jax.ShapeDtypeStruct((1, H), F32)]
    return pl.pallas_call(
        body, name="mixer_prep_bwd", grid=(nt,), in_specs=in_specs, out_specs=out_specs, out_shape=out_shape,
        scratch_shapes=[pltpu.VMEM((8, 128), F32)], compiler_params=_cparams("arbitrary"))(
            p, c2, s2, q_norm, kv_norm, bias, wqb, wkvb, dqm, dkm, dvm, dqs, dks, dvs, dqf, dkf, dvf, dc)


def merge_heads(o_mla, o_swa, o_fox):
    H, S, _ = o_mla.shape
    tm = _tile(S, 512)
    width = H * (VDIM + 2 * HD)

    def body(om_ref, os_ref, of_ref, m_ref):
        for h in range(H):
            m_ref[:, h * VDIM:(h + 1) * VDIM] = om_ref[h]
            m_ref[:, H * VDIM + h * HD:H * VDIM + (h + 1) * HD] = os_ref[h]
            m_ref[:, H * (VDIM + HD) + h * HD:H * (VDIM + HD) + (h + 1) * HD] = of_ref[h]

    return pl.pallas_call(
        body, name="merge_heads", grid=(S // tm,),
        in_specs=[_head_spec(H, tm, VDIM), _head_spec(H, tm, HD), _head_spec(H, tm, HD)],
        out_specs=_tok_spec(tm, width), out_shape=jax.ShapeDtypeStruct((S, width), BF16),
        compiler_params=_cparams("arbitrary"))(o_mla, o_swa, o_fox)


def split_heads(dmixed, o_mla, o_swa, o_fox):
    H, S, _ = o_mla.shape
    tm = _tile(S, 512)

    def body(dm_ref, om_ref, os_ref, of_ref, dom_ref, dos_ref, dof_ref, dm_delta, ds_delta, df_delta):
        def one(h, off, d, o_ref, do_ref, delta_ref):
            dv = dm_ref[:, off:off + d]
            do_ref[h] = dv
            prod = dv.astype(F32) * o_ref[h].astype(F32)
            rows = lax.dot_general(jnp.ones((8, d), F32), prod, (((1,), (1,)), ((), ())),
                                   preferred_element_type=F32, precision=lax.Precision.HIGHEST)
            delta_ref[h] = rows[0:1, :]

        for h in range(H):
            one(h, h * VDIM, VDIM, om_ref, dom_ref, dm_delta)
            one(h, H * VDIM + h * HD, HD, os_ref, dos_ref, ds_delta)
            one(h, H * (VDIM + HD) + h * HD, HD, of_ref, dof_ref, df_delta)

    row_spec = pl.BlockSpec((H, 1, tm), lambda i: (0, 0, i))
    row_shape = jax.ShapeDtypeStruct((H, 1, S), F32)
    hs = lambda d: _head_spec(H, tm, d)
    return pl.pallas_call(
        body, name="split_heads", grid=(S // tm,),
        in_specs=[_tok_spec(tm, dmixed.shape[1]), hs(VDIM), hs(HD), hs(HD)],
        out_specs=[hs(VDIM), hs(HD), hs(HD), row_spec, row_spec, row_spec],
        out_shape=[jax.ShapeDtypeStruct((H, S, VDIM), BF16), jax.ShapeDtypeStruct((H, S, HD), BF16),
                   jax.ShapeDtypeStruct((H, S, HD), BF16), row_shape, row_shape, row_shape],
        compiler_params=_cparams("arbitrary"))(dmixed, o_mla, o_swa, o_fox)


def _attn_tile(S):
    return 512 if (S % 512 == 0 and S > 512) else S // 2


def _valid(q0, k0, shape, q_axis, window):
    qpos = q0 + lax.broadcasted_iota(jnp.int32, shape, q_axis)
    kpos = k0 + lax.broadcasted_iota(jnp.int32, shape, 1 - q_axis)
    ok = kpos <= qpos
    if window is not None:
        ok = jnp.logical_and(ok, kpos > qpos - window)
    return ok


def attn_fwd(name, q, k, v, scale, window=None, sinks=None, ccol=None, crow=None):
    H, S, dq = q.shape
    Hk, _, dv = v.shape
    G = H // Hk
    t = _attn_tile(S)
    fox, use_sink = ccol is not None, sinks is not None

    def body(*refs):
        refs = list(refs)
        sink_ref = refs.pop(0) if use_sink else None
        q_ref, k_ref, v_ref = refs[:3]
        refs = refs[3:]
        ccol_ref, crow_ref = (refs.pop(0), refs.pop(0)) if fox else (None, None)
        o_ref, lse_ref, m_ref, l_ref, acc_ref = refs
        h, i = pl.program_id(0), pl.program_id(1)
        qv = q_ref[...]
        if use_sink:
            m_ref[...] = jnp.full(m_ref.shape, sink_ref[h], F32)
            l_ref[...] = jnp.ones(l_ref.shape, F32)
        else:
            m_ref[...] = jnp.full(m_ref.shape, NEG, F32)
            l_ref[...] = jnp.zeros(l_ref.shape, F32)
        acc_ref[...] = jnp.zeros(acc_ref.shape, F32)

        def step(j, masked):
            off = pl.multiple_of(j * t, t)
            s = _dot_nt(qv, k_ref[pl.ds(off, t), :]) * scale
            if fox:
                s = s + (ccol_ref[...] - crow_ref[:, pl.ds(off, t)])
            if masked:
                s = jnp.where(_valid(i * t, j * t, (t, t), 0, window), s, NEG)
            m_prev = m_ref[...]
            m_new = jnp.maximum(m_prev, jnp.max(s, axis=-1, keepdims=True))
            alpha = jnp.exp(m_prev - m_new)
            pm = jnp.exp(s - m_new)
            l_ref[...] = alpha * l_ref[...] + jnp.sum(pm, axis=-1, keepdims=True)
            acc_ref[...] = alpha * acc_ref[...] + _dot(pm.astype(BF16), v_ref[pl.ds(off, t), :])
            m_ref[...] = m_new

        def loop(lo, hi, masked):
            def it(j, carry):
                step(j, masked)
                return carry
            lax.fori_loop(lo, hi, it, 0)

        if window is None:
            loop(0, i, False)
            step(i, True)
        else:
            loop(jnp.maximum(i * t - (window - 1), 0) // t, i + 1, True)
        l = l_ref[...]
        o_ref[...] = (acc_ref[...] / l).astype(BF16)
        lse = m_ref[...] + jnp.log(l)
        lse_ref[...] = jnp.broadcast_to(lse, (t, 128)).T[0:1, :]

    in_specs, args = [], []
    if use_sink:
        in_specs.append(pl.BlockSpec(memory_space=pltpu.SMEM))
        args.append(sinks)
    in_specs += [pl.BlockSpec((None, t, dq), lambda h, i: (h, i, 0)),
                 pl.BlockSpec((None, S, dq), lambda h, i: (h // G, 0, 0)),
                 pl.BlockSpec((None, S, dv), lambda h, i: (h // G, 0, 0))]
    args += [q, k, v]
    if fox:
        in_specs += [pl.BlockSpec((None, t, 1), lambda h, i: (h, i, 0)), pl.BlockSpec((None, 1, S), lambda h, i: (h, 0, 0))]
        args += [ccol, crow]
    return pl.pallas_call(
        body, name=name, grid=(H, S // t), in_specs=in_specs,
        out_specs=[pl.BlockSpec((None, t, dv), lambda h, i: (h, i, 0)), pl.BlockSpec((None, 1, t), lambda h, i: (h, 0, i))],
        out_shape=[jax.ShapeDtypeStruct((H, S, dv), BF16), jax.ShapeDtypeStruct((H, 1, S), F32)],
        scratch_shapes=[pltpu.VMEM((t, 1), F32), pltpu.VMEM((t, 1), F32), pltpu.VMEM((t, dv), F32)],
        compiler_params=_cparams("arbitrary", "arbitrary"))(*args)


def attn_bwd(name, q, k, v, do, lse, delta, scale, window=None, sinks=None, ccol=None, crow=None):
    H, S, dq = q.shape
    Hk, _, dv = v.shape
    G = H // Hk
    t = _attn_tile(S)
    nq = S // t
    fox, use_sink = ccol is not None, sinks is not None

    def body(*refs):
        refs = list(refs)
        sink_ref = refs.pop(0) if use_sink else None
        q_ref, k_ref, v_ref, do_ref, lse_ref, delta_ref = refs[:6]
        refs = refs[6:]
        ccol_ref, crow_ref = (refs.pop(0), refs.pop(0)) if fox else (None, None)
        dq_ref, dk_ref, dv_ref = refs[:3]
        refs = refs[3:]
        dcq_ref, dck_ref = (refs.pop(0), refs.pop(0)) if fox else (None, None)
        dsink_ref = refs.pop(0) if use_sink else None
        h, j = pl.program_id(0), pl.program_id(1)

        @pl.when(j == 0)
        def _():
            dq_ref[...] = jnp.zeros(dq_ref.shape, F32)
            if fox:
                dcq_ref[...] = jnp.zeros(dcq_ref.shape, F32)
            if use_sink:
                ps = jnp.exp(sink_ref[h] - lse_ref[...]) * delta_ref[...]
                dsink_ref[...] = jnp.broadcast_to(-jnp.sum(ps, axis=-1, keepdims=True), dsink_ref.shape)

        dk_ref[...] = jnp.zeros(dk_ref.shape, F32)
        dv_ref[...] = jnp.zeros(dv_ref.shape, F32)
        if fox:
            dck_ref[...] = jnp.zeros(dck_ref.shape, F32)
        kv, vv = k_ref[...], v_ref[...]

        def step(i, masked):
            off = pl.multiple_of(i * t, t)
            qi, doi = q_ref[pl.ds(off, t), :], do_ref[pl.ds(off, t), :]
            st = _dot_nt(kv, qi) * scale
            if fox:
                st = st + (crow_ref[:, pl.ds(off, t)] - ccol_ref[...])
            if masked:
                st = jnp.where(_valid(i * t, j * t, (t, t), 1, window), st, NEG)
            pt = jnp.exp(st - lse_ref[:, pl.ds(off, t)])
            dv_ref[...] += _dot(pt.astype(BF16), doi)
            dst = pt * (_dot_nt(vv, doi) - delta_ref[:, pl.ds(off, t)])
            if fox:
                dcq_ref[:, pl.ds(off, t)] += jnp.sum(dst, axis=0, keepdims=True)
                dck_ref[...] -= jnp.sum(dst, axis=1, keepdims=True)
            dsb = (dst * scale).astype(BF16)
            dk_ref[...] += _dot(dsb, qi)
            dq_ref[pl.ds(off, t), :] += _dot_tn(dsb, kv)

        def loop(lo, hi, masked):
            def it(i, carry):
                step(i, masked)
                return carry
            lax.fori_loop(lo, hi, it, 0)

        if window is None:
            step(j, True)
            loop(j + 1, nq, False)
        else:
            loop(j, jnp.minimum((j * t + t - 1 + window - 1) // t, nq - 1) + 1, True)

    in_specs, args = [], []
    if use_sink:
        in_specs.append(pl.BlockSpec(memory_space=pltpu.SMEM))
        args.append(sinks)
    whole = lambda d: pl.BlockSpec((None, S, d), lambda h, j: (h, 0, 0))
    row = pl.BlockSpec((None, 1, S), lambda h, j: (h, 0, 0))
    in_specs += [whole(dq), pl.BlockSpec((None, t, dq), lambda h, j: (h // G, j, 0)),
                 pl.BlockSpec((None, t, dv), lambda h, j: (h // G, j, 0)), whole(dv), row, row]
    args += [q, k, v, do, lse, delta]
    out_specs = [whole(dq), pl.BlockSpec((None, t, dq), lambda h, j: (h, j, 0)), pl.BlockSpec((None, t, dv), lambda h, j: (h, j, 0))]
    out_shape = [jax.ShapeDtypeStruct((H, S, dq), F32), jax.ShapeDtypeStruct((H, S, dq), F32),
                 jax.ShapeDtypeStruct((H, S, dv), F32)]
    if fox:
        in_specs += [pl.BlockSpec((None, t, 1), lambda h, j: (h, j, 0)), row]
        args += [ccol, crow]
        out_specs += [row, pl.BlockSpec((None, t, 1), lambda h, j: (h, j, 0))]
        out_shape += [jax.ShapeDtypeStruct((H, 1, S), F32), jax.ShapeDtypeStruct((H, S, 1), F32)]
    if use_sink:
        out_specs.append(pl.BlockSpec((None, 1, 128), lambda h, j: (h, 0, 0)))
        out_shape.append(jax.ShapeDtypeStruct((H, 1, 128), F32))
    return pl.pallas_call(
        body, name=name, grid=(H, nq), in_specs=in_specs, out_specs=out_specs, out_shape=out_shape,
        compiler_params=_cparams("arbitrary", "arbitrary"))(*args)


def _rows_tile(rows):
    return 256 if rows % 256 == 0 else rows


def adamw(w, g, m, v):
    R, C = w.shape
    tr = _rows_tile(R)

    def body(w_ref, g_ref, m_ref, v_ref, d_ref, nm_ref, nv_ref):
        gv = g_ref[...]
        mn = ADAM_B1 * m_ref[...] + (1.0 - ADAM_B1) * gv
        vn = ADAM_B2 * v_ref[...] + (1.0 - ADAM_B2) * (gv * gv)
        m_hat = mn / (1.0 - ADAM_B1 ** ADAM_STEP)
        v_hat = vn / (1.0 - ADAM_B2 ** ADAM_STEP)
        d_ref[...] = -ADAM_LR * (m_hat / (jnp.sqrt(v_hat) + ADAM_EPS) + ADAM_WD * w_ref[...])
        nm_ref[...] = mn
        nv_ref[...] = vn

    spec = pl.BlockSpec((tr, C), lambda i: (i, 0))
    shape = jax.ShapeDtypeStruct((R, C), F32)
    return pl.pallas_call(
        body, name="adamw", grid=(R // tr,), in_specs=[spec] * 4, out_specs=[spec] * 3, out_shape=[shape] * 3,
        compiler_params=_cparams("arbitrary"))(w, g, m, v)


def sum_pair(grad, other, layer):
    _, R, C = grad.shape
    tr = _rows_tile(R)

    def body(l_ref, g_ref, o_ref, out_ref):
        out_ref[...] = (g_ref[...].astype(F32) + o_ref[...].astype(F32)).astype(BF16)

    return pl.pallas_call(
        body, name="sum_pair",
        grid_spec=pltpu.PrefetchScalarGridSpec(
            num_scalar_prefetch=1, grid=(R // tr,),
            in_specs=[pl.BlockSpec((None, tr, C), lambda i, l: (l[0], i, 0)), pl.BlockSpec((tr, C), lambda i, l: (i, 0))],
            out_specs=pl.BlockSpec((tr, C), lambda i, l: (i, 0))),
        out_shape=jax.ShapeDtypeStruct((R, C), BF16), compiler_params=_cparams("arbitrary"))(layer, grad, other)


def sum_chips(part, recv, me):
    _, R, C = part.shape
    tr = _rows_tile(R)

    def body(me_ref, p_ref, r_ref, out_ref):
        acc = p_ref[...].astype(F32)
        for k in range(N_CHIPS - 1):
            acc = acc + r_ref[k].astype(F32)
        out_ref[...] = acc

    return pl.pallas_call(
        body, name="sum_chips",
        grid_spec=pltpu.PrefetchScalarGridSpec(
            num_scalar_prefetch=1, grid=(R // tr,),
            in_specs=[pl.BlockSpec((None, tr, C), lambda i, m: (m[0], i, 0)),
                      pl.BlockSpec((N_CHIPS - 1, tr, C), lambda i, m: (0, i, 0))],
            out_specs=pl.BlockSpec((tr, C), lambda i, m: (i, 0))),
        out_shape=jax.ShapeDtypeStruct((R, C), F32), compiler_params=_cparams("arbitrary"))(me, part, recv)


_ANY = pl.BlockSpec(memory_space=pl.ANY)


def _place():
    x, y, c = lax.axis_index("x"), lax.axis_index("y"), lax.axis_index("c")
    chips = [(1 - x, y), (x, 1 - y), (1 - x, 1 - y)]
    return x, y, c, chips


def all_gather_shards(ws):
    n = len(ws)

    def body(*refs):
        w, o = refs[:n], refs[n:2 * n]
        send, recv, loc = refs[2 * n:]
        x, y, c, chips = _place()
        me, sib = 2 * x + y, (x, y, 1 - c)

        def remote(t, k, layer, shard, to, src=None):
            blk = o[t].at[layer, shard]
            return pltpu.make_async_remote_copy(src_ref=blk if src is None else src, dst_ref=blk, send_sem=send.at[t, k],
                                                recv_sem=recv.at[t, k], device_id=to, device_id_type=MESH)

        own = [pltpu.make_async_copy(w[t].at[layer], o[t].at[layer, me], loc.at[t, layer])
               for t in range(n) for layer in range(2)]
        for cp in own:
            cp.start()
        first = [remote(t, k, c, me, (*chip, c), src=w[t].at[c]) for t in range(n) for k, chip in enumerate(chips)]
        for cp in first:
            cp.start()
        passed = []
        for t in range(n):
            for k, chip in enumerate(chips):
                remote(t, k, c, 2 * chip[0] + chip[1], (x, y, c)).wait_recv()
                passed.append(remote(t, 3 + k, c, 2 * chip[0] + chip[1], sib))
                passed[-1].start()
        for t in range(n):
            for k, chip in enumerate(chips):
                remote(t, 3 + k, 1 - c, 2 * chip[0] + chip[1], (x, y, c)).wait_recv()
        for cp in first + passed:
            cp.wait_send()
        for cp in own:
            cp.wait()

    return pl.pallas_call(
        body, name="all_gather_shards", in_specs=[_ANY] * n, out_specs=[_ANY] * n,
        out_shape=[jax.ShapeDtypeStruct((2, N_CHIPS) + w.shape[1:], w.dtype) for w in ws],
        scratch_shapes=[pltpu.SemaphoreType.DMA((n, 6)), pltpu.SemaphoreType.DMA((n, 6)), pltpu.SemaphoreType.DMA((n, 2))],
        compiler_params=pltpu.CompilerParams(has_side_effects=True))(*ws)


def pair_exchange(gs):
    n = len(gs)

    def body(*refs):
        g, o = refs[:n], refs[n:2 * n]
        send, recv = refs[2 * n:]
        x, y, c, _ = _place()
        cps = [pltpu.make_async_remote_copy(src_ref=g[t].at[1 - c], dst_ref=o[t], send_sem=send.at[t], recv_sem=recv.at[t],
                                            device_id=(x, y, 1 - c), device_id_type=MESH) for t in range(n)]
        for cp in cps:
            cp.start()
        for cp in cps:
            cp.wait()

    return pl.pallas_call(
        body, name="pair_exchange", in_specs=[_ANY] * n, out_specs=[_ANY] * n,
        out_shape=[jax.ShapeDtypeStruct(g.shape[1:], g.dtype) for g in gs],
        scratch_shapes=[pltpu.SemaphoreType.DMA((n,)), pltpu.SemaphoreType.DMA((n,))],
        compiler_params=pltpu.CompilerParams(has_side_effects=True))(*gs)


def chip_scatter(ps):
    n = len(ps)

    def body(*refs):
        p, o = refs[:n], refs[n:2 * n]
        send, recv = refs[2 * n:]
        x, y, c, chips = _place()
        cps = [pltpu.make_async_remote_copy(src_ref=p[t].at[2 * chip[0] + chip[1]], dst_ref=o[t].at[k],
                                            send_sem=send.at[t, k], recv_sem=recv.at[t, k], device_id=(*chip, c),
                                            device_id_type=MESH)
               for t in range(n) for k, chip in enumerate(chips)]
        for cp in cps:
            cp.start()
        for cp in cps:
            cp.wait()

    return pl.pallas_call(
        body, name="chip_scatter", in_specs=[_ANY] * n, out_specs=[_ANY] * n,
        out_shape=[jax.ShapeDtypeStruct((N_CHIPS - 1,) + p.shape[1:], p.dtype) for p in ps],
        scratch_shapes=[pltpu.SemaphoreType.DMA((n, 3)), pltpu.SemaphoreType.DMA((n, 3))],
        compiler_params=pltpu.CompilerParams(has_side_effects=True))(*ps)


def pair_share(rs):
    n = len(rs)

    def body(*refs):
        r, o = refs[:n], refs[n:2 * n]
        send, recv, loc = refs[2 * n:]
        x, y, c, _ = _place()
        own = [pltpu.make_async_copy(r[t], o[t].at[c], loc.at[t]) for t in range(n)]
        for cp in own:
            cp.start()

        def remote(t, layer):
            return pltpu.make_async_remote_copy(src_ref=r[t], dst_ref=o[t].at[layer], send_sem=send.at[t], recv_sem=recv.at[t],
                                                device_id=(x, y, 1 - c), device_id_type=MESH)

        for t in range(n):
            remote(t, c).start()
        for t in range(n):
            remote(t, 1 - c).wait_recv()
            remote(t, c).wait_send()
        for cp in own:
            cp.wait()

    return pl.pallas_call(
        body, name="pair_share", in_specs=[_ANY] * n, out_specs=[_ANY] * n,
        out_shape=[jax.ShapeDtypeStruct((2,) + r.shape, r.dtype) for r in rs],
        scratch_shapes=[pltpu.SemaphoreType.DMA((n,)), pltpu.SemaphoreType.DMA((n,)), pltpu.SemaphoreType.DMA((n,))],
        compiler_params=pltpu.CompilerParams(has_side_effects=True))(*rs)


def all_reduce_small(buf):
    def body(x_ref, o_ref, land, send, recv):
        x, y, c, _ = _place()
        me = 4 * x + 2 * y + c
        land[me] = x_ref[...]
        cps = []
        for mask in range(1, N_DEV):
            px = 1 - x if mask & 4 else x
            py = 1 - y if mask & 2 else y
            pc = 1 - c if mask & 1 else c
            cps.append(pltpu.make_async_remote_copy(src_ref=x_ref, dst_ref=land.at[me], send_sem=send.at[mask - 1],
                                                    recv_sem=recv.at[mask - 1], device_id=(px, py, pc), device_id_type=MESH))
            cps[-1].start()
        for mask in range(1, N_DEV):
            px = 1 - x if mask & 4 else x
            py = 1 - y if mask & 2 else y
            pc = 1 - c if mask & 1 else c
            pltpu.make_async_remote_copy(src_ref=x_ref, dst_ref=land.at[4 * px + 2 * py + pc], send_sem=send.at[mask - 1],
                                         recv_sem=recv.at[mask - 1], device_id=(px, py, pc), device_id_type=MESH).wait_recv()
        for cp in cps:
            cp.wait_send()
        acc = land[0]
        for d in range(1, N_DEV):
            acc = acc + land[d]
        o_ref[...] = acc

    vm = pl.BlockSpec(memory_space=pltpu.VMEM)
    return pl.pallas_call(
        body, name="all_reduce_small", in_specs=[vm], out_specs=vm, out_shape=jax.ShapeDtypeStruct(buf.shape, F32),
        scratch_shapes=[pltpu.VMEM((N_DEV,) + buf.shape, F32), pltpu.SemaphoreType.DMA((N_DEV - 1,)),
                        pltpu.SemaphoreType.DMA((N_DEV - 1,))])(buf)


def _ffn_fwd(x, gain, wg, wu, wd, l):
    h = rms_fwd(x, gain)
    g, u, a = gate_up(h, wg, wu, l)
    return down_proj(a, wd, l, x), (h, g, u, a)


def _ffn_bwd(dy, x, gain, wg, wu, wd, l, saved):
    h, g, u, a = saved
    D = x.shape[1]
    Fs = g.shape[1] // N_CHIPS
    dg, du = down_bwd(dy, wd, l, g, u)
    dwg = mm_tn(h, dg, _tile(D, 1024), Fs, blocked=True)
    dwu = mm_tn(h, du, _tile(D, 1024), Fs, blocked=True)
    dwd = mm_tn(a, dy, Fs, _tile(D, 1024), scale=0.5).reshape(N_CHIPS, Fs, D)
    dh = gate_up_bwd(dg, du, wg, wu, l)
    dx, dgain = rms_bwd(dh, x, gain, dy)
    return dx, dgain, dwg, dwu, dwd


def _mixer_fwd(x, gain, win, q_norm, kv_norm, sinks, bias, wqb, wkvb, wout, c2, s2, l):
    D = x.shape[1]
    h = rms_fwd(x, gain)
    p = mm_nn(h, win, l, 640, F32)
    (cqn, ckvn, qm, km, vm, qs, ks, vs, qf, kf, vf, ccol, crow) = mixer_prep(p, c2, s2, q_norm, kv_norm, bias, wqb, wkvb, l)
    o_mla, lse_mla = attn_fwd("attn_mla", qm, km, vm, QK_MLA ** -0.5)
    o_swa, lse_swa = attn_fwd("attn_swa", qs, ks, vs, HD ** -0.5, window=WINDOW, sinks=sinks)
    o_fox, lse_fox = attn_fwd("attn_fox", qf, kf, vf, HD ** -0.5, ccol=ccol, crow=crow)
    mixed = merge_heads(o_mla, o_swa, o_fox)
    y = mm_nn(mixed, wout, l, _tile(D, 1024), F32, resid=x)
    saved = (h, p, cqn, ckvn, qm, km, vm, qs, ks, vs, qf, kf, vf, ccol, crow, o_mla, o_swa, o_fox, lse_mla, lse_swa, lse_fox,
             mixed)
    return y, saved


def _mixer_bwd(dy, x, gain, win, q_norm, kv_norm, sinks, bias, wqb, wkvb, wout, c2, s2, l, saved):
    (h, p, cqn, ckvn, qm, km, vm, qs, ks, vs, qf, kf, vf, ccol, crow, o_mla, o_swa, o_fox, lse_mla, lse_swa, lse_fox,
     mixed) = saved
    S, D = x.shape
    width = mixed.shape[1]
    dmixed = mm_nt(dy, wout, l, _tile(width, 1024), BF16)
    dwout = mm_tn(mixed, dy, _tile(width, 1024), _tile(D, 1024))
    do_mla, do_swa, do_fox, dl_mla, dl_swa, dl_fox = split_heads(dmixed, o_mla, o_swa, o_fox)
    dqm, dkm, dvm = attn_bwd("attn_mla_bwd", qm, km, vm, do_mla, lse_mla, dl_mla, QK_MLA ** -0.5)
    dqs, dks, dvs, dsink = attn_bwd("attn_swa_bwd", qs, ks, vs, do_swa, lse_swa, dl_swa, HD ** -0.5, window=WINDOW,
                                    sinks=sinks)
    dqf, dkf, dvf, dcq, dck = attn_bwd("attn_fox_bwd", qf, kf, vf, do_fox, lse_fox, dl_fox, HD ** -0.5, ccol=ccol, crow=crow)
    dc = dcq.reshape(N_HEADS, S).T + dck.reshape(N_HEADS, S).T
    dp, dq, dkv, dqn, dkvn, dbias = mixer_prep_bwd(p, c2, s2, q_norm, kv_norm, bias, wqb, wkvb, l, dqm, dkm, dvm, dqs, dks,
                                                   dvs, dqf, dkf, dvf, dc)
    dwqb = mm_tn(cqn, dq, Q_LORA, N_HEADS * QK_MLA)
    dwkvb = mm_tn(ckvn, dkv, KV_LORA, 1024)
    dwin = mm_tn(h, dp, _tile(D, 1024), 640)
    dh = mm_nt(dp, win, l, _tile(D, 1024), F32)
    dx, dgain = rms_bwd(dh, x, gain, dy)
    return dx, dgain, dwin, dqn, dwqb, dkvn, dwkvb, dsink[:, 0, 0], dbias[0], dwout


def _pad_in_cols(w):
    pad = jnp.zeros(w.shape[:-1] + (P_COLS - IN_COLS,), w.dtype)
    return jnp.concatenate([w[..., :IN_KR], w[..., IN_KR + ROPE:IN_COLS - N_HEADS], w[..., IN_KR:IN_KR + ROPE],
                            w[..., IN_COLS - N_HEADS:], pad], axis=-1)


def _unpad_in_cols(w):
    return jnp.concatenate([w[..., :IN_KR], w[..., P_KR:P_KR + ROPE], w[..., IN_KR:P_KR], w[..., P_F:P_F + N_HEADS]], axis=-1)


def _col_shards(w):
    R = w.shape[0]
    return w.reshape(R, N_CHIPS, -1).transpose(1, 0, 2)


def _from_col_shards(w):
    L, _, R, C = w.shape
    return w.transpose(0, 2, 1, 3).reshape(L, R, N_CHIPS * C)


def kernel(x, positions, ffn1_norm, ffn1_w_gate, ffn1_w_up, ffn1_w_down, mix_norm, w_in, mla_q_norm, mla_w_q_b, mla_kv_norm, mla_w_kv_b, swa_sinks, fox_forget_bias, w_out, ffn2_norm, ffn2_w_gate, ffn2_w_up, ffn2_w_down, final_norm, loss_target, m_ffn1_norm, m_ffn1_w_gate, m_ffn1_w_up, m_ffn1_w_down, m_mix_norm, m_w_in, m_mla_q_norm, m_mla_w_q_b, m_mla_kv_norm, m_mla_w_kv_b, m_swa_sinks, m_fox_forget_bias, m_w_out, m_ffn2_norm, m_ffn2_w_gate, m_ffn2_w_up, m_ffn2_w_down, m_final_norm, v_ffn1_norm, v_ffn1_w_gate, v_ffn1_w_up, v_ffn1_w_down, v_mix_norm, v_w_in, v_mla_q_norm, v_mla_w_q_b, v_mla_kv_norm, v_mla_w_kv_b, v_swa_sinks, v_fox_forget_bias, v_w_out, v_ffn2_norm, v_ffn2_w_gate, v_ffn2_w_up, v_ffn2_w_down, v_final_norm):
    L = ffn1_norm.shape[0]
    S, D = x.shape[1], x.shape[2]
    F = ffn1_w_down.shape[1] * N_CHIPS
    xs, target = x[0], loss_target[0]
    cx, cy, cc = lax.axis_index("x"), lax.axis_index("y"), lax.axis_index("c")
    layer_id = jnp.reshape(cc, (1,)).astype(jnp.int32)
    chip_id = jnp.reshape(2 * cx + cy, (1,)).astype(jnp.int32)

    inv_freq = ROPE_THETA ** (-jnp.arange(0, ROPE, 2, dtype=F32) / ROPE)
    ang = positions[0].astype(F32)[:, None] * inv_freq
    cos, sin = jnp.cos(ang), jnp.sin(ang)
    c2, s2 = jnp.concatenate([cos, cos], axis=-1), jnp.concatenate([-sin, sin], axis=-1)

    big = [ffn1_w_gate, ffn1_w_up, ffn1_w_down, w_in, mla_w_q_b, mla_w_kv_b, w_out, ffn2_w_gate, ffn2_w_up, ffn2_w_down]
    wg1, wu1, wd1, win, wqb, wkvb, wout, wg2, wu2, wd2 = all_gather_shards([w.astype(BF16) for w in big])
    wd1, wd2 = wd1.reshape(L, F, D), wd2.reshape(L, F, D)
    wout = wout.reshape(L, -1, D)
    win = _pad_in_cols(_from_col_shards(win))
    wqb, wkvb = _from_col_shards(wqb), _from_col_shards(wkvb)

    acts = []
    h = xs
    for l in range(L):
        x0 = h
        x1, s1 = _ffn_fwd(x0, ffn1_norm[l][None], wg1, wu1, wd1, l)
        x2, sm = _mixer_fwd(x1, mix_norm[l][None], win, mla_q_norm[l][None], mla_kv_norm[l][None], swa_sinks[l],
                            fox_forget_bias[l][None], wqb, wkvb, wout, c2, s2, l)
        x3, s2_ = _ffn_fwd(x2, ffn2_norm[l][None], wg2, wu2, wd2, l)
        acts.append((x0, x1, x2, s1, sm, s2_))
        h = x3
    loss_part, dx, d_final = loss_head(h, final_norm[None], target)

    small = {k: [None] * L for k in ("ffn1_norm", "mix_norm", "q_norm", "kv_norm", "sinks", "bias", "ffn2_norm")}
    bigg = {k: [None] * L for k in ("wg1", "wu1", "wd1", "win", "wqb", "wkvb", "wout", "wg2", "wu2", "wd2")}
    for l in reversed(range(L)):
        x0, x1, x2, s1, sm, s2_ = acts[l]
        dx, small["ffn2_norm"][l], bigg["wg2"][l], bigg["wu2"][l], bigg["wd2"][l] = _ffn_bwd(
            dx, x2, ffn2_norm[l][None], wg2, wu2, wd2, l, s2_)
        (dx, small["mix_norm"][l], dwin, small["q_norm"][l], dwqb, small["kv_norm"][l], dwkvb, small["sinks"][l],
         small["bias"][l], dwout) = _mixer_bwd(dx, x1, mix_norm[l][None], win, mla_q_norm[l][None], mla_kv_norm[l][None],
                                               swa_sinks[l], fox_forget_bias[l][None], wqb, wkvb, wout, c2, s2, l, sm)
        bigg["win"][l] = _col_shards(_unpad_in_cols(dwin))
        bigg["wqb"][l] = _col_shards(dwqb)
        bigg["wkvb"][l] = _col_shards(dwkvb)
        bigg["wout"][l] = dwout.reshape(N_CHIPS, -1, D)
        dx, small["ffn1_norm"][l], bigg["wg1"][l], bigg["wu1"][l], bigg["wd1"][l] = _ffn_bwd(
            dx, x0, ffn1_norm[l][None], wg1, wu1, wd1, l, s1)
    grad_x = dx[None]

    names = ("wg1", "wu1", "wd1", "win", "wqb", "wkvb", "wout", "wg2", "wu2", "wd2")
    full = [jnp.stack(bigg[k]) for k in names]
    flat = [g.reshape(L, -1, g.shape[-1]) for g in full]
    from_sibling = pair_exchange(flat)
    part = [sum_pair(g, o, layer_id).reshape(f.shape[1:]) for g, o, f in zip(flat, from_sibling, full)]
    from_chips = chip_scatter(part)
    mine = [sum_chips(p, r, chip_id) for p, r in zip(part, from_chips)]
    grads_big = pair_share(mine)

    pieces = [jnp.concatenate(small["ffn1_norm"]), jnp.concatenate(small["mix_norm"]), jnp.concatenate(small["q_norm"]),
              jnp.concatenate(small["kv_norm"]), jnp.stack(small["sinks"]), jnp.stack(small["bias"]),
              jnp.concatenate(small["ffn2_norm"]), d_final, loss_part[:, 0:1]]
    sizes = [int(p.size) for p in pieces]
    packed = jnp.concatenate([p.reshape(-1) for p in pieces])
    packed = jnp.pad(packed, (0, SMALL_ROWS * 128 - packed.shape[0])).reshape(SMALL_ROWS, 128)
    summed = all_reduce_small(packed).reshape(-1)
    out_small, off = [], 0
    for p, n in zip(pieces, sizes):
        out_small.append(summed[off:off + n].reshape(p.shape))
        off += n
    g_ffn1_norm, g_mix_norm, g_q_norm, g_kv_norm, g_sinks, g_bias, g_ffn2_norm, g_final, loss = out_small
    loss = loss.reshape(())
    g_final = g_final.reshape(-1)

    gb = dict(zip(names, grads_big))
    grads = [g_ffn1_norm, gb["wg1"], gb["wu1"], gb["wd1"], g_mix_norm, gb["win"], g_q_norm, gb["wqb"], g_kv_norm, gb["wkvb"],
             g_sinks, g_bias, gb["wout"], g_ffn2_norm, gb["wg2"], gb["wu2"], gb["wd2"], g_final]
    weights = [ffn1_norm, ffn1_w_gate, ffn1_w_up, ffn1_w_down, mix_norm, w_in, mla_q_norm, mla_w_q_b, mla_kv_norm, mla_w_kv_b,
               swa_sinks, fox_forget_bias, w_out, ffn2_norm, ffn2_w_gate, ffn2_w_up, ffn2_w_down, final_norm]
    ms = [m_ffn1_norm, m_ffn1_w_gate, m_ffn1_w_up, m_ffn1_w_down, m_mix_norm, m_w_in, m_mla_q_norm, m_mla_w_q_b, m_mla_kv_norm,
          m_mla_w_kv_b, m_swa_sinks, m_fox_forget_bias, m_w_out, m_ffn2_norm, m_ffn2_w_gate, m_ffn2_w_up, m_ffn2_w_down,
          m_final_norm]
    vs = [v_ffn1_norm, v_ffn1_w_gate, v_ffn1_w_up, v_ffn1_w_down, v_mix_norm, v_w_in, v_mla_q_norm, v_mla_w_q_b, v_mla_kv_norm,
          v_mla_w_kv_b, v_swa_sinks, v_fox_forget_bias, v_w_out, v_ffn2_norm, v_ffn2_w_gate, v_ffn2_w_up, v_ffn2_w_down,
          v_final_norm]
    deltas, new_m, new_v = [], [], []
    for w, g, m, v in zip(weights, grads, ms, vs):
        two_d = (-1, w.shape[-1])
        d, nm, nv = adamw(w.reshape(two_d), g.reshape(two_d), m.reshape(two_d), v.reshape(two_d))
        deltas.append(d.reshape(w.shape))
        new_m.append(nm.reshape(w.shape))
        new_v.append(nv.reshape(w.shape))
    return (loss, grad_x, *grads, *deltas, *new_m, *new_v)
```

```python
import jax
import jax.numpy as jnp
from jax import lax
from jax.experimental import pallas as pl
from jax.experimental.pallas import tpu as pltpu

F32, BF16 = jnp.float32, jnp.bfloat16
MESH = pl.DeviceIdType.MESH

RMS_EPS = 1e-6
ROPE_THETA = 10000.0
N_HEADS = 8
Q_LORA, KV_LORA = 512, 256
NOPE, ROPE, VDIM = 128, 64, 128
QK_MLA = NOPE + ROPE
SWA_KV, HD, WINDOW = 2, 64, 128
P_CQ, P_CKV, P_QS, P_KS, P_VS, P_QF, P_KF, P_VF, P_KR, P_F, P_COLS = (
    0, 512, 768, 1280, 1408, 1536, 2048, 2560, 3072, 3136, 3200)
IN_COLS = 3144
IN_KR = 768
ADAM_LR, ADAM_B1, ADAM_B2, ADAM_EPS, ADAM_WD, ADAM_STEP = 0.001, 0.9, 0.999, 1e-08, 0.01, 10
NEG = -1e30
VMEM_LIMIT = 56 * 1024 * 1024
N_CHIPS = 4
N_DEV = 8
SMALL_ROWS = 128


def _tile(n, pref):
    return pref if n % pref == 0 else n


def _cparams(*sem):
    return pltpu.CompilerParams(dimension_semantics=sem, vmem_limit_bytes=VMEM_LIMIT)


def _sigmoid(x):
    return 1.0 / (1.0 + jnp.exp(-x))


def _dot(a, b):
    return jnp.dot(a, b, preferred_element_type=F32)


def _dot_nt(a, b):
    return lax.dot_general(a, b, (((1,), (1,)), ((), ())), preferred_element_type=F32)


def _dot_tn(a, b):
    return lax.dot_general(a, b, (((0,), (0,)), ((), ())), preferred_element_type=F32)


def rms_fwd(x, gain):
    S, D = x.shape
    tm = _tile(S, 512)

    def body(x_ref, g_ref, h_ref):
        xv = x_ref[...]
        r = lax.rsqrt(jnp.mean(xv * xv, axis=-1, keepdims=True) + RMS_EPS)
        h_ref[...] = (xv * r * g_ref[...]).astype(BF16)

    return pl.pallas_call(
        body, name="rms_fwd", grid=(S // tm,),
        in_specs=[pl.BlockSpec((tm, D), lambda i: (i, 0)), pl.BlockSpec((1, D), lambda i: (0, 0))],
        out_specs=pl.BlockSpec((tm, D), lambda i: (i, 0)),
        out_shape=jax.ShapeDtypeStruct((S, D), BF16), compiler_params=_cparams("arbitrary"))(x, gain)


def rms_bwd(dh, x, gain, resid):
    S, D = x.shape
    tm = _tile(S, 512)

    def body(dh_ref, x_ref, g_ref, r_ref, dx_ref, dg_ref):
        xv, dhv = x_ref[...], dh_ref[...]
        r = lax.rsqrt(jnp.mean(xv * xv, axis=-1, keepdims=True) + RMS_EPS)
        xhat = xv * r
        dhg = dhv * g_ref[...]
        dx_ref[...] = r_ref[...] + r * (dhg - xhat * jnp.mean(dhg * xhat, axis=-1, keepdims=True))

        @pl.when(pl.program_id(0) == 0)
        def _():
            dg_ref[...] = jnp.zeros_like(dg_ref)

        dg_ref[...] += jnp.sum(dhv * xhat, axis=0, keepdims=True)

    row = pl.BlockSpec((tm, D), lambda i: (i, 0))
    vec = pl.BlockSpec((1, D), lambda i: (0, 0))
    return pl.pallas_call(
        body, name="rms_bwd", grid=(S // tm,), in_specs=[row, row, vec, row], out_specs=[row, vec],
        out_shape=[jax.ShapeDtypeStruct((S, D), F32), jax.ShapeDtypeStruct((1, D), F32)],
        compiler_params=_cparams("arbitrary"))(dh, x, gain, resid)


def loss_head(x, gain, target):
    S, D = x.shape
    tm = _tile(S, 512)

    def body(x_ref, g_ref, t_ref, loss_ref, dx_ref, dg_ref):
        xv, g = x_ref[...], g_ref[...]
        r = lax.rsqrt(jnp.mean(xv * xv, axis=-1, keepdims=True) + RMS_EPS)
        xhat = xv * r
        err = xhat * g - t_ref[...]
        dy = err * (1.0 / D)
        dyg = dy * g
        dx_ref[...] = r * (dyg - xhat * jnp.mean(dyg * xhat, axis=-1, keepdims=True))

        @pl.when(pl.program_id(0) == 0)
        def _():
            dg_ref[...] = jnp.zeros_like(dg_ref)
            loss_ref[...] = jnp.zeros_like(loss_ref)

        dg_ref[...] += jnp.sum(dy * xhat, axis=0, keepdims=True)
        loss_ref[...] += 0.5 * jnp.sum(jnp.mean(err * err, axis=-1, keepdims=True), axis=0, keepdims=True)

    row = pl.BlockSpec((tm, D), lambda i: (i, 0))
    vec = pl.BlockSpec((1, D), lambda i: (0, 0))
    return pl.pallas_call(
        body, name="loss_head", grid=(S // tm,), in_specs=[row, vec, row],
        out_specs=[pl.BlockSpec((1, 128), lambda i: (0, 0)), row, vec],
        out_shape=[jax.ShapeDtypeStruct((1, 128), F32), jax.ShapeDtypeStruct((S, D), F32),
                   jax.ShapeDtypeStruct((1, D), F32)],
        compiler_params=_cparams("arbitrary"))(x, gain, target)


def gate_up(h, wg, wu, l):
    S, D = h.shape
    Fs = wg.shape[3]
    tm = _tile(S, 512)

    def body(h_ref, wg_ref, wu_ref, g_ref, u_ref, a_ref):
        hv = h_ref[...]
        g = _dot(hv, wg_ref[...])
        u = _dot(hv, wu_ref[...])
        g_ref[...] = g.astype(BF16)
        u_ref[...] = u.astype(BF16)
        a_ref[...] = (g * _sigmoid(g) * u).astype(BF16)

    w_spec = pl.BlockSpec((None, None, D, Fs), lambda j, i: (l, j, 0, 0))
    o_spec = pl.BlockSpec((tm, Fs), lambda j, i: (i, j))
    o_shape = jax.ShapeDtypeStruct((S, N_CHIPS * Fs), BF16)
    return pl.pallas_call(
        body, name="gate_up", grid=(N_CHIPS, S // tm),
        in_specs=[pl.BlockSpec((tm, D), lambda j, i: (i, 0)), w_spec, w_spec],
        out_specs=[o_spec, o_spec, o_spec], out_shape=[o_shape, o_shape, o_shape],
        compiler_params=_cparams("arbitrary", "arbitrary"))(h, wg, wu)


def down_proj(a, wd, l, x):
    S, F = a.shape
    D = wd.shape[2]
    tm, tk = _tile(S, 512), F // N_CHIPS
    nk = F // tk

    def body(a_ref, w_ref, x_ref, o_ref, acc_ref):
        k = pl.program_id(1)

        @pl.when(k == 0)
        def _():
            acc_ref[...] = jnp.zeros_like(acc_ref)

        acc_ref[...] += _dot(a_ref[...], w_ref[...])

        @pl.when(k == nk - 1)
        def _():
            o_ref[...] = x_ref[...] + 0.5 * acc_ref[...]

    return pl.pallas_call(
        body, name="down_proj", grid=(S // tm, nk),
        in_specs=[pl.BlockSpec((tm, tk), lambda i, k: (i, k)), pl.BlockSpec((None, tk, D), lambda i, k: (l, k, 0)),
                  pl.BlockSpec((tm, D), lambda i, k: (i, 0))],
        out_specs=pl.BlockSpec((tm, D), lambda i, k: (i, 0)), out_shape=jax.ShapeDtypeStruct((S, D), F32),
        scratch_shapes=[pltpu.VMEM((tm, D), F32)],
        compiler_params=_cparams("arbitrary", "arbitrary"))(a, wd, x)


def down_bwd(dy, wd, l, g, u):
    S, D = dy.shape
    F = g.shape[1]
    Fs = F // N_CHIPS
    tm = _tile(S, 512)

    def body(dy_ref, w_ref, g_ref, u_ref, dg_ref, du_ref):
        da = 0.5 * _dot_nt(dy_ref[...].astype(BF16), w_ref[...])
        gv, uv = g_ref[...].astype(F32), u_ref[...].astype(F32)
        sig = _sigmoid(gv)
        du_ref[...] = (da * (gv * sig)).astype(BF16)
        dg_ref[...] = (da * uv * (sig * (1.0 + gv * (1.0 - sig)))).astype(BF16)

    t_spec = pl.BlockSpec((tm, Fs), lambda j, i: (i, j))
    o_shape = jax.ShapeDtypeStruct((S, F), BF16)
    return pl.pallas_call(
        body, name="down_bwd", grid=(N_CHIPS, S // tm),
        in_specs=[pl.BlockSpec((tm, D), lambda j, i: (i, 0)), pl.BlockSpec((None, Fs, D), lambda j, i: (l, j, 0)),
                  t_spec, t_spec],
        out_specs=[t_spec, t_spec], out_shape=[o_shape, o_shape],
        compiler_params=_cparams("arbitrary", "arbitrary"))(dy, wd, g, u)


def gate_up_bwd(dg, du, wg, wu, l):
    S, F = dg.shape
    D, Fs = wg.shape[2], wg.shape[3]
    tm = _tile(S, 512)

    def body(dg_ref, du_ref, wg_ref, wu_ref, o_ref):
        k = pl.program_id(1)

        @pl.when(k == 0)
        def _():
            o_ref[...] = jnp.zeros_like(o_ref)

        o_ref[...] += _dot_nt(dg_ref[...], wg_ref[...]) + _dot_nt(du_ref[...], wu_ref[...])

    t_spec = pl.BlockSpec((tm, Fs), lambda i, k: (i, k))
    w_spec = pl.BlockSpec((None, None, D, Fs), lambda i, k: (l, k, 0, 0))
    return pl.pallas_call(
        body, name="gate_up_bwd", grid=(S // tm, N_CHIPS), in_specs=[t_spec, t_spec, w_spec, w_spec],
        out_specs=pl.BlockSpec((tm, D), lambda i, k: (i, 0)), out_shape=jax.ShapeDtypeStruct((S, D), F32),
        compiler_params=_cparams("arbitrary", "arbitrary"))(dg, du, wg, wu)


def mm_nn(a, b, l, tn, out_dtype, resid=None):
    S, K = a.shape
    N = b.shape[2]
    tm = _tile(S, 512)

    def body(a_ref, b_ref, *rest):
        o_ref = rest[-1]
        acc = _dot(a_ref[...].astype(BF16), b_ref[...])
        if resid is not None:
            acc = rest[0][...] + acc
        o_ref[...] = acc.astype(out_dtype)

    o_spec = pl.BlockSpec((tm, tn), lambda n, i: (i, n))
    in_specs = [pl.BlockSpec((tm, K), lambda n, i: (i, 0)), pl.BlockSpec((None, K, tn), lambda n, i: (l, 0, n))]
    args = [a, b]
    if resid is not None:
        in_specs.append(o_spec)
        args.append(resid)
    return pl.pallas_call(
        body, name="mm_nn", grid=(N // tn, S // tm), in_specs=in_specs, out_specs=o_spec,
        out_shape=jax.ShapeDtypeStruct((S, N), out_dtype),
        compiler_params=_cparams("arbitrary", "arbitrary"))(*args)


def mm_nt(a, b, l, tn, out_dtype):
    S, K = a.shape
    N = b.shape[1]
    tm = _tile(S, 512)

    def body(a_ref, b_ref, o_ref):
        o_ref[...] = _dot_nt(a_ref[...].astype(BF16), b_ref[...]).astype(out_dtype)

    return pl.pallas_call(
        body, name="mm_nt", grid=(N // tn, S // tm),
        in_specs=[pl.BlockSpec((tm, K), lambda n, i: (i, 0)), pl.BlockSpec((None, tn, K), lambda n, i: (l, n, 0))],
        out_specs=pl.BlockSpec((tm, tn), lambda n, i: (i, n)), out_shape=jax.ShapeDtypeStruct((S, N), out_dtype),
        compiler_params=_cparams("arbitrary", "arbitrary"))(a, b)


def mm_tn(a, b, tka, tnb, scale=1.0, blocked=False):
    S, Ka = a.shape
    Nb = b.shape[1]
    ts = _tile(S, 1024)
    ns = S // ts

    def body(a_ref, b_ref, o_ref, acc_ref):
        s = pl.program_id(2)

        @pl.when(s == 0)
        def _():
            acc_ref[...] = jnp.zeros_like(acc_ref)

        acc_ref[...] += _dot_tn(a_ref[...].astype(BF16), b_ref[...].astype(BF16))

        @pl.when(s == ns - 1)
        def _():
            o_ref[...] = (scale * acc_ref[...]).astype(BF16)

    if blocked:
        o_spec = pl.BlockSpec((None, tka, tnb), lambda ka, nb, s: (nb, ka, 0))
        o_shape = jax.ShapeDtypeStruct((Nb // tnb, Ka, tnb), BF16)
    else:
        o_spec = pl.BlockSpec((tka, tnb), lambda ka, nb, s: (ka, nb))
        o_shape = jax.ShapeDtypeStruct((Ka, Nb), BF16)
    return pl.pallas_call(
        body, name="mm_tn", grid=(Ka // tka, Nb // tnb, ns),
        in_specs=[pl.BlockSpec((ts, tka), lambda ka, nb, s: (s, ka)),
                  pl.BlockSpec((ts, tnb), lambda ka, nb, s: (s, nb))],
        out_specs=o_spec, out_shape=o_shape, scratch_shapes=[pltpu.VMEM((tka, tnb), F32)],
        compiler_params=_cparams("arbitrary", "arbitrary", "arbitrary"))(a, b)


def _rope(x, c2, s2):
    half = x.shape[-1] // 2
    rot = jnp.concatenate([x[:, half:], x[:, :half]], axis=-1)
    return x * c2 + rot * s2


def _tri(tm, upper):
    r = lax.broadcasted_iota(jnp.int32, (tm, tm), 0)
    c = lax.broadcasted_iota(jnp.int32, (tm, tm), 1)
    return jnp.where((c >= r) if upper else (c <= r), 1.0, 0.0).astype(F32)


def _log_sigmoid(x):
    return jnp.minimum(x, 0.0) - jnp.log(1.0 + jnp.exp(-jnp.abs(x)))


def _norm_hat(c):
    r = lax.rsqrt(jnp.mean(c * c, axis=-1, keepdims=True) + RMS_EPS)
    return c * r, r


def _tok_spec(tm, width, rev_n=None):
    if rev_n is None:
        return pl.BlockSpec((tm, width), lambda i: (i, 0))
    return pl.BlockSpec((tm, width), lambda i: (rev_n - 1 - i, 0))


def _head_spec(heads, tm, width, rev_n=None):
    if rev_n is None:
        return pl.BlockSpec((heads, tm, width), lambda i: (0, i, 0))
    return pl.BlockSpec((heads, tm, width), lambda i: (0, rev_n - 1 - i, 0))


def _lane_spec(heads, width, tm, rev_n=None):
    if rev_n is None:
        return pl.BlockSpec((heads, width, tm), lambda i: (0, 0, i))
    return pl.BlockSpec((heads, width, tm), lambda i: (0, 0, rev_n - 1 - i))


def _full_spec(shape):
    return pl.BlockSpec(shape, lambda i: (0,) * len(shape))


def mixer_prep(p, c2, s2, q_norm, kv_norm, bias, wqb, wkvb, l):
    S = p.shape[0]
    tm = _tile(S, 256)
    H = N_HEADS

    def body(p_ref, c2_ref, s2_ref, qn_ref, kvn_ref, b_ref, wqb_ref, wkvb_ref,
             cqn_ref, ckvn_ref, qmt_ref, km_ref, kmt_ref, vm_ref, vmt_ref, qst_ref, ks_ref, kst_ref, vs_ref, vst_ref,
             qft_ref, kf_ref, kft_ref, vf_ref, vft_ref, ccol_ref, carry_row):
        c2, s2 = c2_ref[...], s2_ref[...]
        cqn = (_norm_hat(p_ref[:, P_CQ:P_CQ + Q_LORA])[0] * qn_ref[...]).astype(BF16)
        ckvn = (_norm_hat(p_ref[:, P_CKV:P_CKV + KV_LORA])[0] * kvn_ref[...]).astype(BF16)
        cqn_ref[...] = cqn
        ckvn_ref[...] = ckvn
        q = _dot(cqn, wqb_ref[...])
        kv = _dot(ckvn, wkvb_ref[...])
        k_pe = _rope(p_ref[:, P_KR:P_KR + ROPE], c2, s2)
        k_pe_t = k_pe.T.astype(BF16)
        k_pe = k_pe.astype(BF16)

        def both_ways(x, tok_ref, lane_ref, h):
            tok_ref[h] = x.astype(BF16)
            lane_ref[h] = x.T.astype(BF16)

        for h in range(H):
            qmt_ref[h, 0:NOPE, :] = q[:, h * QK_MLA:h * QK_MLA + NOPE].T.astype(BF16)
            qmt_ref[h, NOPE:QK_MLA, :] = _rope(q[:, h * QK_MLA + NOPE:(h + 1) * QK_MLA], c2, s2).T.astype(BF16)
            k_nope = kv[:, h * 256:h * 256 + NOPE]
            km_ref[h, :, 0:NOPE] = k_nope.astype(BF16)
            km_ref[h, :, NOPE:QK_MLA] = k_pe
            kmt_ref[h, 0:NOPE, :] = k_nope.T.astype(BF16)
            kmt_ref[h, NOPE:QK_MLA, :] = k_pe_t
            both_ways(kv[:, h * 256 + NOPE:(h + 1) * 256], vm_ref, vmt_ref, h)
            qst_ref[h] = _rope(p_ref[:, P_QS + h * HD:P_QS + (h + 1) * HD], c2, s2).T.astype(BF16)
            qft_ref[h] = p_ref[:, P_QF + h * HD:P_QF + (h + 1) * HD].T.astype(BF16)
            both_ways(p_ref[:, P_KF + h * HD:P_KF + (h + 1) * HD], kf_ref, kft_ref, h)
            both_ways(p_ref[:, P_VF + h * HD:P_VF + (h + 1) * HD], vf_ref, vft_ref, h)
        for h in range(SWA_KV):
            both_ways(_rope(p_ref[:, P_KS + h * HD:P_KS + (h + 1) * HD], c2, s2), ks_ref, kst_ref, h)
            both_ways(p_ref[:, P_VS + h * HD:P_VS + (h + 1) * HD], vs_ref, vst_ref, h)

        @pl.when(pl.program_id(0) == 0)
        def _():
            carry_row[...] = jnp.zeros_like(carry_row)

        log_f = _log_sigmoid(p_ref[:, P_F:P_F + H] + b_ref[...])
        c_tok = jnp.dot(_tri(tm, upper=False), log_f, preferred_element_type=F32,
                        precision=lax.Precision.HIGHEST) + carry_row[0:1, 0:H]
        for h in range(H):
            ccol_ref[h] = c_tok[:, h:h + 1]
        carry_row[0:1, 0:H] = c_tok[tm - 1:tm, :]

    out_shape = [jax.ShapeDtypeStruct((S, Q_LORA), BF16), jax.ShapeDtypeStruct((S, KV_LORA), BF16)]
    out_specs = [_tok_spec(tm, Q_LORA), _tok_spec(tm, KV_LORA)]

    def add(heads, d, lanes):
        out_shape.append(jax.ShapeDtypeStruct((heads, d, S) if lanes else (heads, S, d), BF16))
        out_specs.append(_lane_spec(heads, d, tm) if lanes else _head_spec(heads, tm, d))

    for heads_q, heads_kv, dqk, dv in ((H, H, QK_MLA, VDIM), (H, SWA_KV, HD, HD), (H, H, HD, HD)):
        add(heads_q, dqk, True)
        add(heads_kv, dqk, False)
        add(heads_kv, dqk, True)
        add(heads_kv, dv, False)
        add(heads_kv, dv, True)
    out_shape.append(jax.ShapeDtypeStruct((H, S, 1), F32))
    out_specs.append(_head_spec(H, tm, 1))
    in_specs = [_tok_spec(tm, P_COLS), _tok_spec(tm, ROPE), _tok_spec(tm, ROPE), _full_spec((1, Q_LORA)),
                _full_spec((1, KV_LORA)), _full_spec((1, H)),
                pl.BlockSpec((None,) + wqb.shape[1:], lambda i: (l, 0, 0)),
                pl.BlockSpec((None,) + wkvb.shape[1:], lambda i: (l, 0, 0))]
    return pl.pallas_call(
        body, name="mixer_prep", grid=(S // tm,), in_specs=in_specs, out_specs=out_specs, out_shape=out_shape,
        scratch_shapes=[pltpu.VMEM((8, 128), F32)],
        compiler_params=_cparams("arbitrary"))(p, c2, s2, q_norm, kv_norm, bias, wqb, wkvb)


def mixer_prep_bwd(p, c2, s2, q_norm, kv_norm, bias, wqb, wkvb, l, dqm, dkm, dvm, dqs, dks, dvs, dqf, dkf, dvf, dc):
    S = p.shape[0]
    tm = _tile(S, 256)
    nt = S // tm
    H, G = N_HEADS, N_HEADS // SWA_KV

    def body(p_ref, c2_ref, s2_ref, qn_ref, kvn_ref, b_ref, wqb_ref, wkvb_ref,
             dqm_ref, dkm_ref, dvm_ref, dqs_ref, dks_ref, dvs_ref, dqf_ref, dkf_ref, dvf_ref, dc_ref,
             dp_ref, dq_ref, dkv_ref, dqn_ref, dkvn_ref, db_ref, carry):
        c2, s2 = c2_ref[...], -s2_ref[...]

        @pl.when(pl.program_id(0) == 0)
        def _():
            dqn_ref[...] = jnp.zeros_like(dqn_ref)
            dkvn_ref[...] = jnp.zeros_like(dkvn_ref)
            db_ref[...] = jnp.zeros_like(db_ref)
            carry[...] = jnp.zeros_like(carry)

        dk_pe_t = jnp.zeros((ROPE, tm), F32)
        for h in range(H):
            dq_ref[:, h * QK_MLA:h * QK_MLA + NOPE] = dqm_ref[h, 0:NOPE, :].T.astype(BF16)
            dq_ref[:, h * QK_MLA + NOPE:(h + 1) * QK_MLA] = _rope(dqm_ref[h, NOPE:QK_MLA, :].T, c2, s2).astype(BF16)
            dkv_ref[:, h * 256:h * 256 + NOPE] = dkm_ref[h, 0:NOPE, :].T.astype(BF16)
            dkv_ref[:, h * 256 + NOPE:(h + 1) * 256] = dvm_ref[h].T.astype(BF16)
            dk_pe_t = dk_pe_t + dkm_ref[h, NOPE:QK_MLA, :]
        dk_pe = dk_pe_t.T

        def through_norm(dcn, c, gain, dgain_ref):
            c_hat, r = _norm_hat(c)
            dhg = dcn * gain
            dgain_ref[...] += jnp.sum(dcn * c_hat, axis=0, keepdims=True)
            return r * (dhg - c_hat * jnp.mean(dhg * c_hat, axis=-1, keepdims=True))

        dcqn = _dot_nt(dq_ref[...], wqb_ref[...])
        dckvn = _dot_nt(dkv_ref[...], wkvb_ref[...])
        dp_ref[:, P_CQ:P_CQ + Q_LORA] = through_norm(dcqn, p_ref[:, P_CQ:P_CQ + Q_LORA], qn_ref[...], dqn_ref).astype(BF16)
        dp_ref[:, P_CKV:P_CKV + KV_LORA] = through_norm(
            dckvn, p_ref[:, P_CKV:P_CKV + KV_LORA], kvn_ref[...], dkvn_ref).astype(BF16)
        for h in range(H):
            dp_ref[:, P_QS + h * HD:P_QS + (h + 1) * HD] = _rope(dqs_ref[h].T, c2, s2).astype(BF16)
            dp_ref[:, P_QF + h * HD:P_QF + (h + 1) * HD] = dqf_ref[h].T.astype(BF16)
            dp_ref[:, P_KF + h * HD:P_KF + (h + 1) * HD] = dkf_ref[h].T.astype(BF16)
            dp_ref[:, P_VF + h * HD:P_VF + (h + 1) * HD] = dvf_ref[h].T.astype(BF16)
        for kvh in range(SWA_KV):
            dk = dks_ref[kvh * G]
            dv = dvs_ref[kvh * G]
            for g in range(1, G):
                dk = dk + dks_ref[kvh * G + g]
                dv = dv + dvs_ref[kvh * G + g]
            dp_ref[:, P_KS + kvh * HD:P_KS + (kvh + 1) * HD] = _rope(dk.T, c2, s2).astype(BF16)
            dp_ref[:, P_VS + kvh * HD:P_VS + (kvh + 1) * HD] = dv.T.astype(BF16)

        dcv = dc_ref[...]
        dlog_f = jnp.dot(_tri(tm, upper=True), dcv, preferred_element_type=F32,
                         precision=lax.Precision.HIGHEST) + carry[0:1, 0:H]
        carry[0:1, 0:H] = dlog_f[0:1, :]
        df = dlog_f * _sigmoid(-(p_ref[:, P_F:P_F + H] + b_ref[...]))
        db_ref[...] += jnp.sum(df, axis=0, keepdims=True)
        dp_ref[:, P_KR:P_COLS] = jnp.zeros((tm, P_COLS - P_KR), BF16)
        dp_ref[:, P_KR:P_KR + ROPE] = _rope(dk_pe, c2, s2).astype(BF16)
        dp_ref[:, P_F:P_F + H] = df.astype(BF16)

    rev = nt
    in_specs = [_tok_spec(tm, P_COLS, rev), _tok_spec(tm, ROPE, rev), _tok_spec(tm, ROPE, rev), _full_spec((1, Q_LORA)),
                _full_spec((1, KV_LORA)), _full_spec((1, H)),
                pl.BlockSpec((None,) + wqb.shape[1:], lambda i: (l, 0, 0)),
                pl.BlockSpec((None,) + wkvb.shape[1:], lambda i: (l, 0, 0)),
                _lane_spec(H, QK_MLA, tm, rev), _lane_spec(H, QK_MLA, tm, rev), _lane_spec(H, VDIM, tm, rev)]
    in_specs += [_lane_spec(H, HD, tm, rev)] * 6 + [_tok_spec(tm, H, rev)]
    out_specs = [_tok_spec(tm, P_COLS, rev), _tok_spec(tm, N_HEADS * QK_MLA, rev), _tok_spec(tm, N_HEADS * 256, rev),
                 _full_spec((1, Q_LORA)), _full_spec((1, KV_LORA)), _full_spec((1, H))]
    out_shape = [jax.ShapeDtypeStruct((S, P_COLS), BF16), jax.ShapeDtypeStruct((S, N_HEADS * QK_MLA), BF16),
                 jax.ShapeDtypeStruct((S, N_HEADS * 256), BF16), jax.ShapeDtypeStruct((1, Q_LORA), F32),
                 jax.ShapeDtypeStruct((1, KV_LORA), F32), jax.ShapeDtypeStruct((1, H), F32)]
    return pl.pallas_call(
        body, name="mixer_prep_bwd", grid=(nt,), in_specs=in_specs, out_specs=out_specs, out_shape=out_shape,
        scratch_shapes=[pltpu.VMEM((8, 128), F32)], compiler_params=_cparams("arbitrary"))(
            p, c2, s2, q_norm, kv_norm, bias, wqb, wkvb, dqm, dkm, dvm, dqs, dks, dvs, dqf, dkf, dvf, dc)


def merge_heads(o_mla, o_swa, o_fox):
    H, S, _ = o_mla.shape
    tm = _tile(S, 512)
    width = H * (VDIM + 2 * HD)

    def body(om_ref, os_ref, of_ref, m_ref):
        for h in range(H):
            m_ref[:, h * VDIM:(h + 1) * VDIM] = om_ref[h]
            m_ref[:, H * VDIM + h * HD:H * VDIM + (h + 1) * HD] = os_ref[h]
            m_ref[:, H * (VDIM + HD) + h * HD:H * (VDIM + HD) + (h + 1) * HD] = of_ref[h]

    return pl.pallas_call(
        body, name="merge_heads", grid=(S // tm,),
        in_specs=[_head_spec(H, tm, VDIM), _head_spec(H, tm, HD), _head_spec(H, tm, HD)],
        out_specs=_tok_spec(tm, width), out_shape=jax.ShapeDtypeStruct((S, width), BF16),
        compiler_params=_cparams("arbitrary"))(o_mla, o_swa, o_fox)


def split_heads(dmixed, mixed):
    S, width = mixed.shape
    H = N_HEADS
    tm = _tile(S, 512)

    def body(dm_ref, m_ref, dom_ref, dos_ref, dof_ref, dm_delta, ds_delta, df_delta):
        def one(h, off, d, do_ref, delta_ref):
            dv = dm_ref[:, off:off + d].astype(F32)
            do_ref[h] = dv.T.astype(BF16)
            prod = dv * m_ref[:, off:off + d].astype(F32)
            rows = lax.dot_general(jnp.ones((8, d), F32), prod, (((1,), (1,)), ((), ())),
                                   preferred_element_type=F32, precision=lax.Precision.HIGHEST)
            delta_ref[h] = rows[0:1, :]

        for h in range(H):
            one(h, h * VDIM, VDIM, dom_ref, dm_delta)
            one(h, H * VDIM + h * HD, HD, dos_ref, ds_delta)
            one(h, H * (VDIM + HD) + h * HD, HD, dof_ref, df_delta)

    row_spec = pl.BlockSpec((H, 1, tm), lambda i: (0, 0, i))
    row_shape = jax.ShapeDtypeStruct((H, 1, S), F32)
    return pl.pallas_call(
        body, name="split_heads", grid=(S // tm,),
        in_specs=[_tok_spec(tm, width), _tok_spec(tm, width)],
        out_specs=[_lane_spec(H, VDIM, tm), _lane_spec(H, HD, tm), _lane_spec(H, HD, tm), row_spec, row_spec, row_spec],
        out_shape=[jax.ShapeDtypeStruct((H, VDIM, S), BF16), jax.ShapeDtypeStruct((H, HD, S), BF16),
                   jax.ShapeDtypeStruct((H, HD, S), BF16), row_shape, row_shape, row_shape],
        compiler_params=_cparams("arbitrary"))(dmixed, mixed)


def _attn_tile(S):
    return 512 if (S % 512 == 0 and S > 512) else S // 2


def _valid(q0, k0, shape, q_axis, window):
    qpos = q0 + lax.broadcasted_iota(jnp.int32, shape, q_axis)
    kpos = k0 + lax.broadcasted_iota(jnp.int32, shape, 1 - q_axis)
    ok = kpos <= qpos
    if window is not None:
        ok = jnp.logical_and(ok, kpos > qpos - window)
    return ok


def attn_fwd(name, qt, k, vt, scale, window=None, sinks=None, ccol=None):
    H, dq, S = qt.shape
    Hk, dv, _ = vt.shape
    G = H // Hk
    t = _attn_tile(S)
    fox, use_sink = ccol is not None, sinks is not None

    def body(*refs):
        refs = list(refs)
        sink_ref = refs.pop(0) if use_sink else None
        q_ref, k_ref, vt_ref = refs[:3]
        refs = refs[3:]
        ccol_ref = refs.pop(0) if fox else None
        o_ref, lse_ref, m_ref, l_ref, acc_ref = refs
        h, i = pl.program_id(0), pl.program_id(1)
        qv = q_ref[...]
        if use_sink:
            m_ref[...] = jnp.full(m_ref.shape, sink_ref[h], F32)
            l_ref[...] = jnp.ones(l_ref.shape, F32)
        else:
            m_ref[...] = jnp.full(m_ref.shape, NEG, F32)
            l_ref[...] = jnp.zeros(l_ref.shape, F32)
        acc_ref[...] = jnp.zeros(acc_ref.shape, F32)

        def step(j, masked):
            off = pl.multiple_of(j * t, t)
            st = _dot(k_ref[pl.ds(off, t), :], qv) * scale
            if fox:
                st = st - ccol_ref[pl.ds(off, t), :]
            if masked:
                st = jnp.where(_valid(i * t, j * t, (t, t), 1, window), st, NEG)
            m_prev = m_ref[...]
            m_new = jnp.maximum(m_prev, jnp.max(st, axis=0, keepdims=True))
            alpha = jnp.exp(m_prev - m_new)
            pt = jnp.exp(st - m_new)
            l_ref[...] = alpha * l_ref[...] + jnp.sum(pt, axis=0, keepdims=True)
            acc_ref[...] = alpha * acc_ref[...] + _dot(vt_ref[:, pl.ds(off, t)], pt.astype(BF16))
            m_ref[...] = m_new

        def loop(lo, hi, masked):
            def it(j, carry):
                step(j, masked)
                return carry
            lax.fori_loop(lo, hi, it, 0)

        if window is None:
            loop(0, i, False)
            step(i, True)
        else:
            loop(jnp.maximum(i * t - (window - 1), 0) // t, i + 1, True)
        l = l_ref[...]
        o_ref[...] = (acc_ref[...] / l).T.astype(BF16)
        lse_ref[...] = m_ref[...] + jnp.log(l)

    in_specs, args = [], []
    if use_sink:
        in_specs.append(pl.BlockSpec(memory_space=pltpu.SMEM))
        args.append(sinks)
    in_specs += [pl.BlockSpec((None, dq, t), lambda h, i: (h, 0, i)),
                 pl.BlockSpec((None, S, dq), lambda h, i: (h // G, 0, 0)),
                 pl.BlockSpec((None, dv, S), lambda h, i: (h // G, 0, 0))]
    args += [qt, k, vt]
    if fox:
        in_specs.append(pl.BlockSpec((None, S, 1), lambda h, i: (h, 0, 0)))
        args.append(ccol)
    return pl.pallas_call(
        body, name=name, grid=(H, S // t), in_specs=in_specs,
        out_specs=[pl.BlockSpec((None, t, dv), lambda h, i: (h, i, 0)), pl.BlockSpec((None, 1, t), lambda h, i: (h, 0, i))],
        out_shape=[jax.ShapeDtypeStruct((H, S, dv), BF16), jax.ShapeDtypeStruct((H, 1, S), F32)],
        scratch_shapes=[pltpu.VMEM((1, t), F32), pltpu.VMEM((1, t), F32), pltpu.VMEM((dv, t), F32)],
        compiler_params=_cparams("arbitrary", "arbitrary"))(*args)


def attn_bwd(name, qt, k, kt, v, dot, lse, delta, scale, window=None, sinks=None, ccol=None):
    H, dq, S = qt.shape
    Hk, _, dv = v.shape
    G = H // Hk
    t = _attn_tile(S)
    nq = S // t
    fox, use_sink = ccol is not None, sinks is not None

    def body(*refs):
        refs = list(refs)
        sink_ref = refs.pop(0) if use_sink else None
        qt_ref, k_ref, kt_ref, v_ref, dot_ref, lse_ref, delta_ref = refs[:7]
        refs = refs[7:]
        ccol_ref = refs.pop(0) if fox else None
        dqt_ref, dkt_ref, dvt_ref = refs[:3]
        refs = refs[3:]
        dcq_ref, dck_ref = (refs.pop(0), refs.pop(0)) if fox else (None, None)
        dsink_ref = refs.pop(0) if use_sink else None
        h, j = pl.program_id(0), pl.program_id(1)

        @pl.when(j == 0)
        def _():
            dqt_ref[...] = jnp.zeros(dqt_ref.shape, F32)
            if fox:
                dcq_ref[...] = jnp.zeros(dcq_ref.shape, F32)
            if use_sink:
                ps = jnp.exp(sink_ref[h] - lse_ref[...]) * delta_ref[...]
                dsink_ref[...] = jnp.broadcast_to(-jnp.sum(ps, axis=-1, keepdims=True), dsink_ref.shape)

        dkt_ref[...] = jnp.zeros(dkt_ref.shape, F32)
        dvt_ref[...] = jnp.zeros(dvt_ref.shape, F32)
        if fox:
            dck_ref[...] = jnp.zeros(dck_ref.shape, F32)
        kv, ktv, vv = k_ref[...], kt_ref[...], v_ref[...]

        def step(i, masked):
            off = pl.multiple_of(i * t, t)
            qti, doti = qt_ref[:, pl.ds(off, t)], dot_ref[:, pl.ds(off, t)]
            st = _dot(kv, qti) * scale
            if fox:
                st = st - ccol_ref[...]
            if masked:
                st = jnp.where(_valid(i * t, j * t, (t, t), 1, window), st, NEG)
            pt = jnp.exp(st - lse_ref[:, pl.ds(off, t)])
            dvt_ref[...] += _dot_nt(doti, pt.astype(BF16))
            dst = pt * (_dot(vv, doti) - delta_ref[:, pl.ds(off, t)])
            if fox:
                dcq_ref[:, pl.ds(off, t)] += jnp.sum(dst, axis=0, keepdims=True)
                dck_ref[...] -= jnp.sum(dst, axis=1, keepdims=True)
            dsb = (dst * scale).astype(BF16)
            dkt_ref[...] += _dot_nt(qti, dsb)
            dqt_ref[:, pl.ds(off, t)] += _dot(ktv, dsb)

        def loop(lo, hi, masked):
            def it(i, carry):
                step(i, masked)
                return carry
            lax.fori_loop(lo, hi, it, 0)

        if window is None:
            step(j, True)
            loop(j + 1, nq, False)
        else:
            loop(j, jnp.minimum((j * t + t - 1 + window - 1) // t, nq - 1) + 1, True)

    in_specs, args = [], []
    if use_sink:
        in_specs.append(pl.BlockSpec(memory_space=pltpu.SMEM))
        args.append(sinks)
    whole = lambda d: pl.BlockSpec((None, d, S), lambda h, j: (h, 0, 0))
    keys = lambda d: pl.BlockSpec((None, d, t), lambda h, j: (h, 0, j))
    row = pl.BlockSpec((None, 1, S), lambda h, j: (h, 0, 0))
    in_specs += [whole(dq), pl.BlockSpec((None, t, dq), lambda h, j: (h // G, j, 0)),
                 pl.BlockSpec((None, dq, t), lambda h, j: (h // G, 0, j)),
                 pl.BlockSpec((None, t, dv), lambda h, j: (h // G, j, 0)), whole(dv), row, row]
    args += [qt, k, kt, v, dot, lse, delta]
    out_specs = [whole(dq), keys(dq), keys(dv)]
    out_shape = [jax.ShapeDtypeStruct((H, dq, S), F32), jax.ShapeDtypeStruct((H, dq, S), F32),
                 jax.ShapeDtypeStruct((H, dv, S), F32)]
    if fox:
        in_specs.append(pl.BlockSpec((None, t, 1), lambda h, j: (h, j, 0)))
        args.append(ccol)
        out_specs += [row, pl.BlockSpec((None, t, 1), lambda h, j: (h, j, 0))]
        out_shape += [jax.ShapeDtypeStruct((H, 1, S), F32), jax.ShapeDtypeStruct((H, S, 1), F32)]
    if use_sink:
        out_specs.append(pl.BlockSpec((None, 1, 128), lambda h, j: (h, 0, 0)))
        out_shape.append(jax.ShapeDtypeStruct((H, 1, 128), F32))
    return pl.pallas_call(
        body, name=name, grid=(H, nq), in_specs=in_specs, out_specs=out_specs, out_shape=out_shape,
        compiler_params=_cparams("arbitrary", "arbitrary"))(*args)


def _rows_tile(rows):
    return 256 if rows % 256 == 0 else rows


def adamw(w, g, m, v):
    R, C = w.shape
    tr = _rows_tile(R)

    def body(w_ref, g_ref, m_ref, v_ref, g_out, d_ref, nm_ref, nv_ref):
        gv = g_ref[...]
        mn = ADAM_B1 * m_ref[...] + (1.0 - ADAM_B1) * gv
        vn = ADAM_B2 * v_ref[...] + (1.0 - ADAM_B2) * (gv * gv)
        m_hat = mn / (1.0 - ADAM_B1 ** ADAM_STEP)
        v_hat = vn / (1.0 - ADAM_B2 ** ADAM_STEP)
        d_ref[...] = -ADAM_LR * (m_hat / (jnp.sqrt(v_hat) + ADAM_EPS) + ADAM_WD * w_ref[...])
        nm_ref[...] = mn
        nv_ref[...] = vn
        g_out[...] = gv

    spec = pl.BlockSpec((tr, C), lambda i: (i, 0))
    shape = jax.ShapeDtypeStruct((R, C), F32)
    return pl.pallas_call(
        body, name="adamw", grid=(R // tr,), in_specs=[spec] * 4, out_specs=[spec] * 4, out_shape=[shape] * 4,
        compiler_params=_cparams("arbitrary"))(w, g, m, v)


def place_own(w, chip):
    L, R, C = w.shape
    tr = _rows_tile(R)

    def body(c_ref, w_ref, o_ref):
        o_ref[...] = w_ref[...].astype(BF16)

    return pl.pallas_call(
        body, name="place_own",
        grid_spec=pltpu.PrefetchScalarGridSpec(
            num_scalar_prefetch=1, grid=(L, R // tr),
            in_specs=[pl.BlockSpec((None, tr, C), lambda l, i, c: (l, i, 0))],
            out_specs=pl.BlockSpec((None, None, tr, C), lambda l, i, c: (l, c[0], i, 0))),
        out_shape=jax.ShapeDtypeStruct((L, N_CHIPS, R, C), BF16),
        compiler_params=_cparams("arbitrary", "arbitrary"))(chip, w)


def sum_pair(grad, other, layer):
    _, R, C = grad.shape
    tr = _rows_tile(R)

    def body(l_ref, g_ref, o_ref, out_ref):
        out_ref[...] = (g_ref[...].astype(F32) + o_ref[...].astype(F32)).astype(BF16)

    return pl.pallas_call(
        body, name="sum_pair",
        grid_spec=pltpu.PrefetchScalarGridSpec(
            num_scalar_prefetch=1, grid=(R // tr,),
            in_specs=[pl.BlockSpec((None, tr, C), lambda i, l: (l[0], i, 0)), pl.BlockSpec((tr, C), lambda i, l: (i, 0))],
            out_specs=pl.BlockSpec((tr, C), lambda i, l: (i, 0))),
        out_shape=jax.ShapeDtypeStruct((R, C), BF16), compiler_params=_cparams("arbitrary"))(layer, grad, other)


def sum_chips(part, recv, chip, layer):
    _, R, C = part.shape
    tr = _rows_tile(R)

    def body(c_ref, l_ref, p_ref, r_ref, out_ref):
        acc = p_ref[...].astype(F32)
        for k in range(N_CHIPS - 1):
            acc = acc + r_ref[k].astype(F32)
        out_ref[...] = acc

    return pl.pallas_call(
        body, name="sum_chips",
        grid_spec=pltpu.PrefetchScalarGridSpec(
            num_scalar_prefetch=2, grid=(R // tr,),
            in_specs=[pl.BlockSpec((None, tr, C), lambda i, c, l: (c[0], i, 0)),
                      pl.BlockSpec((N_CHIPS - 1, tr, C), lambda i, c, l: (0, i, 0))],
            out_specs=pl.BlockSpec((None, tr, C), lambda i, c, l: (l[0], i, 0))),
        out_shape=jax.ShapeDtypeStruct((2, R, C), F32), compiler_params=_cparams("arbitrary"))(chip, layer, part, recv)


_ANY = pl.BlockSpec(memory_space=pl.ANY)


def _place():
    x, y, c = lax.axis_index("x"), lax.axis_index("y"), lax.axis_index("c")
    chips = [(1 - x, y), (x, 1 - y), (1 - x, 1 - y)]
    return x, y, c, chips


def all_gather_shards(ws):
    n = len(ws)

    def body(*refs):
        w, o = refs[:n], refs[n:2 * n]
        send, recv = refs[2 * n:]
        x, y, c, chips = _place()
        me, sib = 2 * x + y, (x, y, 1 - c)

        def remote(t, k, layer, shard, to, src=None):
            blk = o[t].at[layer, shard]
            return pltpu.make_async_remote_copy(src_ref=blk if src is None else src, dst_ref=blk, send_sem=send.at[t, k],
                                                recv_sem=recv.at[t, k], device_id=to, device_id_type=MESH)

        first = [remote(t, k, c, me, (*chip, c), src=w[t].at[c, me]) for t in range(n) for k, chip in enumerate(chips)]
        for cp in first:
            cp.start()
        passed = []
        for t in range(n):
            for k, chip in enumerate(chips):
                remote(t, k, c, 2 * chip[0] + chip[1], (x, y, c)).wait_recv()
                passed.append(remote(t, 3 + k, c, 2 * chip[0] + chip[1], sib))
                passed[-1].start()
        for t in range(n):
            for k, chip in enumerate(chips):
                remote(t, 3 + k, 1 - c, 2 * chip[0] + chip[1], (x, y, c)).wait_recv()
        for cp in first + passed:
            cp.wait_send()

    return pl.pallas_call(
        body, name="all_gather_shards", in_specs=[_ANY] * n, out_specs=[_ANY] * n,
        out_shape=[jax.ShapeDtypeStruct(w.shape, w.dtype) for w in ws], input_output_aliases={t: t for t in range(n)},
        scratch_shapes=[pltpu.SemaphoreType.DMA((n, 6)), pltpu.SemaphoreType.DMA((n, 6))],
        compiler_params=pltpu.CompilerParams(has_side_effects=True))(*ws)


def pair_exchange(gs):
    n = len(gs)

    def body(*refs):
        g, o = refs[:n], refs[n:2 * n]
        send, recv = refs[2 * n:]
        x, y, c, _ = _place()
        cps = [pltpu.make_async_remote_copy(src_ref=g[t].at[1 - c], dst_ref=o[t], send_sem=send.at[t], recv_sem=recv.at[t],
                                            device_id=(x, y, 1 - c), device_id_type=MESH) for t in range(n)]
        for cp in cps:
            cp.start()
        for cp in cps:
            cp.wait()

    return pl.pallas_call(
        body, name="pair_exchange", in_specs=[_ANY] * n, out_specs=[_ANY] * n,
        out_shape=[jax.ShapeDtypeStruct(g.shape[1:], g.dtype) for g in gs],
        scratch_shapes=[pltpu.SemaphoreType.DMA((n,)), pltpu.SemaphoreType.DMA((n,))],
        compiler_params=pltpu.CompilerParams(has_side_effects=True))(*gs)


def chip_scatter(ps):
    n = len(ps)

    def body(*refs):
        p, o = refs[:n], refs[n:2 * n]
        send, recv = refs[2 * n:]
        x, y, c, chips = _place()
        cps = [pltpu.make_async_remote_copy(src_ref=p[t].at[2 * chip[0] + chip[1]], dst_ref=o[t].at[k],
                                            send_sem=send.at[t, k], recv_sem=recv.at[t, k], device_id=(*chip, c),
                                            device_id_type=MESH)
               for t in range(n) for k, chip in enumerate(chips)]
        for cp in cps:
            cp.start()
        for cp in cps:
            cp.wait()

    return pl.pallas_call(
        body, name="chip_scatter", in_specs=[_ANY] * n, out_specs=[_ANY] * n,
        out_shape=[jax.ShapeDtypeStruct((N_CHIPS - 1,) + p.shape[1:], p.dtype) for p in ps],
        scratch_shapes=[pltpu.SemaphoreType.DMA((n, 3)), pltpu.SemaphoreType.DMA((n, 3))],
        compiler_params=pltpu.CompilerParams(has_side_effects=True))(*ps)


def pair_share(rs):
    n = len(rs)

    def body(*refs):
        r, o = refs[:n], refs[n:2 * n]
        send, recv = refs[2 * n:]
        x, y, c, _ = _place()

        def remote(t, layer):
            return pltpu.make_async_remote_copy(src_ref=r[t].at[layer], dst_ref=o[t].at[layer], send_sem=send.at[t],
                                                recv_sem=recv.at[t], device_id=(x, y, 1 - c), device_id_type=MESH)

        for t in range(n):
            remote(t, c).start()
        for t in range(n):
            remote(t, 1 - c).wait_recv()
            remote(t, c).wait_send()

    return pl.pallas_call(
        body, name="pair_share", in_specs=[_ANY] * n, out_specs=[_ANY] * n,
        out_shape=[jax.ShapeDtypeStruct(r.shape, r.dtype) for r in rs], input_output_aliases={t: t for t in range(n)},
        scratch_shapes=[pltpu.SemaphoreType.DMA((n,)), pltpu.SemaphoreType.DMA((n,))],
        compiler_params=pltpu.CompilerParams(has_side_effects=True))(*rs)


def all_reduce_small(buf):
    def body(x_ref, o_ref, land, send, recv):
        x, y, c, _ = _place()
        me = 4 * x + 2 * y + c
        land[me] = x_ref[...]
        cps = []
        for mask in range(1, N_DEV):
            px = 1 - x if mask & 4 else x
            py = 1 - y if mask & 2 else y
            pc = 1 - c if mask & 1 else c
            cps.append(pltpu.make_async_remote_copy(src_ref=x_ref, dst_ref=land.at[me], send_sem=send.at[mask - 1],
                                                    recv_sem=recv.at[mask - 1], device_id=(px, py, pc), device_id_type=MESH))
            cps[-1].start()
        for mask in range(1, N_DEV):
            px = 1 - x if mask & 4 else x
            py = 1 - y if mask & 2 else y
            pc = 1 - c if mask & 1 else c
            pltpu.make_async_remote_copy(src_ref=x_ref, dst_ref=land.at[4 * px + 2 * py + pc], send_sem=send.at[mask - 1],
                                         recv_sem=recv.at[mask - 1], device_id=(px, py, pc), device_id_type=MESH).wait_recv()
        for cp in cps:
            cp.wait_send()
        acc = land[0]
        for d in range(1, N_DEV):
            acc = acc + land[d]
        o_ref[...] = acc

    vm = pl.BlockSpec(memory_space=pltpu.VMEM)
    return pl.pallas_call(
        body, name="all_reduce_small", in_specs=[vm], out_specs=vm, out_shape=jax.ShapeDtypeStruct(buf.shape, F32),
        scratch_shapes=[pltpu.VMEM((N_DEV,) + buf.shape, F32), pltpu.SemaphoreType.DMA((N_DEV - 1,)),
                        pltpu.SemaphoreType.DMA((N_DEV - 1,))])(buf)


def _ffn_fwd(x, gain, wg, wu, wd, l):
    h = rms_fwd(x, gain)
    g, u, a = gate_up(h, wg, wu, l)
    return down_proj(a, wd, l, x), (h, g, u, a)


def _ffn_bwd(dy, x, gain, wg, wu, wd, l, saved):
    h, g, u, a = saved
    D = x.shape[1]
    Fs = g.shape[1] // N_CHIPS
    dg, du = down_bwd(dy, wd, l, g, u)
    dwg = mm_tn(h, dg, _tile(D, 1024), Fs, blocked=True)
    dwu = mm_tn(h, du, _tile(D, 1024), Fs, blocked=True)
    dwd = mm_tn(a, dy, Fs, _tile(D, 1024), scale=0.5).reshape(N_CHIPS, Fs, D)
    dh = gate_up_bwd(dg, du, wg, wu, l)
    dx, dgain = rms_bwd(dh, x, gain, dy)
    return dx, dgain, dwg, dwu, dwd


def _mixer_fwd(x, gain, win, q_norm, kv_norm, sinks, bias, wqb, wkvb, wout, c2, s2, l):
    D = x.shape[1]
    h = rms_fwd(x, gain)
    p = mm_nn(h, win, l, 640, F32)
    (cqn, ckvn, qmt, km, kmt, vm, vmt, qst, ks, kst, vs, vst, qft, kf, kft, vf, vft, ccol) = mixer_prep(
        p, c2, s2, q_norm, kv_norm, bias, wqb, wkvb, l)
    o_mla, lse_mla = attn_fwd("attn_mla", qmt, km, vmt, QK_MLA ** -0.5)
    o_swa, lse_swa = attn_fwd("attn_swa", qst, ks, vst, HD ** -0.5, window=WINDOW, sinks=sinks)
    o_fox, lse_fox = attn_fwd("attn_fox", qft, kf, vft, HD ** -0.5, ccol=ccol)
    mixed = merge_heads(o_mla, o_swa, o_fox)
    y = mm_nn(mixed, wout, l, _tile(D, 1024), F32, resid=x)
    saved = (h, p, cqn, ckvn, qmt, km, kmt, vm, qst, ks, kst, vs, qft, kf, kft, vf, ccol, lse_mla, lse_swa, lse_fox, mixed)
    return y, saved


def _mixer_bwd(dy, x, gain, win, q_norm, kv_norm, sinks, bias, wqb, wkvb, wout, c2, s2, l, saved):
    (h, p, cqn, ckvn, qmt, km, kmt, vm, qst, ks, kst, vs, qft, kf, kft, vf, ccol, lse_mla, lse_swa, lse_fox, mixed) = saved
    S, D = x.shape
    width = mixed.shape[1]
    dmixed = mm_nt(dy, wout, l, _tile(width, 1024), BF16)
    dwout = mm_tn(mixed, dy, _tile(width, 1024), _tile(D, 1024))
    do_mla, do_swa, do_fox, dl_mla, dl_swa, dl_fox = split_heads(dmixed, mixed)
    dqm, dkm, dvm = attn_bwd("attn_mla_bwd", qmt, km, kmt, vm, do_mla, lse_mla, dl_mla, QK_MLA ** -0.5)
    dqs, dks, dvs, dsink = attn_bwd("attn_swa_bwd", qst, ks, kst, vs, do_swa, lse_swa, dl_swa, HD ** -0.5, window=WINDOW,
                                    sinks=sinks)
    dqf, dkf, dvf, dcq, dck = attn_bwd("attn_fox_bwd", qft, kf, kft, vf, do_fox, lse_fox, dl_fox, HD ** -0.5, ccol=ccol)
    dc = dcq.reshape(N_HEADS, S).T + dck.reshape(N_HEADS, S).T
    dp, dq, dkv, dqn, dkvn, dbias = mixer_prep_bwd(p, c2, s2, q_norm, kv_norm, bias, wqb, wkvb, l, dqm, dkm, dvm, dqs, dks,
                                                   dvs, dqf, dkf, dvf, dc)
    dwqb = mm_tn(cqn, dq, Q_LORA, N_HEADS * QK_MLA)
    dwkvb = mm_tn(ckvn, dkv, KV_LORA, 1024)
    dwin = mm_tn(h, dp, _tile(D, 1024), 640)
    dh = mm_nt(dp, win, l, _tile(D, 1024), F32)
    dx, dgain = rms_bwd(dh, x, gain, dy)
    return dx, dgain, dwin, dqn, dwqb, dkvn, dwkvb, dsink[:, 0, 0], dbias[0], dwout


def _pad_in_cols(w):
    pad = jnp.zeros(w.shape[:-1] + (P_COLS - IN_COLS,), w.dtype)
    return jnp.concatenate([w[..., :IN_KR], w[..., IN_KR + ROPE:IN_COLS - N_HEADS], w[..., IN_KR:IN_KR + ROPE],
                            w[..., IN_COLS - N_HEADS:], pad], axis=-1)


def _unpad_in_cols(w):
    return jnp.concatenate([w[..., :IN_KR], w[..., P_KR:P_KR + ROPE], w[..., IN_KR:P_KR], w[..., P_F:P_F + N_HEADS]], axis=-1)


def _col_shards(w):
    R = w.shape[0]
    return w.reshape(R, N_CHIPS, -1).transpose(1, 0, 2)


def _from_col_shards(w):
    L, _, R, C = w.shape
    return w.transpose(0, 2, 1, 3).reshape(L, R, N_CHIPS * C)


def kernel(x, positions, ffn1_norm, ffn1_w_gate, ffn1_w_up, ffn1_w_down, mix_norm, w_in, mla_q_norm, mla_w_q_b, mla_kv_norm, mla_w_kv_b, swa_sinks, fox_forget_bias, w_out, ffn2_norm, ffn2_w_gate, ffn2_w_up, ffn2_w_down, final_norm, loss_target, m_ffn1_norm, m_ffn1_w_gate, m_ffn1_w_up, m_ffn1_w_down, m_mix_norm, m_w_in, m_mla_q_norm, m_mla_w_q_b, m_mla_kv_norm, m_mla_w_kv_b, m_swa_sinks, m_fox_forget_bias, m_w_out, m_ffn2_norm, m_ffn2_w_gate, m_ffn2_w_up, m_ffn2_w_down, m_final_norm, v_ffn1_norm, v_ffn1_w_gate, v_ffn1_w_up, v_ffn1_w_down, v_mix_norm, v_w_in, v_mla_q_norm, v_mla_w_q_b, v_mla_kv_norm, v_mla_w_kv_b, v_swa_sinks, v_fox_forget_bias, v_w_out, v_ffn2_norm, v_ffn2_w_gate, v_ffn2_w_up, v_ffn2_w_down, v_final_norm):
    L = ffn1_norm.shape[0]
    S, D = x.shape[1], x.shape[2]
    F = ffn1_w_down.shape[1] * N_CHIPS
    xs, target = x[0], loss_target[0]
    cx, cy, cc = lax.axis_index("x"), lax.axis_index("y"), lax.axis_index("c")
    layer_id = jnp.reshape(cc, (1,)).astype(jnp.int32)
    chip_id = jnp.reshape(2 * cx + cy, (1,)).astype(jnp.int32)

    inv_freq = ROPE_THETA ** (-jnp.arange(0, ROPE, 2, dtype=F32) / ROPE)
    ang = positions[0].astype(F32)[:, None] * inv_freq
    cos, sin = jnp.cos(ang), jnp.sin(ang)
    c2, s2 = jnp.concatenate([cos, cos], axis=-1), jnp.concatenate([-sin, sin], axis=-1)

    big = [ffn1_w_gate, ffn1_w_up, ffn1_w_down, w_in, mla_w_q_b, mla_w_kv_b, w_out, ffn2_w_gate, ffn2_w_up, ffn2_w_down]
    wg1, wu1, wd1, win, wqb, wkvb, wout, wg2, wu2, wd2 = all_gather_shards([place_own(w, chip_id) for w in big])
    wd1, wd2 = wd1.reshape(L, F, D), wd2.reshape(L, F, D)
    wout = wout.reshape(L, -1, D)
    win = _pad_in_cols(_from_col_shards(win))
    wqb, wkvb = _from_col_shards(wqb), _from_col_shards(wkvb)

    acts = []
    h = xs
    for l in range(L):
        x0 = h
        x1, s1 = _ffn_fwd(x0, ffn1_norm[l][None], wg1, wu1, wd1, l)
        x2, sm = _mixer_fwd(x1, mix_norm[l][None], win, mla_q_norm[l][None], mla_kv_norm[l][None], swa_sinks[l],
                            fox_forget_bias[l][None], wqb, wkvb, wout, c2, s2, l)
        x3, s2_ = _ffn_fwd(x2, ffn2_norm[l][None], wg2, wu2, wd2, l)
        acts.append((x0, x1, x2, s1, sm, s2_))
        h = x3
    loss_part, dx, d_final = loss_head(h, final_norm[None], target)

    small = {k: [None] * L for k in ("ffn1_norm", "mix_norm", "q_norm", "kv_norm", "sinks", "bias", "ffn2_norm")}
    bigg = {k: [None] * L for k in ("wg1", "wu1", "wd1", "win", "wqb", "wkvb", "wout", "wg2", "wu2", "wd2")}
    for l in reversed(range(L)):
        x0, x1, x2, s1, sm, s2_ = acts[l]
        dx, small["ffn2_norm"][l], bigg["wg2"][l], bigg["wu2"][l], bigg["wd2"][l] = _ffn_bwd(
            dx, x2, ffn2_norm[l][None], wg2, wu2, wd2, l, s2_)
        (dx, small["mix_norm"][l], dwin, small["q_norm"][l], dwqb, small["kv_norm"][l], dwkvb, small["sinks"][l],
         small["bias"][l], dwout) = _mixer_bwd(dx, x1, mix_norm[l][None], win, mla_q_norm[l][None], mla_kv_norm[l][None],
                                               swa_sinks[l], fox_forget_bias[l][None], wqb, wkvb, wout, c2, s2, l, sm)
        bigg["win"][l] = _col_shards(_unpad_in_cols(dwin))
        bigg["wqb"][l] = _col_shards(dwqb)
        bigg["wkvb"][l] = _col_shards(dwkvb)
        bigg["wout"][l] = dwout.reshape(N_CHIPS, -1, D)
        dx, small["ffn1_norm"][l], bigg["wg1"][l], bigg["wu1"][l], bigg["wd1"][l] = _ffn_bwd(
            dx, x0, ffn1_norm[l][None], wg1, wu1, wd1, l, s1)
    grad_x = dx[None]

    names = ("wg1", "wu1", "wd1", "win", "wqb", "wkvb", "wout", "wg2", "wu2", "wd2")
    full = [jnp.stack(bigg[k]) for k in names]
    flat = [g.reshape(L, -1, g.shape[-1]) for g in full]
    from_sibling = pair_exchange(flat)
    part = [sum_pair(g, o, layer_id).reshape(f.shape[1:]) for g, o, f in zip(flat, from_sibling, full)]
    from_chips = chip_scatter(part)
    mine = [sum_chips(p, r, chip_id, layer_id) for p, r in zip(part, from_chips)]
    grads_big = pair_share(mine)

    pieces = [jnp.concatenate(small["ffn1_norm"]), jnp.concatenate(small["mix_norm"]), jnp.concatenate(small["q_norm"]),
              jnp.concatenate(small["kv_norm"]), jnp.stack(small["sinks"]), jnp.stack(small["bias"]),
              jnp.concatenate(small["ffn2_norm"]), d_final, loss_part[:, 0:1]]
    sizes = [int(p.size) for p in pieces]
    packed = jnp.concatenate([p.reshape(-1) for p in pieces])
    packed = jnp.pad(packed, (0, SMALL_ROWS * 128 - packed.shape[0])).reshape(SMALL_ROWS, 128)
    summed = all_reduce_small(packed).reshape(-1)
    out_small, off = [], 0
    for p, n in zip(pieces, sizes):
        out_small.append(summed[off:off + n].reshape(p.shape))
        off += n
    g_ffn1_norm, g_mix_norm, g_q_norm, g_kv_norm, g_sinks, g_bias, g_ffn2_norm, g_final, loss = out_small
    loss = loss.reshape(())
    g_final = g_final.reshape(-1)

    gb = dict(zip(names, grads_big))
    summed_grads = [g_ffn1_norm, gb["wg1"], gb["wu1"], gb["wd1"], g_mix_norm, gb["win"], g_q_norm, gb["wqb"], g_kv_norm,
                    gb["wkvb"], g_sinks, g_bias, gb["wout"], g_ffn2_norm, gb["wg2"], gb["wu2"], gb["wd2"], g_final]
    weights = [ffn1_norm, ffn1_w_gate, ffn1_w_up, ffn1_w_down, mix_norm, w_in, mla_q_norm, mla_w_q_b, mla_kv_norm, mla_w_kv_b,
               swa_sinks, fox_forget_bias, w_out, ffn2_norm, ffn2_w_gate, ffn2_w_up, ffn2_w_down, final_norm]
    ms = [m_ffn1_norm, m_ffn1_w_gate, m_ffn1_w_up, m_ffn1_w_down, m_mix_norm, m_w_in, m_mla_q_norm, m_mla_w_q_b, m_mla_kv_norm,
          m_mla_w_kv_b, m_swa_sinks, m_fox_forget_bias, m_w_out, m_ffn2_norm, m_ffn2_w_gate, m_ffn2_w_up, m_ffn2_w_down,
          m_final_norm]
    vs = [v_ffn1_norm, v_ffn1_w_gate, v_ffn1_w_up, v_ffn1_w_down, v_mix_norm, v_w_in, v_mla_q_norm, v_mla_w_q_b, v_mla_kv_norm,
          v_mla_w_kv_b, v_swa_sinks, v_fox_forget_bias, v_w_out, v_ffn2_norm, v_ffn2_w_gate, v_ffn2_w_up, v_ffn2_w_down,
          v_final_norm]
    grads, deltas, new_m, new_v = [], [], [], []
    for w, g, m, v in zip(weights, summed_grads, ms, vs):
        two_d = (-1, w.shape[-1])
        g_out, d, nm, nv = adamw(w.reshape(two_d), g.reshape(two_d), m.reshape(two_d), v.reshape(two_d))
        grads.append(g_out.reshape(w.shape))
        deltas.append(d.reshape(w.shape))
        new_m.append(nm.reshape(w.shape))
        new_v.append(nv.reshape(w.shape))
    return (loss, grad_x, *grads, *deltas, *new_m, *new_v)
```

```python
import jax
import jax.numpy as jnp
from jax import lax
from jax.experimental import pallas as pl
from jax.experimental.pallas import tpu as pltpu

F32, BF16 = jnp.float32, jnp.bfloat16
MESH = pl.DeviceIdType.MESH

RMS_EPS = 1e-6
ROPE_THETA = 10000.0
N_HEADS = 8
Q_LORA, KV_LORA = 512, 256
NOPE, ROPE, VDIM = 128, 64, 128
QK_MLA = NOPE + ROPE
SWA_KV, HD, WINDOW = 2, 64, 128
P_CQ, P_CKV, P_QS, P_KS, P_VS, P_QF, P_KF, P_VF, P_KR, P_F, P_COLS = (
    0, 512, 768, 1280, 1408, 1536, 2048, 2560, 3072, 3136, 3200)
IN_COLS = 3144
IN_KR = 768
ADAM_LR, ADAM_B1, ADAM_B2, ADAM_EPS, ADAM_WD, ADAM_STEP = 0.001, 0.9, 0.999, 1e-08, 0.01, 10
NEG = -1e30
VMEM_LIMIT = 56 * 1024 * 1024
N_CHIPS = 4
N_DEV = 8
SMALL_ROWS = 128


def _tile(n, pref):
    return pref if n % pref == 0 else n


def _cparams(*sem):
    return pltpu.CompilerParams(dimension_semantics=sem, vmem_limit_bytes=VMEM_LIMIT)


def _sigmoid(x):
    return 1.0 / (1.0 + jnp.exp(-x))


def _dot(a, b):
    return jnp.dot(a, b, preferred_element_type=F32)


def _dot_nt(a, b):
    return lax.dot_general(a, b, (((1,), (1,)), ((), ())), preferred_element_type=F32)


def _dot_tn(a, b):
    return lax.dot_general(a, b, (((0,), (0,)), ((), ())), preferred_element_type=F32)


def rms_fwd(x, gain):
    S, D = x.shape
    tm = _tile(S, 512)

    def body(x_ref, g_ref, h_ref):
        xv = x_ref[...]
        r = lax.rsqrt(jnp.mean(xv * xv, axis=-1, keepdims=True) + RMS_EPS)
        h_ref[...] = (xv * r * g_ref[...]).astype(BF16)

    return pl.pallas_call(
        body, name="rms_fwd", grid=(S // tm,),
        in_specs=[pl.BlockSpec((tm, D), lambda i: (i, 0)), pl.BlockSpec((1, D), lambda i: (0, 0))],
        out_specs=pl.BlockSpec((tm, D), lambda i: (i, 0)),
        out_shape=jax.ShapeDtypeStruct((S, D), BF16), compiler_params=_cparams("arbitrary"))(x, gain)


def rms_bwd(dh, x, gain, resid):
    S, D = x.shape
    tm = _tile(S, 512)

    def body(dh_ref, x_ref, g_ref, r_ref, dx_ref, dg_ref):
        xv, dhv = x_ref[...], dh_ref[...]
        r = lax.rsqrt(jnp.mean(xv * xv, axis=-1, keepdims=True) + RMS_EPS)
        xhat = xv * r
        dhg = dhv * g_ref[...]
        dx_ref[...] = r_ref[...] + r * (dhg - xhat * jnp.mean(dhg * xhat, axis=-1, keepdims=True))

        @pl.when(pl.program_id(0) == 0)
        def _():
            dg_ref[...] = jnp.zeros_like(dg_ref)

        dg_ref[...] += jnp.sum(dhv * xhat, axis=0, keepdims=True)

    row = pl.BlockSpec((tm, D), lambda i: (i, 0))
    vec = pl.BlockSpec((1, D), lambda i: (0, 0))
    return pl.pallas_call(
        body, name="rms_bwd", grid=(S // tm,), in_specs=[row, row, vec, row], out_specs=[row, vec],
        out_shape=[jax.ShapeDtypeStruct((S, D), F32), jax.ShapeDtypeStruct((1, D), F32)],
        compiler_params=_cparams("arbitrary"))(dh, x, gain, resid)


def loss_head(x, gain, target):
    S, D = x.shape
    tm = _tile(S, 512)

    def body(x_ref, g_ref, t_ref, loss_ref, dx_ref, dg_ref):
        xv, g = x_ref[...], g_ref[...]
        r = lax.rsqrt(jnp.mean(xv * xv, axis=-1, keepdims=True) + RMS_EPS)
        xhat = xv * r
        err = xhat * g - t_ref[...]
        dy = err * (1.0 / D)
        dyg = dy * g
        dx_ref[...] = r * (dyg - xhat * jnp.mean(dyg * xhat, axis=-1, keepdims=True))

        @pl.when(pl.program_id(0) == 0)
        def _():
            dg_ref[...] = jnp.zeros_like(dg_ref)
            loss_ref[...] = jnp.zeros_like(loss_ref)

        dg_ref[...] += jnp.sum(dy * xhat, axis=0, keepdims=True)
        loss_ref[...] += 0.5 * jnp.sum(jnp.mean(err * err, axis=-1, keepdims=True), axis=0, keepdims=True)

    row = pl.BlockSpec((tm, D), lambda i: (i, 0))
    vec = pl.BlockSpec((1, D), lambda i: (0, 0))
    return pl.pallas_call(
        body, name="loss_head", grid=(S // tm,), in_specs=[row, vec, row],
        out_specs=[pl.BlockSpec((1, 128), lambda i: (0, 0)), row, vec],
        out_shape=[jax.ShapeDtypeStruct((1, 128), F32), jax.ShapeDtypeStruct((S, D), F32),
                   jax.ShapeDtypeStruct((1, D), F32)],
        compiler_params=_cparams("arbitrary"))(x, gain, target)


def gate_up(h, wg, wu, l):
    S, D = h.shape
    Fs = wg.shape[3]
    tm = _tile(S, 512)

    def body(h_ref, wg_ref, wu_ref, g_ref, u_ref, a_ref):
        hv = h_ref[...]
        g = _dot(hv, wg_ref[...])
        u = _dot(hv, wu_ref[...])
        g_ref[...] = g.astype(BF16)
        u_ref[...] = u.astype(BF16)
        a_ref[...] = (g * _sigmoid(g) * u).astype(BF16)

    w_spec = pl.BlockSpec((None, None, D, Fs), lambda j, i: (l, j, 0, 0))
    o_spec = pl.BlockSpec((tm, Fs), lambda j, i: (i, j))
    o_shape = jax.ShapeDtypeStruct((S, N_CHIPS * Fs), BF16)
    return pl.pallas_call(
        body, name="gate_up", grid=(N_CHIPS, S // tm),
        in_specs=[pl.BlockSpec((tm, D), lambda j, i: (i, 0)), w_spec, w_spec],
        out_specs=[o_spec, o_spec, o_spec], out_shape=[o_shape, o_shape, o_shape],
        compiler_params=_cparams("arbitrary", "arbitrary"))(h, wg, wu)


def down_proj(a, wd, l, x):
    S, F = a.shape
    D = wd.shape[2]
    tm, tk = _tile(S, 512), F // N_CHIPS
    nk = F // tk

    def body(a_ref, w_ref, x_ref, o_ref, acc_ref):
        k = pl.program_id(1)

        @pl.when(k == 0)
        def _():
            acc_ref[...] = jnp.zeros_like(acc_ref)

        acc_ref[...] += _dot(a_ref[...], w_ref[...])

        @pl.when(k == nk - 1)
        def _():
            o_ref[...] = x_ref[...] + 0.5 * acc_ref[...]

    return pl.pallas_call(
        body, name="down_proj", grid=(S // tm, nk),
        in_specs=[pl.BlockSpec((tm, tk), lambda i, k: (i, k)), pl.BlockSpec((None, tk, D), lambda i, k: (l, k, 0)),
                  pl.BlockSpec((tm, D), lambda i, k: (i, 0))],
        out_specs=pl.BlockSpec((tm, D), lambda i, k: (i, 0)), out_shape=jax.ShapeDtypeStruct((S, D), F32),
        scratch_shapes=[pltpu.VMEM((tm, D), F32)],
        compiler_params=_cparams("arbitrary", "arbitrary"))(a, wd, x)


def down_bwd(dy, wd, l, g, u):
    S, D = dy.shape
    F = g.shape[1]
    Fs = F // N_CHIPS
    tm = _tile(S, 512)

    def body(dy_ref, w_ref, g_ref, u_ref, dg_ref, du_ref):
        da = 0.5 * _dot_nt(dy_ref[...].astype(BF16), w_ref[...])
        gv, uv = g_ref[...].astype(F32), u_ref[...].astype(F32)
        sig = _sigmoid(gv)
        du_ref[...] = (da * (gv * sig)).astype(BF16)
        dg_ref[...] = (da * uv * (sig * (1.0 + gv * (1.0 - sig)))).astype(BF16)

    t_spec = pl.BlockSpec((tm, Fs), lambda j, i: (i, j))
    o_shape = jax.ShapeDtypeStruct((S, F), BF16)
    return pl.pallas_call(
        body, name="down_bwd", grid=(N_CHIPS, S // tm),
        in_specs=[pl.BlockSpec((tm, D), lambda j, i: (i, 0)), pl.BlockSpec((None, Fs, D), lambda j, i: (l, j, 0)),
                  t_spec, t_spec],
        out_specs=[t_spec, t_spec], out_shape=[o_shape, o_shape],
        compiler_params=_cparams("arbitrary", "arbitrary"))(dy, wd, g, u)


def gate_up_bwd(dg, du, wg, wu, l):
    S, F = dg.shape
    D, Fs = wg.shape[2], wg.shape[3]
    tm = _tile(S, 512)

    def body(dg_ref, du_ref, wg_ref, wu_ref, o_ref):
        k = pl.program_id(1)

        @pl.when(k == 0)
        def _():
            o_ref[...] = jnp.zeros_like(o_ref)

        o_ref[...] += _dot_nt(dg_ref[...], wg_ref[...]) + _dot_nt(du_ref[...], wu_ref[...])

    t_spec = pl.BlockSpec((tm, Fs), lambda i, k: (i, k))
    w_spec = pl.BlockSpec((None, None, D, Fs), lambda i, k: (l, k, 0, 0))
    return pl.pallas_call(
        body, name="gate_up_bwd", grid=(S // tm, N_CHIPS), in_specs=[t_spec, t_spec, w_spec, w_spec],
        out_specs=pl.BlockSpec((tm, D), lambda i, k: (i, 0)), out_shape=jax.ShapeDtypeStruct((S, D), F32),
        compiler_params=_cparams("arbitrary", "arbitrary"))(dg, du, wg, wu)


def mm_nn(a, b, l, tn, out_dtype, resid=None):
    S, K = a.shape
    N = b.shape[2]
    tm = _tile(S, 512)

    def body(a_ref, b_ref, *rest):
        o_ref = rest[-1]
        acc = _dot(a_ref[...].astype(BF16), b_ref[...])
        if resid is not None:
            acc = rest[0][...] + acc
        o_ref[...] = acc.astype(out_dtype)

    o_spec = pl.BlockSpec((tm, tn), lambda n, i: (i, n))
    in_specs = [pl.BlockSpec((tm, K), lambda n, i: (i, 0)), pl.BlockSpec((None, K, tn), lambda n, i: (l, 0, n))]
    args = [a, b]
    if resid is not None:
        in_specs.append(o_spec)
        args.append(resid)
    return pl.pallas_call(
        body, name="mm_nn", grid=(N // tn, S // tm), in_specs=in_specs, out_specs=o_spec,
        out_shape=jax.ShapeDtypeStruct((S, N), out_dtype),
        compiler_params=_cparams("arbitrary", "arbitrary"))(*args)


def mm_nt(a, b, l, tn, out_dtype):
    S, K = a.shape
    N = b.shape[1]
    tm = _tile(S, 512)

    def body(a_ref, b_ref, o_ref):
        o_ref[...] = _dot_nt(a_ref[...].astype(BF16), b_ref[...]).astype(out_dtype)

    return pl.pallas_call(
        body, name="mm_nt", grid=(N // tn, S // tm),
        in_specs=[pl.BlockSpec((tm, K), lambda n, i: (i, 0)), pl.BlockSpec((None, tn, K), lambda n, i: (l, n, 0))],
        out_specs=pl.BlockSpec((tm, tn), lambda n, i: (i, n)), out_shape=jax.ShapeDtypeStruct((S, N), out_dtype),
        compiler_params=_cparams("arbitrary", "arbitrary"))(a, b)


def mm_tn(a, b, tka, tnb, scale=1.0, blocked=False, layer=None, layers=None, into=None):
    S, Ka = a.shape
    Nb = b.shape[1]
    ts = _tile(S, 1024)
    ns = S // ts

    def body(a_ref, b_ref, *rest):
        o_ref, acc_ref = rest[-2:]
        s = pl.program_id(2)

        @pl.when(s == 0)
        def _():
            acc_ref[...] = jnp.zeros_like(acc_ref)

        acc_ref[...] += _dot_tn(a_ref[...].astype(BF16), b_ref[...].astype(BF16))

        @pl.when(s == ns - 1)
        def _():
            o_ref[...] = (scale * acc_ref[...]).astype(BF16)

    if blocked:
        block, shape = (None, tka, tnb), (Nb // tnb, Ka, tnb)
        index = lambda ka, nb, s: (nb, ka, 0)
    else:
        block, shape = (tka, tnb), (Ka, Nb)
        index = lambda ka, nb, s: (ka, nb)
    if layer is not None:
        block, shape = (None,) + block, (layers,) + shape
        inner = index
        index = lambda ka, nb, s: (layer,) + inner(ka, nb, s)
    in_specs = [pl.BlockSpec((ts, tka), lambda ka, nb, s: (s, ka)), pl.BlockSpec((ts, tnb), lambda ka, nb, s: (s, nb))]
    args, aliases = [a, b], {}
    if into is not None:
        in_specs.append(pl.BlockSpec(memory_space=pl.ANY))
        args.append(into)
        aliases = {2: 0}
    return pl.pallas_call(
        body, name="mm_tn", grid=(Ka // tka, Nb // tnb, ns), in_specs=in_specs,
        out_specs=pl.BlockSpec(block, index), out_shape=jax.ShapeDtypeStruct(shape, BF16),
        input_output_aliases=aliases, scratch_shapes=[pltpu.VMEM((tka, tnb), F32)],
        compiler_params=_cparams("arbitrary", "arbitrary", "arbitrary"))(*args)


def _rope(x, c2, s2):
    half = x.shape[-1] // 2
    rot = jnp.concatenate([x[:, half:], x[:, :half]], axis=-1)
    return x * c2 + rot * s2


def _tri(tm, upper):
    r = lax.broadcasted_iota(jnp.int32, (tm, tm), 0)
    c = lax.broadcasted_iota(jnp.int32, (tm, tm), 1)
    return jnp.where((c >= r) if upper else (c <= r), 1.0, 0.0).astype(F32)


def _log_sigmoid(x):
    return jnp.minimum(x, 0.0) - jnp.log(1.0 + jnp.exp(-jnp.abs(x)))


def _norm_hat(c):
    r = lax.rsqrt(jnp.mean(c * c, axis=-1, keepdims=True) + RMS_EPS)
    return c * r, r


def _tok_spec(tm, width, rev_n=None):
    if rev_n is None:
        return pl.BlockSpec((tm, width), lambda i: (i, 0))
    return pl.BlockSpec((tm, width), lambda i: (rev_n - 1 - i, 0))


def _head_spec(heads, tm, width, rev_n=None):
    if rev_n is None:
        return pl.BlockSpec((heads, tm, width), lambda i: (0, i, 0))
    return pl.BlockSpec((heads, tm, width), lambda i: (0, rev_n - 1 - i, 0))


def _lane_spec(heads, width, tm, rev_n=None):
    if rev_n is None:
        return pl.BlockSpec((heads, width, tm), lambda i: (0, 0, i))
    return pl.BlockSpec((heads, width, tm), lambda i: (0, 0, rev_n - 1 - i))


def _full_spec(shape):
    return pl.BlockSpec(shape, lambda i: (0,) * len(shape))


def mixer_prep(p, c2, s2, q_norm, kv_norm, bias, wqb, wkvb, l):
    S = p.shape[0]
    tm = _tile(S, 256)
    H = N_HEADS

    def body(p_ref, c2_ref, s2_ref, qn_ref, kvn_ref, b_ref, wqb_ref, wkvb_ref,
             cqn_ref, ckvn_ref, qmt_ref, km_ref, kmt_ref, vm_ref, vmt_ref, qst_ref, ks_ref, kst_ref, vs_ref, vst_ref,
             qft_ref, kf_ref, kft_ref, vf_ref, vft_ref, ccol_ref, carry_row):
        c2, s2 = c2_ref[...], s2_ref[...]
        cqn = (_norm_hat(p_ref[:, P_CQ:P_CQ + Q_LORA])[0] * qn_ref[...]).astype(BF16)
        ckvn = (_norm_hat(p_ref[:, P_CKV:P_CKV + KV_LORA])[0] * kvn_ref[...]).astype(BF16)
        cqn_ref[...] = cqn
        ckvn_ref[...] = ckvn
        q = _dot(cqn, wqb_ref[...])
        kv = _dot(ckvn, wkvb_ref[...])
        k_pe = _rope(p_ref[:, P_KR:P_KR + ROPE], c2, s2)
        k_pe_t = k_pe.T.astype(BF16)
        k_pe = k_pe.astype(BF16)

        def both_ways(x, tok_ref, lane_ref, h):
            tok_ref[h] = x.astype(BF16)
            lane_ref[h] = x.T.astype(BF16)

        for h in range(H):
            qmt_ref[h, 0:NOPE, :] = q[:, h * QK_MLA:h * QK_MLA + NOPE].T.astype(BF16)
            qmt_ref[h, NOPE:QK_MLA, :] = _rope(q[:, h * QK_MLA + NOPE:(h + 1) * QK_MLA], c2, s2).T.astype(BF16)
            k_nope = kv[:, h * 256:h * 256 + NOPE]
            km_ref[h, :, 0:NOPE] = k_nope.astype(BF16)
            km_ref[h, :, NOPE:QK_MLA] = k_pe
            kmt_ref[h, 0:NOPE, :] = k_nope.T.astype(BF16)
            kmt_ref[h, NOPE:QK_MLA, :] = k_pe_t
            both_ways(kv[:, h * 256 + NOPE:(h + 1) * 256], vm_ref, vmt_ref, h)
            qst_ref[h] = _rope(p_ref[:, P_QS + h * HD:P_QS + (h + 1) * HD], c2, s2).T.astype(BF16)
            qft_ref[h] = p_ref[:, P_QF + h * HD:P_QF + (h + 1) * HD].T.astype(BF16)
            both_ways(p_ref[:, P_KF + h * HD:P_KF + (h + 1) * HD], kf_ref, kft_ref, h)
            both_ways(p_ref[:, P_VF + h * HD:P_VF + (h + 1) * HD], vf_ref, vft_ref, h)
        for h in range(SWA_KV):
            both_ways(_rope(p_ref[:, P_KS + h * HD:P_KS + (h + 1) * HD], c2, s2), ks_ref, kst_ref, h)
            both_ways(p_ref[:, P_VS + h * HD:P_VS + (h + 1) * HD], vs_ref, vst_ref, h)

        @pl.when(pl.program_id(0) == 0)
        def _():
            carry_row[...] = jnp.zeros_like(carry_row)

        log_f = _log_sigmoid(p_ref[:, P_F:P_F + H] + b_ref[...])
        c_tok = jnp.dot(_tri(tm, upper=False), log_f, preferred_element_type=F32,
                        precision=lax.Precision.HIGHEST) + carry_row[0:1, 0:H]
        for h in range(H):
            ccol_ref[h] = c_tok[:, h:h + 1]
        carry_row[0:1, 0:H] = c_tok[tm - 1:tm, :]

    out_shape = [jax.ShapeDtypeStruct((S, Q_LORA), BF16), jax.ShapeDtypeStruct((S, KV_LORA), BF16)]
    out_specs = [_tok_spec(tm, Q_LORA), _tok_spec(tm, KV_LORA)]

    def add(heads, d, lanes):
        out_shape.append(jax.ShapeDtypeStruct((heads, d, S) if lanes else (heads, S, d), BF16))
        out_specs.append(_lane_spec(heads, d, tm) if lanes else _head_spec(heads, tm, d))

    for heads_q, heads_kv, dqk, dv in ((H, H, QK_MLA, VDIM), (H, SWA_KV, HD, HD), (H, H, HD, HD)):
        add(heads_q, dqk, True)
        add(heads_kv, dqk, False)
        add(heads_kv, dqk, True)
        add(heads_kv, dv, False)
        add(heads_kv, dv, True)
    out_shape.append(jax.ShapeDtypeStruct((H, S, 1), F32))
    out_specs.append(_head_spec(H, tm, 1))
    in_specs = [_tok_spec(tm, P_COLS), _tok_spec(tm, ROPE), _tok_spec(tm, ROPE), _full_spec((1, Q_LORA)),
                _full_spec((1, KV_LORA)), _full_spec((1, H)),
                pl.BlockSpec((None,) + wqb.shape[1:], lambda i: (l, 0, 0)),
                pl.BlockSpec((None,) + wkvb.shape[1:], lambda i: (l, 0, 0))]
    return pl.pallas_call(
        body, name="mixer_prep", grid=(S // tm,), in_specs=in_specs, out_specs=out_specs, out_shape=out_shape,
        scratch_shapes=[pltpu.VMEM((8, 128), F32)],
        compiler_params=_cparams("arbitrary"))(p, c2, s2, q_norm, kv_norm, bias, wqb, wkvb)


def mixer_prep_bwd(p, c2, s2, q_norm, kv_norm, bias, wqb, wkvb, l, dqm, dkm, dvm, dqs, dks, dvs, dqf, dkf, dvf, dc):
    S = p.shape[0]
    tm = _tile(S, 256)
    nt = S // tm
    H, G = N_HEADS, N_HEADS // SWA_KV

    def body(p_ref, c2_ref, s2_ref, qn_ref, kvn_ref, b_ref, wqb_ref, wkvb_ref,
             dqm_ref, dkm_ref, dvm_ref, dqs_ref, dks_ref, dvs_ref, dqf_ref, dkf_ref, dvf_ref, dc_ref,
             dp_ref, dq_ref, dkv_ref, dqn_ref, dkvn_ref, db_ref, carry):
        c2, s2 = c2_ref[...], -s2_ref[...]

        @pl.when(pl.program_id(0) == 0)
        def _():
            dqn_ref[...] = jnp.zeros_like(dqn_ref)
            dkvn_ref[...] = jnp.zeros_like(dkvn_ref)
            db_ref[...] = jnp.zeros_like(db_ref)
            carry[...] = jnp.zeros_like(carry)

        dk_pe_t = jnp.zeros((ROPE, tm), F32)
        for h in range(H):
            dq_ref[:, h * QK_MLA:h * QK_MLA + NOPE] = dqm_ref[h, 0:NOPE, :].T.astype(BF16)
            dq_ref[:, h * QK_MLA + NOPE:(h + 1) * QK_MLA] = _rope(dqm_ref[h, NOPE:QK_MLA, :].T, c2, s2).astype(BF16)
            dkv_ref[:, h * 256:h * 256 + NOPE] = dkm_ref[h, 0:NOPE, :].T.astype(BF16)
            dkv_ref[:, h * 256 + NOPE:(h + 1) * 256] = dvm_ref[h].T.astype(BF16)
            dk_pe_t = dk_pe_t + dkm_ref[h, NOPE:QK_MLA, :]
        dk_pe = dk_pe_t.T

        def through_norm(dcn, c, gain, dgain_ref):
            c_hat, r = _norm_hat(c)
            dhg = dcn * gain
            dgain_ref[...] += jnp.sum(dcn * c_hat, axis=0, keepdims=True)
            return r * (dhg - c_hat * jnp.mean(dhg * c_hat, axis=-1, keepdims=True))

        dcqn = _dot_nt(dq_ref[...], wqb_ref[...])
        dckvn = _dot_nt(dkv_ref[...], wkvb_ref[...])
        dp_ref[:, P_CQ:P_CQ + Q_LORA] = through_norm(dcqn, p_ref[:, P_CQ:P_CQ + Q_LORA], qn_ref[...], dqn_ref).astype(BF16)
        dp_ref[:, P_CKV:P_CKV + KV_LORA] = through_norm(
            dckvn, p_ref[:, P_CKV:P_CKV + KV_LORA], kvn_ref[...], dkvn_ref).astype(BF16)
        for h in range(H):
            dp_ref[:, P_QS + h * HD:P_QS + (h + 1) * HD] = _rope(dqs_ref[h].T, c2, s2).astype(BF16)
            dp_ref[:, P_QF + h * HD:P_QF + (h + 1) * HD] = dqf_ref[h].T.astype(BF16)
            dp_ref[:, P_KF + h * HD:P_KF + (h + 1) * HD] = dkf_ref[h].T.astype(BF16)
            dp_ref[:, P_VF + h * HD:P_VF + (h + 1) * HD] = dvf_ref[h].T.astype(BF16)
        for kvh in range(SWA_KV):
            dk = dks_ref[kvh * G]
            dv = dvs_ref[kvh * G]
            for g in range(1, G):
                dk = dk + dks_ref[kvh * G + g]
                dv = dv + dvs_ref[kvh * G + g]
            dp_ref[:, P_KS + kvh * HD:P_KS + (kvh + 1) * HD] = _rope(dk.T, c2, s2).astype(BF16)
            dp_ref[:, P_VS + kvh * HD:P_VS + (kvh + 1) * HD] = dv.T.astype(BF16)

        dcv = dc_ref[...]
        dlog_f = jnp.dot(_tri(tm, upper=True), dcv, preferred_element_type=F32,
                         precision=lax.Precision.HIGHEST) + carry[0:1, 0:H]
        carry[0:1, 0:H] = dlog_f[0:1, :]
        df = dlog_f * _sigmoid(-(p_ref[:, P_F:P_F + H] + b_ref[...]))
        db_ref[...] += jnp.sum(df, axis=0, keepdims=True)
        dp_ref[:, P_KR:P_COLS] = jnp.zeros((tm, P_COLS - P_KR), BF16)
        dp_ref[:, P_KR:P_KR + ROPE] = _rope(dk_pe, c2, s2).astype(BF16)
        dp_ref[:, P_F:P_F + H] = df.astype(BF16)

    rev = nt
    in_specs = [_tok_spec(tm, P_COLS, rev), _tok_spec(tm, ROPE, rev), _tok_spec(tm, ROPE, rev), _full_spec((1, Q_LORA)),
                _full_spec((1, KV_LORA)), _full_spec((1, H)),
                pl.BlockSpec((None,) + wqb.shape[1:], lambda i: (l, 0, 0)),
                pl.BlockSpec((None,) + wkvb.shape[1:], lambda i: (l, 0, 0)),
                _lane_spec(H, QK_MLA, tm, rev), _lane_spec(H, QK_MLA, tm, rev), _lane_spec(H, VDIM, tm, rev)]
    in_specs += [_lane_spec(H, HD, tm, rev)] * 6 + [_tok_spec(tm, H, rev)]
    out_specs = [_tok_spec(tm, P_COLS, rev), _tok_spec(tm, N_HEADS * QK_MLA, rev), _tok_spec(tm, N_HEADS * 256, rev),
                 _full_spec((1, Q_LORA)), _full_spec((1, KV_LORA)), _full_spec((1, H))]
    out_shape = [jax.ShapeDtypeStruct((S, P_COLS), BF16), jax.ShapeDtypeStruct((S, N_HEADS * QK_MLA), BF16),
                 jax.ShapeDtypeStruct((S, N_HEADS * 256), BF16), jax.ShapeDtypeStruct((1, Q_LORA), F32),
                 jax.ShapeDtypeStruct((1, KV_LORA), F32), jax.ShapeDtypeStruct((1, H), F32)]
    return pl.pallas_call(
        body, name="mixer_prep_bwd", grid=(nt,), in_specs=in_specs, out_specs=out_specs, out_shape=out_shape,
        scratch_shapes=[pltpu.VMEM((8, 128), F32)], compiler_params=_cparams("arbitrary"))(
            p, c2, s2, q_norm, kv_norm, bias, wqb, wkvb, dqm, dkm, dvm, dqs, dks, dvs, dqf, dkf, dvf, dc)


def merge_heads(o_mla, o_swa, o_fox):
    H, S, _ = o_mla.shape
    tm = _tile(S, 512)
    width = H * (VDIM + 2 * HD)

    def body(om_ref, os_ref, of_ref, m_ref):
        for h in range(H):
            m_ref[:, h * VDIM:(h + 1) * VDIM] = om_ref[h]
            m_ref[:, H * VDIM + h * HD:H * VDIM + (h + 1) * HD] = os_ref[h]
            m_ref[:, H * (VDIM + HD) + h * HD:H * (VDIM + HD) + (h + 1) * HD] = of_ref[h]

    return pl.pallas_call(
        body, name="merge_heads", grid=(S // tm,),
        in_specs=[_head_spec(H, tm, VDIM), _head_spec(H, tm, HD), _head_spec(H, tm, HD)],
        out_specs=_tok_spec(tm, width), out_shape=jax.ShapeDtypeStruct((S, width), BF16),
        compiler_params=_cparams("arbitrary"))(o_mla, o_swa, o_fox)


def split_heads(dmixed, mixed):
    S, width = mixed.shape
    H = N_HEADS
    tm = _tile(S, 512)

    def body(dm_ref, m_ref, dom_ref, dos_ref, dof_ref, dm_delta, ds_delta, df_delta):
        def one(h, off, d, do_ref, delta_ref):
            dv = dm_ref[:, off:off + d].astype(F32)
            do_ref[h] = dv.T.astype(BF16)
            prod = dv * m_ref[:, off:off + d].astype(F32)
            rows = lax.dot_general(jnp.ones((8, d), F32), prod, (((1,), (1,)), ((), ())),
                                   preferred_element_type=F32, precision=lax.Precision.HIGHEST)
            delta_ref[h] = rows[0:1, :]

        for h in range(H):
            one(h, h * VDIM, VDIM, dom_ref, dm_delta)
            one(h, H * VDIM + h * HD, HD, dos_ref, ds_delta)
            one(h, H * (VDIM + HD) + h * HD, HD, dof_ref, df_delta)

    row_spec = pl.BlockSpec((H, 1, tm), lambda i: (0, 0, i))
    row_shape = jax.ShapeDtypeStruct((H, 1, S), F32)
    return pl.pallas_call(
        body, name="split_heads", grid=(S // tm,),
        in_specs=[_tok_spec(tm, width), _tok_spec(tm, width)],
        out_specs=[_lane_spec(H, VDIM, tm), _lane_spec(H, HD, tm), _lane_spec(H, HD, tm), row_spec, row_spec, row_spec],
        out_shape=[jax.ShapeDtypeStruct((H, VDIM, S), BF16), jax.ShapeDtypeStruct((H, HD, S), BF16),
                   jax.ShapeDtypeStruct((H, HD, S), BF16), row_shape, row_shape, row_shape],
        compiler_params=_cparams("arbitrary"))(dmixed, mixed)


def _attn_tile(S):
    return 512 if (S % 512 == 0 and S > 512) else S // 2


def _valid(q0, k0, shape, q_axis, window):
    qpos = q0 + lax.broadcasted_iota(jnp.int32, shape, q_axis)
    kpos = k0 + lax.broadcasted_iota(jnp.int32, shape, 1 - q_axis)
    ok = kpos <= qpos
    if window is not None:
        ok = jnp.logical_and(ok, kpos > qpos - window)
    return ok


def attn_fwd(name, qt, k, vt, scale, window=None, sinks=None, ccol=None):
    H, dq, S = qt.shape
    Hk, dv, _ = vt.shape
    G = H // Hk
    t = _attn_tile(S)
    fox, use_sink = ccol is not None, sinks is not None

    def body(*refs):
        refs = list(refs)
        sink_ref = refs.pop(0) if use_sink else None
        q_ref, k_ref, vt_ref = refs[:3]
        refs = refs[3:]
        ccol_ref = refs.pop(0) if fox else None
        o_ref, lse_ref, m_ref, l_ref, acc_ref = refs
        h, i = pl.program_id(0), pl.program_id(1)
        qv = q_ref[...]
        if use_sink:
            m_ref[...] = jnp.full(m_ref.shape, sink_ref[h], F32)
            l_ref[...] = jnp.ones(l_ref.shape, F32)
        else:
            m_ref[...] = jnp.full(m_ref.shape, NEG, F32)
            l_ref[...] = jnp.zeros(l_ref.shape, F32)
        acc_ref[...] = jnp.zeros(acc_ref.shape, F32)

        def steps(blocks):
            offs = [pl.multiple_of(j * t, t) for j, _ in blocks]
            scores = [_dot(k_ref[pl.ds(off, t), :], qv) * scale for off in offs]
            for (j, masked), off, st in zip(blocks, offs, scores):
                if fox:
                    st = st - ccol_ref[pl.ds(off, t), :]
                if masked:
                    st = jnp.where(_valid(i * t, j * t, (t, t), 1, window), st, NEG)
                m_prev = m_ref[...]
                m_new = jnp.maximum(m_prev, jnp.max(st, axis=0, keepdims=True))
                alpha = jnp.exp(m_prev - m_new)
                pt = jnp.exp(st - m_new)
                l_ref[...] = alpha * l_ref[...] + jnp.sum(pt, axis=0, keepdims=True)
                acc_ref[...] = alpha * acc_ref[...] + _dot(vt_ref[:, pl.ds(off, t)], pt.astype(BF16))
                m_ref[...] = m_new

        if window is None:
            def pair(n, carry):
                steps([(2 * n, False), (2 * n + 1, False)])
                return carry
            lax.fori_loop(0, i // 2, pair, 0)

            @pl.when(i % 2 == 1)
            def _():
                steps([(i - 1, False), (i, True)])

            @pl.when(i % 2 == 0)
            def _():
                steps([(i, True)])
        else:
            def one(j, carry):
                steps([(j, True)])
                return carry
            lax.fori_loop(jnp.maximum(i * t - (window - 1), 0) // t, i + 1, one, 0)
        l = l_ref[...]
        o_ref[...] = (acc_ref[...] / l).T.astype(BF16)
        lse_ref[...] = m_ref[...] + jnp.log(l)

    in_specs, args = [], []
    if use_sink:
        in_specs.append(pl.BlockSpec(memory_space=pltpu.SMEM))
        args.append(sinks)
    in_specs += [pl.BlockSpec((None, dq, t), lambda h, i: (h, 0, i)),
                 pl.BlockSpec((None, S, dq), lambda h, i: (h // G, 0, 0)),
                 pl.BlockSpec((None, dv, S), lambda h, i: (h // G, 0, 0))]
    args += [qt, k, vt]
    if fox:
        in_specs.append(pl.BlockSpec((None, S, 1), lambda h, i: (h, 0, 0)))
        args.append(ccol)
    return pl.pallas_call(
        body, name=name, grid=(H, S // t), in_specs=in_specs,
        out_specs=[pl.BlockSpec((None, t, dv), lambda h, i: (h, i, 0)), pl.BlockSpec((None, 1, t), lambda h, i: (h, 0, i))],
        out_shape=[jax.ShapeDtypeStruct((H, S, dv), BF16), jax.ShapeDtypeStruct((H, 1, S), F32)],
        scratch_shapes=[pltpu.VMEM((1, t), F32), pltpu.VMEM((1, t), F32), pltpu.VMEM((dv, t), F32)],
        compiler_params=_cparams("arbitrary", "arbitrary"))(*args)


def attn_bwd(name, qt, k, kt, v, dot, lse, delta, scale, window=None, sinks=None, ccol=None):
    H, dq, S = qt.shape
    Hk, _, dv = v.shape
    G = H // Hk
    t = _attn_tile(S)
    nq = S // t
    fox, use_sink = ccol is not None, sinks is not None

    def body(*refs):
        refs = list(refs)
        sink_ref = refs.pop(0) if use_sink else None
        qt_ref, k_ref, kt_ref, v_ref, dot_ref, lse_ref, delta_ref = refs[:7]
        refs = refs[7:]
        ccol_ref = refs.pop(0) if fox else None
        dqt_ref, dkt_ref, dvt_ref = refs[:3]
        refs = refs[3:]
        dcq_ref, dck_ref = (refs.pop(0), refs.pop(0)) if fox else (None, None)
        dsink_ref = refs.pop(0) if use_sink else None
        h, j = pl.program_id(0), pl.program_id(1)

        @pl.when(j == 0)
        def _():
            dqt_ref[...] = jnp.zeros(dqt_ref.shape, F32)
            if fox:
                dcq_ref[...] = jnp.zeros(dcq_ref.shape, F32)
            if use_sink:
                ps = jnp.exp(sink_ref[h] - lse_ref[...]) * delta_ref[...]
                dsink_ref[...] = jnp.broadcast_to(-jnp.sum(ps, axis=-1, keepdims=True), dsink_ref.shape)

        dkt_ref[...] = jnp.zeros(dkt_ref.shape, F32)
        dvt_ref[...] = jnp.zeros(dvt_ref.shape, F32)
        if fox:
            dck_ref[...] = jnp.zeros(dck_ref.shape, F32)
        kv, ktv, vv = k_ref[...], kt_ref[...], v_ref[...]

        def steps(blocks):
            offs = [pl.multiple_of(i * t, t) for i, _ in blocks]
            qts = [qt_ref[:, pl.ds(off, t)] for off in offs]
            dots = [dot_ref[:, pl.ds(off, t)] for off in offs]
            scores = [_dot(kv, qti) * scale for qti in qts]
            dprobs = [_dot(vv, doti) for doti in dots]
            for (i, masked), off, qti, doti, st, dpt in zip(blocks, offs, qts, dots, scores, dprobs):
                if fox:
                    st = st - ccol_ref[...]
                if masked:
                    st = jnp.where(_valid(i * t, j * t, (t, t), 1, window), st, NEG)
                pt = jnp.exp(st - lse_ref[:, pl.ds(off, t)])
                dvt_ref[...] += _dot_nt(doti, pt.astype(BF16))
                dst = pt * (dpt - delta_ref[:, pl.ds(off, t)])
                if fox:
                    dcq_ref[:, pl.ds(off, t)] += jnp.sum(dst, axis=0, keepdims=True)
                    dck_ref[...] -= jnp.sum(dst, axis=1, keepdims=True)
                dsb = (dst * scale).astype(BF16)
                dkt_ref[...] += _dot_nt(qti, dsb)
                dqt_ref[:, pl.ds(off, t)] += _dot(ktv, dsb)

        if window is None:
            odd = (nq - 1 - j) % 2

            @pl.when(odd == 1)
            def _():
                steps([(j, True), (j + 1, False)])

            @pl.when(odd == 0)
            def _():
                steps([(j, True)])

            first = j + 1 + odd

            def pair(n, carry):
                steps([(first + 2 * n, False), (first + 2 * n + 1, False)])
                return carry
            lax.fori_loop(0, (nq - first) // 2, pair, 0)
        else:
            def one(i, carry):
                steps([(i, True)])
                return carry
            lax.fori_loop(j, jnp.minimum((j * t + t - 1 + window - 1) // t, nq - 1) + 1, one, 0)

    in_specs, args = [], []
    if use_sink:
        in_specs.append(pl.BlockSpec(memory_space=pltpu.SMEM))
        args.append(sinks)
    whole = lambda d: pl.BlockSpec((None, d, S), lambda h, j: (h, 0, 0))
    keys = lambda d: pl.BlockSpec((None, d, t), lambda h, j: (h, 0, j))
    row = pl.BlockSpec((None, 1, S), lambda h, j: (h, 0, 0))
    in_specs += [whole(dq), pl.BlockSpec((None, t, dq), lambda h, j: (h // G, j, 0)),
                 pl.BlockSpec((None, dq, t), lambda h, j: (h // G, 0, j)),
                 pl.BlockSpec((None, t, dv), lambda h, j: (h // G, j, 0)), whole(dv), row, row]
    args += [qt, k, kt, v, dot, lse, delta]
    out_specs = [whole(dq), keys(dq), keys(dv)]
    out_shape = [jax.ShapeDtypeStruct((H, dq, S), F32), jax.ShapeDtypeStruct((H, dq, S), F32),
                 jax.ShapeDtypeStruct((H, dv, S), F32)]
    if fox:
        in_specs.append(pl.BlockSpec((None, t, 1), lambda h, j: (h, j, 0)))
        args.append(ccol)
        out_specs += [row, pl.BlockSpec((None, t, 1), lambda h, j: (h, j, 0))]
        out_shape += [jax.ShapeDtypeStruct((H, 1, S), F32), jax.ShapeDtypeStruct((H, S, 1), F32)]
    if use_sink:
        out_specs.append(pl.BlockSpec((None, 1, 128), lambda h, j: (h, 0, 0)))
        out_shape.append(jax.ShapeDtypeStruct((H, 1, 128), F32))
    return pl.pallas_call(
        body, name=name, grid=(H, nq), in_specs=in_specs, out_specs=out_specs, out_shape=out_shape,
        compiler_params=_cparams("arbitrary", "arbitrary"))(*args)


def _rows_tile(rows):
    return 256 if rows % 256 == 0 else rows


def adamw(w, g, m, v):
    R, C = w.shape
    tr = _rows_tile(R)

    def body(w_ref, g_ref, m_ref, v_ref, g_out, d_ref, nm_ref, nv_ref):
        gv = g_ref[...]
        mn = ADAM_B1 * m_ref[...] + (1.0 - ADAM_B1) * gv
        vn = ADAM_B2 * v_ref[...] + (1.0 - ADAM_B2) * (gv * gv)
        m_hat = mn / (1.0 - ADAM_B1 ** ADAM_STEP)
        v_hat = vn / (1.0 - ADAM_B2 ** ADAM_STEP)
        d_ref[...] = -ADAM_LR * (m_hat / (jnp.sqrt(v_hat) + ADAM_EPS) + ADAM_WD * w_ref[...])
        nm_ref[...] = mn
        nv_ref[...] = vn
        g_out[...] = gv

    spec = pl.BlockSpec((tr, C), lambda i: (i, 0))
    shape = jax.ShapeDtypeStruct((R, C), F32)
    return pl.pallas_call(
        body, name="adamw", grid=(R // tr,), in_specs=[spec] * 4, out_specs=[spec] * 4, out_shape=[shape] * 4,
        compiler_params=_cparams("arbitrary"))(w, g, m, v)


def place_own(w, chip):
    L, R, C = w.shape
    tr = _rows_tile(R)

    def body(c_ref, w_ref, o_ref):
        o_ref[...] = w_ref[...].astype(BF16)

    return pl.pallas_call(
        body, name="place_own",
        grid_spec=pltpu.PrefetchScalarGridSpec(
            num_scalar_prefetch=1, grid=(L, R // tr),
            in_specs=[pl.BlockSpec((None, tr, C), lambda l, i, c: (l, i, 0))],
            out_specs=pl.BlockSpec((None, None, tr, C), lambda l, i, c: (l, c[0], i, 0))),
        out_shape=jax.ShapeDtypeStruct((L, N_CHIPS, R, C), BF16),
        compiler_params=_cparams("arbitrary", "arbitrary"))(chip, w)


def sum_pair(grad, other, layer):
    _, R, C = grad.shape
    tr = _rows_tile(R)

    def body(l_ref, g_ref, o_ref, out_ref):
        out_ref[...] = (g_ref[...].astype(F32) + o_ref[...].astype(F32)).astype(BF16)

    return pl.pallas_call(
        body, name="sum_pair",
        grid_spec=pltpu.PrefetchScalarGridSpec(
            num_scalar_prefetch=1, grid=(R // tr,),
            in_specs=[pl.BlockSpec((None, tr, C), lambda i, l: (l[0], i, 0)), pl.BlockSpec((tr, C), lambda i, l: (i, 0))],
            out_specs=pl.BlockSpec((tr, C), lambda i, l: (i, 0))),
        out_shape=jax.ShapeDtypeStruct((R, C), BF16), compiler_params=_cparams("arbitrary"))(layer, grad, other)


def sum_chips(part, recv, chip, layer):
    _, R, C = part.shape
    tr = _rows_tile(R)

    def body(c_ref, l_ref, p_ref, r_ref, out_ref):
        acc = p_ref[...].astype(F32)
        for k in range(N_CHIPS - 1):
            acc = acc + r_ref[k].astype(F32)
        out_ref[...] = acc

    return pl.pallas_call(
        body, name="sum_chips",
        grid_spec=pltpu.PrefetchScalarGridSpec(
            num_scalar_prefetch=2, grid=(R // tr,),
            in_specs=[pl.BlockSpec((None, tr, C), lambda i, c, l: (c[0], i, 0)),
                      pl.BlockSpec((N_CHIPS - 1, tr, C), lambda i, c, l: (0, i, 0))],
            out_specs=pl.BlockSpec((None, tr, C), lambda i, c, l: (l[0], i, 0))),
        out_shape=jax.ShapeDtypeStruct((2, R, C), F32), compiler_params=_cparams("arbitrary"))(chip, layer, part, recv)


_ANY = pl.BlockSpec(memory_space=pl.ANY)


def _place():
    x, y, c = lax.axis_index("x"), lax.axis_index("y"), lax.axis_index("c")
    chips = [(1 - x, y), (x, 1 - y), (1 - x, 1 - y)]
    return x, y, c, chips


def all_gather_shards(ws):
    n = len(ws)

    def body(*refs):
        w, o = refs[:n], refs[n:2 * n]
        send, recv = refs[2 * n:]
        x, y, c, chips = _place()
        me, sib = 2 * x + y, (x, y, 1 - c)

        def remote(t, k, layer, shard, to, src=None):
            blk = o[t].at[layer, shard]
            return pltpu.make_async_remote_copy(src_ref=blk if src is None else src, dst_ref=blk, send_sem=send.at[t, k],
                                                recv_sem=recv.at[t, k], device_id=to, device_id_type=MESH)

        first = [remote(t, k, c, me, (*chip, c), src=w[t].at[c, me]) for t in range(n) for k, chip in enumerate(chips)]
        for cp in first:
            cp.start()
        passed = []
        for t in range(n):
            for k, chip in enumerate(chips):
                remote(t, k, c, 2 * chip[0] + chip[1], (x, y, c)).wait_recv()
                passed.append(remote(t, 3 + k, c, 2 * chip[0] + chip[1], sib))
                passed[-1].start()
        for t in range(n):
            for k, chip in enumerate(chips):
                remote(t, 3 + k, 1 - c, 2 * chip[0] + chip[1], (x, y, c)).wait_recv()
        for cp in first + passed:
            cp.wait_send()

    return pl.pallas_call(
        body, name="all_gather_shards", in_specs=[_ANY] * n, out_specs=[_ANY] * n,
        out_shape=[jax.ShapeDtypeStruct(w.shape, w.dtype) for w in ws], input_output_aliases={t: t for t in range(n)},
        scratch_shapes=[pltpu.SemaphoreType.DMA((n, 6)), pltpu.SemaphoreType.DMA((n, 6))],
        compiler_params=pltpu.CompilerParams(has_side_effects=True))(*ws)


def pair_exchange(gs):
    n = len(gs)

    def body(*refs):
        g, o = refs[:n], refs[n:2 * n]
        send, recv = refs[2 * n:]
        x, y, c, _ = _place()
        cps = [pltpu.make_async_remote_copy(src_ref=g[t].at[1 - c], dst_ref=o[t], send_sem=send.at[t], recv_sem=recv.at[t],
                                            device_id=(x, y, 1 - c), device_id_type=MESH) for t in range(n)]
        for cp in cps:
            cp.start()
        for cp in cps:
            cp.wait()

    return pl.pallas_call(
        body, name="pair_exchange", in_specs=[_ANY] * n, out_specs=[_ANY] * n,
        out_shape=[jax.ShapeDtypeStruct(g.shape[1:], g.dtype) for g in gs],
        scratch_shapes=[pltpu.SemaphoreType.DMA((n,)), pltpu.SemaphoreType.DMA((n,))],
        compiler_params=pltpu.CompilerParams(has_side_effects=True))(*gs)


def chip_scatter(ps):
    n = len(ps)

    def body(*refs):
        p, o = refs[:n], refs[n:2 * n]
        send, recv = refs[2 * n:]
        x, y, c, chips = _place()
        cps = [pltpu.make_async_remote_copy(src_ref=p[t].at[2 * chip[0] + chip[1]], dst_ref=o[t].at[k],
                                            send_sem=send.at[t, k], recv_sem=recv.at[t, k], device_id=(*chip, c),
                                            device_id_type=MESH)
               for t in range(n) for k, chip in enumerate(chips)]
        for cp in cps:
            cp.start()
        for cp in cps:
            cp.wait()

    return pl.pallas_call(
        body, name="chip_scatter", in_specs=[_ANY] * n, out_specs=[_ANY] * n,
        out_shape=[jax.ShapeDtypeStruct((N_CHIPS - 1,) + p.shape[1:], p.dtype) for p in ps],
        scratch_shapes=[pltpu.SemaphoreType.DMA((n, 3)), pltpu.SemaphoreType.DMA((n, 3))],
        compiler_params=pltpu.CompilerParams(has_side_effects=True))(*ps)


def pair_share(rs):
    n = len(rs)

    def body(*refs):
        r, o = refs[:n], refs[n:2 * n]
        send, recv = refs[2 * n:]
        x, y, c, _ = _place()

        def remote(t, layer):
            return pltpu.make_async_remote_copy(src_ref=r[t].at[layer], dst_ref=o[t].at[layer], send_sem=send.at[t],
                                                recv_sem=recv.at[t], device_id=(x, y, 1 - c), device_id_type=MESH)

        for t in range(n):
            remote(t, c).start()
        for t in range(n):
            remote(t, 1 - c).wait_recv()
            remote(t, c).wait_send()

    return pl.pallas_call(
        body, name="pair_share", in_specs=[_ANY] * n, out_specs=[_ANY] * n,
        out_shape=[jax.ShapeDtypeStruct(r.shape, r.dtype) for r in rs], input_output_aliases={t: t for t in range(n)},
        scratch_shapes=[pltpu.SemaphoreType.DMA((n,)), pltpu.SemaphoreType.DMA((n,))],
        compiler_params=pltpu.CompilerParams(has_side_effects=True))(*rs)


def all_reduce_small(buf):
    def body(x_ref, o_ref, land, send, recv):
        x, y, c, _ = _place()
        me = 4 * x + 2 * y + c
        land[me] = x_ref[...]
        cps = []
        for mask in range(1, N_DEV):
            px = 1 - x if mask & 4 else x
            py = 1 - y if mask & 2 else y
            pc = 1 - c if mask & 1 else c
            cps.append(pltpu.make_async_remote_copy(src_ref=x_ref, dst_ref=land.at[me], send_sem=send.at[mask - 1],
                                                    recv_sem=recv.at[mask - 1], device_id=(px, py, pc), device_id_type=MESH))
            cps[-1].start()
        for mask in range(1, N_DEV):
            px = 1 - x if mask & 4 else x
            py = 1 - y if mask & 2 else y
            pc = 1 - c if mask & 1 else c
            pltpu.make_async_remote_copy(src_ref=x_ref, dst_ref=land.at[4 * px + 2 * py + pc], send_sem=send.at[mask - 1],
                                         recv_sem=recv.at[mask - 1], device_id=(px, py, pc), device_id_type=MESH).wait_recv()
        for cp in cps:
            cp.wait_send()
        acc = land[0]
        for d in range(1, N_DEV):
            acc = acc + land[d]
        o_ref[...] = acc

    vm = pl.BlockSpec(memory_space=pltpu.VMEM)
    return pl.pallas_call(
        body, name="all_reduce_small", in_specs=[vm], out_specs=vm, out_shape=jax.ShapeDtypeStruct(buf.shape, F32),
        scratch_shapes=[pltpu.VMEM((N_DEV,) + buf.shape, F32), pltpu.SemaphoreType.DMA((N_DEV - 1,)),
                        pltpu.SemaphoreType.DMA((N_DEV - 1,))])(buf)


def _ffn_fwd(x, gain, wg, wu, wd, l):
    h = rms_fwd(x, gain)
    g, u, a = gate_up(h, wg, wu, l)
    return down_proj(a, wd, l, x), (h, g, u, a)


def _ffn_bwd(dy, x, gain, wg, wu, wd, l, saved, grads):
    h, g, u, a = saved
    D = x.shape[1]
    L = wg.shape[0]
    Fs = g.shape[1] // N_CHIPS
    dg, du = down_bwd(dy, wd, l, g, u)
    dwg = mm_tn(h, dg, _tile(D, 1024), Fs, blocked=True, layer=l, layers=L, into=grads[0])
    dwu = mm_tn(h, du, _tile(D, 1024), Fs, blocked=True, layer=l, layers=L, into=grads[1])
    dwd = mm_tn(a, dy, Fs, _tile(D, 1024), scale=0.5, layer=l, layers=L, into=grads[2])
    dh = gate_up_bwd(dg, du, wg, wu, l)
    dx, dgain = rms_bwd(dh, x, gain, dy)
    return dx, dgain, (dwg, dwu, dwd)


def _mixer_fwd(x, gain, win, q_norm, kv_norm, sinks, bias, wqb, wkvb, wout, c2, s2, l):
    D = x.shape[1]
    h = rms_fwd(x, gain)
    p = mm_nn(h, win, l, 640, F32)
    (cqn, ckvn, qmt, km, kmt, vm, vmt, qst, ks, kst, vs, vst, qft, kf, kft, vf, vft, ccol) = mixer_prep(
        p, c2, s2, q_norm, kv_norm, bias, wqb, wkvb, l)
    o_mla, lse_mla = attn_fwd("attn_mla", qmt, km, vmt, QK_MLA ** -0.5)
    o_swa, lse_swa = attn_fwd("attn_swa", qst, ks, vst, HD ** -0.5, window=WINDOW, sinks=sinks)
    o_fox, lse_fox = attn_fwd("attn_fox", qft, kf, vft, HD ** -0.5, ccol=ccol)
    mixed = merge_heads(o_mla, o_swa, o_fox)
    y = mm_nn(mixed, wout, l, _tile(D, 1024), F32, resid=x)
    saved = (h, p, cqn, ckvn, qmt, km, kmt, vm, qst, ks, kst, vs, qft, kf, kft, vf, ccol, lse_mla, lse_swa, lse_fox, mixed)
    return y, saved


def _mixer_bwd(dy, x, gain, win, q_norm, kv_norm, sinks, bias, wqb, wkvb, wout, c2, s2, l, saved, dwout_so_far):
    (h, p, cqn, ckvn, qmt, km, kmt, vm, qst, ks, kst, vs, qft, kf, kft, vf, ccol, lse_mla, lse_swa, lse_fox, mixed) = saved
    S, D = x.shape
    width = mixed.shape[1]
    dmixed = mm_nt(dy, wout, l, _tile(width, 1024), BF16)
    dwout = mm_tn(mixed, dy, _tile(width, 1024), _tile(D, 1024), layer=l, layers=wout.shape[0], into=dwout_so_far)
    do_mla, do_swa, do_fox, dl_mla, dl_swa, dl_fox = split_heads(dmixed, mixed)
    dqm, dkm, dvm = attn_bwd("attn_mla_bwd", qmt, km, kmt, vm, do_mla, lse_mla, dl_mla, QK_MLA ** -0.5)
    dqs, dks, dvs, dsink = attn_bwd("attn_swa_bwd", qst, ks, kst, vs, do_swa, lse_swa, dl_swa, HD ** -0.5, window=WINDOW,
                                    sinks=sinks)
    dqf, dkf, dvf, dcq, dck = attn_bwd("attn_fox_bwd", qft, kf, kft, vf, do_fox, lse_fox, dl_fox, HD ** -0.5, ccol=ccol)
    dc = dcq.reshape(N_HEADS, S).T + dck.reshape(N_HEADS, S).T
    dp, dq, dkv, dqn, dkvn, dbias = mixer_prep_bwd(p, c2, s2, q_norm, kv_norm, bias, wqb, wkvb, l, dqm, dkm, dvm, dqs, dks,
                                                   dvs, dqf, dkf, dvf, dc)
    dwqb = mm_tn(cqn, dq, Q_LORA, N_HEADS * QK_MLA)
    dwkvb = mm_tn(ckvn, dkv, KV_LORA, 1024)
    dwin = mm_tn(h, dp, _tile(D, 1024), 640)
    dh = mm_nt(dp, win, l, _tile(D, 1024), F32)
    dx, dgain = rms_bwd(dh, x, gain, dy)
    return dx, dgain, dwin, dqn, dwqb, dkvn, dwkvb, dsink[:, 0, 0], dbias[0], dwout


def _pad_in_cols(w):
    pad = jnp.zeros(w.shape[:-1] + (P_COLS - IN_COLS,), w.dtype)
    return jnp.concatenate([w[..., :IN_KR], w[..., IN_KR + ROPE:IN_COLS - N_HEADS], w[..., IN_KR:IN_KR + ROPE],
                            w[..., IN_COLS - N_HEADS:], pad], axis=-1)


def _unpad_in_cols(w):
    return jnp.concatenate([w[..., :IN_KR], w[..., P_KR:P_KR + ROPE], w[..., IN_KR:P_KR], w[..., P_F:P_F + N_HEADS]], axis=-1)


def _col_shards(w):
    R = w.shape[0]
    return w.reshape(R, N_CHIPS, -1).transpose(1, 0, 2)


def _from_col_shards(w):
    L, _, R, C = w.shape
    return w.transpose(0, 2, 1, 3).reshape(L, R, N_CHIPS * C)


def kernel(x, positions, ffn1_norm, ffn1_w_gate, ffn1_w_up, ffn1_w_down, mix_norm, w_in, mla_q_norm, mla_w_q_b, mla_kv_norm, mla_w_kv_b, swa_sinks, fox_forget_bias, w_out, ffn2_norm, ffn2_w_gate, ffn2_w_up, ffn2_w_down, final_norm, loss_target, m_ffn1_norm, m_ffn1_w_gate, m_ffn1_w_up, m_ffn1_w_down, m_mix_norm, m_w_in, m_mla_q_norm, m_mla_w_q_b, m_mla_kv_norm, m_mla_w_kv_b, m_swa_sinks, m_fox_forget_bias, m_w_out, m_ffn2_norm, m_ffn2_w_gate, m_ffn2_w_up, m_ffn2_w_down, m_final_norm, v_ffn1_norm, v_ffn1_w_gate, v_ffn1_w_up, v_ffn1_w_down, v_mix_norm, v_w_in, v_mla_q_norm, v_mla_w_q_b, v_mla_kv_norm, v_mla_w_kv_b, v_swa_sinks, v_fox_forget_bias, v_w_out, v_ffn2_norm, v_ffn2_w_gate, v_ffn2_w_up, v_ffn2_w_down, v_final_norm):
    L = ffn1_norm.shape[0]
    S, D = x.shape[1], x.shape[2]
    F = ffn1_w_down.shape[1] * N_CHIPS
    xs, target = x[0], loss_target[0]
    cx, cy, cc = lax.axis_index("x"), lax.axis_index("y"), lax.axis_index("c")
    layer_id = jnp.reshape(cc, (1,)).astype(jnp.int32)
    chip_id = jnp.reshape(2 * cx + cy, (1,)).astype(jnp.int32)

    inv_freq = ROPE_THETA ** (-jnp.arange(0, ROPE, 2, dtype=F32) / ROPE)
    ang = positions[0].astype(F32)[:, None] * inv_freq
    cos, sin = jnp.cos(ang), jnp.sin(ang)
    c2, s2 = jnp.concatenate([cos, cos], axis=-1), jnp.concatenate([-sin, sin], axis=-1)

    big = [ffn1_w_gate, ffn1_w_up, ffn1_w_down, w_in, mla_w_q_b, mla_w_kv_b, w_out, ffn2_w_gate, ffn2_w_up, ffn2_w_down]
    wg1, wu1, wd1, win, wqb, wkvb, wout, wg2, wu2, wd2 = all_gather_shards([place_own(w, chip_id) for w in big])
    wd1, wd2 = wd1.reshape(L, F, D), wd2.reshape(L, F, D)
    wout = wout.reshape(L, -1, D)
    win = _pad_in_cols(_from_col_shards(win))
    wqb, wkvb = _from_col_shards(wqb), _from_col_shards(wkvb)

    acts = []
    h = xs
    for l in range(L):
        x0 = h
        x1, s1 = _ffn_fwd(x0, ffn1_norm[l][None], wg1, wu1, wd1, l)
        x2, sm = _mixer_fwd(x1, mix_norm[l][None], win, mla_q_norm[l][None], mla_kv_norm[l][None], swa_sinks[l],
                            fox_forget_bias[l][None], wqb, wkvb, wout, c2, s2, l)
        x3, s2_ = _ffn_fwd(x2, ffn2_norm[l][None], wg2, wu2, wd2, l)
        acts.append((x0, x1, x2, s1, sm, s2_))
        h = x3
    loss_part, dx, d_final = loss_head(h, final_norm[None], target)

    small = {k: [None] * L for k in ("ffn1_norm", "mix_norm", "q_norm", "kv_norm", "sinks", "bias", "ffn2_norm")}
    per_layer = {k: [None] * L for k in ("win", "wqb", "wkvb")}
    ffn1_grads, ffn2_grads, dwout = (None,) * 3, (None,) * 3, None
    for l in reversed(range(L)):
        x0, x1, x2, s1, sm, s2_ = acts[l]
        dx, small["ffn2_norm"][l], ffn2_grads = _ffn_bwd(dx, x2, ffn2_norm[l][None], wg2, wu2, wd2, l, s2_, ffn2_grads)
        (dx, small["mix_norm"][l], dwin, small["q_norm"][l], dwqb, small["kv_norm"][l], dwkvb, small["sinks"][l],
         small["bias"][l], dwout) = _mixer_bwd(dx, x1, mix_norm[l][None], win, mla_q_norm[l][None], mla_kv_norm[l][None],
                                               swa_sinks[l], fox_forget_bias[l][None], wqb, wkvb, wout, c2, s2, l, sm,
                                               dwout)
        per_layer["win"][l] = _col_shards(_unpad_in_cols(dwin))
        per_layer["wqb"][l] = _col_shards(dwqb)
        per_layer["wkvb"][l] = _col_shards(dwkvb)
        dx, small["ffn1_norm"][l], ffn1_grads = _ffn_bwd(dx, x0, ffn1_norm[l][None], wg1, wu1, wd1, l, s1, ffn1_grads)
    grad_x = dx[None]

    names = ("wg1", "wu1", "wd1", "win", "wqb", "wkvb", "wout", "wg2", "wu2", "wd2")
    Fs = F // N_CHIPS
    full = [ffn1_grads[0], ffn1_grads[1], ffn1_grads[2].reshape(L, N_CHIPS, Fs, D), jnp.stack(per_layer["win"]),
            jnp.stack(per_layer["wqb"]), jnp.stack(per_layer["wkvb"]), dwout.reshape(L, N_CHIPS, -1, D),
            ffn2_grads[0], ffn2_grads[1], ffn2_grads[2].reshape(L, N_CHIPS, Fs, D)]
    flat = [g.reshape(L, -1, g.shape[-1]) for g in full]
    from_sibling = pair_exchange(flat)
    part = [sum_pair(g, o, layer_id).reshape(f.shape[1:]) for g, o, f in zip(flat, from_sibling, full)]
    from_chips = chip_scatter(part)
    mine = [sum_chips(p, r, chip_id, layer_id) for p, r in zip(part, from_chips)]
    grads_big = pair_share(mine)

    pieces = [jnp.concatenate(small["ffn1_norm"]), jnp.concatenate(small["mix_norm"]), jnp.concatenate(small["q_norm"]),
              jnp.concatenate(small["kv_norm"]), jnp.stack(small["sinks"]), jnp.stack(small["bias"]),
              jnp.concatenate(small["ffn2_norm"]), d_final, loss_part[:, 0:1]]
    sizes = [int(p.size) for p in pieces]
    packed = jnp.concatenate([p.reshape(-1) for p in pieces])
    packed = jnp.pad(packed, (0, SMALL_ROWS * 128 - packed.shape[0])).reshape(SMALL_ROWS, 128)
    summed = all_reduce_small(packed).reshape(-1)
    out_small, off = [], 0
    for p, n in zip(pieces, sizes):
        out_small.append(summed[off:off + n].reshape(p.shape))
        off += n
    g_ffn1_norm, g_mix_norm, g_q_norm, g_kv_norm, g_sinks, g_bias, g_ffn2_norm, g_final, loss = out_small
    loss = loss.reshape(())
    g_final = g_final.reshape(-1)

    gb = dict(zip(names, grads_big))
    summed_grads = [g_ffn1_norm, gb["wg1"], gb["wu1"], gb["wd1"], g_mix_norm, gb["win"], g_q_norm, gb["wqb"], g_kv_norm,
                    gb["wkvb"], g_sinks, g_bias, gb["wout"], g_ffn2_norm, gb["wg2"], gb["wu2"], gb["wd2"], g_final]
    weights = [ffn1_norm, ffn1_w_gate, ffn1_w_up, ffn1_w_down, mix_norm, w_in, mla_q_norm, mla_w_q_b, mla_kv_norm, mla_w_kv_b,
               swa_sinks, fox_forget_bias, w_out, ffn2_norm, ffn2_w_gate, ffn2_w_up, ffn2_w_down, final_norm]
    ms = [m_ffn1_norm, m_ffn1_w_gate, m_ffn1_w_up, m_ffn1_w_down, m_mix_norm, m_w_in, m_mla_q_norm, m_mla_w_q_b, m_mla_kv_norm,
          m_mla_w_kv_b, m_swa_sinks, m_fox_forget_bias, m_w_out, m_ffn2_norm, m_ffn2_w_gate, m_ffn2_w_up, m_ffn2_w_down,
          m_final_norm]
    vs = [v_ffn1_norm, v_ffn1_w_gate, v_ffn1_w_up, v_ffn1_w_down, v_mix_norm, v_w_in, v_mla_q_norm, v_mla_w_q_b, v_mla_kv_norm,
          v_mla_w_kv_b, v_swa_sinks, v_fox_forget_bias, v_w_out, v_ffn2_norm, v_ffn2_w_gate, v_ffn2_w_up, v_ffn2_w_down,
          v_final_norm]
    grads, deltas, new_m, new_v = [], [], [], []
    for w, g, m, v in zip(weights, summed_grads, ms, vs):
        two_d = (-1, w.shape[-1])
        g_out, d, nm, nv = adamw(w.reshape(two_d), g.reshape(two_d), m.reshape(two_d), v.reshape(two_d))
        grads.append(g_out.reshape(w.shape))
        deltas.append(d.reshape(w.shape))
        new_m.append(nm.reshape(w.shape))
        new_v.append(nv.reshape(w.shape))
    return (loss, grad_x, *grads, *deltas, *new_m, *new_v)
```

```python
import jax
import jax.numpy as jnp
from jax import lax
from jax.experimental import pallas as pl
from jax.experimental.pallas import tpu as pltpu

F32, BF16 = jnp.float32, jnp.bfloat16
MESH = pl.DeviceIdType.MESH

RMS_EPS = 1e-6
ROPE_THETA = 10000.0
N_HEADS = 8
Q_LORA, KV_LORA = 512, 256
NOPE, ROPE, VDIM = 128, 64, 128
QK_MLA = NOPE + ROPE
SWA_KV, HD, WINDOW = 2, 64, 128
P_CQ, P_CKV, P_QS, P_KS, P_VS, P_QF, P_KF, P_VF, P_KR, P_F, P_COLS = (
    0, 512, 768, 1280, 1408, 1536, 2048, 2560, 3072, 3136, 3200)
IN_COLS = 3144
IN_KR = 768
ADAM_LR, ADAM_B1, ADAM_B2, ADAM_EPS, ADAM_WD, ADAM_STEP = 0.001, 0.9, 0.999, 1e-08, 0.01, 10
NEG = -1e30
LOG2E, LN2 = 1.4426950408889634, 0.6931471805599453
VMEM_LIMIT = 56 * 1024 * 1024
N_CHIPS = 4
N_DEV = 8
SMALL_ROWS = 128


def _tile(n, pref):
    return pref if n % pref == 0 else n


def _cparams(*sem):
    return pltpu.CompilerParams(dimension_semantics=sem, vmem_limit_bytes=VMEM_LIMIT)


def _sigmoid(x):
    return 1.0 / (1.0 + jnp.exp(-x))


def _dot(a, b):
    return jnp.dot(a, b, preferred_element_type=F32)


def _dot_nt(a, b):
    return lax.dot_general(a, b, (((1,), (1,)), ((), ())), preferred_element_type=F32)


def _dot_tn(a, b):
    return lax.dot_general(a, b, (((0,), (0,)), ((), ())), preferred_element_type=F32)


def rms_fwd(x, gain):
    S, D = x.shape
    tm = _tile(S, 512)

    def body(x_ref, g_ref, h_ref):
        xv = x_ref[...]
        r = lax.rsqrt(jnp.mean(xv * xv, axis=-1, keepdims=True) + RMS_EPS)
        h_ref[...] = (xv * r * g_ref[...]).astype(BF16)

    return pl.pallas_call(
        body, name="rms_fwd", grid=(S // tm,),
        in_specs=[pl.BlockSpec((tm, D), lambda i: (i, 0)), pl.BlockSpec((1, D), lambda i: (0, 0))],
        out_specs=pl.BlockSpec((tm, D), lambda i: (i, 0)),
        out_shape=jax.ShapeDtypeStruct((S, D), BF16), compiler_params=_cparams("arbitrary"))(x, gain)


def rms_bwd(dh, x, gain, resid):
    S, D = x.shape
    tm = _tile(S, 512)

    def body(dh_ref, x_ref, g_ref, r_ref, dx_ref, dg_ref):
        xv, dhv = x_ref[...], dh_ref[...]
        r = lax.rsqrt(jnp.mean(xv * xv, axis=-1, keepdims=True) + RMS_EPS)
        xhat = xv * r
        dhg = dhv * g_ref[...]
        dx_ref[...] = r_ref[...] + r * (dhg - xhat * jnp.mean(dhg * xhat, axis=-1, keepdims=True))

        @pl.when(pl.program_id(0) == 0)
        def _():
            dg_ref[...] = jnp.zeros_like(dg_ref)

        dg_ref[...] += jnp.sum(dhv * xhat, axis=0, keepdims=True)

    row = pl.BlockSpec((tm, D), lambda i: (i, 0))
    vec = pl.BlockSpec((1, D), lambda i: (0, 0))
    return pl.pallas_call(
        body, name="rms_bwd", grid=(S // tm,), in_specs=[row, row, vec, row], out_specs=[row, vec],
        out_shape=[jax.ShapeDtypeStruct((S, D), F32), jax.ShapeDtypeStruct((1, D), F32)],
        compiler_params=_cparams("arbitrary"))(dh, x, gain, resid)


def loss_head(x, gain, target):
    S, D = x.shape
    tm = _tile(S, 512)

    def body(x_ref, g_ref, t_ref, loss_ref, dx_ref, dg_ref):
        xv, g = x_ref[...], g_ref[...]
        r = lax.rsqrt(jnp.mean(xv * xv, axis=-1, keepdims=True) + RMS_EPS)
        xhat = xv * r
        err = xhat * g - t_ref[...]
        dy = err * (1.0 / D)
        dyg = dy * g
        dx_ref[...] = r * (dyg - xhat * jnp.mean(dyg * xhat, axis=-1, keepdims=True))

        @pl.when(pl.program_id(0) == 0)
        def _():
            dg_ref[...] = jnp.zeros_like(dg_ref)
            loss_ref[...] = jnp.zeros_like(loss_ref)

        dg_ref[...] += jnp.sum(dy * xhat, axis=0, keepdims=True)
        loss_ref[...] += 0.5 * jnp.sum(jnp.mean(err * err, axis=-1, keepdims=True), axis=0, keepdims=True)

    row = pl.BlockSpec((tm, D), lambda i: (i, 0))
    vec = pl.BlockSpec((1, D), lambda i: (0, 0))
    return pl.pallas_call(
        body, name="loss_head", grid=(S // tm,), in_specs=[row, vec, row],
        out_specs=[pl.BlockSpec((1, 128), lambda i: (0, 0)), row, vec],
        out_shape=[jax.ShapeDtypeStruct((1, 128), F32), jax.ShapeDtypeStruct((S, D), F32),
                   jax.ShapeDtypeStruct((1, D), F32)],
        compiler_params=_cparams("arbitrary"))(x, gain, target)


def gate_up(h, wg, wu, l):
    S, D = h.shape
    Fs = wg.shape[3]
    tm = _tile(S, 512)

    def body(h_ref, wg_ref, wu_ref, g_ref, u_ref, a_ref):
        hv = h_ref[...]
        g = _dot(hv, wg_ref[...])
        u = _dot(hv, wu_ref[...])
        g_ref[...] = g.astype(BF16)
        u_ref[...] = u.astype(BF16)
        a_ref[...] = (g * _sigmoid(g) * u).astype(BF16)

    w_spec = pl.BlockSpec((None, None, D, Fs), lambda j, i: (l, j, 0, 0))
    o_spec = pl.BlockSpec((tm, Fs), lambda j, i: (i, j))
    o_shape = jax.ShapeDtypeStruct((S, N_CHIPS * Fs), BF16)
    return pl.pallas_call(
        body, name="gate_up", grid=(N_CHIPS, S // tm),
        in_specs=[pl.BlockSpec((tm, D), lambda j, i: (i, 0)), w_spec, w_spec],
        out_specs=[o_spec, o_spec, o_spec], out_shape=[o_shape, o_shape, o_shape],
        compiler_params=_cparams("arbitrary", "arbitrary"))(h, wg, wu)


def _normed(x, gain):
    r = lax.rsqrt(jnp.mean(x * x, axis=-1, keepdims=True) + RMS_EPS)
    return (x * r * gain).astype(BF16)


def down_proj(a, wd, l, x, next_gain=None):
    S, F = a.shape
    D = wd.shape[2]
    tm, tk = _tile(S, 512), F // N_CHIPS
    nk = F // tk
    emit = next_gain is not None

    def body(a_ref, w_ref, x_ref, *rest):
        g_ref = rest[0] if emit else None
        o_ref = rest[1] if emit else rest[0]
        acc_ref = rest[-1]
        k = pl.program_id(1)

        @pl.when(k == 0)
        def _():
            acc_ref[...] = jnp.zeros_like(acc_ref)

        acc_ref[...] += _dot(a_ref[...], w_ref[...])

        @pl.when(k == nk - 1)
        def _():
            y = x_ref[...] + 0.5 * acc_ref[...]
            o_ref[...] = y
            if emit:
                rest[2][...] = _normed(y, g_ref[...])

    row = pl.BlockSpec((tm, D), lambda i, k: (i, 0))
    in_specs = [pl.BlockSpec((tm, tk), lambda i, k: (i, k)), pl.BlockSpec((None, tk, D), lambda i, k: (l, k, 0)), row]
    args, out_specs, out_shape = [a, wd, x], [row], [jax.ShapeDtypeStruct((S, D), F32)]
    if emit:
        in_specs.append(pl.BlockSpec((1, D), lambda i, k: (0, 0)))
        args.append(next_gain)
        out_specs.append(row)
        out_shape.append(jax.ShapeDtypeStruct((S, D), BF16))
    out = pl.pallas_call(
        body, name="down_proj", grid=(S // tm, nk), in_specs=in_specs, out_specs=out_specs, out_shape=out_shape,
        scratch_shapes=[pltpu.VMEM((tm, D), F32)], compiler_params=_cparams("arbitrary", "arbitrary"))(*args)
    return out if emit else (out[0], None)


def down_bwd(dy, wd, l, g, u):
    S, D = dy.shape
    F = g.shape[1]
    Fs = F // N_CHIPS
    tm = _tile(S, 512)

    def body(dy_ref, w_ref, g_ref, u_ref, dg_ref, du_ref):
        da = 0.5 * _dot_nt(dy_ref[...].astype(BF16), w_ref[...])
        gv, uv = g_ref[...].astype(F32), u_ref[...].astype(F32)
        sig = _sigmoid(gv)
        du_ref[...] = (da * (gv * sig)).astype(BF16)
        dg_ref[...] = (da * uv * (sig * (1.0 + gv * (1.0 - sig)))).astype(BF16)

    t_spec = pl.BlockSpec((tm, Fs), lambda j, i: (i, j))
    o_shape = jax.ShapeDtypeStruct((S, F), BF16)
    return pl.pallas_call(
        body, name="down_bwd", grid=(N_CHIPS, S // tm),
        in_specs=[pl.BlockSpec((tm, D), lambda j, i: (i, 0)), pl.BlockSpec((None, Fs, D), lambda j, i: (l, j, 0)),
                  t_spec, t_spec],
        out_specs=[t_spec, t_spec], out_shape=[o_shape, o_shape],
        compiler_params=_cparams("arbitrary", "arbitrary"))(dy, wd, g, u)


def gate_up_bwd(dg, du, wg, wu, l):
    S, F = dg.shape
    D, Fs = wg.shape[2], wg.shape[3]
    tm = _tile(S, 512)

    def body(dg_ref, du_ref, wg_ref, wu_ref, o_ref):
        k = pl.program_id(1)

        @pl.when(k == 0)
        def _():
            o_ref[...] = jnp.zeros_like(o_ref)

        o_ref[...] += _dot_nt(dg_ref[...], wg_ref[...]) + _dot_nt(du_ref[...], wu_ref[...])

    t_spec = pl.BlockSpec((tm, Fs), lambda i, k: (i, k))
    w_spec = pl.BlockSpec((None, None, D, Fs), lambda i, k: (l, k, 0, 0))
    return pl.pallas_call(
        body, name="gate_up_bwd", grid=(S // tm, N_CHIPS), in_specs=[t_spec, t_spec, w_spec, w_spec],
        out_specs=pl.BlockSpec((tm, D), lambda i, k: (i, 0)), out_shape=jax.ShapeDtypeStruct((S, D), F32),
        compiler_params=_cparams("arbitrary", "arbitrary"))(dg, du, wg, wu)


def mm_nn(a, b, l, tn, out_dtype, resid=None, next_gain=None):
    S, K = a.shape
    N = b.shape[2]
    tm = _tile(S, 512)
    emit = next_gain is not None
    assert not emit or tn == N

    def body(a_ref, b_ref, *rest):
        rest = list(rest)
        acc = _dot(a_ref[...].astype(BF16), b_ref[...])
        if resid is not None:
            acc = rest.pop(0)[...] + acc
        g_ref = rest.pop(0) if emit else None
        rest[0][...] = acc.astype(out_dtype)
        if emit:
            rest[1][...] = _normed(acc, g_ref[...])

    o_spec = pl.BlockSpec((tm, tn), lambda n, i: (i, n))
    in_specs = [pl.BlockSpec((tm, K), lambda n, i: (i, 0)), pl.BlockSpec((None, K, tn), lambda n, i: (l, 0, n))]
    args, out_specs, out_shape = [a, b], [o_spec], [jax.ShapeDtypeStruct((S, N), out_dtype)]
    if resid is not None:
        in_specs.append(o_spec)
        args.append(resid)
    if emit:
        in_specs.append(pl.BlockSpec((1, N), lambda n, i: (0, 0)))
        args.append(next_gain)
        out_specs.append(o_spec)
        out_shape.append(jax.ShapeDtypeStruct((S, N), BF16))
    out = pl.pallas_call(
        body, name="mm_nn", grid=(N // tn, S // tm), in_specs=in_specs, out_specs=out_specs, out_shape=out_shape,
        compiler_params=_cparams("arbitrary", "arbitrary"))(*args)
    return out if emit else out[0]


def mm_nt(a, b, l, tn, out_dtype):
    S, K = a.shape
    N = b.shape[1]
    tm = _tile(S, 512)

    def body(a_ref, b_ref, o_ref):
        o_ref[...] = _dot_nt(a_ref[...].astype(BF16), b_ref[...]).astype(out_dtype)

    return pl.pallas_call(
        body, name="mm_nt", grid=(N // tn, S // tm),
        in_specs=[pl.BlockSpec((tm, K), lambda n, i: (i, 0)), pl.BlockSpec((None, tn, K), lambda n, i: (l, n, 0))],
        out_specs=pl.BlockSpec((tm, tn), lambda n, i: (i, n)), out_shape=jax.ShapeDtypeStruct((S, N), out_dtype),
        compiler_params=_cparams("arbitrary", "arbitrary"))(a, b)


def mm_tn(a, b, tka, tnb, scale=1.0, blocked=False, layer=None, layers=None, into=None):
    S, Ka = a.shape
    Nb = b.shape[1]
    ts = _tile(S, 1024)
    ns = S // ts

    def body(a_ref, b_ref, *rest):
        o_ref, acc_ref = rest[-2:]
        s = pl.program_id(2)

        @pl.when(s == 0)
        def _():
            acc_ref[...] = jnp.zeros_like(acc_ref)

        acc_ref[...] += _dot_tn(a_ref[...].astype(BF16), b_ref[...].astype(BF16))

        @pl.when(s == ns - 1)
        def _():
            o_ref[...] = (scale * acc_ref[...]).astype(BF16)

    if blocked:
        block, shape = (None, tka, tnb), (Nb // tnb, Ka, tnb)
        index = lambda ka, nb, s: (nb, ka, 0)
    else:
        block, shape = (tka, tnb), (Ka, Nb)
        index = lambda ka, nb, s: (ka, nb)
    if layer is not None:
        block, shape = (None,) + block, (layers,) + shape
        inner = index
        index = lambda ka, nb, s: (layer,) + inner(ka, nb, s)
    in_specs = [pl.BlockSpec((ts, tka), lambda ka, nb, s: (s, ka)), pl.BlockSpec((ts, tnb), lambda ka, nb, s: (s, nb))]
    args, aliases = [a, b], {}
    if into is not None:
        in_specs.append(pl.BlockSpec(memory_space=pl.ANY))
        args.append(into)
        aliases = {2: 0}
    return pl.pallas_call(
        body, name="mm_tn", grid=(Ka // tka, Nb // tnb, ns), in_specs=in_specs,
        out_specs=pl.BlockSpec(block, index), out_shape=jax.ShapeDtypeStruct(shape, BF16),
        input_output_aliases=aliases, scratch_shapes=[pltpu.VMEM((tka, tnb), F32)],
        compiler_params=_cparams("arbitrary", "arbitrary", "arbitrary"))(*args)


def _rope(x, c2, s2):
    half = x.shape[-1] // 2
    rot = jnp.concatenate([x[:, half:], x[:, :half]], axis=-1)
    return x * c2 + rot * s2


def _tri(tm, upper):
    r = lax.broadcasted_iota(jnp.int32, (tm, tm), 0)
    c = lax.broadcasted_iota(jnp.int32, (tm, tm), 1)
    return jnp.where((c >= r) if upper else (c <= r), 1.0, 0.0).astype(F32)


def _log_sigmoid(x):
    return jnp.minimum(x, 0.0) - jnp.log(1.0 + jnp.exp(-jnp.abs(x)))


def _norm_hat(c):
    r = lax.rsqrt(jnp.mean(c * c, axis=-1, keepdims=True) + RMS_EPS)
    return c * r, r


def _tok_spec(tm, width, rev_n=None):
    if rev_n is None:
        return pl.BlockSpec((tm, width), lambda i: (i, 0))
    return pl.BlockSpec((tm, width), lambda i: (rev_n - 1 - i, 0))


def _head_spec(heads, tm, width, rev_n=None):
    if rev_n is None:
        return pl.BlockSpec((heads, tm, width), lambda i: (0, i, 0))
    return pl.BlockSpec((heads, tm, width), lambda i: (0, rev_n - 1 - i, 0))


def _lane_spec(heads, width, tm, rev_n=None):
    if rev_n is None:
        return pl.BlockSpec((heads, width, tm), lambda i: (0, 0, i))
    return pl.BlockSpec((heads, width, tm), lambda i: (0, 0, rev_n - 1 - i))


def _full_spec(shape):
    return pl.BlockSpec(shape, lambda i: (0,) * len(shape))


def mixer_prep(p, c2, s2, q_norm, kv_norm, bias, wqb, wkvb, l):
    S = p.shape[0]
    tm = _tile(S, 256)
    H = N_HEADS

    def body(p_ref, c2_ref, s2_ref, qn_ref, kvn_ref, b_ref, wqb_ref, wkvb_ref,
             cqn_ref, ckvn_ref, qmt_ref, km_ref, kmt_ref, vm_ref, vmt_ref, qst_ref, ks_ref, kst_ref, vs_ref, vst_ref,
             qft_ref, kf_ref, kft_ref, vf_ref, vft_ref, ccol_ref, carry_row):
        c2, s2 = c2_ref[...], s2_ref[...]
        cqn = (_norm_hat(p_ref[:, P_CQ:P_CQ + Q_LORA])[0] * qn_ref[...]).astype(BF16)
        ckvn = (_norm_hat(p_ref[:, P_CKV:P_CKV + KV_LORA])[0] * kvn_ref[...]).astype(BF16)
        cqn_ref[...] = cqn
        ckvn_ref[...] = ckvn
        q = _dot(cqn, wqb_ref[...])
        kv = _dot(ckvn, wkvb_ref[...])
        k_pe = _rope(p_ref[:, P_KR:P_KR + ROPE], c2, s2)
        k_pe_t = k_pe.T.astype(BF16)
        k_pe = k_pe.astype(BF16)

        def both_ways(x, tok_ref, lane_ref, h):
            tok_ref[h] = x.astype(BF16)
            lane_ref[h] = x.T.astype(BF16)

        qs_mla, qs_hd = QK_MLA ** -0.5 * LOG2E, HD ** -0.5 * LOG2E
        for h in range(H):
            qmt_ref[h, 0:NOPE, :] = (q[:, h * QK_MLA:h * QK_MLA + NOPE] * qs_mla).T.astype(BF16)
            qmt_ref[h, NOPE:QK_MLA, :] = (_rope(q[:, h * QK_MLA + NOPE:(h + 1) * QK_MLA], c2, s2) * qs_mla).T.astype(BF16)
            k_nope = kv[:, h * 256:h * 256 + NOPE]
            km_ref[h, :, 0:NOPE] = k_nope.astype(BF16)
            km_ref[h, :, NOPE:QK_MLA] = k_pe
            kmt_ref[h, 0:NOPE, :] = k_nope.T.astype(BF16)
            kmt_ref[h, NOPE:QK_MLA, :] = k_pe_t
            both_ways(kv[:, h * 256 + NOPE:(h + 1) * 256], vm_ref, vmt_ref, h)
            qst_ref[h] = (_rope(p_ref[:, P_QS + h * HD:P_QS + (h + 1) * HD], c2, s2) * qs_hd).T.astype(BF16)
            qft_ref[h] = (p_ref[:, P_QF + h * HD:P_QF + (h + 1) * HD] * qs_hd).T.astype(BF16)
            both_ways(p_ref[:, P_KF + h * HD:P_KF + (h + 1) * HD], kf_ref, kft_ref, h)
            both_ways(p_ref[:, P_VF + h * HD:P_VF + (h + 1) * HD], vf_ref, vft_ref, h)
        for h in range(SWA_KV):
            both_ways(_rope(p_ref[:, P_KS + h * HD:P_KS + (h + 1) * HD], c2, s2), ks_ref, kst_ref, h)
            both_ways(p_ref[:, P_VS + h * HD:P_VS + (h + 1) * HD], vs_ref, vst_ref, h)

        @pl.when(pl.program_id(0) == 0)
        def _():
            carry_row[...] = jnp.zeros_like(carry_row)

        log_f = _log_sigmoid(p_ref[:, P_F:P_F + H] + b_ref[...])
        c_tok = jnp.dot(_tri(tm, upper=False), log_f, preferred_element_type=F32,
                        precision=lax.Precision.HIGHEST) + carry_row[0:1, 0:H]
        for h in range(H):
            ccol_ref[h] = c_tok[:, h:h + 1] * LOG2E
        carry_row[0:1, 0:H] = c_tok[tm - 1:tm, :]

    out_shape = [jax.ShapeDtypeStruct((S, Q_LORA), BF16), jax.ShapeDtypeStruct((S, KV_LORA), BF16)]
    out_specs = [_tok_spec(tm, Q_LORA), _tok_spec(tm, KV_LORA)]

    def add(heads, d, lanes):
        out_shape.append(jax.ShapeDtypeStruct((heads, d, S) if lanes else (heads, S, d), BF16))
        out_specs.append(_lane_spec(heads, d, tm) if lanes else _head_spec(heads, tm, d))

    for heads_q, heads_kv, dqk, dv in ((H, H, QK_MLA, VDIM), (H, SWA_KV, HD, HD), (H, H, HD, HD)):
        add(heads_q, dqk, True)
        add(heads_kv, dqk, False)
        add(heads_kv, dqk, True)
        add(heads_kv, dv, False)
        add(heads_kv, dv, True)
    out_shape.append(jax.ShapeDtypeStruct((H, S, 1), F32))
    out_specs.append(_head_spec(H, tm, 1))
    in_specs = [_tok_spec(tm, P_COLS), _tok_spec(tm, ROPE), _tok_spec(tm, ROPE), _full_spec((1, Q_LORA)),
                _full_spec((1, KV_LORA)), _full_spec((1, H)),
                pl.BlockSpec((None,) + wqb.shape[1:], lambda i: (l, 0, 0)),
                pl.BlockSpec((None,) + wkvb.shape[1:], lambda i: (l, 0, 0))]
    return pl.pallas_call(
        body, name="mixer_prep", grid=(S // tm,), in_specs=in_specs, out_specs=out_specs, out_shape=out_shape,
        scratch_shapes=[pltpu.VMEM((8, 128), F32)],
        compiler_params=_cparams("arbitrary"))(p, c2, s2, q_norm, kv_norm, bias, wqb, wkvb)


def mixer_prep_bwd(p, c2, s2, q_norm, kv_norm, bias, wqb, wkvb, l, dqm, dkm, dvm, dqs, dks, dvs, dqf, dkf, dvf, dc):
    S = p.shape[0]
    tm = _tile(S, 256)
    nt = S // tm
    H, G = N_HEADS, N_HEADS // SWA_KV

    def body(p_ref, c2_ref, s2_ref, qn_ref, kvn_ref, b_ref, wqb_ref, wkvb_ref,
             dqm_ref, dkm_ref, dvm_ref, dqs_ref, dks_ref, dvs_ref, dqf_ref, dkf_ref, dvf_ref, dc_ref,
             dp_ref, dq_ref, dkv_ref, dqn_ref, dkvn_ref, db_ref, carry):
        c2, s2 = c2_ref[...], -s2_ref[...]

        @pl.when(pl.program_id(0) == 0)
        def _():
            dqn_ref[...] = jnp.zeros_like(dqn_ref)
            dkvn_ref[...] = jnp.zeros_like(dkvn_ref)
            db_ref[...] = jnp.zeros_like(db_ref)
            carry[...] = jnp.zeros_like(carry)

        sc_mla, sc_hd = QK_MLA ** -0.5, HD ** -0.5
        dk_pe_t = jnp.zeros((ROPE, tm), F32)
        for h in range(H):
            dq_ref[:, h * QK_MLA:h * QK_MLA + NOPE] = (dqm_ref[h, 0:NOPE, :] * sc_mla).T.astype(BF16)
            dq_ref[:, h * QK_MLA + NOPE:(h + 1) * QK_MLA] = _rope(
                (dqm_ref[h, NOPE:QK_MLA, :] * sc_mla).T, c2, s2).astype(BF16)
            dkv_ref[:, h * 256:h * 256 + NOPE] = (dkm_ref[h, 0:NOPE, :] * LN2).T.astype(BF16)
            dkv_ref[:, h * 256 + NOPE:(h + 1) * 256] = dvm_ref[h].T.astype(BF16)
            dk_pe_t = dk_pe_t + dkm_ref[h, NOPE:QK_MLA, :]
        dk_pe = (dk_pe_t * LN2).T

        def through_norm(dcn, c, gain, dgain_ref):
            c_hat, r = _norm_hat(c)
            dhg = dcn * gain
            dgain_ref[...] += jnp.sum(dcn * c_hat, axis=0, keepdims=True)
            return r * (dhg - c_hat * jnp.mean(dhg * c_hat, axis=-1, keepdims=True))

        dcqn = _dot_nt(dq_ref[...], wqb_ref[...])
        dckvn = _dot_nt(dkv_ref[...], wkvb_ref[...])
        dp_ref[:, P_CQ:P_CQ + Q_LORA] = through_norm(dcqn, p_ref[:, P_CQ:P_CQ + Q_LORA], qn_ref[...], dqn_ref).astype(BF16)
        dp_ref[:, P_CKV:P_CKV + KV_LORA] = through_norm(
            dckvn, p_ref[:, P_CKV:P_CKV + KV_LORA], kvn_ref[...], dkvn_ref).astype(BF16)
        for h in range(H):
            dp_ref[:, P_QS + h * HD:P_QS + (h + 1) * HD] = _rope((dqs_ref[h] * sc_hd).T, c2, s2).astype(BF16)
            dp_ref[:, P_QF + h * HD:P_QF + (h + 1) * HD] = (dqf_ref[h] * sc_hd).T.astype(BF16)
            dp_ref[:, P_KF + h * HD:P_KF + (h + 1) * HD] = (dkf_ref[h] * LN2).T.astype(BF16)
            dp_ref[:, P_VF + h * HD:P_VF + (h + 1) * HD] = dvf_ref[h].T.astype(BF16)
        for kvh in range(SWA_KV):
            dk = dks_ref[kvh * G]
            dv = dvs_ref[kvh * G]
            for g in range(1, G):
                dk = dk + dks_ref[kvh * G + g]
                dv = dv + dvs_ref[kvh * G + g]
            dp_ref[:, P_KS + kvh * HD:P_KS + (kvh + 1) * HD] = _rope((dk * LN2).T, c2, s2).astype(BF16)
            dp_ref[:, P_VS + kvh * HD:P_VS + (kvh + 1) * HD] = dv.T.astype(BF16)

        dcv = dc_ref[...]
        dlog_f = jnp.dot(_tri(tm, upper=True), dcv, preferred_element_type=F32,
                         precision=lax.Precision.HIGHEST) + carry[0:1, 0:H]
        carry[0:1, 0:H] = dlog_f[0:1, :]
        df = dlog_f * _sigmoid(-(p_ref[:, P_F:P_F + H] + b_ref[...]))
        db_ref[...] += jnp.sum(df, axis=0, keepdims=True)
        dp_ref[:, P_KR:P_COLS] = jnp.zeros((tm, P_COLS - P_KR), BF16)
        dp_ref[:, P_KR:P_KR + ROPE] = _rope(dk_pe, c2, s2).astype(BF16)
        dp_ref[:, P_F:P_F + H] = df.astype(BF16)

    rev = nt
    in_specs = [_tok_spec(tm, P_COLS, rev), _tok_spec(tm, ROPE, rev), _tok_spec(tm, ROPE, rev), _full_spec((1, Q_LORA)),
                _full_spec((1, KV_LORA)), _full_spec((1, H)),
                pl.BlockSpec((None,) + wqb.shape[1:], lambda i: (l, 0, 0)),
                pl.BlockSpec((None,) + wkvb.shape[1:], lambda i: (l, 0, 0)),
                _lane_spec(H, QK_MLA, tm, rev), _lane_spec(H, QK_MLA, tm, rev), _lane_spec(H, VDIM, tm, rev)]
    in_specs += [_lane_spec(H, HD, tm, rev)] * 6 + [_tok_spec(tm, H, rev)]
    out_specs = [_tok_spec(tm, P_COLS, rev), _tok_spec(tm, N_HEADS * QK_MLA, rev), _tok_spec(tm, N_HEADS * 256, rev),
                 _full_spec((1, Q_LORA)), _full_spec((1, KV_LORA)), _full_spec((1, H))]
    out_shape = [jax.ShapeDtypeStruct((S, P_COLS), BF16), jax.ShapeDtypeStruct((S, N_HEADS * QK_MLA), BF16),
                 jax.ShapeDtypeStruct((S, N_HEADS * 256), BF16), jax.ShapeDtypeStruct((1, Q_LORA), F32),
                 jax.ShapeDtypeStruct((1, KV_LORA), F32), jax.ShapeDtypeStruct((1, H), F32)]
    return pl.pallas_call(
        body, name="mixer_prep_bwd", grid=(nt,), in_specs=in_specs, out_specs=out_specs, out_shape=out_shape,
        scratch_shapes=[pltpu.VMEM((8, 128), F32)], compiler_params=_cparams("arbitrary"))(
            p, c2, s2, q_norm, kv_norm, bias, wqb, wkvb, dqm, dkm, dvm, dqs, dks, dvs, dqf, dkf, dvf, dc)


def merge_heads(o_mla, o_swa, o_fox):
    H, S, _ = o_mla.shape
    tm = _tile(S, 512)
    width = H * (VDIM + 2 * HD)

    def body(om_ref, os_ref, of_ref, m_ref):
        for h in range(H):
            m_ref[:, h * VDIM:(h + 1) * VDIM] = om_ref[h]
            m_ref[:, H * VDIM + h * HD:H * VDIM + (h + 1) * HD] = os_ref[h]
            m_ref[:, H * (VDIM + HD) + h * HD:H * (VDIM + HD) + (h + 1) * HD] = of_ref[h]

    return pl.pallas_call(
        body, name="merge_heads", grid=(S // tm,),
        in_specs=[_head_spec(H, tm, VDIM), _head_spec(H, tm, HD), _head_spec(H, tm, HD)],
        out_specs=_tok_spec(tm, width), out_shape=jax.ShapeDtypeStruct((S, width), BF16),
        compiler_params=_cparams("arbitrary"))(o_mla, o_swa, o_fox)


def split_heads(dmixed, mixed):
    S, width = mixed.shape
    H = N_HEADS
    tm = _tile(S, 512)

    def body(dm_ref, m_ref, dom_ref, dos_ref, dof_ref, dm_delta, ds_delta, df_delta):
        def one(h, off, d, do_ref, delta_ref):
            dv = dm_ref[:, off:off + d].astype(F32)
            do_ref[h] = dv.T.astype(BF16)
            prod = dv * m_ref[:, off:off + d].astype(F32)
            rows = lax.dot_general(jnp.ones((8, d), F32), prod, (((1,), (1,)), ((), ())),
                                   preferred_element_type=F32, precision=lax.Precision.HIGHEST)
            delta_ref[h] = rows[0:1, :]

        for h in range(H):
            one(h, h * VDIM, VDIM, dom_ref, dm_delta)
            one(h, H * VDIM + h * HD, HD, dos_ref, ds_delta)
            one(h, H * (VDIM + HD) + h * HD, HD, dof_ref, df_delta)

    row_spec = pl.BlockSpec((H, 1, tm), lambda i: (0, 0, i))
    row_shape = jax.ShapeDtypeStruct((H, 1, S), F32)
    return pl.pallas_call(
        body, name="split_heads", grid=(S // tm,),
        in_specs=[_tok_spec(tm, width), _tok_spec(tm, width)],
        out_specs=[_lane_spec(H, VDIM, tm), _lane_spec(H, HD, tm), _lane_spec(H, HD, tm), row_spec, row_spec, row_spec],
        out_shape=[jax.ShapeDtypeStruct((H, VDIM, S), BF16), jax.ShapeDtypeStruct((H, HD, S), BF16),
                   jax.ShapeDtypeStruct((H, HD, S), BF16), row_shape, row_shape, row_shape],
        compiler_params=_cparams("arbitrary"))(dmixed, mixed)


def _attn_tile(S):
    return 512 if (S % 512 == 0 and S > 512) else S // 2


def _valid(q0, k0, shape, q_axis, window):
    qpos = q0 + lax.broadcasted_iota(jnp.int32, shape, q_axis)
    kpos = k0 + lax.broadcasted_iota(jnp.int32, shape, 1 - q_axis)
    ok = kpos <= qpos
    if window is not None:
        ok = jnp.logical_and(ok, kpos > qpos - window)
    return ok


def attn_fwd(name, qt, k, vt, window=None, sinks=None, ccol=None):
    H, dq, S = qt.shape
    Hk, dv, _ = vt.shape
    G = H // Hk
    t = _attn_tile(S)
    fox, use_sink = ccol is not None, sinks is not None

    def body(*refs):
        refs = list(refs)
        sink_ref = refs.pop(0) if use_sink else None
        q_ref, k_ref, vt_ref = refs[:3]
        refs = refs[3:]
        ccol_ref = refs.pop(0) if fox else None
        o_ref, lse_ref, m_ref, l_ref, acc_ref = refs
        h, i = pl.program_id(0), pl.program_id(1)
        qv = q_ref[...]
        if use_sink:
            m_ref[...] = jnp.full(m_ref.shape, sink_ref[h] * LOG2E, F32)
            l_ref[...] = jnp.ones(l_ref.shape, F32)
        else:
            m_ref[...] = jnp.full(m_ref.shape, NEG, F32)
            l_ref[...] = jnp.zeros(l_ref.shape, F32)
        acc_ref[...] = jnp.zeros(acc_ref.shape, F32)

        def steps(blocks):
            offs = [pl.multiple_of(j * t, t) for j, _ in blocks]
            scores = [_dot(k_ref[pl.ds(off, t), :], qv) for off in offs]
            for (j, masked), off, st in zip(blocks, offs, scores):
                if fox:
                    st = st - ccol_ref[pl.ds(off, t), :]
                if masked:
                    st = jnp.where(_valid(i * t, j * t, (t, t), 1, window), st, NEG)
                m_prev = m_ref[...]
                m_new = jnp.maximum(m_prev, jnp.max(st, axis=0, keepdims=True))
                alpha = jnp.exp2(m_prev - m_new)
                pt = jnp.exp2(st - m_new)
                l_ref[...] = alpha * l_ref[...] + jnp.sum(pt, axis=0, keepdims=True)
                acc_ref[...] = alpha * acc_ref[...] + _dot(vt_ref[:, pl.ds(off, t)], pt.astype(BF16))
                m_ref[...] = m_new

        if window is None:
            def pair(n, carry):
                steps([(2 * n, False), (2 * n + 1, False)])
                return carry
            lax.fori_loop(0, i // 2, pair, 0)

            @pl.when(i % 2 == 1)
            def _():
                steps([(i - 1, False), (i, True)])

            @pl.when(i % 2 == 0)
            def _():
                steps([(i, True)])
        else:
            def one(j, carry):
                steps([(j, True)])
                return carry
            lax.fori_loop(jnp.maximum(i * t - (window - 1), 0) // t, i + 1, one, 0)
        l = l_ref[...]
        o_ref[...] = (acc_ref[...] / l).T.astype(BF16)
        lse_ref[...] = m_ref[...] + jnp.log2(l)

    in_specs, args = [], []
    if use_sink:
        in_specs.append(pl.BlockSpec(memory_space=pltpu.SMEM))
        args.append(sinks)
    in_specs += [pl.BlockSpec((None, dq, t), lambda h, i: (h, 0, i)),
                 pl.BlockSpec((None, S, dq), lambda h, i: (h // G, 0, 0)),
                 pl.BlockSpec((None, dv, S), lambda h, i: (h // G, 0, 0))]
    args += [qt, k, vt]
    if fox:
        in_specs.append(pl.BlockSpec((None, S, 1), lambda h, i: (h, 0, 0)))
        args.append(ccol)
    return pl.pallas_call(
        body, name=name, grid=(H, S // t), in_specs=in_specs,
        out_specs=[pl.BlockSpec((None, t, dv), lambda h, i: (h, i, 0)), pl.BlockSpec((None, 1, t), lambda h, i: (h, 0, i))],
        out_shape=[jax.ShapeDtypeStruct((H, S, dv), BF16), jax.ShapeDtypeStruct((H, 1, S), F32)],
        scratch_shapes=[pltpu.VMEM((1, t), F32), pltpu.VMEM((1, t), F32), pltpu.VMEM((dv, t), F32)],
        compiler_params=_cparams("arbitrary", "arbitrary"))(*args)


def attn_bwd(name, qt, k, kt, v, dot, lse, delta, window=None, sinks=None, ccol=None):
    H, dq, S = qt.shape
    Hk, _, dv = v.shape
    G = H // Hk
    t = _attn_tile(S)
    nq = S // t
    fox, use_sink = ccol is not None, sinks is not None

    def body(*refs):
        refs = list(refs)
        sink_ref = refs.pop(0) if use_sink else None
        qt_ref, k_ref, kt_ref, v_ref, dot_ref, lse_ref, delta_ref = refs[:7]
        refs = refs[7:]
        ccol_ref = refs.pop(0) if fox else None
        dqt_ref, dkt_ref, dvt_ref = refs[:3]
        refs = refs[3:]
        dcq_ref, dck_ref = (refs.pop(0), refs.pop(0)) if fox else (None, None)
        dsink_ref = refs.pop(0) if use_sink else None
        h, j = pl.program_id(0), pl.program_id(1)

        @pl.when(j == 0)
        def _():
            dqt_ref[...] = jnp.zeros(dqt_ref.shape, F32)
            if fox:
                dcq_ref[...] = jnp.zeros(dcq_ref.shape, F32)
            if use_sink:
                ps = jnp.exp2(sink_ref[h] * LOG2E - lse_ref[...]) * delta_ref[...]
                dsink_ref[...] = jnp.broadcast_to(-jnp.sum(ps, axis=-1, keepdims=True), dsink_ref.shape)

        dkt_ref[...] = jnp.zeros(dkt_ref.shape, F32)
        dvt_ref[...] = jnp.zeros(dvt_ref.shape, F32)
        if fox:
            dck_ref[...] = jnp.zeros(dck_ref.shape, F32)
        kv, ktv, vv = k_ref[...], kt_ref[...], v_ref[...]

        def steps(blocks):
            offs = [pl.multiple_of(i * t, t) for i, _ in blocks]
            qts = [qt_ref[:, pl.ds(off, t)] for off in offs]
            dots = [dot_ref[:, pl.ds(off, t)] for off in offs]
            scores = [_dot(kv, qti) for qti in qts]
            dprobs = [_dot(vv, doti) for doti in dots]
            for (i, masked), off, qti, doti, st, dpt in zip(blocks, offs, qts, dots, scores, dprobs):
                if fox:
                    st = st - ccol_ref[...]
                if masked:
                    st = jnp.where(_valid(i * t, j * t, (t, t), 1, window), st, NEG)
                pt = jnp.exp2(st - lse_ref[:, pl.ds(off, t)])
                dvt_ref[...] += _dot_nt(doti, pt.astype(BF16))
                dst = pt * (dpt - delta_ref[:, pl.ds(off, t)])
                if fox:
                    dcq_ref[:, pl.ds(off, t)] += jnp.sum(dst, axis=0, keepdims=True)
                    dck_ref[...] -= jnp.sum(dst, axis=1, keepdims=True)
                dsb = dst.astype(BF16)
                dkt_ref[...] += _dot_nt(qti, dsb)
                dqt_ref[:, pl.ds(off, t)] += _dot(ktv, dsb)

        if window is None:
            odd = (nq - 1 - j) % 2

            @pl.when(odd == 1)
            def _():
                steps([(j, True), (j + 1, False)])

            @pl.when(odd == 0)
            def _():
                steps([(j, True)])

            first = j + 1 + odd

            def pair(n, carry):
                steps([(first + 2 * n, False), (first + 2 * n + 1, False)])
                return carry
            lax.fori_loop(0, (nq - first) // 2, pair, 0)
        else:
            def one(i, carry):
                steps([(i, True)])
                return carry
            lax.fori_loop(j, jnp.minimum((j * t + t - 1 + window - 1) // t, nq - 1) + 1, one, 0)

    in_specs, args = [], []
    if use_sink:
        in_specs.append(pl.BlockSpec(memory_space=pltpu.SMEM))
        args.append(sinks)
    whole = lambda d: pl.BlockSpec((None, d, S), lambda h, j: (h, 0, 0))
    keys = lambda d: pl.BlockSpec((None, d, t), lambda h, j: (h, 0, j))
    row = pl.BlockSpec((None, 1, S), lambda h, j: (h, 0, 0))
    in_specs += [whole(dq), pl.BlockSpec((None, t, dq), lambda h, j: (h // G, j, 0)),
                 pl.BlockSpec((None, dq, t), lambda h, j: (h // G, 0, j)),
                 pl.BlockSpec((None, t, dv), lambda h, j: (h // G, j, 0)), whole(dv), row, row]
    args += [qt, k, kt, v, dot, lse, delta]
    out_specs = [whole(dq), keys(dq), keys(dv)]
    out_shape = [jax.ShapeDtypeStruct((H, dq, S), F32), jax.ShapeDtypeStruct((H, dq, S), F32),
                 jax.ShapeDtypeStruct((H, dv, S), F32)]
    if fox:
        in_specs.append(pl.BlockSpec((None, t, 1), lambda h, j: (h, j, 0)))
        args.append(ccol)
        out_specs += [row, pl.BlockSpec((None, t, 1), lambda h, j: (h, j, 0))]
        out_shape += [jax.ShapeDtypeStruct((H, 1, S), F32), jax.ShapeDtypeStruct((H, S, 1), F32)]
    if use_sink:
        out_specs.append(pl.BlockSpec((None, 1, 128), lambda h, j: (h, 0, 0)))
        out_shape.append(jax.ShapeDtypeStruct((H, 1, 128), F32))
    return pl.pallas_call(
        body, name=name, grid=(H, nq), in_specs=in_specs, out_specs=out_specs, out_shape=out_shape,
        compiler_params=_cparams("arbitrary", "arbitrary"))(*args)


def _rows_tile(rows):
    for tr in (256, 128, 64, 32, 16, 8):
        if rows % tr == 0:
            return tr
    return rows


def adamw(w, g, m, v):
    L, R, C = w.shape
    tr = _rows_tile(R)

    def body(w_ref, g_ref, m_ref, v_ref, g_out, d_ref, nm_ref, nv_ref):
        gv = g_ref[...]
        mn = ADAM_B1 * m_ref[...] + (1.0 - ADAM_B1) * gv
        vn = ADAM_B2 * v_ref[...] + (1.0 - ADAM_B2) * (gv * gv)
        m_hat = mn / (1.0 - ADAM_B1 ** ADAM_STEP)
        v_hat = vn / (1.0 - ADAM_B2 ** ADAM_STEP)
        d_ref[...] = -ADAM_LR * (m_hat / (jnp.sqrt(v_hat) + ADAM_EPS) + ADAM_WD * w_ref[...])
        nm_ref[...] = mn
        nv_ref[...] = vn
        g_out[...] = gv

    spec = pl.BlockSpec((None, tr, C), lambda l, i: (l, i, 0))
    shape = jax.ShapeDtypeStruct((L, R, C), F32)
    return pl.pallas_call(
        body, name="adamw", grid=(L, R // tr), in_specs=[spec] * 4, out_specs=[spec] * 4, out_shape=[shape] * 4,
        compiler_params=_cparams("arbitrary", "arbitrary"))(w, g, m, v)


def place_own(w, chip):
    L, R, C = w.shape
    tr = _rows_tile(R)

    def body(c_ref, w_ref, o_ref):
        o_ref[...] = w_ref[...].astype(BF16)

    return pl.pallas_call(
        body, name="place_own",
        grid_spec=pltpu.PrefetchScalarGridSpec(
            num_scalar_prefetch=1, grid=(L, R // tr),
            in_specs=[pl.BlockSpec((None, tr, C), lambda l, i, c: (l, i, 0))],
            out_specs=pl.BlockSpec((None, None, tr, C), lambda l, i, c: (l, c[0], i, 0))),
        out_shape=jax.ShapeDtypeStruct((L, N_CHIPS, R, C), BF16),
        compiler_params=_cparams("arbitrary", "arbitrary"))(chip, w)


def sum_pair(grad, other, layer):
    _, R, C = grad.shape
    tr = _rows_tile(R)

    def body(l_ref, g_ref, o_ref, out_ref):
        out_ref[...] = (g_ref[...].astype(F32) + o_ref[...].astype(F32)).astype(BF16)

    return pl.pallas_call(
        body, name="sum_pair",
        grid_spec=pltpu.PrefetchScalarGridSpec(
            num_scalar_prefetch=1, grid=(R // tr,),
            in_specs=[pl.BlockSpec((None, tr, C), lambda i, l: (l[0], i, 0)), pl.BlockSpec((tr, C), lambda i, l: (i, 0))],
            out_specs=pl.BlockSpec((tr, C), lambda i, l: (i, 0))),
        out_shape=jax.ShapeDtypeStruct((R, C), BF16), compiler_params=_cparams("arbitrary"))(layer, grad, other)


def sum_chips(part, recv, chip, layer):
    _, R, C = part.shape
    tr = _rows_tile(R)

    def body(c_ref, l_ref, p_ref, r_ref, out_ref):
        acc = p_ref[...].astype(F32)
        for k in range(N_CHIPS - 1):
            acc = acc + r_ref[k].astype(F32)
        out_ref[...] = acc

    return pl.pallas_call(
        body, name="sum_chips",
        grid_spec=pltpu.PrefetchScalarGridSpec(
            num_scalar_prefetch=2, grid=(R // tr,),
            in_specs=[pl.BlockSpec((None, tr, C), lambda i, c, l: (c[0], i, 0)),
                      pl.BlockSpec((N_CHIPS - 1, tr, C), lambda i, c, l: (0, i, 0))],
            out_specs=pl.BlockSpec((None, tr, C), lambda i, c, l: (l[0], i, 0))),
        out_shape=jax.ShapeDtypeStruct((2, R, C), F32), compiler_params=_cparams("arbitrary"))(chip, layer, part, recv)


_ANY = pl.BlockSpec(memory_space=pl.ANY)


def _place():
    x, y, c = lax.axis_index("x"), lax.axis_index("y"), lax.axis_index("c")
    chips = [(1 - x, y), (x, 1 - y), (1 - x, 1 - y)]
    return x, y, c, chips


def all_gather_shards(ws):
    n = len(ws)

    def body(*refs):
        w, o = refs[:n], refs[n:2 * n]
        send, recv = refs[2 * n:]
        x, y, c, chips = _place()
        me, sib = 2 * x + y, (x, y, 1 - c)

        def remote(t, k, layer, shard, to, src=None):
            blk = o[t].at[layer, shard]
            return pltpu.make_async_remote_copy(src_ref=blk if src is None else src, dst_ref=blk, send_sem=send.at[t, k],
                                                recv_sem=recv.at[t, k], device_id=to, device_id_type=MESH)

        first = [remote(t, k, c, me, (*chip, c), src=w[t].at[c, me]) for t in range(n) for k, chip in enumerate(chips)]
        for cp in first:
            cp.start()
        passed = []
        for t in range(n):
            for k, chip in enumerate(chips):
                remote(t, k, c, 2 * chip[0] + chip[1], (x, y, c)).wait_recv()
                passed.append(remote(t, 3 + k, c, 2 * chip[0] + chip[1], sib))
                passed[-1].start()
        for t in range(n):
            for k, chip in enumerate(chips):
                remote(t, 3 + k, 1 - c, 2 * chip[0] + chip[1], (x, y, c)).wait_recv()
        for cp in first + passed:
            cp.wait_send()

    return pl.pallas_call(
        body, name="all_gather_shards", in_specs=[_ANY] * n, out_specs=[_ANY] * n,
        out_shape=[jax.ShapeDtypeStruct(w.shape, w.dtype) for w in ws], input_output_aliases={t: t for t in range(n)},
        scratch_shapes=[pltpu.SemaphoreType.DMA((n, 6)), pltpu.SemaphoreType.DMA((n, 6))],
        compiler_params=pltpu.CompilerParams(has_side_effects=True))(*ws)


def pair_exchange(gs):
    n = len(gs)

    def body(*refs):
        g, o = refs[:n], refs[n:2 * n]
        send, recv = refs[2 * n:]
        x, y, c, _ = _place()
        cps = [pltpu.make_async_remote_copy(src_ref=g[t].at[1 - c], dst_ref=o[t], send_sem=send.at[t], recv_sem=recv.at[t],
                                            device_id=(x, y, 1 - c), device_id_type=MESH) for t in range(n)]
        for cp in cps:
            cp.start()
        for cp in cps:
            cp.wait()

    return pl.pallas_call(
        body, name="pair_exchange", in_specs=[_ANY] * n, out_specs=[_ANY] * n,
        out_shape=[jax.ShapeDtypeStruct(g.shape[1:], g.dtype) for g in gs],
        scratch_shapes=[pltpu.SemaphoreType.DMA((n,)), pltpu.SemaphoreType.DMA((n,))],
        compiler_params=pltpu.CompilerParams(has_side_effects=True))(*gs)


def chip_scatter(ps):
    n = len(ps)

    def body(*refs):
        p, o = refs[:n], refs[n:2 * n]
        send, recv = refs[2 * n:]
        x, y, c, chips = _place()
        cps = [pltpu.make_async_remote_copy(src_ref=p[t].at[2 * chip[0] + chip[1]], dst_ref=o[t].at[k],
                                            send_sem=send.at[t, k], recv_sem=recv.at[t, k], device_id=(*chip, c),
                                            device_id_type=MESH)
               for t in range(n) for k, chip in enumerate(chips)]
        for cp in cps:
            cp.start()
        for cp in cps:
            cp.wait()

    return pl.pallas_call(
        body, name="chip_scatter", in_specs=[_ANY] * n, out_specs=[_ANY] * n,
        out_shape=[jax.ShapeDtypeStruct((N_CHIPS - 1,) + p.shape[1:], p.dtype) for p in ps],
        scratch_shapes=[pltpu.SemaphoreType.DMA((n, 3)), pltpu.SemaphoreType.DMA((n, 3))],
        compiler_params=pltpu.CompilerParams(has_side_effects=True))(*ps)


def pair_share(rs):
    n = len(rs)

    def body(*refs):
        r, o = refs[:n], refs[n:2 * n]
        send, recv = refs[2 * n:]
        x, y, c, _ = _place()

        def remote(t, layer):
            return pltpu.make_async_remote_copy(src_ref=r[t].at[layer], dst_ref=o[t].at[layer], send_sem=send.at[t],
                                                recv_sem=recv.at[t], device_id=(x, y, 1 - c), device_id_type=MESH)

        for t in range(n):
            remote(t, c).start()
        for t in range(n):
            remote(t, 1 - c).wait_recv()
            remote(t, c).wait_send()

    return pl.pallas_call(
        body, name="pair_share", in_specs=[_ANY] * n, out_specs=[_ANY] * n,
        out_shape=[jax.ShapeDtypeStruct(r.shape, r.dtype) for r in rs], input_output_aliases={t: t for t in range(n)},
        scratch_shapes=[pltpu.SemaphoreType.DMA((n,)), pltpu.SemaphoreType.DMA((n,))],
        compiler_params=pltpu.CompilerParams(has_side_effects=True))(*rs)


def all_reduce_small(buf):
    def body(x_ref, o_ref, land, send, recv):
        x, y, c, _ = _place()
        me = 4 * x + 2 * y + c
        land[me] = x_ref[...]
        cps = []
        for mask in range(1, N_DEV):
            px = 1 - x if mask & 4 else x
            py = 1 - y if mask & 2 else y
            pc = 1 - c if mask & 1 else c
            cps.append(pltpu.make_async_remote_copy(src_ref=x_ref, dst_ref=land.at[me], send_sem=send.at[mask - 1],
                                                    recv_sem=recv.at[mask - 1], device_id=(px, py, pc), device_id_type=MESH))
            cps[-1].start()
        for mask in range(1, N_DEV):
            px = 1 - x if mask & 4 else x
            py = 1 - y if mask & 2 else y
            pc = 1 - c if mask & 1 else c
            pltpu.make_async_remote_copy(src_ref=x_ref, dst_ref=land.at[4 * px + 2 * py + pc], send_sem=send.at[mask - 1],
                                         recv_sem=recv.at[mask - 1], device_id=(px, py, pc), device_id_type=MESH).wait_recv()
        for cp in cps:
            cp.wait_send()
        acc = land[0]
        for d in range(1, N_DEV):
            acc = acc + land[d]
        o_ref[...] = acc

    vm = pl.BlockSpec(memory_space=pltpu.VMEM)
    return pl.pallas_call(
        body, name="all_reduce_small", in_specs=[vm], out_specs=vm, out_shape=jax.ShapeDtypeStruct(buf.shape, F32),
        scratch_shapes=[pltpu.VMEM((N_DEV,) + buf.shape, F32), pltpu.SemaphoreType.DMA((N_DEV - 1,)),
                        pltpu.SemaphoreType.DMA((N_DEV - 1,))])(buf)


def _ffn_fwd(x, h, wg, wu, wd, l, next_gain):
    g, u, a = gate_up(h, wg, wu, l)
    y, h_next = down_proj(a, wd, l, x, next_gain)
    return y, h_next, (h, g, u, a)


def _ffn_bwd(dy, x, gain, wg, wu, wd, l, saved, grads):
    h, g, u, a = saved
    D = x.shape[1]
    L = wg.shape[0]
    Fs = g.shape[1] // N_CHIPS
    dg, du = down_bwd(dy, wd, l, g, u)
    dwg = mm_tn(h, dg, _tile(D, 1024), Fs, blocked=True, layer=l, layers=L, into=grads[0])
    dwu = mm_tn(h, du, _tile(D, 1024), Fs, blocked=True, layer=l, layers=L, into=grads[1])
    dwd = mm_tn(a, dy, Fs, _tile(D, 1024), scale=0.5, layer=l, layers=L, into=grads[2])
    dh = gate_up_bwd(dg, du, wg, wu, l)
    dx, dgain = rms_bwd(dh, x, gain, dy)
    return dx, dgain, (dwg, dwu, dwd)


def _mixer_fwd(x, h, win, q_norm, kv_norm, sinks, bias, wqb, wkvb, wout, c2, s2, l, next_gain):
    D = x.shape[1]
    p = mm_nn(h, win, l, 640, F32)
    (cqn, ckvn, qmt, km, kmt, vm, vmt, qst, ks, kst, vs, vst, qft, kf, kft, vf, vft, ccol) = mixer_prep(
        p, c2, s2, q_norm, kv_norm, bias, wqb, wkvb, l)
    o_mla, lse_mla = attn_fwd("attn_mla", qmt, km, vmt)
    o_swa, lse_swa = attn_fwd("attn_swa", qst, ks, vst, window=WINDOW, sinks=sinks)
    o_fox, lse_fox = attn_fwd("attn_fox", qft, kf, vft, ccol=ccol)
    mixed = merge_heads(o_mla, o_swa, o_fox)
    y, h_next = mm_nn(mixed, wout, l, D, F32, resid=x, next_gain=next_gain)
    saved = (h, p, cqn, ckvn, qmt, km, kmt, vm, qst, ks, kst, vs, qft, kf, kft, vf, ccol, lse_mla, lse_swa, lse_fox, mixed)
    return y, h_next, saved


def _mixer_bwd(dy, x, gain, win, q_norm, kv_norm, sinks, bias, wqb, wkvb, wout, c2, s2, l, saved, dwout_so_far):
    (h, p, cqn, ckvn, qmt, km, kmt, vm, qst, ks, kst, vs, qft, kf, kft, vf, ccol, lse_mla, lse_swa, lse_fox, mixed) = saved
    S, D = x.shape
    width = mixed.shape[1]
    dmixed = mm_nt(dy, wout, l, _tile(width, 1024), BF16)
    dwout = mm_tn(mixed, dy, _tile(width, 1024), _tile(D, 1024), layer=l, layers=wout.shape[0], into=dwout_so_far)
    do_mla, do_swa, do_fox, dl_mla, dl_swa, dl_fox = split_heads(dmixed, mixed)
    dqm, dkm, dvm = attn_bwd("attn_mla_bwd", qmt, km, kmt, vm, do_mla, lse_mla, dl_mla)
    dqs, dks, dvs, dsink = attn_bwd("attn_swa_bwd", qst, ks, kst, vs, do_swa, lse_swa, dl_swa, window=WINDOW, sinks=sinks)
    dqf, dkf, dvf, dcq, dck = attn_bwd("attn_fox_bwd", qft, kf, kft, vf, do_fox, lse_fox, dl_fox, ccol=ccol)
    dc = dcq.reshape(N_HEADS, S).T + dck.reshape(N_HEADS, S).T
    dp, dq, dkv, dqn, dkvn, dbias = mixer_prep_bwd(p, c2, s2, q_norm, kv_norm, bias, wqb, wkvb, l, dqm, dkm, dvm, dqs, dks,
                                                   dvs, dqf, dkf, dvf, dc)
    dwqb = mm_tn(cqn, dq, Q_LORA, N_HEADS * QK_MLA)
    dwkvb = mm_tn(ckvn, dkv, KV_LORA, 1024)
    dwin = mm_tn(h, dp, _tile(D, 1024), 640)
    dh = mm_nt(dp, win, l, _tile(D, 1024), F32)
    dx, dgain = rms_bwd(dh, x, gain, dy)
    return dx, dgain, dwin, dqn, dwqb, dkvn, dwkvb, dsink[:, 0, 0], dbias[0], dwout


def _pad_in_cols(w):
    pad = jnp.zeros(w.shape[:-1] + (P_COLS - IN_COLS,), w.dtype)
    return jnp.concatenate([w[..., :IN_KR], w[..., IN_KR + ROPE:IN_COLS - N_HEADS], w[..., IN_KR:IN_KR + ROPE],
                            w[..., IN_COLS - N_HEADS:], pad], axis=-1)


def _unpad_in_cols(w):
    return jnp.concatenate([w[..., :IN_KR], w[..., P_KR:P_KR + ROPE], w[..., IN_KR:P_KR], w[..., P_F:P_F + N_HEADS]], axis=-1)


def _col_shards(w):
    R = w.shape[0]
    return w.reshape(R, N_CHIPS, -1).transpose(1, 0, 2)


def _from_col_shards(w):
    L, _, R, C = w.shape
    return w.transpose(0, 2, 1, 3).reshape(L, R, N_CHIPS * C)


def kernel(x, positions, ffn1_norm, ffn1_w_gate, ffn1_w_up, ffn1_w_down, mix_norm, w_in, mla_q_norm, mla_w_q_b, mla_kv_norm, mla_w_kv_b, swa_sinks, fox_forget_bias, w_out, ffn2_norm, ffn2_w_gate, ffn2_w_up, ffn2_w_down, final_norm, loss_target, m_ffn1_norm, m_ffn1_w_gate, m_ffn1_w_up, m_ffn1_w_down, m_mix_norm, m_w_in, m_mla_q_norm, m_mla_w_q_b, m_mla_kv_norm, m_mla_w_kv_b, m_swa_sinks, m_fox_forget_bias, m_w_out, m_ffn2_norm, m_ffn2_w_gate, m_ffn2_w_up, m_ffn2_w_down, m_final_norm, v_ffn1_norm, v_ffn1_w_gate, v_ffn1_w_up, v_ffn1_w_down, v_mix_norm, v_w_in, v_mla_q_norm, v_mla_w_q_b, v_mla_kv_norm, v_mla_w_kv_b, v_swa_sinks, v_fox_forget_bias, v_w_out, v_ffn2_norm, v_ffn2_w_gate, v_ffn2_w_up, v_ffn2_w_down, v_final_norm):
    L = ffn1_norm.shape[0]
    S, D = x.shape[1], x.shape[2]
    F = ffn1_w_down.shape[1] * N_CHIPS
    xs, target = x[0], loss_target[0]
    cx, cy, cc = lax.axis_index("x"), lax.axis_index("y"), lax.axis_index("c")
    layer_id = jnp.reshape(cc, (1,)).astype(jnp.int32)
    chip_id = jnp.reshape(2 * cx + cy, (1,)).astype(jnp.int32)

    inv_freq = ROPE_THETA ** (-jnp.arange(0, ROPE, 2, dtype=F32) / ROPE)
    ang = positions[0].astype(F32)[:, None] * inv_freq
    cos, sin = jnp.cos(ang), jnp.sin(ang)
    c2, s2 = jnp.concatenate([cos, cos], axis=-1), jnp.concatenate([-sin, sin], axis=-1)

    big = [ffn1_w_gate, ffn1_w_up, ffn1_w_down, w_in, mla_w_q_b, mla_w_kv_b, w_out, ffn2_w_gate, ffn2_w_up, ffn2_w_down]
    wg1, wu1, wd1, win, wqb, wkvb, wout, wg2, wu2, wd2 = all_gather_shards([place_own(w, chip_id) for w in big])
    wd1, wd2 = wd1.reshape(L, F, D), wd2.reshape(L, F, D)
    wout = wout.reshape(L, -1, D)
    win = _pad_in_cols(_from_col_shards(win))
    wqb, wkvb = _from_col_shards(wqb), _from_col_shards(wkvb)

    acts = []
    x0, h0 = xs, rms_fwd(xs, ffn1_norm[0][None])
    for l in range(L):
        x1, h1, s1 = _ffn_fwd(x0, h0, wg1, wu1, wd1, l, mix_norm[l][None])
        x2, h2, sm = _mixer_fwd(x1, h1, win, mla_q_norm[l][None], mla_kv_norm[l][None], swa_sinks[l],
                                fox_forget_bias[l][None], wqb, wkvb, wout, c2, s2, l, ffn2_norm[l][None])
        x3, h3, s2_ = _ffn_fwd(x2, h2, wg2, wu2, wd2, l, ffn1_norm[l + 1][None] if l + 1 < L else None)
        acts.append((x0, x1, x2, s1, sm, s2_))
        x0, h0 = x3, h3
    loss_part, dx, d_final = loss_head(x0, final_norm[None], target)

    small = {k: [None] * L for k in ("ffn1_norm", "mix_norm", "q_norm", "kv_norm", "sinks", "bias", "ffn2_norm")}
    per_layer = {k: [None] * L for k in ("win", "wqb", "wkvb")}
    ffn1_grads, ffn2_grads, dwout = (None,) * 3, (None,) * 3, None
    for l in reversed(range(L)):
        x0, x1, x2, s1, sm, s2_ = acts[l]
        dx, small["ffn2_norm"][l], ffn2_grads = _ffn_bwd(dx, x2, ffn2_norm[l][None], wg2, wu2, wd2, l, s2_, ffn2_grads)
        (dx, small["mix_norm"][l], dwin, small["q_norm"][l], dwqb, small["kv_norm"][l], dwkvb, small["sinks"][l],
         small["bias"][l], dwout) = _mixer_bwd(dx, x1, mix_norm[l][None], win, mla_q_norm[l][None], mla_kv_norm[l][None],
                                               swa_sinks[l], fox_forget_bias[l][None], wqb, wkvb, wout, c2, s2, l, sm,
                                               dwout)
        per_layer["win"][l] = _col_shards(_unpad_in_cols(dwin))
        per_layer["wqb"][l] = _col_shards(dwqb)
        per_layer["wkvb"][l] = _col_shards(dwkvb)
        dx, small["ffn1_norm"][l], ffn1_grads = _ffn_bwd(dx, x0, ffn1_norm[l][None], wg1, wu1, wd1, l, s1, ffn1_grads)
    grad_x = dx[None]

    names = ("wg1", "wu1", "wd1", "win", "wqb", "wkvb", "wout", "wg2", "wu2", "wd2")
    Fs = F // N_CHIPS
    full = [ffn1_grads[0], ffn1_grads[1], ffn1_grads[2].reshape(L, N_CHIPS, Fs, D), jnp.stack(per_layer["win"]),
            jnp.stack(per_layer["wqb"]), jnp.stack(per_layer["wkvb"]), dwout.reshape(L, N_CHIPS, -1, D),
            ffn2_grads[0], ffn2_grads[1], ffn2_grads[2].reshape(L, N_CHIPS, Fs, D)]
    flat = [g.reshape(L, -1, g.shape[-1]) for g in full]
    from_sibling = pair_exchange(flat)
    part = [sum_pair(g, o, layer_id).reshape(f.shape[1:]) for g, o, f in zip(flat, from_sibling, full)]
    from_chips = chip_scatter(part)
    mine = [sum_chips(p, r, chip_id, layer_id) for p, r in zip(part, from_chips)]
    grads_big = pair_share(mine)

    pieces = [jnp.concatenate(small["ffn1_norm"]), jnp.concatenate(small["mix_norm"]), jnp.concatenate(small["q_norm"]),
              jnp.concatenate(small["kv_norm"]), jnp.stack(small["sinks"]), jnp.stack(small["bias"]),
              jnp.concatenate(small["ffn2_norm"]), d_final, loss_part[:, 0:1]]
    sizes = [int(p.size) for p in pieces]
    packed = jnp.concatenate([p.reshape(-1) for p in pieces])
    packed = jnp.pad(packed, (0, SMALL_ROWS * 128 - packed.shape[0])).reshape(SMALL_ROWS, 128)
    summed = all_reduce_small(packed).reshape(-1)
    out_small, off = [], 0
    for p, n in zip(pieces, sizes):
        out_small.append(summed[off:off + n].reshape(p.shape))
        off += n
    g_ffn1_norm, g_mix_norm, g_q_norm, g_kv_norm, g_sinks, g_bias, g_ffn2_norm, g_final, loss = out_small
    loss = loss.reshape(())
    g_final = g_final.reshape(-1)

    gb = dict(zip(names, grads_big))
    summed_grads = [g_ffn1_norm, gb["wg1"], gb["wu1"], gb["wd1"], g_mix_norm, gb["win"], g_q_norm, gb["wqb"], g_kv_norm,
                    gb["wkvb"], g_sinks, g_bias, gb["wout"], g_ffn2_norm, gb["wg2"], gb["wu2"], gb["wd2"], g_final]
    weights = [ffn1_norm, ffn1_w_gate, ffn1_w_up, ffn1_w_down, mix_norm, w_in, mla_q_norm, mla_w_q_b, mla_kv_norm, mla_w_kv_b,
               swa_sinks, fox_forget_bias, w_out, ffn2_norm, ffn2_w_gate, ffn2_w_up, ffn2_w_down, final_norm]
    ms = [m_ffn1_norm, m_ffn1_w_gate, m_ffn1_w_up, m_ffn1_w_down, m_mix_norm, m_w_in, m_mla_q_norm, m_mla_w_q_b, m_mla_kv_norm,
          m_mla_w_kv_b, m_swa_sinks, m_fox_forget_bias, m_w_out, m_ffn2_norm, m_ffn2_w_gate, m_ffn2_w_up, m_ffn2_w_down,
          m_final_norm]
    vs = [v_ffn1_norm, v_ffn1_w_gate, v_ffn1_w_up, v_ffn1_w_down, v_mix_norm, v_w_in, v_mla_q_norm, v_mla_w_q_b, v_mla_kv_norm,
          v_mla_w_kv_b, v_swa_sinks, v_fox_forget_bias, v_w_out, v_ffn2_norm, v_ffn2_w_gate, v_ffn2_w_up, v_ffn2_w_down,
          v_final_norm]
    grads, deltas, new_m, new_v = [], [], [], []
    for w, g, m, v in zip(weights, summed_grads, ms, vs):
        three_d = w.shape if w.ndim == 3 else (1, -1, w.shape[-1])
        g_out, d, nm, nv = adamw(w.reshape(three_d), g.reshape(three_d), m.reshape(three_d), v.reshape(three_d))
        grads.append(g_out.reshape(w.shape))
        deltas.append(d.reshape(w.shape))
        new_m.append(nm.reshape(w.shape))
        new_v.append(nv.reshape(w.shape))
    return (loss, grad_x, *grads, *deltas, *new_m, *new_v)
```

```python
import jax
import jax.numpy as jnp
from jax import lax
from jax.experimental import pallas as pl
from jax.experimental.pallas import tpu as pltpu

F32, BF16 = jnp.float32, jnp.bfloat16
MESH = pl.DeviceIdType.MESH

RMS_EPS = 1e-6
ROPE_THETA = 10000.0
N_HEADS = 8
Q_LORA, KV_LORA = 512, 256
NOPE, ROPE, VDIM = 128, 64, 128
QK_MLA = NOPE + ROPE
SWA_KV, HD, WINDOW = 2, 64, 128
P_CQ, P_CKV, P_QS, P_KS, P_VS, P_QF, P_KF, P_VF, P_KR, P_F, P_COLS = (
    0, 512, 768, 1280, 1408, 1536, 2048, 2560, 3072, 3136, 3200)
IN_COLS = 3144
IN_KR = 768
ADAM_LR, ADAM_B1, ADAM_B2, ADAM_EPS, ADAM_WD, ADAM_STEP = 0.001, 0.9, 0.999, 1e-08, 0.01, 10
NEG = -1e30
LOG2E, LN2 = 1.4426950408889634, 0.6931471805599453
VMEM_LIMIT = 56 * 1024 * 1024
N_CHIPS = 4
N_DEV = 8
SMALL_ROWS = 128


def _tile(n, pref):
    return pref if n % pref == 0 else n


def _cparams(*sem):
    return pltpu.CompilerParams(dimension_semantics=sem, vmem_limit_bytes=VMEM_LIMIT)


def _sigmoid(x):
    return 1.0 / (1.0 + jnp.exp(-x))


def _dot(a, b):
    return jnp.dot(a, b, preferred_element_type=F32)


def _dot_nt(a, b):
    return lax.dot_general(a, b, (((1,), (1,)), ((), ())), preferred_element_type=F32)


def _dot_tn(a, b):
    return lax.dot_general(a, b, (((0,), (0,)), ((), ())), preferred_element_type=F32)


def rms_fwd(x, gain):
    S, D = x.shape
    tm = _tile(S, 512)

    def body(x_ref, g_ref, h_ref):
        xv = x_ref[...]
        r = lax.rsqrt(jnp.mean(xv * xv, axis=-1, keepdims=True) + RMS_EPS)
        h_ref[...] = (xv * r * g_ref[...]).astype(BF16)

    return pl.pallas_call(
        body, name="rms_fwd", grid=(S // tm,),
        in_specs=[pl.BlockSpec((tm, D), lambda i: (i, 0)), pl.BlockSpec((1, D), lambda i: (0, 0))],
        out_specs=pl.BlockSpec((tm, D), lambda i: (i, 0)),
        out_shape=jax.ShapeDtypeStruct((S, D), BF16), compiler_params=_cparams("arbitrary"))(x, gain)


def rms_bwd(dh, x, gain, resid):
    S, D = x.shape
    tm = _tile(S, 512)

    def body(dh_ref, x_ref, g_ref, r_ref, dx_ref, dg_ref):
        xv, dhv = x_ref[...], dh_ref[...]
        r = lax.rsqrt(jnp.mean(xv * xv, axis=-1, keepdims=True) + RMS_EPS)
        xhat = xv * r
        dhg = dhv * g_ref[...]
        dx_ref[...] = r_ref[...] + r * (dhg - xhat * jnp.mean(dhg * xhat, axis=-1, keepdims=True))

        @pl.when(pl.program_id(0) == 0)
        def _():
            dg_ref[...] = jnp.zeros_like(dg_ref)

        dg_ref[...] += jnp.sum(dhv * xhat, axis=0, keepdims=True)

    row = pl.BlockSpec((tm, D), lambda i: (i, 0))
    vec = pl.BlockSpec((1, D), lambda i: (0, 0))
    return pl.pallas_call(
        body, name="rms_bwd", grid=(S // tm,), in_specs=[row, row, vec, row], out_specs=[row, vec],
        out_shape=[jax.ShapeDtypeStruct((S, D), F32), jax.ShapeDtypeStruct((1, D), F32)],
        compiler_params=_cparams("arbitrary"))(dh, x, gain, resid)


def loss_head(x, gain, target):
    S, D = x.shape
    tm = _tile(S, 512)

    def body(x_ref, g_ref, t_ref, loss_ref, dx_ref, dg_ref):
        xv, g = x_ref[...], g_ref[...]
        r = lax.rsqrt(jnp.mean(xv * xv, axis=-1, keepdims=True) + RMS_EPS)
        xhat = xv * r
        err = xhat * g - t_ref[...]
        dy = err * (1.0 / D)
        dyg = dy * g
        dx_ref[...] = r * (dyg - xhat * jnp.mean(dyg * xhat, axis=-1, keepdims=True))

        @pl.when(pl.program_id(0) == 0)
        def _():
            dg_ref[...] = jnp.zeros_like(dg_ref)
            loss_ref[...] = jnp.zeros_like(loss_ref)

        dg_ref[...] += jnp.sum(dy * xhat, axis=0, keepdims=True)
        loss_ref[...] += 0.5 * jnp.sum(jnp.mean(err * err, axis=-1, keepdims=True), axis=0, keepdims=True)

    row = pl.BlockSpec((tm, D), lambda i: (i, 0))
    vec = pl.BlockSpec((1, D), lambda i: (0, 0))
    return pl.pallas_call(
        body, name="loss_head", grid=(S // tm,), in_specs=[row, vec, row],
        out_specs=[pl.BlockSpec((1, 128), lambda i: (0, 0)), row, vec],
        out_shape=[jax.ShapeDtypeStruct((1, 128), F32), jax.ShapeDtypeStruct((S, D), F32),
                   jax.ShapeDtypeStruct((1, D), F32)],
        compiler_params=_cparams("arbitrary"))(x, gain, target)


def gate_up(h, wg, wu, l):
    S, D = h.shape
    Fs = wg.shape[3]
    tm = _tile(S, 512)

    def body(h_ref, wg_ref, wu_ref, g_ref, u_ref, a_ref):
        hv = h_ref[...]
        g = _dot(hv, wg_ref[...])
        u = _dot(hv, wu_ref[...])
        g_ref[...] = g.astype(BF16)
        u_ref[...] = u.astype(BF16)
        a_ref[...] = (g * _sigmoid(g) * u).astype(BF16)

    w_spec = pl.BlockSpec((None, None, D, Fs), lambda j, i: (l, j, 0, 0))
    o_spec = pl.BlockSpec((tm, Fs), lambda j, i: (i, j))
    o_shape = jax.ShapeDtypeStruct((S, N_CHIPS * Fs), BF16)
    return pl.pallas_call(
        body, name="gate_up", grid=(N_CHIPS, S // tm),
        in_specs=[pl.BlockSpec((tm, D), lambda j, i: (i, 0)), w_spec, w_spec],
        out_specs=[o_spec, o_spec, o_spec], out_shape=[o_shape, o_shape, o_shape],
        compiler_params=_cparams("arbitrary", "arbitrary"))(h, wg, wu)


def _normed(x, gain):
    r = lax.rsqrt(jnp.mean(x * x, axis=-1, keepdims=True) + RMS_EPS)
    return (x * r * gain).astype(BF16)


def down_proj(a, wd, l, x, next_gain=None):
    S, F = a.shape
    D = wd.shape[2]
    tm, tk = _tile(S, 512), F // N_CHIPS
    nk = F // tk
    emit = next_gain is not None

    def body(a_ref, w_ref, x_ref, *rest):
        g_ref = rest[0] if emit else None
        o_ref = rest[1] if emit else rest[0]
        acc_ref = rest[-1]
        k = pl.program_id(1)

        @pl.when(k == 0)
        def _():
            acc_ref[...] = jnp.zeros_like(acc_ref)

        acc_ref[...] += _dot(a_ref[...], w_ref[...])

        @pl.when(k == nk - 1)
        def _():
            y = x_ref[...] + 0.5 * acc_ref[...]
            o_ref[...] = y
            if emit:
                rest[2][...] = _normed(y, g_ref[...])

    row = pl.BlockSpec((tm, D), lambda i, k: (i, 0))
    in_specs = [pl.BlockSpec((tm, tk), lambda i, k: (i, k)), pl.BlockSpec((None, tk, D), lambda i, k: (l, k, 0)), row]
    args, out_specs, out_shape = [a, wd, x], [row], [jax.ShapeDtypeStruct((S, D), F32)]
    if emit:
        in_specs.append(pl.BlockSpec((1, D), lambda i, k: (0, 0)))
        args.append(next_gain)
        out_specs.append(row)
        out_shape.append(jax.ShapeDtypeStruct((S, D), BF16))
    out = pl.pallas_call(
        body, name="down_proj", grid=(S // tm, nk), in_specs=in_specs, out_specs=out_specs, out_shape=out_shape,
        scratch_shapes=[pltpu.VMEM((tm, D), F32)], compiler_params=_cparams("arbitrary", "arbitrary"))(*args)
    return out if emit else (out[0], None)


def down_bwd(dy, wd, l, g, u):
    S, D = dy.shape
    F = g.shape[1]
    Fs = F // N_CHIPS
    tm = _tile(S, 512)

    def body(dy_ref, w_ref, g_ref, u_ref, dg_ref, du_ref):
        da = 0.5 * _dot_nt(dy_ref[...].astype(BF16), w_ref[...])
        gv, uv = g_ref[...].astype(F32), u_ref[...].astype(F32)
        sig = _sigmoid(gv)
        du_ref[...] = (da * (gv * sig)).astype(BF16)
        dg_ref[...] = (da * uv * (sig * (1.0 + gv * (1.0 - sig)))).astype(BF16)

    t_spec = pl.BlockSpec((tm, Fs), lambda j, i: (i, j))
    o_shape = jax.ShapeDtypeStruct((S, F), BF16)
    return pl.pallas_call(
        body, name="down_bwd", grid=(N_CHIPS, S // tm),
        in_specs=[pl.BlockSpec((tm, D), lambda j, i: (i, 0)), pl.BlockSpec((None, Fs, D), lambda j, i: (l, j, 0)),
                  t_spec, t_spec],
        out_specs=[t_spec, t_spec], out_shape=[o_shape, o_shape],
        compiler_params=_cparams("arbitrary", "arbitrary"))(dy, wd, g, u)


def gate_up_bwd(dg, du, wg, wu, l):
    S, F = dg.shape
    D, Fs = wg.shape[2], wg.shape[3]
    tm = _tile(S, 512)

    def body(dg_ref, du_ref, wg_ref, wu_ref, o_ref):
        k = pl.program_id(1)

        @pl.when(k == 0)
        def _():
            o_ref[...] = jnp.zeros_like(o_ref)

        o_ref[...] += _dot_nt(dg_ref[...], wg_ref[...]) + _dot_nt(du_ref[...], wu_ref[...])

    t_spec = pl.BlockSpec((tm, Fs), lambda i, k: (i, k))
    w_spec = pl.BlockSpec((None, None, D, Fs), lambda i, k: (l, k, 0, 0))
    return pl.pallas_call(
        body, name="gate_up_bwd", grid=(S // tm, N_CHIPS), in_specs=[t_spec, t_spec, w_spec, w_spec],
        out_specs=pl.BlockSpec((tm, D), lambda i, k: (i, 0)), out_shape=jax.ShapeDtypeStruct((S, D), F32),
        compiler_params=_cparams("arbitrary", "arbitrary"))(dg, du, wg, wu)


def mm_nn(a, b, l, tn, out_dtype, resid=None, next_gain=None):
    S, K = a.shape
    N = b.shape[2]
    tm = _tile(S, 512)
    emit = next_gain is not None
    assert not emit or tn == N

    def body(a_ref, b_ref, *rest):
        rest = list(rest)
        acc = _dot(a_ref[...].astype(BF16), b_ref[...])
        if resid is not None:
            acc = rest.pop(0)[...] + acc
        g_ref = rest.pop(0) if emit else None
        rest[0][...] = acc.astype(out_dtype)
        if emit:
            rest[1][...] = _normed(acc, g_ref[...])

    o_spec = pl.BlockSpec((tm, tn), lambda n, i: (i, n))
    in_specs = [pl.BlockSpec((tm, K), lambda n, i: (i, 0)), pl.BlockSpec((None, K, tn), lambda n, i: (l, 0, n))]
    args, out_specs, out_shape = [a, b], [o_spec], [jax.ShapeDtypeStruct((S, N), out_dtype)]
    if resid is not None:
        in_specs.append(o_spec)
        args.append(resid)
    if emit:
        in_specs.append(pl.BlockSpec((1, N), lambda n, i: (0, 0)))
        args.append(next_gain)
        out_specs.append(o_spec)
        out_shape.append(jax.ShapeDtypeStruct((S, N), BF16))
    out = pl.pallas_call(
        body, name="mm_nn", grid=(N // tn, S // tm), in_specs=in_specs, out_specs=out_specs, out_shape=out_shape,
        compiler_params=_cparams("arbitrary", "arbitrary"))(*args)
    return out if emit else out[0]


def mm_nt(a, b, l, tn, out_dtype):
    S, K = a.shape
    N = b.shape[1]
    tm = _tile(S, 512)

    def body(a_ref, b_ref, o_ref):
        o_ref[...] = _dot_nt(a_ref[...].astype(BF16), b_ref[...]).astype(out_dtype)

    return pl.pallas_call(
        body, name="mm_nt", grid=(N // tn, S // tm),
        in_specs=[pl.BlockSpec((tm, K), lambda n, i: (i, 0)), pl.BlockSpec((None, tn, K), lambda n, i: (l, n, 0))],
        out_specs=pl.BlockSpec((tm, tn), lambda n, i: (i, n)), out_shape=jax.ShapeDtypeStruct((S, N), out_dtype),
        compiler_params=_cparams("arbitrary", "arbitrary"))(a, b)


def mm_tn(a, b, tka, tnb, scale=1.0, blocked=False, layer=None, layers=None, into=None):
    S, Ka = a.shape
    Nb = b.shape[1]
    ts = _tile(S, 1024)
    ns = S // ts

    def body(a_ref, b_ref, *rest):
        o_ref, acc_ref = rest[-2:]
        s = pl.program_id(2)

        @pl.when(s == 0)
        def _():
            acc_ref[...] = jnp.zeros_like(acc_ref)

        acc_ref[...] += _dot_tn(a_ref[...].astype(BF16), b_ref[...].astype(BF16))

        @pl.when(s == ns - 1)
        def _():
            o_ref[...] = (scale * acc_ref[...]).astype(BF16)

    if blocked:
        block, shape = (None, tka, tnb), (Nb // tnb, Ka, tnb)
        index = lambda ka, nb, s: (nb, ka, 0)
    else:
        block, shape = (tka, tnb), (Ka, Nb)
        index = lambda ka, nb, s: (ka, nb)
    if layer is not None:
        block, shape = (None,) + block, (layers,) + shape
        inner = index
        index = lambda ka, nb, s: (layer,) + inner(ka, nb, s)
    in_specs = [pl.BlockSpec((ts, tka), lambda ka, nb, s: (s, ka)), pl.BlockSpec((ts, tnb), lambda ka, nb, s: (s, nb))]
    args, aliases = [a, b], {}
    if into is not None:
        in_specs.append(pl.BlockSpec(memory_space=pl.ANY))
        args.append(into)
        aliases = {2: 0}
    return pl.pallas_call(
        body, name="mm_tn", grid=(Ka // tka, Nb // tnb, ns), in_specs=in_specs,
        out_specs=pl.BlockSpec(block, index), out_shape=jax.ShapeDtypeStruct(shape, BF16),
        input_output_aliases=aliases, scratch_shapes=[pltpu.VMEM((tka, tnb), F32)],
        compiler_params=_cparams("arbitrary", "arbitrary", "arbitrary"))(*args)


def _rope(x, c2, s2):
    half = x.shape[-1] // 2
    rot = jnp.concatenate([x[:, half:], x[:, :half]], axis=-1)
    return x * c2 + rot * s2


def _tri(tm, upper):
    r = lax.broadcasted_iota(jnp.int32, (tm, tm), 0)
    c = lax.broadcasted_iota(jnp.int32, (tm, tm), 1)
    return jnp.where((c >= r) if upper else (c <= r), 1.0, 0.0).astype(F32)


def _log_sigmoid(x):
    return jnp.minimum(x, 0.0) - jnp.log(1.0 + jnp.exp(-jnp.abs(x)))


def _norm_hat(c):
    r = lax.rsqrt(jnp.mean(c * c, axis=-1, keepdims=True) + RMS_EPS)
    return c * r, r


def _tok_spec(tm, width, rev_n=None):
    if rev_n is None:
        return pl.BlockSpec((tm, width), lambda i: (i, 0))
    return pl.BlockSpec((tm, width), lambda i: (rev_n - 1 - i, 0))


def _head_spec(heads, tm, width, rev_n=None):
    if rev_n is None:
        return pl.BlockSpec((heads, tm, width), lambda i: (0, i, 0))
    return pl.BlockSpec((heads, tm, width), lambda i: (0, rev_n - 1 - i, 0))


def _lane_spec(heads, width, tm, rev_n=None):
    if rev_n is None:
        return pl.BlockSpec((heads, width, tm), lambda i: (0, 0, i))
    return pl.BlockSpec((heads, width, tm), lambda i: (0, 0, rev_n - 1 - i))


def _full_spec(shape):
    return pl.BlockSpec(shape, lambda i: (0,) * len(shape))


def mixer_prep(p, c2, s2, q_norm, kv_norm, bias, wqb, wkvb, l):
    S = p.shape[0]
    tm = _tile(S, 256)
    H = N_HEADS

    def body(p_ref, c2_ref, s2_ref, qn_ref, kvn_ref, b_ref, wqb_ref, wkvb_ref,
             cqn_ref, ckvn_ref, qmt_ref, km_ref, kmt_ref, vm_ref, vmt_ref, qst_ref, ks_ref, kst_ref, vs_ref, vst_ref,
             qft_ref, kf_ref, kft_ref, vf_ref, vft_ref, ccol_ref, carry_row):
        c2, s2 = c2_ref[...], s2_ref[...]
        cqn = (_norm_hat(p_ref[:, P_CQ:P_CQ + Q_LORA])[0] * qn_ref[...]).astype(BF16)
        ckvn = (_norm_hat(p_ref[:, P_CKV:P_CKV + KV_LORA])[0] * kvn_ref[...]).astype(BF16)
        cqn_ref[...] = cqn
        ckvn_ref[...] = ckvn
        q = _dot(cqn, wqb_ref[...])
        kv = _dot(ckvn, wkvb_ref[...])
        k_pe = _rope(p_ref[:, P_KR:P_KR + ROPE], c2, s2)
        k_pe_t = k_pe.T.astype(BF16)
        k_pe = k_pe.astype(BF16)

        def both_ways(x, tok_ref, lane_ref, h):
            tok_ref[h] = x.astype(BF16)
            lane_ref[h] = x.T.astype(BF16)

        qs_mla, qs_hd = QK_MLA ** -0.5 * LOG2E, HD ** -0.5 * LOG2E
        for h in range(H):
            qmt_ref[h, 0:NOPE, :] = (q[:, h * QK_MLA:h * QK_MLA + NOPE] * qs_mla).T.astype(BF16)
            qmt_ref[h, NOPE:QK_MLA, :] = (_rope(q[:, h * QK_MLA + NOPE:(h + 1) * QK_MLA], c2, s2) * qs_mla).T.astype(BF16)
            k_nope = kv[:, h * 256:h * 256 + NOPE]
            km_ref[h, :, 0:NOPE] = k_nope.astype(BF16)
            km_ref[h, :, NOPE:QK_MLA] = k_pe
            kmt_ref[h, 0:NOPE, :] = k_nope.T.astype(BF16)
            kmt_ref[h, NOPE:QK_MLA, :] = k_pe_t
            both_ways(kv[:, h * 256 + NOPE:(h + 1) * 256], vm_ref, vmt_ref, h)
            qst_ref[h] = (_rope(p_ref[:, P_QS + h * HD:P_QS + (h + 1) * HD], c2, s2) * qs_hd).T.astype(BF16)
            qft_ref[h] = (p_ref[:, P_QF + h * HD:P_QF + (h + 1) * HD] * qs_hd).T.astype(BF16)
            both_ways(p_ref[:, P_KF + h * HD:P_KF + (h + 1) * HD], kf_ref, kft_ref, h)
            both_ways(p_ref[:, P_VF + h * HD:P_VF + (h + 1) * HD], vf_ref, vft_ref, h)
        for h in range(SWA_KV):
            both_ways(_rope(p_ref[:, P_KS + h * HD:P_KS + (h + 1) * HD], c2, s2), ks_ref, kst_ref, h)
            both_ways(p_ref[:, P_VS + h * HD:P_VS + (h + 1) * HD], vs_ref, vst_ref, h)

        @pl.when(pl.program_id(0) == 0)
        def _():
            carry_row[...] = jnp.zeros_like(carry_row)

        log_f = _log_sigmoid(p_ref[:, P_F:P_F + H] + b_ref[...])
        c_tok = jnp.dot(_tri(tm, upper=False), log_f, preferred_element_type=F32,
                        precision=lax.Precision.HIGHEST) + carry_row[0:1, 0:H]
        for h in range(H):
            ccol_ref[h] = c_tok[:, h:h + 1] * LOG2E
        carry_row[0:1, 0:H] = c_tok[tm - 1:tm, :]

    out_shape = [jax.ShapeDtypeStruct((S, Q_LORA), BF16), jax.ShapeDtypeStruct((S, KV_LORA), BF16)]
    out_specs = [_tok_spec(tm, Q_LORA), _tok_spec(tm, KV_LORA)]

    def add(heads, d, lanes):
        out_shape.append(jax.ShapeDtypeStruct((heads, d, S) if lanes else (heads, S, d), BF16))
        out_specs.append(_lane_spec(heads, d, tm) if lanes else _head_spec(heads, tm, d))

    for heads_q, heads_kv, dqk, dv in ((H, H, QK_MLA, VDIM), (H, SWA_KV, HD, HD), (H, H, HD, HD)):
        add(heads_q, dqk, True)
        add(heads_kv, dqk, False)
        add(heads_kv, dqk, True)
        add(heads_kv, dv, False)
        add(heads_kv, dv, True)
    out_shape.append(jax.ShapeDtypeStruct((H, S, 1), F32))
    out_specs.append(_head_spec(H, tm, 1))
    in_specs = [_tok_spec(tm, P_COLS), _tok_spec(tm, ROPE), _tok_spec(tm, ROPE), _full_spec((1, Q_LORA)),
                _full_spec((1, KV_LORA)), _full_spec((1, H)),
                pl.BlockSpec((None,) + wqb.shape[1:], lambda i: (l, 0, 0)),
                pl.BlockSpec((None,) + wkvb.shape[1:], lambda i: (l, 0, 0))]
    return pl.pallas_call(
        body, name="mixer_prep", grid=(S // tm,), in_specs=in_specs, out_specs=out_specs, out_shape=out_shape,
        scratch_shapes=[pltpu.VMEM((8, 128), F32)],
        compiler_params=_cparams("arbitrary"))(p, c2, s2, q_norm, kv_norm, bias, wqb, wkvb)


def mixer_prep_bwd(p, c2, s2, q_norm, kv_norm, bias, wqb, wkvb, l, dqm, dkm, dvm, dqs, dks, dvs, dqf, dkf, dvf, dc):
    S = p.shape[0]
    tm = _tile(S, 256)
    nt = S // tm
    H, G = N_HEADS, N_HEADS // SWA_KV

    def body(p_ref, c2_ref, s2_ref, qn_ref, kvn_ref, b_ref, wqb_ref, wkvb_ref,
             dqm_ref, dkm_ref, dvm_ref, dqs_ref, dks_ref, dvs_ref, dqf_ref, dkf_ref, dvf_ref, dc_ref,
             dp_ref, dq_ref, dkv_ref, dqn_ref, dkvn_ref, db_ref, carry):
        c2, s2 = c2_ref[...], -s2_ref[...]

        @pl.when(pl.program_id(0) == 0)
        def _():
            dqn_ref[...] = jnp.zeros_like(dqn_ref)
            dkvn_ref[...] = jnp.zeros_like(dkvn_ref)
            db_ref[...] = jnp.zeros_like(db_ref)
            carry[...] = jnp.zeros_like(carry)

        sc_mla, sc_hd = QK_MLA ** -0.5, HD ** -0.5
        dk_pe_t = jnp.zeros((ROPE, tm), F32)
        for h in range(H):
            dq_ref[:, h * QK_MLA:h * QK_MLA + NOPE] = (dqm_ref[h, 0:NOPE, :] * sc_mla).T.astype(BF16)
            dq_ref[:, h * QK_MLA + NOPE:(h + 1) * QK_MLA] = _rope(
                (dqm_ref[h, NOPE:QK_MLA, :] * sc_mla).T, c2, s2).astype(BF16)
            dkv_ref[:, h * 256:h * 256 + NOPE] = (dkm_ref[h, 0:NOPE, :] * LN2).T.astype(BF16)
            dkv_ref[:, h * 256 + NOPE:(h + 1) * 256] = dvm_ref[h].T.astype(BF16)
            dk_pe_t = dk_pe_t + dkm_ref[h, NOPE:QK_MLA, :]
        dk_pe = (dk_pe_t * LN2).T

        def through_norm(dcn, c, gain, dgain_ref):
            c_hat, r = _norm_hat(c)
            dhg = dcn * gain
            dgain_ref[...] += jnp.sum(dcn * c_hat, axis=0, keepdims=True)
            return r * (dhg - c_hat * jnp.mean(dhg * c_hat, axis=-1, keepdims=True))

        dcqn = _dot_nt(dq_ref[...], wqb_ref[...])
        dckvn = _dot_nt(dkv_ref[...], wkvb_ref[...])
        dp_ref[:, P_CQ:P_CQ + Q_LORA] = through_norm(dcqn, p_ref[:, P_CQ:P_CQ + Q_LORA], qn_ref[...], dqn_ref).astype(BF16)
        dp_ref[:, P_CKV:P_CKV + KV_LORA] = through_norm(
            dckvn, p_ref[:, P_CKV:P_CKV + KV_LORA], kvn_ref[...], dkvn_ref).astype(BF16)
        for h in range(H):
            dp_ref[:, P_QS + h * HD:P_QS + (h + 1) * HD] = _rope((dqs_ref[h] * sc_hd).T, c2, s2).astype(BF16)
            dp_ref[:, P_QF + h * HD:P_QF + (h + 1) * HD] = (dqf_ref[h] * sc_hd).T.astype(BF16)
            dp_ref[:, P_KF + h * HD:P_KF + (h + 1) * HD] = (dkf_ref[h] * LN2).T.astype(BF16)
            dp_ref[:, P_VF + h * HD:P_VF + (h + 1) * HD] = dvf_ref[h].T.astype(BF16)
        for kvh in range(SWA_KV):
            dk = dks_ref[kvh * G]
            dv = dvs_ref[kvh * G]
            for g in range(1, G):
                dk = dk + dks_ref[kvh * G + g]
                dv = dv + dvs_ref[kvh * G + g]
            dp_ref[:, P_KS + kvh * HD:P_KS + (kvh + 1) * HD] = _rope((dk * LN2).T, c2, s2).astype(BF16)
            dp_ref[:, P_VS + kvh * HD:P_VS + (kvh + 1) * HD] = dv.T.astype(BF16)

        dcv = dc_ref[...]
        dlog_f = jnp.dot(_tri(tm, upper=True), dcv, preferred_element_type=F32,
                         precision=lax.Precision.HIGHEST) + carry[0:1, 0:H]
        carry[0:1, 0:H] = dlog_f[0:1, :]
        df = dlog_f * _sigmoid(-(p_ref[:, P_F:P_F + H] + b_ref[...]))
        db_ref[...] += jnp.sum(df, axis=0, keepdims=True)
        dp_ref[:, P_KR:P_COLS] = jnp.zeros((tm, P_COLS - P_KR), BF16)
        dp_ref[:, P_KR:P_KR + ROPE] = _rope(dk_pe, c2, s2).astype(BF16)
        dp_ref[:, P_F:P_F + H] = df.astype(BF16)

    rev = nt
    in_specs = [_tok_spec(tm, P_COLS, rev), _tok_spec(tm, ROPE, rev), _tok_spec(tm, ROPE, rev), _full_spec((1, Q_LORA)),
                _full_spec((1, KV_LORA)), _full_spec((1, H)),
                pl.BlockSpec((None,) + wqb.shape[1:], lambda i: (l, 0, 0)),
                pl.BlockSpec((None,) + wkvb.shape[1:], lambda i: (l, 0, 0)),
                _lane_spec(H, QK_MLA, tm, rev), _lane_spec(H, QK_MLA, tm, rev), _lane_spec(H, VDIM, tm, rev)]
    in_specs += [_lane_spec(H, HD, tm, rev)] * 6 + [_tok_spec(tm, H, rev)]
    out_specs = [_tok_spec(tm, P_COLS, rev), _tok_spec(tm, N_HEADS * QK_MLA, rev), _tok_spec(tm, N_HEADS * 256, rev),
                 _full_spec((1, Q_LORA)), _full_spec((1, KV_LORA)), _full_spec((1, H))]
    out_shape = [jax.ShapeDtypeStruct((S, P_COLS), BF16), jax.ShapeDtypeStruct((S, N_HEADS * QK_MLA), BF16),
                 jax.ShapeDtypeStruct((S, N_HEADS * 256), BF16), jax.ShapeDtypeStruct((1, Q_LORA), F32),
                 jax.ShapeDtypeStruct((1, KV_LORA), F32), jax.ShapeDtypeStruct((1, H), F32)]
    return pl.pallas_call(
        body, name="mixer_prep_bwd", grid=(nt,), in_specs=in_specs, out_specs=out_specs, out_shape=out_shape,
        scratch_shapes=[pltpu.VMEM((8, 128), F32)], compiler_params=_cparams("arbitrary"))(
            p, c2, s2, q_norm, kv_norm, bias, wqb, wkvb, dqm, dkm, dvm, dqs, dks, dvs, dqf, dkf, dvf, dc)


def merge_heads(o_mla, o_swa, o_fox):
    H, S, _ = o_mla.shape
    tm = _tile(S, 512)
    width = H * (VDIM + 2 * HD)

    def body(om_ref, os_ref, of_ref, m_ref):
        for h in range(H):
            m_ref[:, h * VDIM:(h + 1) * VDIM] = om_ref[h]
            m_ref[:, H * VDIM + h * HD:H * VDIM + (h + 1) * HD] = os_ref[h]
            m_ref[:, H * (VDIM + HD) + h * HD:H * (VDIM + HD) + (h + 1) * HD] = of_ref[h]

    return pl.pallas_call(
        body, name="merge_heads", grid=(S // tm,),
        in_specs=[_head_spec(H, tm, VDIM), _head_spec(H, tm, HD), _head_spec(H, tm, HD)],
        out_specs=_tok_spec(tm, width), out_shape=jax.ShapeDtypeStruct((S, width), BF16),
        compiler_params=_cparams("arbitrary"))(o_mla, o_swa, o_fox)


def split_heads(dmixed, mixed):
    S, width = mixed.shape
    H = N_HEADS
    tm = _tile(S, 512)

    def body(dm_ref, m_ref, dom_ref, dos_ref, dof_ref, dm_delta, ds_delta, df_delta):
        def one(h, off, d, do_ref, delta_ref):
            dv = dm_ref[:, off:off + d].astype(F32)
            do_ref[h] = dv.T.astype(BF16)
            prod = dv * m_ref[:, off:off + d].astype(F32)
            rows = lax.dot_general(jnp.ones((8, d), F32), prod, (((1,), (1,)), ((), ())),
                                   preferred_element_type=F32, precision=lax.Precision.HIGHEST)
            delta_ref[h] = rows[0:1, :]

        for h in range(H):
            one(h, h * VDIM, VDIM, dom_ref, dm_delta)
            one(h, H * VDIM + h * HD, HD, dos_ref, ds_delta)
            one(h, H * (VDIM + HD) + h * HD, HD, dof_ref, df_delta)

    row_spec = pl.BlockSpec((H, 1, tm), lambda i: (0, 0, i))
    row_shape = jax.ShapeDtypeStruct((H, 1, S), F32)
    return pl.pallas_call(
        body, name="split_heads", grid=(S // tm,),
        in_specs=[_tok_spec(tm, width), _tok_spec(tm, width)],
        out_specs=[_lane_spec(H, VDIM, tm), _lane_spec(H, HD, tm), _lane_spec(H, HD, tm), row_spec, row_spec, row_spec],
        out_shape=[jax.ShapeDtypeStruct((H, VDIM, S), BF16), jax.ShapeDtypeStruct((H, HD, S), BF16),
                   jax.ShapeDtypeStruct((H, HD, S), BF16), row_shape, row_shape, row_shape],
        compiler_params=_cparams("arbitrary"))(dmixed, mixed)


def _attn_tile(S):
    return 512 if (S % 512 == 0 and S > 512) else S // 2


def _valid(q0, k0, shape, q_axis, window):
    qpos = q0 + lax.broadcasted_iota(jnp.int32, shape, q_axis)
    kpos = k0 + lax.broadcasted_iota(jnp.int32, shape, 1 - q_axis)
    ok = kpos <= qpos
    if window is not None:
        ok = jnp.logical_and(ok, kpos > qpos - window)
    return ok


def attn_fwd(name, qt, k, vt, window=None, sinks=None, ccol=None, carry=None):
    H, dq, S = qt.shape
    Hk, dv, _ = vt.shape
    G = H // Hk
    t = _attn_tile(S)
    nq = S // t
    fox, use_sink = ccol is not None, sinks is not None
    n_carry = len(carry[2]) if carry else 0

    def body(*refs):
        refs = list(refs)
        sink_ref = refs.pop(0) if use_sink else None
        q_ref, k_ref, vt_ref = refs[:3]
        refs = refs[3:]
        ccol_ref = refs.pop(0) if fox else None
        carried_in = [refs.pop(0) for _ in range(n_carry)]
        o_ref, lse_ref = refs[:2]
        refs = refs[2:]
        carried_out = [refs.pop(0) for _ in range(n_carry)]
        m_ref, l_ref, acc_ref = refs[:3]
        h, i = pl.program_id(0), pl.program_id(1)
        if carry:
            @pl.when(jnp.logical_and(h == 0, i == 0))
            def _():
                _gather_phase(carry[0], carry[1], carried_in, carried_out, refs[3], refs[4], start=True)
        qv = q_ref[...]
        if use_sink:
            m_ref[...] = jnp.full(m_ref.shape, sink_ref[h] * LOG2E, F32)
            l_ref[...] = jnp.ones(l_ref.shape, F32)
        else:
            m_ref[...] = jnp.full(m_ref.shape, NEG, F32)
            l_ref[...] = jnp.zeros(l_ref.shape, F32)
        acc_ref[...] = jnp.zeros(acc_ref.shape, F32)

        def steps(blocks):
            offs = [pl.multiple_of(j * t, t) for j, _ in blocks]
            scores = [_dot(k_ref[pl.ds(off, t), :], qv) for off in offs]
            for (j, masked), off, st in zip(blocks, offs, scores):
                if fox:
                    st = st - ccol_ref[pl.ds(off, t), :]
                if masked:
                    st = jnp.where(_valid(i * t, j * t, (t, t), 1, window), st, NEG)
                m_prev = m_ref[...]
                m_new = jnp.maximum(m_prev, jnp.max(st, axis=0, keepdims=True))
                alpha = jnp.exp2(m_prev - m_new)
                pt = jnp.exp2(st - m_new)
                l_ref[...] = alpha * l_ref[...] + jnp.sum(pt, axis=0, keepdims=True)
                acc_ref[...] = alpha * acc_ref[...] + _dot(vt_ref[:, pl.ds(off, t)], pt.astype(BF16))
                m_ref[...] = m_new

        if window is None:
            def pair(n, carry):
                steps([(2 * n, False), (2 * n + 1, False)])
                return carry
            lax.fori_loop(0, i // 2, pair, 0)

            @pl.when(i % 2 == 1)
            def _():
                steps([(i - 1, False), (i, True)])

            @pl.when(i % 2 == 0)
            def _():
                steps([(i, True)])
        else:
            def one(j, carry):
                steps([(j, True)])
                return carry
            lax.fori_loop(jnp.maximum(i * t - (window - 1), 0) // t, i + 1, one, 0)
        l = l_ref[...]
        o_ref[...] = (acc_ref[...] / l).T.astype(BF16)
        lse_ref[...] = m_ref[...] + jnp.log2(l)
        if carry:
            @pl.when(jnp.logical_and(h == H - 1, i == nq - 1))
            def _():
                _gather_phase(carry[0], carry[1], carried_in, carried_out, refs[3], refs[4], start=False)

    in_specs, args = [], []
    if use_sink:
        in_specs.append(pl.BlockSpec(memory_space=pltpu.SMEM))
        args.append(sinks)
    in_specs += [pl.BlockSpec((None, dq, t), lambda h, i: (h, 0, i)),
                 pl.BlockSpec((None, S, dq), lambda h, i: (h // G, 0, 0)),
                 pl.BlockSpec((None, dv, S), lambda h, i: (h // G, 0, 0))]
    args += [qt, k, vt]
    if fox:
        in_specs.append(pl.BlockSpec((None, S, 1), lambda h, i: (h, 0, 0)))
        args.append(ccol)
    out_specs = [pl.BlockSpec((None, t, dv), lambda h, i: (h, i, 0)), pl.BlockSpec((None, 1, t), lambda h, i: (h, 0, i))]
    out_shape = [jax.ShapeDtypeStruct((H, S, dv), BF16), jax.ShapeDtypeStruct((H, 1, S), F32)]
    scratch = [pltpu.VMEM((1, t), F32), pltpu.VMEM((1, t), F32), pltpu.VMEM((dv, t), F32)]
    aliases = {}
    if carry:
        aliases = {len(args) + n: 2 + n for n in range(n_carry)}
        in_specs += [_ANY] * n_carry
        args += list(carry[2])
        out_specs += [_ANY] * n_carry
        out_shape += [jax.ShapeDtypeStruct(a.shape, a.dtype) for a in carry[2]]
        scratch += [pltpu.SemaphoreType.DMA((n_carry, 3)), pltpu.SemaphoreType.DMA((n_carry, 3))]
    out = pl.pallas_call(
        body, name=name, grid=(H, nq), in_specs=in_specs, out_specs=out_specs, out_shape=out_shape,
        scratch_shapes=scratch, input_output_aliases=aliases,
        compiler_params=_cparams("arbitrary", "arbitrary"))(*args)
    return (out[0], out[1], list(out[2:])) if carry else (out[0], out[1])


def attn_bwd(name, qt, k, kt, v, dot, lse, delta, window=None, sinks=None, ccol=None):
    H, dq, S = qt.shape
    Hk, _, dv = v.shape
    G = H // Hk
    t = _attn_tile(S)
    nq = S // t
    fox, use_sink = ccol is not None, sinks is not None

    def body(*refs):
        refs = list(refs)
        sink_ref = refs.pop(0) if use_sink else None
        qt_ref, k_ref, kt_ref, v_ref, dot_ref, lse_ref, delta_ref = refs[:7]
        refs = refs[7:]
        ccol_ref = refs.pop(0) if fox else None
        dqt_ref, dkt_ref, dvt_ref = refs[:3]
        refs = refs[3:]
        dcq_ref, dck_ref = (refs.pop(0), refs.pop(0)) if fox else (None, None)
        dsink_ref = refs.pop(0) if use_sink else None
        h, j = pl.program_id(0), pl.program_id(1)

        @pl.when(j == 0)
        def _():
            dqt_ref[...] = jnp.zeros(dqt_ref.shape, F32)
            if fox:
                dcq_ref[...] = jnp.zeros(dcq_ref.shape, F32)
            if use_sink:
                ps = jnp.exp2(sink_ref[h] * LOG2E - lse_ref[...]) * delta_ref[...]
                dsink_ref[...] = jnp.broadcast_to(-jnp.sum(ps, axis=-1, keepdims=True), dsink_ref.shape)

        dkt_ref[...] = jnp.zeros(dkt_ref.shape, F32)
        dvt_ref[...] = jnp.zeros(dvt_ref.shape, F32)
        if fox:
            dck_ref[...] = jnp.zeros(dck_ref.shape, F32)
        kv, ktv, vv = k_ref[...], kt_ref[...], v_ref[...]

        def steps(blocks):
            offs = [pl.multiple_of(i * t, t) for i, _ in blocks]
            qts = [qt_ref[:, pl.ds(off, t)] for off in offs]
            dots = [dot_ref[:, pl.ds(off, t)] for off in offs]
            scores = [_dot(kv, qti) for qti in qts]
            dprobs = [_dot(vv, doti) for doti in dots]
            for (i, masked), off, qti, doti, st, dpt in zip(blocks, offs, qts, dots, scores, dprobs):
                if fox:
                    st = st - ccol_ref[...]
                if masked:
                    st = jnp.where(_valid(i * t, j * t, (t, t), 1, window), st, NEG)
                pt = jnp.exp2(st - lse_ref[:, pl.ds(off, t)])
                dvt_ref[...] += _dot_nt(doti, pt.astype(BF16))
                dst = pt * (dpt - delta_ref[:, pl.ds(off, t)])
                if fox:
                    dcq_ref[:, pl.ds(off, t)] += jnp.sum(dst, axis=0, keepdims=True)
                    dck_ref[...] -= jnp.sum(dst, axis=1, keepdims=True)
                dsb = dst.astype(BF16)
                dkt_ref[...] += _dot_nt(qti, dsb)
                dqt_ref[:, pl.ds(off, t)] += _dot(ktv, dsb)

        if window is None:
            odd = (nq - 1 - j) % 2

            @pl.when(odd == 1)
            def _():
                steps([(j, True), (j + 1, False)])

            @pl.when(odd == 0)
            def _():
                steps([(j, True)])

            first = j + 1 + odd

            def pair(n, carry):
                steps([(first + 2 * n, False), (first + 2 * n + 1, False)])
                return carry
            lax.fori_loop(0, (nq - first) // 2, pair, 0)
        else:
            def one(i, carry):
                steps([(i, True)])
                return carry
            lax.fori_loop(j, jnp.minimum((j * t + t - 1 + window - 1) // t, nq - 1) + 1, one, 0)

    in_specs, args = [], []
    if use_sink:
        in_specs.append(pl.BlockSpec(memory_space=pltpu.SMEM))
        args.append(sinks)
    whole = lambda d: pl.BlockSpec((None, d, S), lambda h, j: (h, 0, 0))
    keys = lambda d: pl.BlockSpec((None, d, t), lambda h, j: (h, 0, j))
    row = pl.BlockSpec((None, 1, S), lambda h, j: (h, 0, 0))
    in_specs += [whole(dq), pl.BlockSpec((None, t, dq), lambda h, j: (h // G, j, 0)),
                 pl.BlockSpec((None, dq, t), lambda h, j: (h // G, 0, j)),
                 pl.BlockSpec((None, t, dv), lambda h, j: (h // G, j, 0)), whole(dv), row, row]
    args += [qt, k, kt, v, dot, lse, delta]
    out_specs = [whole(dq), keys(dq), keys(dv)]
    out_shape = [jax.ShapeDtypeStruct((H, dq, S), F32), jax.ShapeDtypeStruct((H, dq, S), F32),
                 jax.ShapeDtypeStruct((H, dv, S), F32)]
    if fox:
        in_specs.append(pl.BlockSpec((None, t, 1), lambda h, j: (h, j, 0)))
        args.append(ccol)
        out_specs += [row, pl.BlockSpec((None, t, 1), lambda h, j: (h, j, 0))]
        out_shape += [jax.ShapeDtypeStruct((H, 1, S), F32), jax.ShapeDtypeStruct((H, S, 1), F32)]
    if use_sink:
        out_specs.append(pl.BlockSpec((None, 1, 128), lambda h, j: (h, 0, 0)))
        out_shape.append(jax.ShapeDtypeStruct((H, 1, 128), F32))
    return pl.pallas_call(
        body, name=name, grid=(H, nq), in_specs=in_specs, out_specs=out_specs, out_shape=out_shape,
        compiler_params=_cparams("arbitrary", "arbitrary"))(*args)


def _rows_tile(rows):
    for tr in (256, 128, 64, 32, 16, 8):
        if rows % tr == 0:
            return tr
    return rows


def adamw(w, g, m, v):
    L, R, C = w.shape
    tr = _rows_tile(R)

    def body(w_ref, g_ref, m_ref, v_ref, g_out, d_ref, nm_ref, nv_ref):
        gv = g_ref[...]
        mn = ADAM_B1 * m_ref[...] + (1.0 - ADAM_B1) * gv
        vn = ADAM_B2 * v_ref[...] + (1.0 - ADAM_B2) * (gv * gv)
        m_hat = mn / (1.0 - ADAM_B1 ** ADAM_STEP)
        v_hat = vn / (1.0 - ADAM_B2 ** ADAM_STEP)
        d_ref[...] = -ADAM_LR * (m_hat / (jnp.sqrt(v_hat) + ADAM_EPS) + ADAM_WD * w_ref[...])
        nm_ref[...] = mn
        nv_ref[...] = vn
        g_out[...] = gv

    spec = pl.BlockSpec((None, tr, C), lambda l, i: (l, i, 0))
    shape = jax.ShapeDtypeStruct((L, R, C), F32)
    return pl.pallas_call(
        body, name="adamw", grid=(L, R // tr), in_specs=[spec] * 4, out_specs=[spec] * 4, out_shape=[shape] * 4,
        compiler_params=_cparams("arbitrary", "arbitrary"))(w, g, m, v)


def place_own(w, chip):
    L, R, C = w.shape
    tr = _rows_tile(R)

    def body(c_ref, w_ref, o_ref):
        o_ref[...] = w_ref[...].astype(BF16)

    return pl.pallas_call(
        body, name="place_own",
        grid_spec=pltpu.PrefetchScalarGridSpec(
            num_scalar_prefetch=1, grid=(L, R // tr),
            in_specs=[pl.BlockSpec((None, tr, C), lambda l, i, c: (l, i, 0))],
            out_specs=pl.BlockSpec((None, None, tr, C), lambda l, i, c: (l, c[0], i, 0))),
        out_shape=jax.ShapeDtypeStruct((L, N_CHIPS, R, C), BF16),
        compiler_params=_cparams("arbitrary", "arbitrary"))(chip, w)


def sum_pair(grad, other, layer):
    _, R, C = grad.shape
    tr = _rows_tile(R)

    def body(l_ref, g_ref, o_ref, out_ref):
        out_ref[...] = (g_ref[...].astype(F32) + o_ref[...].astype(F32)).astype(BF16)

    return pl.pallas_call(
        body, name="sum_pair",
        grid_spec=pltpu.PrefetchScalarGridSpec(
            num_scalar_prefetch=1, grid=(R // tr,),
            in_specs=[pl.BlockSpec((None, tr, C), lambda i, l: (l[0], i, 0)), pl.BlockSpec((tr, C), lambda i, l: (i, 0))],
            out_specs=pl.BlockSpec((tr, C), lambda i, l: (i, 0))),
        out_shape=jax.ShapeDtypeStruct((R, C), BF16), compiler_params=_cparams("arbitrary"))(layer, grad, other)


def sum_chips(part, recv, chip, layer):
    _, R, C = part.shape
    tr = _rows_tile(R)

    def body(c_ref, l_ref, p_ref, r_ref, out_ref):
        acc = p_ref[...].astype(F32)
        for k in range(N_CHIPS - 1):
            acc = acc + r_ref[k].astype(F32)
        out_ref[...] = acc

    return pl.pallas_call(
        body, name="sum_chips",
        grid_spec=pltpu.PrefetchScalarGridSpec(
            num_scalar_prefetch=2, grid=(R // tr,),
            in_specs=[pl.BlockSpec((None, tr, C), lambda i, c, l: (c[0], i, 0)),
                      pl.BlockSpec((N_CHIPS - 1, tr, C), lambda i, c, l: (0, i, 0))],
            out_specs=pl.BlockSpec((None, tr, C), lambda i, c, l: (l[0], i, 0))),
        out_shape=jax.ShapeDtypeStruct((2, R, C), F32), compiler_params=_cparams("arbitrary"))(chip, layer, part, recv)


_ANY = pl.BlockSpec(memory_space=pl.ANY)


def _place():
    x, y, c = lax.axis_index("x"), lax.axis_index("y"), lax.axis_index("c")
    chips = [(1 - x, y), (x, 1 - y), (1 - x, 1 - y)]
    return x, y, c, chips


def _gather_phase(phase, layer, w, o, send, recv, start):
    x, y, c, chips = _place()
    me, sib = 2 * x + y, (x, y, 1 - c)
    n = len(w)
    works = c == layer

    def copy(t, k, shard, to, src=None):
        blk = o[t].at[layer, shard]
        return pltpu.make_async_remote_copy(src_ref=blk if src is None else src, dst_ref=blk, send_sem=send.at[t, k],
                                            recv_sem=recv.at[t, k], device_id=to, device_id_type=MESH)

    def outgoing():
        if phase == "ici":
            return [copy(t, k, me, (*chip, c), src=w[t].at[layer, me]) for t in range(n) for k, chip in enumerate(chips)]
        return [copy(t, k, 2 * chip[0] + chip[1], sib) for t in range(n) for k, chip in enumerate(chips)]

    def incoming():
        return [copy(t, k, 2 * chip[0] + chip[1], (x, y, c)) for t in range(n) for k, chip in enumerate(chips)]

    if start:
        @pl.when(works)
        def _():
            for cp in outgoing():
                cp.start()
    else:
        @pl.when(works)
        def _():
            if phase == "ici":
                for cp in incoming():
                    cp.wait_recv()
            for cp in outgoing():
                cp.wait_send()

        if phase == "pass":
            @pl.when(jnp.logical_not(works))
            def _():
                for cp in incoming():
                    cp.wait_recv()


def gather_layer(ws, layer):
    n = len(ws)

    def body(*refs):
        w, o = refs[:n], refs[n:2 * n]
        ici_send, ici_recv, pass_send, pass_recv = refs[2 * n:]
        _gather_phase("ici", layer, w, o, ici_send, ici_recv, start=True)
        _gather_phase("ici", layer, w, o, ici_send, ici_recv, start=False)
        _gather_phase("pass", layer, w, o, pass_send, pass_recv, start=True)
        _gather_phase("pass", layer, w, o, pass_send, pass_recv, start=False)

    sems = [pltpu.SemaphoreType.DMA((n, 3))] * 4
    return pl.pallas_call(
        body, name="gather_layer", in_specs=[_ANY] * n, out_specs=[_ANY] * n,
        out_shape=[jax.ShapeDtypeStruct(w.shape, w.dtype) for w in ws], input_output_aliases={t: t for t in range(n)},
        scratch_shapes=sems, compiler_params=pltpu.CompilerParams(has_side_effects=True))(*ws)


def pair_exchange(gs):
    n = len(gs)

    def body(*refs):
        g, o = refs[:n], refs[n:2 * n]
        send, recv = refs[2 * n:]
        x, y, c, _ = _place()
        cps = [pltpu.make_async_remote_copy(src_ref=g[t].at[1 - c], dst_ref=o[t], send_sem=send.at[t], recv_sem=recv.at[t],
                                            device_id=(x, y, 1 - c), device_id_type=MESH) for t in range(n)]
        for cp in cps:
            cp.start()
        for cp in cps:
            cp.wait()

    return pl.pallas_call(
        body, name="pair_exchange", in_specs=[_ANY] * n, out_specs=[_ANY] * n,
        out_shape=[jax.ShapeDtypeStruct(g.shape[1:], g.dtype) for g in gs],
        scratch_shapes=[pltpu.SemaphoreType.DMA((n,)), pltpu.SemaphoreType.DMA((n,))],
        compiler_params=pltpu.CompilerParams(has_side_effects=True))(*gs)


def chip_scatter(ps):
    n = len(ps)

    def body(*refs):
        p, o = refs[:n], refs[n:2 * n]
        send, recv = refs[2 * n:]
        x, y, c, chips = _place()
        cps = [pltpu.make_async_remote_copy(src_ref=p[t].at[2 * chip[0] + chip[1]], dst_ref=o[t].at[k],
                                            send_sem=send.at[t, k], recv_sem=recv.at[t, k], device_id=(*chip, c),
                                            device_id_type=MESH)
               for t in range(n) for k, chip in enumerate(chips)]
        for cp in cps:
            cp.start()
        for cp in cps:
            cp.wait()

    return pl.pallas_call(
        body, name="chip_scatter", in_specs=[_ANY] * n, out_specs=[_ANY] * n,
        out_shape=[jax.ShapeDtypeStruct((N_CHIPS - 1,) + p.shape[1:], p.dtype) for p in ps],
        scratch_shapes=[pltpu.SemaphoreType.DMA((n, 3)), pltpu.SemaphoreType.DMA((n, 3))],
        compiler_params=pltpu.CompilerParams(has_side_effects=True))(*ps)


def pair_share(rs):
    n = len(rs)

    def body(*refs):
        r, o = refs[:n], refs[n:2 * n]
        send, recv = refs[2 * n:]
        x, y, c, _ = _place()

        def remote(t, layer):
            return pltpu.make_async_remote_copy(src_ref=r[t].at[layer], dst_ref=o[t].at[layer], send_sem=send.at[t],
                                                recv_sem=recv.at[t], device_id=(x, y, 1 - c), device_id_type=MESH)

        for t in range(n):
            remote(t, c).start()
        for t in range(n):
            remote(t, 1 - c).wait_recv()
            remote(t, c).wait_send()

    return pl.pallas_call(
        body, name="pair_share", in_specs=[_ANY] * n, out_specs=[_ANY] * n,
        out_shape=[jax.ShapeDtypeStruct(r.shape, r.dtype) for r in rs], input_output_aliases={t: t for t in range(n)},
        scratch_shapes=[pltpu.SemaphoreType.DMA((n,)), pltpu.SemaphoreType.DMA((n,))],
        compiler_params=pltpu.CompilerParams(has_side_effects=True))(*rs)


def all_reduce_small(buf):
    def body(x_ref, o_ref, land, send, recv):
        x, y, c, _ = _place()
        me = 4 * x + 2 * y + c
        land[me] = x_ref[...]
        cps = []
        for mask in range(1, N_DEV):
            px = 1 - x if mask & 4 else x
            py = 1 - y if mask & 2 else y
            pc = 1 - c if mask & 1 else c
            cps.append(pltpu.make_async_remote_copy(src_ref=x_ref, dst_ref=land.at[me], send_sem=send.at[mask - 1],
                                                    recv_sem=recv.at[mask - 1], device_id=(px, py, pc), device_id_type=MESH))
            cps[-1].start()
        for mask in range(1, N_DEV):
            px = 1 - x if mask & 4 else x
            py = 1 - y if mask & 2 else y
            pc = 1 - c if mask & 1 else c
            pltpu.make_async_remote_copy(src_ref=x_ref, dst_ref=land.at[4 * px + 2 * py + pc], send_sem=send.at[mask - 1],
                                         recv_sem=recv.at[mask - 1], device_id=(px, py, pc), device_id_type=MESH).wait_recv()
        for cp in cps:
            cp.wait_send()
        acc = land[0]
        for d in range(1, N_DEV):
            acc = acc + land[d]
        o_ref[...] = acc

    vm = pl.BlockSpec(memory_space=pltpu.VMEM)
    return pl.pallas_call(
        body, name="all_reduce_small", in_specs=[vm], out_specs=vm, out_shape=jax.ShapeDtypeStruct(buf.shape, F32),
        scratch_shapes=[pltpu.VMEM((N_DEV,) + buf.shape, F32), pltpu.SemaphoreType.DMA((N_DEV - 1,)),
                        pltpu.SemaphoreType.DMA((N_DEV - 1,))])(buf)


def _ffn_fwd(x, h, wg, wu, wd, l, next_gain):
    g, u, a = gate_up(h, wg, wu, l)
    y, h_next = down_proj(a, wd, l, x, next_gain)
    return y, h_next, (h, g, u, a)


def _ffn_bwd(dy, x, gain, wg, wu, wd, l, saved, grads):
    h, g, u, a = saved
    D = x.shape[1]
    L = wg.shape[0]
    Fs = g.shape[1] // N_CHIPS
    dg, du = down_bwd(dy, wd, l, g, u)
    dwg = mm_tn(h, dg, _tile(D, 1024), Fs, blocked=True, layer=l, layers=L, into=grads[0])
    dwu = mm_tn(h, du, _tile(D, 1024), Fs, blocked=True, layer=l, layers=L, into=grads[1])
    dwd = mm_tn(a, dy, Fs, _tile(D, 1024), scale=0.5, layer=l, layers=L, into=grads[2])
    dh = gate_up_bwd(dg, du, wg, wu, l)
    dx, dgain = rms_bwd(dh, x, gain, dy)
    return dx, dgain, (dwg, dwu, dwd)


def _mixer_fwd(x, h, win, q_norm, kv_norm, sinks, bias, wqb, wkvb, gathered, c2, s2, l, next_gain, carry_layer=None):
    D = x.shape[1]
    p = mm_nn(h, win, 0, 640, F32)
    (cqn, ckvn, qmt, km, kmt, vm, vmt, qst, ks, kst, vs, vst, qft, kf, kft, vf, vft, ccol) = mixer_prep(
        p, c2, s2, q_norm, kv_norm, bias, wqb, wkvb, 0)
    if carry_layer is None:
        o_mla, lse_mla = attn_fwd("attn_mla", qmt, km, vmt)
        o_fox, lse_fox = attn_fwd("attn_fox", qft, kf, vft, ccol=ccol)
    else:
        o_mla, lse_mla, gathered = attn_fwd("attn_mla", qmt, km, vmt, carry=("ici", carry_layer, gathered))
        o_fox, lse_fox, gathered = attn_fwd("attn_fox", qft, kf, vft, ccol=ccol, carry=("pass", carry_layer, gathered))
    o_swa, lse_swa = attn_fwd("attn_swa", qst, ks, vst, window=WINDOW, sinks=sinks)
    mixed = merge_heads(o_mla, o_swa, o_fox)
    wout = gathered[6].reshape(gathered[6].shape[0], -1, D)
    y, h_next = mm_nn(mixed, wout, l, D, F32, resid=x, next_gain=next_gain)
    saved = (h, p, cqn, ckvn, qmt, km, kmt, vm, qst, ks, kst, vs, qft, kf, kft, vf, ccol, lse_mla, lse_swa, lse_fox, mixed)
    return y, h_next, saved, gathered


def _mixer_bwd(dy, x, gain, win, q_norm, kv_norm, sinks, bias, wqb, wkvb, wout, c2, s2, l, saved, dwout_so_far):
    (h, p, cqn, ckvn, qmt, km, kmt, vm, qst, ks, kst, vs, qft, kf, kft, vf, ccol, lse_mla, lse_swa, lse_fox, mixed) = saved
    S, D = x.shape
    width = mixed.shape[1]
    dmixed = mm_nt(dy, wout, l, _tile(width, 1024), BF16)
    dwout = mm_tn(mixed, dy, _tile(width, 1024), _tile(D, 1024), layer=l, layers=wout.shape[0], into=dwout_so_far)
    do_mla, do_swa, do_fox, dl_mla, dl_swa, dl_fox = split_heads(dmixed, mixed)
    dqm, dkm, dvm = attn_bwd("attn_mla_bwd", qmt, km, kmt, vm, do_mla, lse_mla, dl_mla)
    dqs, dks, dvs, dsink = attn_bwd("attn_swa_bwd", qst, ks, kst, vs, do_swa, lse_swa, dl_swa, window=WINDOW, sinks=sinks)
    dqf, dkf, dvf, dcq, dck = attn_bwd("attn_fox_bwd", qft, kf, kft, vf, do_fox, lse_fox, dl_fox, ccol=ccol)
    dc = dcq.reshape(N_HEADS, S).T + dck.reshape(N_HEADS, S).T
    dp, dq, dkv, dqn, dkvn, dbias = mixer_prep_bwd(p, c2, s2, q_norm, kv_norm, bias, wqb, wkvb, 0, dqm, dkm, dvm, dqs, dks,
                                                   dvs, dqf, dkf, dvf, dc)
    dwqb = mm_tn(cqn, dq, Q_LORA, N_HEADS * QK_MLA)
    dwkvb = mm_tn(ckvn, dkv, KV_LORA, 1024)
    dwin = mm_tn(h, dp, _tile(D, 1024), 640)
    dh = mm_nt(dp, win, 0, _tile(D, 1024), F32)
    dx, dgain = rms_bwd(dh, x, gain, dy)
    return dx, dgain, dwin, dqn, dwqb, dkvn, dwkvb, dsink[:, 0, 0], dbias[0], dwout


def _pad_in_cols(w):
    pad = jnp.zeros(w.shape[:-1] + (P_COLS - IN_COLS,), w.dtype)
    return jnp.concatenate([w[..., :IN_KR], w[..., IN_KR + ROPE:IN_COLS - N_HEADS], w[..., IN_KR:IN_KR + ROPE],
                            w[..., IN_COLS - N_HEADS:], pad], axis=-1)


def _unpad_in_cols(w):
    return jnp.concatenate([w[..., :IN_KR], w[..., P_KR:P_KR + ROPE], w[..., IN_KR:P_KR], w[..., P_F:P_F + N_HEADS]], axis=-1)


def _col_shards(w):
    R = w.shape[0]
    return w.reshape(R, N_CHIPS, -1).transpose(1, 0, 2)


def _from_col_shards(w):
    L, _, R, C = w.shape
    return w.transpose(0, 2, 1, 3).reshape(L, R, N_CHIPS * C)


def kernel(x, positions, ffn1_norm, ffn1_w_gate, ffn1_w_up, ffn1_w_down, mix_norm, w_in, mla_q_norm, mla_w_q_b, mla_kv_norm, mla_w_kv_b, swa_sinks, fox_forget_bias, w_out, ffn2_norm, ffn2_w_gate, ffn2_w_up, ffn2_w_down, final_norm, loss_target, m_ffn1_norm, m_ffn1_w_gate, m_ffn1_w_up, m_ffn1_w_down, m_mix_norm, m_w_in, m_mla_q_norm, m_mla_w_q_b, m_mla_kv_norm, m_mla_w_kv_b, m_swa_sinks, m_fox_forget_bias, m_w_out, m_ffn2_norm, m_ffn2_w_gate, m_ffn2_w_up, m_ffn2_w_down, m_final_norm, v_ffn1_norm, v_ffn1_w_gate, v_ffn1_w_up, v_ffn1_w_down, v_mix_norm, v_w_in, v_mla_q_norm, v_mla_w_q_b, v_mla_kv_norm, v_mla_w_kv_b, v_swa_sinks, v_fox_forget_bias, v_w_out, v_ffn2_norm, v_ffn2_w_gate, v_ffn2_w_up, v_ffn2_w_down, v_final_norm):
    L = ffn1_norm.shape[0]
    S, D = x.shape[1], x.shape[2]
    F = ffn1_w_down.shape[1] * N_CHIPS
    xs, target = x[0], loss_target[0]
    cx, cy, cc = lax.axis_index("x"), lax.axis_index("y"), lax.axis_index("c")
    layer_id = jnp.reshape(cc, (1,)).astype(jnp.int32)
    chip_id = jnp.reshape(2 * cx + cy, (1,)).astype(jnp.int32)

    inv_freq = ROPE_THETA ** (-jnp.arange(0, ROPE, 2, dtype=F32) / ROPE)
    ang = positions[0].astype(F32)[:, None] * inv_freq
    cos, sin = jnp.cos(ang), jnp.sin(ang)
    c2, s2 = jnp.concatenate([cos, cos], axis=-1), jnp.concatenate([-sin, sin], axis=-1)

    big = [ffn1_w_gate, ffn1_w_up, ffn1_w_down, w_in, mla_w_q_b, mla_w_kv_b, w_out, ffn2_w_gate, ffn2_w_up, ffn2_w_down]
    gathered = gather_layer([place_own(w, chip_id) for w in big], 0)

    def mixer_weights(ws, l):
        return (_pad_in_cols(_from_col_shards(ws[3][l:l + 1])), _from_col_shards(ws[4][l:l + 1]),
                _from_col_shards(ws[5][l:l + 1]))

    acts, small_w = [], []
    x0, h0 = xs, rms_fwd(xs, ffn1_norm[0][None])
    for l in range(L):
        small_w.append(mixer_weights(gathered, l))
        x1, h1, s1 = _ffn_fwd(x0, h0, gathered[0], gathered[1], gathered[2].reshape(L, F, D), l, mix_norm[l][None])
        x2, h2, sm, gathered = _mixer_fwd(x1, h1, small_w[l][0], mla_q_norm[l][None], mla_kv_norm[l][None], swa_sinks[l],
                                          fox_forget_bias[l][None], small_w[l][1], small_w[l][2], gathered, c2, s2, l,
                                          ffn2_norm[l][None], carry_layer=l + 1 if l + 1 < L else None)
        x3, h3, s2_ = _ffn_fwd(x2, h2, gathered[7], gathered[8], gathered[9].reshape(L, F, D), l,
                               ffn1_norm[l + 1][None] if l + 1 < L else None)
        acts.append((x0, x1, x2, s1, sm, s2_))
        x0, h0 = x3, h3
    loss_part, dx, d_final = loss_head(x0, final_norm[None], target)
    wg1, wu1, wd1, _, _, _, wout, wg2, wu2, wd2 = gathered
    wd1, wd2, wout = wd1.reshape(L, F, D), wd2.reshape(L, F, D), wout.reshape(L, -1, D)

    small = {k: [None] * L for k in ("ffn1_norm", "mix_norm", "q_norm", "kv_norm", "sinks", "bias", "ffn2_norm")}
    per_layer = {k: [None] * L for k in ("win", "wqb", "wkvb")}
    ffn1_grads, ffn2_grads, dwout = (None,) * 3, (None,) * 3, None
    for l in reversed(range(L)):
        x0, x1, x2, s1, sm, s2_ = acts[l]
        dx, small["ffn2_norm"][l], ffn2_grads = _ffn_bwd(dx, x2, ffn2_norm[l][None], wg2, wu2, wd2, l, s2_, ffn2_grads)
        (dx, small["mix_norm"][l], dwin, small["q_norm"][l], dwqb, small["kv_norm"][l], dwkvb, small["sinks"][l],
         small["bias"][l], dwout) = _mixer_bwd(dx, x1, mix_norm[l][None], small_w[l][0], mla_q_norm[l][None],
                                               mla_kv_norm[l][None], swa_sinks[l], fox_forget_bias[l][None], small_w[l][1],
                                               small_w[l][2], wout, c2, s2, l, sm, dwout)
        per_layer["win"][l] = _col_shards(_unpad_in_cols(dwin))
        per_layer["wqb"][l] = _col_shards(dwqb)
        per_layer["wkvb"][l] = _col_shards(dwkvb)
        dx, small["ffn1_norm"][l], ffn1_grads = _ffn_bwd(dx, x0, ffn1_norm[l][None], wg1, wu1, wd1, l, s1, ffn1_grads)
    grad_x = dx[None]

    names = ("wg1", "wu1", "wd1", "win", "wqb", "wkvb", "wout", "wg2", "wu2", "wd2")
    Fs = F // N_CHIPS
    full = [ffn1_grads[0], ffn1_grads[1], ffn1_grads[2].reshape(L, N_CHIPS, Fs, D), jnp.stack(per_layer["win"]),
            jnp.stack(per_layer["wqb"]), jnp.stack(per_layer["wkvb"]), dwout.reshape(L, N_CHIPS, -1, D),
            ffn2_grads[0], ffn2_grads[1], ffn2_grads[2].reshape(L, N_CHIPS, Fs, D)]
    flat = [g.reshape(L, -1, g.shape[-1]) for g in full]
    from_sibling = pair_exchange(flat)
    part = [sum_pair(g, o, layer_id).reshape(f.shape[1:]) for g, o, f in zip(flat, from_sibling, full)]
    from_chips = chip_scatter(part)
    mine = [sum_chips(p, r, chip_id, layer_id) for p, r in zip(part, from_chips)]
    grads_big = pair_share(mine)

    pieces = [jnp.concatenate(small["ffn1_norm"]), jnp.concatenate(small["mix_norm"]), jnp.concatenate(small["q_norm"]),
              jnp.concatenate(small["kv_norm"]), jnp.stack(small["sinks"]), jnp.stack(small["bias"]),
              jnp.concatenate(small["ffn2_norm"]), d_final, loss_part[:, 0:1]]
    sizes = [int(p.size) for p in pieces]
    packed = jnp.concatenate([p.reshape(-1) for p in pieces])
    packed = jnp.pad(packed, (0, SMALL_ROWS * 128 - packed.shape[0])).reshape(SMALL_ROWS, 128)
    summed = all_reduce_small(packed).reshape(-1)
    out_small, off = [], 0
    for p, n in zip(pieces, sizes):
        out_small.append(summed[off:off + n].reshape(p.shape))
        off += n
    g_ffn1_norm, g_mix_norm, g_q_norm, g_kv_norm, g_sinks, g_bias, g_ffn2_norm, g_final, loss = out_small
    loss = loss.reshape(())
    g_final = g_final.reshape(-1)

    gb = dict(zip(names, grads_big))
    summed_grads = [g_ffn1_norm, gb["wg1"], gb["wu1"], gb["wd1"], g_mix_norm, gb["win"], g_q_norm, gb["wqb"], g_kv_norm,
                    gb["wkvb"], g_sinks, g_bias, gb["wout"], g_ffn2_norm, gb["wg2"], gb["wu2"], gb["wd2"], g_final]
    weights = [ffn1_norm, ffn1_w_gate, ffn1_w_up, ffn1_w_down, mix_norm, w_in, mla_q_norm, mla_w_q_b, mla_kv_norm, mla_w_kv_b,
               swa_sinks, fox_forget_bias, w_out, ffn2_norm, ffn2_w_gate, ffn2_w_up, ffn2_w_down, final_norm]
    ms = [m_ffn1_norm, m_ffn1_w_gate, m_ffn1_w_up, m_ffn1_w_down, m_mix_norm, m_w_in, m_mla_q_norm, m_mla_w_q_b, m_mla_kv_norm,
          m_mla_w_kv_b, m_swa_sinks, m_fox_forget_bias, m_w_out, m_ffn2_norm, m_ffn2_w_gate, m_ffn2_w_up, m_ffn2_w_down,
          m_final_norm]
    vs = [v_ffn1_norm, v_ffn1_w_gate, v_ffn1_w_up, v_ffn1_w_down, v_mix_norm, v_w_in, v_mla_q_norm, v_mla_w_q_b, v_mla_kv_norm,
          v_mla_w_kv_b, v_swa_sinks, v_fox_forget_bias, v_w_out, v_ffn2_norm, v_ffn2_w_gate, v_ffn2_w_up, v_ffn2_w_down,
          v_final_norm]
    grads, deltas, new_m, new_v = [], [], [], []
    for w, g, m, v in zip(weights, summed_grads, ms, vs):
        three_d = w.shape if w.ndim == 3 else (1, -1, w.shape[-1])
        g_out, d, nm, nv = adamw(w.reshape(three_d), g.reshape(three_d), m.reshape(three_d), v.reshape(three_d))
        grads.append(g_out.reshape(w.shape))
        deltas.append(d.reshape(w.shape))
        new_m.append(nm.reshape(w.shape))
        new_v.append(nv.reshape(w.shape))
    return (loss, grad_x, *grads, *deltas, *new_m, *new_v)
```

```python
import jax
import jax.numpy as jnp
from jax import lax
from jax.experimental import pallas as pl
from jax.experimental.pallas import tpu as pltpu

F32, BF16 = jnp.float32, jnp.bfloat16
MESH = pl.DeviceIdType.MESH

RMS_EPS = 1e-6
ROPE_THETA = 10000.0
N_HEADS = 8
Q_LORA, KV_LORA = 512, 256
NOPE, ROPE, VDIM = 128, 64, 128
QK_MLA = NOPE + ROPE
SWA_KV, HD, WINDOW = 2, 64, 128
P_CQ, P_CKV, P_QS, P_KS, P_VS, P_QF, P_KF, P_VF, P_KR, P_F, P_COLS = (
    0, 512, 768, 1280, 1408, 1536, 2048, 2560, 3072, 3136, 3200)
IN_COLS = 3144
IN_KR = 768
ADAM_LR, ADAM_B1, ADAM_B2, ADAM_EPS, ADAM_WD, ADAM_STEP = 0.001, 0.9, 0.999, 1e-08, 0.01, 10
NEG = -1e30
LOG2E, LN2 = 1.4426950408889634, 0.6931471805599453
VMEM_LIMIT = 56 * 1024 * 1024
N_CHIPS = 4
N_DEV = 8
W_FFN1, W_REST, W_HEAD, W_TAIL = [0, 1, 2], [3, 4, 5, 6, 7, 8, 9], [0, 1, 2, 3, 4, 5, 6], [7, 8, 9]
SMALL_ROWS = 128


def _tile(n, pref):
    return pref if n % pref == 0 else n


def _cparams(*sem):
    return pltpu.CompilerParams(dimension_semantics=sem, vmem_limit_bytes=VMEM_LIMIT)


def _sigmoid(x):
    return 1.0 / (1.0 + jnp.exp(-x))


def _dot(a, b):
    return jnp.dot(a, b, preferred_element_type=F32)


def _dot_nt(a, b):
    return lax.dot_general(a, b, (((1,), (1,)), ((), ())), preferred_element_type=F32)


def _dot_tn(a, b):
    return lax.dot_general(a, b, (((0,), (0,)), ((), ())), preferred_element_type=F32)


def rms_fwd(x, gain):
    S, D = x.shape
    tm = _tile(S, 512)

    def body(x_ref, g_ref, h_ref):
        xv = x_ref[...]
        r = lax.rsqrt(jnp.mean(xv * xv, axis=-1, keepdims=True) + RMS_EPS)
        h_ref[...] = (xv * r * g_ref[...]).astype(BF16)

    return pl.pallas_call(
        body, name="rms_fwd", grid=(S // tm,),
        in_specs=[pl.BlockSpec((tm, D), lambda i: (i, 0)), pl.BlockSpec((1, D), lambda i: (0, 0))],
        out_specs=pl.BlockSpec((tm, D), lambda i: (i, 0)),
        out_shape=jax.ShapeDtypeStruct((S, D), BF16), compiler_params=_cparams("arbitrary"))(x, gain)


def rms_bwd(dh, x, gain, resid):
    S, D = x.shape
    tm = _tile(S, 512)

    def body(dh_ref, x_ref, g_ref, r_ref, dx_ref, dg_ref):
        xv, dhv = x_ref[...], dh_ref[...]
        r = lax.rsqrt(jnp.mean(xv * xv, axis=-1, keepdims=True) + RMS_EPS)
        xhat = xv * r
        dhg = dhv * g_ref[...]
        dx_ref[...] = r_ref[...] + r * (dhg - xhat * jnp.mean(dhg * xhat, axis=-1, keepdims=True))

        @pl.when(pl.program_id(0) == 0)
        def _():
            dg_ref[...] = jnp.zeros_like(dg_ref)

        dg_ref[...] += jnp.sum(dhv * xhat, axis=0, keepdims=True)

    row = pl.BlockSpec((tm, D), lambda i: (i, 0))
    vec = pl.BlockSpec((1, D), lambda i: (0, 0))
    return pl.pallas_call(
        body, name="rms_bwd", grid=(S // tm,), in_specs=[row, row, vec, row], out_specs=[row, vec],
        out_shape=[jax.ShapeDtypeStruct((S, D), F32), jax.ShapeDtypeStruct((1, D), F32)],
        compiler_params=_cparams("arbitrary"))(dh, x, gain, resid)


def loss_head(x, gain, target):
    S, D = x.shape
    tm = _tile(S, 512)

    def body(x_ref, g_ref, t_ref, loss_ref, dx_ref, dg_ref):
        xv, g = x_ref[...], g_ref[...]
        r = lax.rsqrt(jnp.mean(xv * xv, axis=-1, keepdims=True) + RMS_EPS)
        xhat = xv * r
        err = xhat * g - t_ref[...]
        dy = err * (1.0 / D)
        dyg = dy * g
        dx_ref[...] = r * (dyg - xhat * jnp.mean(dyg * xhat, axis=-1, keepdims=True))

        @pl.when(pl.program_id(0) == 0)
        def _():
            dg_ref[...] = jnp.zeros_like(dg_ref)
            loss_ref[...] = jnp.zeros_like(loss_ref)

        dg_ref[...] += jnp.sum(dy * xhat, axis=0, keepdims=True)
        loss_ref[...] += 0.5 * jnp.sum(jnp.mean(err * err, axis=-1, keepdims=True), axis=0, keepdims=True)

    row = pl.BlockSpec((tm, D), lambda i: (i, 0))
    vec = pl.BlockSpec((1, D), lambda i: (0, 0))
    return pl.pallas_call(
        body, name="loss_head", grid=(S // tm,), in_specs=[row, vec, row],
        out_specs=[pl.BlockSpec((1, 128), lambda i: (0, 0)), row, vec],
        out_shape=[jax.ShapeDtypeStruct((1, 128), F32), jax.ShapeDtypeStruct((S, D), F32),
                   jax.ShapeDtypeStruct((1, D), F32)],
        compiler_params=_cparams("arbitrary"))(x, gain, target)


def _carry_plumbing(carry, n_in, n_out):
    if not carry:
        return [], [], [], [], [], {}
    arrays = list(carry[2])
    n = len(arrays)
    sems = [pltpu.SemaphoreType.DMA((n, 3)), pltpu.SemaphoreType.DMA((n, 3))]
    return ([_ANY] * n, arrays, [_ANY] * n, [jax.ShapeDtypeStruct(a.shape, a.dtype) for a in arrays], sems,
            {n_in + t: n_out + t for t in range(n)})


def gate_up(h, wg, wu, l, carry=None):
    S, D = h.shape
    Fs = wg.shape[3]
    tm = _tile(S, 512)
    nt = S // tm
    n_carry = len(carry[2]) if carry else 0

    def body(h_ref, wg_ref, wu_ref, *rest):
        carried_in, rest = rest[:n_carry], rest[n_carry:]
        g_ref, u_ref, a_ref = rest[:3]
        carried_out, sems = rest[3:3 + n_carry], rest[3 + n_carry:]
        j, i = pl.program_id(0), pl.program_id(1)
        if carry:
            @pl.when(jnp.logical_and(j == 0, i == 0))
            def _():
                _gather_phase(carry[0], carry[1], carried_in, carried_out, sems[0], sems[1], start=True)
        hv = h_ref[...]
        g = _dot(hv, wg_ref[...])
        u = _dot(hv, wu_ref[...])
        g_ref[...] = g.astype(BF16)
        u_ref[...] = u.astype(BF16)
        a_ref[...] = (g * _sigmoid(g) * u).astype(BF16)
        if carry:
            @pl.when(jnp.logical_and(j == N_CHIPS - 1, i == nt - 1))
            def _():
                _gather_phase(carry[0], carry[1], carried_in, carried_out, sems[0], sems[1], start=False)

    w_spec = pl.BlockSpec((None, None, D, Fs), lambda j, i: (l, j, 0, 0))
    o_spec = pl.BlockSpec((tm, Fs), lambda j, i: (i, j))
    o_shape = jax.ShapeDtypeStruct((S, N_CHIPS * Fs), BF16)
    c_in, c_args, c_out, c_shape, c_scratch, aliases = _carry_plumbing(carry, 3, 3)
    out = pl.pallas_call(
        body, name="gate_up", grid=(N_CHIPS, nt),
        in_specs=[pl.BlockSpec((tm, D), lambda j, i: (i, 0)), w_spec, w_spec] + c_in,
        out_specs=[o_spec, o_spec, o_spec] + c_out, out_shape=[o_shape, o_shape, o_shape] + c_shape,
        scratch_shapes=c_scratch, input_output_aliases=aliases,
        compiler_params=_cparams("arbitrary", "arbitrary"))(h, wg, wu, *c_args)
    return (out[0], out[1], out[2], list(out[3:])) if carry else tuple(out)


def _normed(x, gain):
    r = lax.rsqrt(jnp.mean(x * x, axis=-1, keepdims=True) + RMS_EPS)
    return (x * r * gain).astype(BF16)


def down_proj(a, wd, l, x, next_gain=None, carry=None):
    S, F = a.shape
    D = wd.shape[2]
    tm, tk = _tile(S, 512), F // N_CHIPS
    nk, nt = F // tk, S // tm
    emit = next_gain is not None
    n_carry = len(carry[2]) if carry else 0
    n_out = 2 if emit else 1

    def body(a_ref, w_ref, x_ref, *rest):
        rest = list(rest)
        g_ref = rest.pop(0) if emit else None
        carried_in = [rest.pop(0) for _ in range(n_carry)]
        o_ref = rest.pop(0)
        h_ref = rest.pop(0) if emit else None
        carried_out = [rest.pop(0) for _ in range(n_carry)]
        acc_ref = rest.pop(0)
        i, k = pl.program_id(0), pl.program_id(1)
        if carry:
            @pl.when(jnp.logical_and(i == 0, k == 0))
            def _():
                _gather_phase(carry[0], carry[1], carried_in, carried_out, rest[0], rest[1], start=True)

        @pl.when(k == 0)
        def _():
            acc_ref[...] = jnp.zeros_like(acc_ref)

        acc_ref[...] += _dot(a_ref[...], w_ref[...])

        @pl.when(k == nk - 1)
        def _():
            y = x_ref[...] + 0.5 * acc_ref[...]
            o_ref[...] = y
            if emit:
                h_ref[...] = _normed(y, g_ref[...])

        if carry:
            @pl.when(jnp.logical_and(i == nt - 1, k == nk - 1))
            def _():
                _gather_phase(carry[0], carry[1], carried_in, carried_out, rest[0], rest[1], start=False)

    row = pl.BlockSpec((tm, D), lambda i, k: (i, 0))
    in_specs = [pl.BlockSpec((tm, tk), lambda i, k: (i, k)), pl.BlockSpec((None, tk, D), lambda i, k: (l, k, 0)), row]
    args, out_specs, out_shape = [a, wd, x], [row], [jax.ShapeDtypeStruct((S, D), F32)]
    if emit:
        in_specs.append(pl.BlockSpec((1, D), lambda i, k: (0, 0)))
        args.append(next_gain)
        out_specs.append(row)
        out_shape.append(jax.ShapeDtypeStruct((S, D), BF16))
    c_in, c_args, c_out, c_shape, c_scratch, aliases = _carry_plumbing(carry, len(args), n_out)
    out = pl.pallas_call(
        body, name="down_proj", grid=(nt, nk), in_specs=in_specs + c_in, out_specs=out_specs + c_out,
        out_shape=out_shape + c_shape, scratch_shapes=[pltpu.VMEM((tm, D), F32)] + c_scratch,
        input_output_aliases=aliases, compiler_params=_cparams("arbitrary", "arbitrary"))(*args, *c_args)
    result = (out[0], out[1] if emit else None)
    return result + (list(out[n_out:]),) if carry else result


def down_bwd(dy, wd, l, g, u):
    S, D = dy.shape
    F = g.shape[1]
    Fs = F // N_CHIPS
    tm = _tile(S, 512)

    def body(dy_ref, w_ref, g_ref, u_ref, dg_ref, du_ref):
        da = 0.5 * _dot_nt(dy_ref[...].astype(BF16), w_ref[...])
        gv, uv = g_ref[...].astype(F32), u_ref[...].astype(F32)
        sig = _sigmoid(gv)
        du_ref[...] = (da * (gv * sig)).astype(BF16)
        dg_ref[...] = (da * uv * (sig * (1.0 + gv * (1.0 - sig)))).astype(BF16)

    t_spec = pl.BlockSpec((tm, Fs), lambda j, i: (i, j))
    o_shape = jax.ShapeDtypeStruct((S, F), BF16)
    return pl.pallas_call(
        body, name="down_bwd", grid=(N_CHIPS, S // tm),
        in_specs=[pl.BlockSpec((tm, D), lambda j, i: (i, 0)), pl.BlockSpec((None, Fs, D), lambda j, i: (l, j, 0)),
                  t_spec, t_spec],
        out_specs=[t_spec, t_spec], out_shape=[o_shape, o_shape],
        compiler_params=_cparams("arbitrary", "arbitrary"))(dy, wd, g, u)


def gate_up_bwd(dg, du, wg, wu, l):
    S, F = dg.shape
    D, Fs = wg.shape[2], wg.shape[3]
    tm = _tile(S, 512)

    def body(dg_ref, du_ref, wg_ref, wu_ref, o_ref):
        k = pl.program_id(1)

        @pl.when(k == 0)
        def _():
            o_ref[...] = jnp.zeros_like(o_ref)

        o_ref[...] += _dot_nt(dg_ref[...], wg_ref[...]) + _dot_nt(du_ref[...], wu_ref[...])

    t_spec = pl.BlockSpec((tm, Fs), lambda i, k: (i, k))
    w_spec = pl.BlockSpec((None, None, D, Fs), lambda i, k: (l, k, 0, 0))
    return pl.pallas_call(
        body, name="gate_up_bwd", grid=(S // tm, N_CHIPS), in_specs=[t_spec, t_spec, w_spec, w_spec],
        out_specs=pl.BlockSpec((tm, D), lambda i, k: (i, 0)), out_shape=jax.ShapeDtypeStruct((S, D), F32),
        compiler_params=_cparams("arbitrary", "arbitrary"))(dg, du, wg, wu)


def mm_nn(a, b, l, tn, out_dtype, resid=None, next_gain=None):
    S, K = a.shape
    N = b.shape[2]
    tm = _tile(S, 512)
    emit = next_gain is not None
    assert not emit or tn == N

    def body(a_ref, b_ref, *rest):
        rest = list(rest)
        acc = _dot(a_ref[...].astype(BF16), b_ref[...])
        if resid is not None:
            acc = rest.pop(0)[...] + acc
        g_ref = rest.pop(0) if emit else None
        rest[0][...] = acc.astype(out_dtype)
        if emit:
            rest[1][...] = _normed(acc, g_ref[...])

    o_spec = pl.BlockSpec((tm, tn), lambda n, i: (i, n))
    in_specs = [pl.BlockSpec((tm, K), lambda n, i: (i, 0)), pl.BlockSpec((None, K, tn), lambda n, i: (l, 0, n))]
    args, out_specs, out_shape = [a, b], [o_spec], [jax.ShapeDtypeStruct((S, N), out_dtype)]
    if resid is not None:
        in_specs.append(o_spec)
        args.append(resid)
    if emit:
        in_specs.append(pl.BlockSpec((1, N), lambda n, i: (0, 0)))
        args.append(next_gain)
        out_specs.append(o_spec)
        out_shape.append(jax.ShapeDtypeStruct((S, N), BF16))
    out = pl.pallas_call(
        body, name="mm_nn", grid=(N // tn, S // tm), in_specs=in_specs, out_specs=out_specs, out_shape=out_shape,
        compiler_params=_cparams("arbitrary", "arbitrary"))(*args)
    return out if emit else out[0]


def mm_nt(a, b, l, tn, out_dtype):
    S, K = a.shape
    N = b.shape[1]
    tm = _tile(S, 512)

    def body(a_ref, b_ref, o_ref):
        o_ref[...] = _dot_nt(a_ref[...].astype(BF16), b_ref[...]).astype(out_dtype)

    return pl.pallas_call(
        body, name="mm_nt", grid=(N // tn, S // tm),
        in_specs=[pl.BlockSpec((tm, K), lambda n, i: (i, 0)), pl.BlockSpec((None, tn, K), lambda n, i: (l, n, 0))],
        out_specs=pl.BlockSpec((tm, tn), lambda n, i: (i, n)), out_shape=jax.ShapeDtypeStruct((S, N), out_dtype),
        compiler_params=_cparams("arbitrary", "arbitrary"))(a, b)


def mm_tn(a, b, tka, tnb, scale=1.0, blocked=False, layer=None, layers=None, into=None):
    S, Ka = a.shape
    Nb = b.shape[1]
    ts = _tile(S, 1024)
    ns = S // ts

    def body(a_ref, b_ref, *rest):
        o_ref, acc_ref = rest[-2:]
        s = pl.program_id(2)

        @pl.when(s == 0)
        def _():
            acc_ref[...] = jnp.zeros_like(acc_ref)

        acc_ref[...] += _dot_tn(a_ref[...].astype(BF16), b_ref[...].astype(BF16))

        @pl.when(s == ns - 1)
        def _():
            o_ref[...] = (scale * acc_ref[...]).astype(BF16)

    if blocked:
        block, shape = (None, tka, tnb), (Nb // tnb, Ka, tnb)
        index = lambda ka, nb, s: (nb, ka, 0)
    else:
        block, shape = (tka, tnb), (Ka, Nb)
        index = lambda ka, nb, s: (ka, nb)
    if layer is not None:
        block, shape = (None,) + block, (layers,) + shape
        inner = index
        index = lambda ka, nb, s: (layer,) + inner(ka, nb, s)
    in_specs = [pl.BlockSpec((ts, tka), lambda ka, nb, s: (s, ka)), pl.BlockSpec((ts, tnb), lambda ka, nb, s: (s, nb))]
    args, aliases = [a, b], {}
    if into is not None:
        in_specs.append(pl.BlockSpec(memory_space=pl.ANY))
        args.append(into)
        aliases = {2: 0}
    return pl.pallas_call(
        body, name="mm_tn", grid=(Ka // tka, Nb // tnb, ns), in_specs=in_specs,
        out_specs=pl.BlockSpec(block, index), out_shape=jax.ShapeDtypeStruct(shape, BF16),
        input_output_aliases=aliases, scratch_shapes=[pltpu.VMEM((tka, tnb), F32)],
        compiler_params=_cparams("arbitrary", "arbitrary", "arbitrary"))(*args)


def _rope(x, c2, s2):
    half = x.shape[-1] // 2
    rot = jnp.concatenate([x[:, half:], x[:, :half]], axis=-1)
    return x * c2 + rot * s2


def _tri(tm, upper):
    r = lax.broadcasted_iota(jnp.int32, (tm, tm), 0)
    c = lax.broadcasted_iota(jnp.int32, (tm, tm), 1)
    return jnp.where((c >= r) if upper else (c <= r), 1.0, 0.0).astype(F32)


def _log_sigmoid(x):
    return jnp.minimum(x, 0.0) - jnp.log(1.0 + jnp.exp(-jnp.abs(x)))


def _norm_hat(c):
    r = lax.rsqrt(jnp.mean(c * c, axis=-1, keepdims=True) + RMS_EPS)
    return c * r, r


def _tok_spec(tm, width, rev_n=None):
    if rev_n is None:
        return pl.BlockSpec((tm, width), lambda i: (i, 0))
    return pl.BlockSpec((tm, width), lambda i: (rev_n - 1 - i, 0))


def _head_spec(heads, tm, width, rev_n=None):
    if rev_n is None:
        return pl.BlockSpec((heads, tm, width), lambda i: (0, i, 0))
    return pl.BlockSpec((heads, tm, width), lambda i: (0, rev_n - 1 - i, 0))


def _lane_spec(heads, width, tm, rev_n=None):
    if rev_n is None:
        return pl.BlockSpec((heads, width, tm), lambda i: (0, 0, i))
    return pl.BlockSpec((heads, width, tm), lambda i: (0, 0, rev_n - 1 - i))


def _full_spec(shape):
    return pl.BlockSpec(shape, lambda i: (0,) * len(shape))


def mixer_prep(p, c2, s2, q_norm, kv_norm, bias, wqb, wkvb, l):
    S = p.shape[0]
    tm = _tile(S, 256)
    H = N_HEADS

    def body(p_ref, c2_ref, s2_ref, qn_ref, kvn_ref, b_ref, wqb_ref, wkvb_ref,
             cqn_ref, ckvn_ref, qmt_ref, km_ref, kmt_ref, vm_ref, vmt_ref, qst_ref, ks_ref, kst_ref, vs_ref, vst_ref,
             qft_ref, kf_ref, kft_ref, vf_ref, vft_ref, ccol_ref, carry_row):
        c2, s2 = c2_ref[...], s2_ref[...]
        cqn = (_norm_hat(p_ref[:, P_CQ:P_CQ + Q_LORA])[0] * qn_ref[...]).astype(BF16)
        ckvn = (_norm_hat(p_ref[:, P_CKV:P_CKV + KV_LORA])[0] * kvn_ref[...]).astype(BF16)
        cqn_ref[...] = cqn
        ckvn_ref[...] = ckvn
        q = _dot(cqn, wqb_ref[...])
        kv = _dot(ckvn, wkvb_ref[...])
        k_pe = _rope(p_ref[:, P_KR:P_KR + ROPE], c2, s2)
        k_pe_t = k_pe.T.astype(BF16)
        k_pe = k_pe.astype(BF16)

        def both_ways(x, tok_ref, lane_ref, h):
            tok_ref[h] = x.astype(BF16)
            lane_ref[h] = x.T.astype(BF16)

        qs_mla, qs_hd = QK_MLA ** -0.5 * LOG2E, HD ** -0.5 * LOG2E
        for h in range(H):
            qmt_ref[h, 0:NOPE, :] = (q[:, h * QK_MLA:h * QK_MLA + NOPE] * qs_mla).T.astype(BF16)
            qmt_ref[h, NOPE:QK_MLA, :] = (_rope(q[:, h * QK_MLA + NOPE:(h + 1) * QK_MLA], c2, s2) * qs_mla).T.astype(BF16)
            k_nope = kv[:, h * 256:h * 256 + NOPE]
            km_ref[h, :, 0:NOPE] = k_nope.astype(BF16)
            km_ref[h, :, NOPE:QK_MLA] = k_pe
            kmt_ref[h, 0:NOPE, :] = k_nope.T.astype(BF16)
            kmt_ref[h, NOPE:QK_MLA, :] = k_pe_t
            both_ways(kv[:, h * 256 + NOPE:(h + 1) * 256], vm_ref, vmt_ref, h)
            qst_ref[h] = (_rope(p_ref[:, P_QS + h * HD:P_QS + (h + 1) * HD], c2, s2) * qs_hd).T.astype(BF16)
            qft_ref[h] = (p_ref[:, P_QF + h * HD:P_QF + (h + 1) * HD] * qs_hd).T.astype(BF16)
            both_ways(p_ref[:, P_KF + h * HD:P_KF + (h + 1) * HD], kf_ref, kft_ref, h)
            both_ways(p_ref[:, P_VF + h * HD:P_VF + (h + 1) * HD], vf_ref, vft_ref, h)
        for h in range(SWA_KV):
            both_ways(_rope(p_ref[:, P_KS + h * HD:P_KS + (h + 1) * HD], c2, s2), ks_ref, kst_ref, h)
            both_ways(p_ref[:, P_VS + h * HD:P_VS + (h + 1) * HD], vs_ref, vst_ref, h)

        @pl.when(pl.program_id(0) == 0)
        def _():
            carry_row[...] = jnp.zeros_like(carry_row)

        log_f = _log_sigmoid(p_ref[:, P_F:P_F + H] + b_ref[...])
        c_tok = jnp.dot(_tri(tm, upper=False), log_f, preferred_element_type=F32,
                        precision=lax.Precision.HIGHEST) + carry_row[0:1, 0:H]
        for h in range(H):
            ccol_ref[h] = c_tok[:, h:h + 1] * LOG2E
        carry_row[0:1, 0:H] = c_tok[tm - 1:tm, :]

    out_shape = [jax.ShapeDtypeStruct((S, Q_LORA), BF16), jax.ShapeDtypeStruct((S, KV_LORA), BF16)]
    out_specs = [_tok_spec(tm, Q_LORA), _tok_spec(tm, KV_LORA)]

    def add(heads, d, lanes):
        out_shape.append(jax.ShapeDtypeStruct((heads, d, S) if lanes else (heads, S, d), BF16))
        out_specs.append(_lane_spec(heads, d, tm) if lanes else _head_spec(heads, tm, d))

    for heads_q, heads_kv, dqk, dv in ((H, H, QK_MLA, VDIM), (H, SWA_KV, HD, HD), (H, H, HD, HD)):
        add(heads_q, dqk, True)
        add(heads_kv, dqk, False)
        add(heads_kv, dqk, True)
        add(heads_kv, dv, False)
        add(heads_kv, dv, True)
    out_shape.append(jax.ShapeDtypeStruct((H, S, 1), F32))
    out_specs.append(_head_spec(H, tm, 1))
    in_specs = [_tok_spec(tm, P_COLS), _tok_spec(tm, ROPE), _tok_spec(tm, ROPE), _full_spec((1, Q_LORA)),
                _full_spec((1, KV_LORA)), _full_spec((1, H)),
                pl.BlockSpec((None,) + wqb.shape[1:], lambda i: (l, 0, 0)),
                pl.BlockSpec((None,) + wkvb.shape[1:], lambda i: (l, 0, 0))]
    return pl.pallas_call(
        body, name="mixer_prep", grid=(S // tm,), in_specs=in_specs, out_specs=out_specs, out_shape=out_shape,
        scratch_shapes=[pltpu.VMEM((8, 128), F32)],
        compiler_params=_cparams("arbitrary"))(p, c2, s2, q_norm, kv_norm, bias, wqb, wkvb)


def mixer_prep_bwd(p, c2, s2, q_norm, kv_norm, bias, wqb, wkvb, l, dqm, dkm, dvm, dqs, dks, dvs, dqf, dkf, dvf, dc):
    S = p.shape[0]
    tm = _tile(S, 256)
    nt = S // tm
    H, G = N_HEADS, N_HEADS // SWA_KV

    def body(p_ref, c2_ref, s2_ref, qn_ref, kvn_ref, b_ref, wqb_ref, wkvb_ref,
             dqm_ref, dkm_ref, dvm_ref, dqs_ref, dks_ref, dvs_ref, dqf_ref, dkf_ref, dvf_ref, dc_ref,
             dp_ref, dq_ref, dkv_ref, dqn_ref, dkvn_ref, db_ref, carry):
        c2, s2 = c2_ref[...], -s2_ref[...]

        @pl.when(pl.program_id(0) == 0)
        def _():
            dqn_ref[...] = jnp.zeros_like(dqn_ref)
            dkvn_ref[...] = jnp.zeros_like(dkvn_ref)
            db_ref[...] = jnp.zeros_like(db_ref)
            carry[...] = jnp.zeros_like(carry)

        sc_mla, sc_hd = QK_MLA ** -0.5, HD ** -0.5
        dk_pe_t = jnp.zeros((ROPE, tm), F32)
        for h in range(H):
            dq_ref[:, h * QK_MLA:h * QK_MLA + NOPE] = (dqm_ref[h, 0:NOPE, :] * sc_mla).T.astype(BF16)
            dq_ref[:, h * QK_MLA + NOPE:(h + 1) * QK_MLA] = _rope(
                (dqm_ref[h, NOPE:QK_MLA, :] * sc_mla).T, c2, s2).astype(BF16)
            dkv_ref[:, h * 256:h * 256 + NOPE] = (dkm_ref[h, 0:NOPE, :] * LN2).T.astype(BF16)
            dkv_ref[:, h * 256 + NOPE:(h + 1) * 256] = dvm_ref[h].T.astype(BF16)
            dk_pe_t = dk_pe_t + dkm_ref[h, NOPE:QK_MLA, :]
        dk_pe = (dk_pe_t * LN2).T

        def through_norm(dcn, c, gain, dgain_ref):
            c_hat, r = _norm_hat(c)
            dhg = dcn * gain
            dgain_ref[...] += jnp.sum(dcn * c_hat, axis=0, keepdims=True)
            return r * (dhg - c_hat * jnp.mean(dhg * c_hat, axis=-1, keepdims=True))

        dcqn = _dot_nt(dq_ref[...], wqb_ref[...])
        dckvn = _dot_nt(dkv_ref[...], wkvb_ref[...])
        dp_ref[:, P_CQ:P_CQ + Q_LORA] = through_norm(dcqn, p_ref[:, P_CQ:P_CQ + Q_LORA], qn_ref[...], dqn_ref).astype(BF16)
        dp_ref[:, P_CKV:P_CKV + KV_LORA] = through_norm(
            dckvn, p_ref[:, P_CKV:P_CKV + KV_LORA], kvn_ref[...], dkvn_ref).astype(BF16)
        for h in range(H):
            dp_ref[:, P_QS + h * HD:P_QS + (h + 1) * HD] = _rope((dqs_ref[h] * sc_hd).T, c2, s2).astype(BF16)
            dp_ref[:, P_QF + h * HD:P_QF + (h + 1) * HD] = (dqf_ref[h] * sc_hd).T.astype(BF16)
            dp_ref[:, P_KF + h * HD:P_KF + (h + 1) * HD] = (dkf_ref[h] * LN2).T.astype(BF16)
            dp_ref[:, P_VF + h * HD:P_VF + (h + 1) * HD] = dvf_ref[h].T.astype(BF16)
        for kvh in range(SWA_KV):
            dk = dks_ref[kvh * G]
            dv = dvs_ref[kvh * G]
            for g in range(1, G):
                dk = dk + dks_ref[kvh * G + g]
                dv = dv + dvs_ref[kvh * G + g]
            dp_ref[:, P_KS + kvh * HD:P_KS + (kvh + 1) * HD] = _rope((dk * LN2).T, c2, s2).astype(BF16)
            dp_ref[:, P_VS + kvh * HD:P_VS + (kvh + 1) * HD] = dv.T.astype(BF16)

        dcv = dc_ref[...]
        dlog_f = jnp.dot(_tri(tm, upper=True), dcv, preferred_element_type=F32,
                         precision=lax.Precision.HIGHEST) + carry[0:1, 0:H]
        carry[0:1, 0:H] = dlog_f[0:1, :]
        df = dlog_f * _sigmoid(-(p_ref[:, P_F:P_F + H] + b_ref[...]))
        db_ref[...] += jnp.sum(df, axis=0, keepdims=True)
        dp_ref[:, P_KR:P_COLS] = jnp.zeros((tm, P_COLS - P_KR), BF16)
        dp_ref[:, P_KR:P_KR + ROPE] = _rope(dk_pe, c2, s2).astype(BF16)
        dp_ref[:, P_F:P_F + H] = df.astype(BF16)

    rev = nt
    in_specs = [_tok_spec(tm, P_COLS, rev), _tok_spec(tm, ROPE, rev), _tok_spec(tm, ROPE, rev), _full_spec((1, Q_LORA)),
                _full_spec((1, KV_LORA)), _full_spec((1, H)),
                pl.BlockSpec((None,) + wqb.shape[1:], lambda i: (l, 0, 0)),
                pl.BlockSpec((None,) + wkvb.shape[1:], lambda i: (l, 0, 0)),
                _lane_spec(H, QK_MLA, tm, rev), _lane_spec(H, QK_MLA, tm, rev), _lane_spec(H, VDIM, tm, rev)]
    in_specs += [_lane_spec(H, HD, tm, rev)] * 6 + [_tok_spec(tm, H, rev)]
    out_specs = [_tok_spec(tm, P_COLS, rev), _tok_spec(tm, N_HEADS * QK_MLA, rev), _tok_spec(tm, N_HEADS * 256, rev),
                 _full_spec((1, Q_LORA)), _full_spec((1, KV_LORA)), _full_spec((1, H))]
    out_shape = [jax.ShapeDtypeStruct((S, P_COLS), BF16), jax.ShapeDtypeStruct((S, N_HEADS * QK_MLA), BF16),
                 jax.ShapeDtypeStruct((S, N_HEADS * 256), BF16), jax.ShapeDtypeStruct((1, Q_LORA), F32),
                 jax.ShapeDtypeStruct((1, KV_LORA), F32), jax.ShapeDtypeStruct((1, H), F32)]
    return pl.pallas_call(
        body, name="mixer_prep_bwd", grid=(nt,), in_specs=in_specs, out_specs=out_specs, out_shape=out_shape,
        scratch_shapes=[pltpu.VMEM((8, 128), F32)], compiler_params=_cparams("arbitrary"))(
            p, c2, s2, q_norm, kv_norm, bias, wqb, wkvb, dqm, dkm, dvm, dqs, dks, dvs, dqf, dkf, dvf, dc)


def merge_heads(o_mla, o_swa, o_fox):
    H, S, _ = o_mla.shape
    tm = _tile(S, 512)
    width = H * (VDIM + 2 * HD)

    def body(om_ref, os_ref, of_ref, m_ref):
        for h in range(H):
            m_ref[:, h * VDIM:(h + 1) * VDIM] = om_ref[h]
            m_ref[:, H * VDIM + h * HD:H * VDIM + (h + 1) * HD] = os_ref[h]
            m_ref[:, H * (VDIM + HD) + h * HD:H * (VDIM + HD) + (h + 1) * HD] = of_ref[h]

    return pl.pallas_call(
        body, name="merge_heads", grid=(S // tm,),
        in_specs=[_head_spec(H, tm, VDIM), _head_spec(H, tm, HD), _head_spec(H, tm, HD)],
        out_specs=_tok_spec(tm, width), out_shape=jax.ShapeDtypeStruct((S, width), BF16),
        compiler_params=_cparams("arbitrary"))(o_mla, o_swa, o_fox)


def split_heads(dmixed, mixed):
    S, width = mixed.shape
    H = N_HEADS
    tm = _tile(S, 512)

    def body(dm_ref, m_ref, dom_ref, dos_ref, dof_ref, dm_delta, ds_delta, df_delta):
        def one(h, off, d, do_ref, delta_ref):
            dv = dm_ref[:, off:off + d].astype(F32)
            do_ref[h] = dv.T.astype(BF16)
            prod = dv * m_ref[:, off:off + d].astype(F32)
            rows = lax.dot_general(jnp.ones((8, d), F32), prod, (((1,), (1,)), ((), ())),
                                   preferred_element_type=F32, precision=lax.Precision.HIGHEST)
            delta_ref[h] = rows[0:1, :]

        for h in range(H):
            one(h, h * VDIM, VDIM, dom_ref, dm_delta)
            one(h, H * VDIM + h * HD, HD, dos_ref, ds_delta)
            one(h, H * (VDIM + HD) + h * HD, HD, dof_ref, df_delta)

    row_spec = pl.BlockSpec((H, 1, tm), lambda i: (0, 0, i))
    row_shape = jax.ShapeDtypeStruct((H, 1, S), F32)
    return pl.pallas_call(
        body, name="split_heads", grid=(S // tm,),
        in_specs=[_tok_spec(tm, width), _tok_spec(tm, width)],
        out_specs=[_lane_spec(H, VDIM, tm), _lane_spec(H, HD, tm), _lane_spec(H, HD, tm), row_spec, row_spec, row_spec],
        out_shape=[jax.ShapeDtypeStruct((H, VDIM, S), BF16), jax.ShapeDtypeStruct((H, HD, S), BF16),
                   jax.ShapeDtypeStruct((H, HD, S), BF16), row_shape, row_shape, row_shape],
        compiler_params=_cparams("arbitrary"))(dmixed, mixed)


def _attn_tile(S):
    return 512 if (S % 512 == 0 and S > 512) else S // 2


def _valid(q0, k0, shape, q_axis, window):
    qpos = q0 + lax.broadcasted_iota(jnp.int32, shape, q_axis)
    kpos = k0 + lax.broadcasted_iota(jnp.int32, shape, 1 - q_axis)
    ok = kpos <= qpos
    if window is not None:
        ok = jnp.logical_and(ok, kpos > qpos - window)
    return ok


def attn_fwd(name, qt, k, vt, window=None, sinks=None, ccol=None, carry=None):
    H, dq, S = qt.shape
    Hk, dv, _ = vt.shape
    G = H // Hk
    t = _attn_tile(S)
    nq = S // t
    fox, use_sink = ccol is not None, sinks is not None
    n_carry = len(carry[2]) if carry else 0

    def body(*refs):
        refs = list(refs)
        sink_ref = refs.pop(0) if use_sink else None
        q_ref, k_ref, vt_ref = refs[:3]
        refs = refs[3:]
        ccol_ref = refs.pop(0) if fox else None
        carried_in = [refs.pop(0) for _ in range(n_carry)]
        o_ref, lse_ref = refs[:2]
        refs = refs[2:]
        carried_out = [refs.pop(0) for _ in range(n_carry)]
        m_ref, l_ref, acc_ref = refs[:3]
        h, i = pl.program_id(0), pl.program_id(1)
        if carry:
            @pl.when(jnp.logical_and(h == 0, i == 0))
            def _():
                _gather_phase(carry[0], carry[1], carried_in, carried_out, refs[3], refs[4], start=True)
        qv = q_ref[...]
        if use_sink:
            m_ref[...] = jnp.full(m_ref.shape, sink_ref[h] * LOG2E, F32)
            l_ref[...] = jnp.ones(l_ref.shape, F32)
        else:
            m_ref[...] = jnp.full(m_ref.shape, NEG, F32)
            l_ref[...] = jnp.zeros(l_ref.shape, F32)
        acc_ref[...] = jnp.zeros(acc_ref.shape, F32)

        def steps(blocks):
            offs = [pl.multiple_of(j * t, t) for j, _ in blocks]
            scores = [_dot(k_ref[pl.ds(off, t), :], qv) for off in offs]
            for (j, masked), off, st in zip(blocks, offs, scores):
                if fox:
                    st = st - ccol_ref[pl.ds(off, t), :]
                if masked:
                    st = jnp.where(_valid(i * t, j * t, (t, t), 1, window), st, NEG)
                m_prev = m_ref[...]
                m_new = jnp.maximum(m_prev, jnp.max(st, axis=0, keepdims=True))
                alpha = jnp.exp2(m_prev - m_new)
                pt = jnp.exp2(st - m_new)
                l_ref[...] = alpha * l_ref[...] + jnp.sum(pt, axis=0, keepdims=True)
                acc_ref[...] = alpha * acc_ref[...] + _dot(vt_ref[:, pl.ds(off, t)], pt.astype(BF16))
                m_ref[...] = m_new

        if window is None:
            def pair(n, carry):
                steps([(2 * n, False), (2 * n + 1, False)])
                return carry
            lax.fori_loop(0, i // 2, pair, 0)

            @pl.when(i % 2 == 1)
            def _():
                steps([(i - 1, False), (i, True)])

            @pl.when(i % 2 == 0)
            def _():
                steps([(i, True)])
        else:
            def one(j, carry):
                steps([(j, True)])
                return carry
            lax.fori_loop(jnp.maximum(i * t - (window - 1), 0) // t, i + 1, one, 0)
        l = l_ref[...]
        o_ref[...] = (acc_ref[...] / l).T.astype(BF16)
        lse_ref[...] = m_ref[...] + jnp.log2(l)
        if carry:
            @pl.when(jnp.logical_and(h == H - 1, i == nq - 1))
            def _():
                _gather_phase(carry[0], carry[1], carried_in, carried_out, refs[3], refs[4], start=False)

    in_specs, args = [], []
    if use_sink:
        in_specs.append(pl.BlockSpec(memory_space=pltpu.SMEM))
        args.append(sinks)
    in_specs += [pl.BlockSpec((None, dq, t), lambda h, i: (h, 0, i)),
                 pl.BlockSpec((None, S, dq), lambda h, i: (h // G, 0, 0)),
                 pl.BlockSpec((None, dv, S), lambda h, i: (h // G, 0, 0))]
    args += [qt, k, vt]
    if fox:
        in_specs.append(pl.BlockSpec((None, S, 1), lambda h, i: (h, 0, 0)))
        args.append(ccol)
    out_specs = [pl.BlockSpec((None, t, dv), lambda h, i: (h, i, 0)), pl.BlockSpec((None, 1, t), lambda h, i: (h, 0, i))]
    out_shape = [jax.ShapeDtypeStruct((H, S, dv), BF16), jax.ShapeDtypeStruct((H, 1, S), F32)]
    scratch = [pltpu.VMEM((1, t), F32), pltpu.VMEM((1, t), F32), pltpu.VMEM((dv, t), F32)]
    aliases = {}
    if carry:
        aliases = {len(args) + n: 2 + n for n in range(n_carry)}
        in_specs += [_ANY] * n_carry
        args += list(carry[2])
        out_specs += [_ANY] * n_carry
        out_shape += [jax.ShapeDtypeStruct(a.shape, a.dtype) for a in carry[2]]
        scratch += [pltpu.SemaphoreType.DMA((n_carry, 3)), pltpu.SemaphoreType.DMA((n_carry, 3))]
    out = pl.pallas_call(
        body, name=name, grid=(H, nq), in_specs=in_specs, out_specs=out_specs, out_shape=out_shape,
        scratch_shapes=scratch, input_output_aliases=aliases,
        compiler_params=_cparams("arbitrary", "arbitrary"))(*args)
    return (out[0], out[1], list(out[2:])) if carry else (out[0], out[1])


def attn_bwd(name, qt, k, kt, v, dot, lse, delta, window=None, sinks=None, ccol=None):
    H, dq, S = qt.shape
    Hk, _, dv = v.shape
    G = H // Hk
    t = _attn_tile(S)
    nq = S // t
    fox, use_sink = ccol is not None, sinks is not None

    def body(*refs):
        refs = list(refs)
        sink_ref = refs.pop(0) if use_sink else None
        qt_ref, k_ref, kt_ref, v_ref, dot_ref, lse_ref, delta_ref = refs[:7]
        refs = refs[7:]
        ccol_ref = refs.pop(0) if fox else None
        dqt_ref, dkt_ref, dvt_ref = refs[:3]
        refs = refs[3:]
        dcq_ref, dck_ref = (refs.pop(0), refs.pop(0)) if fox else (None, None)
        dsink_ref = refs.pop(0) if use_sink else None
        h, j = pl.program_id(0), pl.program_id(1)

        @pl.when(j == 0)
        def _():
            dqt_ref[...] = jnp.zeros(dqt_ref.shape, F32)
            if fox:
                dcq_ref[...] = jnp.zeros(dcq_ref.shape, F32)
            if use_sink:
                ps = jnp.exp2(sink_ref[h] * LOG2E - lse_ref[...]) * delta_ref[...]
                dsink_ref[...] = jnp.broadcast_to(-jnp.sum(ps, axis=-1, keepdims=True), dsink_ref.shape)

        dkt_ref[...] = jnp.zeros(dkt_ref.shape, F32)
        dvt_ref[...] = jnp.zeros(dvt_ref.shape, F32)
        if fox:
            dck_ref[...] = jnp.zeros(dck_ref.shape, F32)
        kv, ktv, vv = k_ref[...], kt_ref[...], v_ref[...]

        def steps(blocks):
            offs = [pl.multiple_of(i * t, t) for i, _ in blocks]
            qts = [qt_ref[:, pl.ds(off, t)] for off in offs]
            dots = [dot_ref[:, pl.ds(off, t)] for off in offs]
            scores = [_dot(kv, qti) for qti in qts]
            dprobs = [_dot(vv, doti) for doti in dots]
            for (i, masked), off, qti, doti, st, dpt in zip(blocks, offs, qts, dots, scores, dprobs):
                if fox:
                    st = st - ccol_ref[...]
                if masked:
                    st = jnp.where(_valid(i * t, j * t, (t, t), 1, window), st, NEG)
                pt = jnp.exp2(st - lse_ref[:, pl.ds(off, t)])
                dvt_ref[...] += _dot_nt(doti, pt.astype(BF16))
                dst = pt * (dpt - delta_ref[:, pl.ds(off, t)])
                if fox:
                    dcq_ref[:, pl.ds(off, t)] += jnp.sum(dst, axis=0, keepdims=True)
                    dck_ref[...] -= jnp.sum(dst, axis=1, keepdims=True)
                dsb = dst.astype(BF16)
                dkt_ref[...] += _dot_nt(qti, dsb)
                dqt_ref[:, pl.ds(off, t)] += _dot(ktv, dsb)

        if window is None:
            odd = (nq - 1 - j) % 2

            @pl.when(odd == 1)
            def _():
                steps([(j, True), (j + 1, False)])

            @pl.when(odd == 0)
            def _():
                steps([(j, True)])

            first = j + 1 + odd

            def pair(n, carry):
                steps([(first + 2 * n, False), (first + 2 * n + 1, False)])
                return carry
            lax.fori_loop(0, (nq - first) // 2, pair, 0)
        else:
            def one(i, carry):
                steps([(i, True)])
                return carry
            lax.fori_loop(j, jnp.minimum((j * t + t - 1 + window - 1) // t, nq - 1) + 1, one, 0)

    in_specs, args = [], []
    if use_sink:
        in_specs.append(pl.BlockSpec(memory_space=pltpu.SMEM))
        args.append(sinks)
    whole = lambda d: pl.BlockSpec((None, d, S), lambda h, j: (h, 0, 0))
    keys = lambda d: pl.BlockSpec((None, d, t), lambda h, j: (h, 0, j))
    row = pl.BlockSpec((None, 1, S), lambda h, j: (h, 0, 0))
    in_specs += [whole(dq), pl.BlockSpec((None, t, dq), lambda h, j: (h // G, j, 0)),
                 pl.BlockSpec((None, dq, t), lambda h, j: (h // G, 0, j)),
                 pl.BlockSpec((None, t, dv), lambda h, j: (h // G, j, 0)), whole(dv), row, row]
    args += [qt, k, kt, v, dot, lse, delta]
    out_specs = [whole(dq), keys(dq), keys(dv)]
    out_shape = [jax.ShapeDtypeStruct((H, dq, S), F32), jax.ShapeDtypeStruct((H, dq, S), F32),
                 jax.ShapeDtypeStruct((H, dv, S), F32)]
    if fox:
        in_specs.append(pl.BlockSpec((None, t, 1), lambda h, j: (h, j, 0)))
        args.append(ccol)
        out_specs += [row, pl.BlockSpec((None, t, 1), lambda h, j: (h, j, 0))]
        out_shape += [jax.ShapeDtypeStruct((H, 1, S), F32), jax.ShapeDtypeStruct((H, S, 1), F32)]
    if use_sink:
        out_specs.append(pl.BlockSpec((None, 1, 128), lambda h, j: (h, 0, 0)))
        out_shape.append(jax.ShapeDtypeStruct((H, 1, 128), F32))
    return pl.pallas_call(
        body, name=name, grid=(H, nq), in_specs=in_specs, out_specs=out_specs, out_shape=out_shape,
        compiler_params=_cparams("arbitrary", "arbitrary"))(*args)


def _rows_tile(rows):
    for tr in (256, 128, 64, 32, 16, 8):
        if rows % tr == 0:
            return tr
    return rows


def adamw(w, g, m, v):
    L, R, C = w.shape
    tr = _rows_tile(R)

    def body(w_ref, g_ref, m_ref, v_ref, g_out, d_ref, nm_ref, nv_ref):
        gv = g_ref[...]
        mn = ADAM_B1 * m_ref[...] + (1.0 - ADAM_B1) * gv
        vn = ADAM_B2 * v_ref[...] + (1.0 - ADAM_B2) * (gv * gv)
        m_hat = mn / (1.0 - ADAM_B1 ** ADAM_STEP)
        v_hat = vn / (1.0 - ADAM_B2 ** ADAM_STEP)
        d_ref[...] = -ADAM_LR * (m_hat / (jnp.sqrt(v_hat) + ADAM_EPS) + ADAM_WD * w_ref[...])
        nm_ref[...] = mn
        nv_ref[...] = vn
        g_out[...] = gv

    spec = pl.BlockSpec((None, tr, C), lambda l, i: (l, i, 0))
    shape = jax.ShapeDtypeStruct((L, R, C), F32)
    return pl.pallas_call(
        body, name="adamw", grid=(L, R // tr), in_specs=[spec] * 4, out_specs=[spec] * 4, out_shape=[shape] * 4,
        compiler_params=_cparams("arbitrary", "arbitrary"))(w, g, m, v)


def place_own(w, chip):
    L, R, C = w.shape
    tr = _rows_tile(R)

    def body(c_ref, w_ref, o_ref):
        o_ref[...] = w_ref[...].astype(BF16)

    return pl.pallas_call(
        body, name="place_own",
        grid_spec=pltpu.PrefetchScalarGridSpec(
            num_scalar_prefetch=1, grid=(L, R // tr),
            in_specs=[pl.BlockSpec((None, tr, C), lambda l, i, c: (l, i, 0))],
            out_specs=pl.BlockSpec((None, None, tr, C), lambda l, i, c: (l, c[0], i, 0))),
        out_shape=jax.ShapeDtypeStruct((L, N_CHIPS, R, C), BF16),
        compiler_params=_cparams("arbitrary", "arbitrary"))(chip, w)


def sum_pair(grad, other, layer):
    _, R, C = grad.shape
    tr = _rows_tile(R)

    def body(l_ref, g_ref, o_ref, out_ref):
        out_ref[...] = (g_ref[...].astype(F32) + o_ref[...].astype(F32)).astype(BF16)

    return pl.pallas_call(
        body, name="sum_pair",
        grid_spec=pltpu.PrefetchScalarGridSpec(
            num_scalar_prefetch=1, grid=(R // tr,),
            in_specs=[pl.BlockSpec((None, tr, C), lambda i, l: (l[0], i, 0)), pl.BlockSpec((tr, C), lambda i, l: (i, 0))],
            out_specs=pl.BlockSpec((tr, C), lambda i, l: (i, 0))),
        out_shape=jax.ShapeDtypeStruct((R, C), BF16), compiler_params=_cparams("arbitrary"))(layer, grad, other)


def sum_chips(part, recv, chip, layer):
    _, R, C = part.shape
    tr = _rows_tile(R)

    def body(c_ref, l_ref, p_ref, r_ref, out_ref):
        acc = p_ref[...].astype(F32)
        for k in range(N_CHIPS - 1):
            acc = acc + r_ref[k].astype(F32)
        out_ref[...] = acc

    return pl.pallas_call(
        body, name="sum_chips",
        grid_spec=pltpu.PrefetchScalarGridSpec(
            num_scalar_prefetch=2, grid=(R // tr,),
            in_specs=[pl.BlockSpec((None, tr, C), lambda i, c, l: (c[0], i, 0)),
                      pl.BlockSpec((N_CHIPS - 1, tr, C), lambda i, c, l: (0, i, 0))],
            out_specs=pl.BlockSpec((None, tr, C), lambda i, c, l: (l[0], i, 0))),
        out_shape=jax.ShapeDtypeStruct((2, R, C), F32), compiler_params=_cparams("arbitrary"))(chip, layer, part, recv)


_ANY = pl.BlockSpec(memory_space=pl.ANY)


def _place():
    x, y, c = lax.axis_index("x"), lax.axis_index("y"), lax.axis_index("c")
    chips = [(1 - x, y), (x, 1 - y), (1 - x, 1 - y)]
    return x, y, c, chips


def _gather_phase(phase, layer, w, o, send, recv, start):
    x, y, c, chips = _place()
    me, sib = 2 * x + y, (x, y, 1 - c)
    n = len(w)
    works = c == layer

    def copy(t, k, shard, to, src=None):
        blk = o[t].at[layer, shard]
        return pltpu.make_async_remote_copy(src_ref=blk if src is None else src, dst_ref=blk, send_sem=send.at[t, k],
                                            recv_sem=recv.at[t, k], device_id=to, device_id_type=MESH)

    def outgoing():
        if phase == "ici":
            return [copy(t, k, me, (*chip, c), src=w[t].at[layer, me]) for t in range(n) for k, chip in enumerate(chips)]
        return [copy(t, k, 2 * chip[0] + chip[1], sib) for t in range(n) for k, chip in enumerate(chips)]

    def incoming():
        return [copy(t, k, 2 * chip[0] + chip[1], (x, y, c)) for t in range(n) for k, chip in enumerate(chips)]

    if start:
        @pl.when(works)
        def _():
            for cp in outgoing():
                cp.start()
    else:
        @pl.when(works)
        def _():
            if phase == "ici":
                for cp in incoming():
                    cp.wait_recv()
            for cp in outgoing():
                cp.wait_send()

        if phase == "pass":
            @pl.when(jnp.logical_not(works))
            def _():
                for cp in incoming():
                    cp.wait_recv()


def gather_layer(ws, layer):
    n = len(ws)

    def body(*refs):
        w, o = refs[:n], refs[n:2 * n]
        ici_send, ici_recv, pass_send, pass_recv = refs[2 * n:]
        _gather_phase("ici", layer, w, o, ici_send, ici_recv, start=True)
        _gather_phase("ici", layer, w, o, ici_send, ici_recv, start=False)
        _gather_phase("pass", layer, w, o, pass_send, pass_recv, start=True)
        _gather_phase("pass", layer, w, o, pass_send, pass_recv, start=False)

    sems = [pltpu.SemaphoreType.DMA((n, 3))] * 4
    return pl.pallas_call(
        body, name="gather_layer", in_specs=[_ANY] * n, out_specs=[_ANY] * n,
        out_shape=[jax.ShapeDtypeStruct(w.shape, w.dtype) for w in ws], input_output_aliases={t: t for t in range(n)},
        scratch_shapes=sems, compiler_params=pltpu.CompilerParams(has_side_effects=True))(*ws)


def pair_exchange(gs):
    n = len(gs)

    def body(*refs):
        g, o = refs[:n], refs[n:2 * n]
        send, recv = refs[2 * n:]
        x, y, c, _ = _place()
        cps = [pltpu.make_async_remote_copy(src_ref=g[t].at[1 - c], dst_ref=o[t], send_sem=send.at[t], recv_sem=recv.at[t],
                                            device_id=(x, y, 1 - c), device_id_type=MESH) for t in range(n)]
        for cp in cps:
            cp.start()
        for cp in cps:
            cp.wait()

    return pl.pallas_call(
        body, name="pair_exchange", in_specs=[_ANY] * n, out_specs=[_ANY] * n,
        out_shape=[jax.ShapeDtypeStruct(g.shape[1:], g.dtype) for g in gs],
        scratch_shapes=[pltpu.SemaphoreType.DMA((n,)), pltpu.SemaphoreType.DMA((n,))],
        compiler_params=pltpu.CompilerParams(has_side_effects=True))(*gs)


def chip_scatter(ps):
    n = len(ps)

    def body(*refs):
        p, o = refs[:n], refs[n:2 * n]
        send, recv = refs[2 * n:]
        x, y, c, chips = _place()
        cps = [pltpu.make_async_remote_copy(src_ref=p[t].at[2 * chip[0] + chip[1]], dst_ref=o[t].at[k],
                                            send_sem=send.at[t, k], recv_sem=recv.at[t, k], device_id=(*chip, c),
                                            device_id_type=MESH)
               for t in range(n) for k, chip in enumerate(chips)]
        for cp in cps:
            cp.start()
        for cp in cps:
            cp.wait()

    return pl.pallas_call(
        body, name="chip_scatter", in_specs=[_ANY] * n, out_specs=[_ANY] * n,
        out_shape=[jax.ShapeDtypeStruct((N_CHIPS - 1,) + p.shape[1:], p.dtype) for p in ps],
        scratch_shapes=[pltpu.SemaphoreType.DMA((n, 3)), pltpu.SemaphoreType.DMA((n, 3))],
        compiler_params=pltpu.CompilerParams(has_side_effects=True))(*ps)


def pair_share(rs):
    n = len(rs)

    def body(*refs):
        r, o = refs[:n], refs[n:2 * n]
        send, recv = refs[2 * n:]
        x, y, c, _ = _place()

        def remote(t, layer):
            return pltpu.make_async_remote_copy(src_ref=r[t].at[layer], dst_ref=o[t].at[layer], send_sem=send.at[t],
                                                recv_sem=recv.at[t], device_id=(x, y, 1 - c), device_id_type=MESH)

        for t in range(n):
            remote(t, c).start()
        for t in range(n):
            remote(t, 1 - c).wait_recv()
            remote(t, c).wait_send()

    return pl.pallas_call(
        body, name="pair_share", in_specs=[_ANY] * n, out_specs=[_ANY] * n,
        out_shape=[jax.ShapeDtypeStruct(r.shape, r.dtype) for r in rs], input_output_aliases={t: t for t in range(n)},
        scratch_shapes=[pltpu.SemaphoreType.DMA((n,)), pltpu.SemaphoreType.DMA((n,))],
        compiler_params=pltpu.CompilerParams(has_side_effects=True))(*rs)


def all_reduce_small(buf):
    def body(x_ref, o_ref, land, send, recv):
        x, y, c, _ = _place()
        me = 4 * x + 2 * y + c
        land[me] = x_ref[...]
        cps = []
        for mask in range(1, N_DEV):
            px = 1 - x if mask & 4 else x
            py = 1 - y if mask & 2 else y
            pc = 1 - c if mask & 1 else c
            cps.append(pltpu.make_async_remote_copy(src_ref=x_ref, dst_ref=land.at[me], send_sem=send.at[mask - 1],
                                                    recv_sem=recv.at[mask - 1], device_id=(px, py, pc), device_id_type=MESH))
            cps[-1].start()
        for mask in range(1, N_DEV):
            px = 1 - x if mask & 4 else x
            py = 1 - y if mask & 2 else y
            pc = 1 - c if mask & 1 else c
            pltpu.make_async_remote_copy(src_ref=x_ref, dst_ref=land.at[4 * px + 2 * py + pc], send_sem=send.at[mask - 1],
                                         recv_sem=recv.at[mask - 1], device_id=(px, py, pc), device_id_type=MESH).wait_recv()
        for cp in cps:
            cp.wait_send()
        acc = land[0]
        for d in range(1, N_DEV):
            acc = acc + land[d]
        o_ref[...] = acc

    vm = pl.BlockSpec(memory_space=pltpu.VMEM)
    return pl.pallas_call(
        body, name="all_reduce_small", in_specs=[vm], out_specs=vm, out_shape=jax.ShapeDtypeStruct(buf.shape, F32),
        scratch_shapes=[pltpu.VMEM((N_DEV,) + buf.shape, F32), pltpu.SemaphoreType.DMA((N_DEV - 1,)),
                        pltpu.SemaphoreType.DMA((N_DEV - 1,))])(buf)


def _take(ws, idx):
    return [ws[t] for t in idx]


def _put(ws, idx, new):
    ws = list(ws)
    for t, a in zip(idx, new):
        ws[t] = a
    return ws


def _ffn_fwd(x, h, gathered, first, l, next_gain, up_carry=None, down_carry=None):
    D = x.shape[1]
    if up_carry:
        phase, layer, idx = up_carry
        g, u, a, arrays = gate_up(h, gathered[first], gathered[first + 1], l, carry=(phase, layer, _take(gathered, idx)))
        gathered = _put(gathered, idx, arrays)
    else:
        g, u, a = gate_up(h, gathered[first], gathered[first + 1], l)
    wd = gathered[first + 2].reshape(gathered[first + 2].shape[0], -1, D)
    if down_carry:
        phase, layer, idx = down_carry
        y, h_next, arrays = down_proj(a, wd, l, x, next_gain, carry=(phase, layer, _take(gathered, idx)))
        gathered = _put(gathered, idx, arrays)
    else:
        y, h_next = down_proj(a, wd, l, x, next_gain)
    return y, h_next, (h, g, u, a), gathered


def _ffn_bwd(dy, x, gain, wg, wu, wd, l, saved, grads):
    h, g, u, a = saved
    D = x.shape[1]
    L = wg.shape[0]
    Fs = g.shape[1] // N_CHIPS
    dg, du = down_bwd(dy, wd, l, g, u)
    dwg = mm_tn(h, dg, _tile(D, 1024), Fs, blocked=True, layer=l, layers=L, into=grads[0])
    dwu = mm_tn(h, du, _tile(D, 1024), Fs, blocked=True, layer=l, layers=L, into=grads[1])
    dwd = mm_tn(a, dy, Fs, _tile(D, 1024), scale=0.5, layer=l, layers=L, into=grads[2])
    dh = gate_up_bwd(dg, du, wg, wu, l)
    dx, dgain = rms_bwd(dh, x, gain, dy)
    return dx, dgain, (dwg, dwu, dwd)


def _mixer_fwd(x, h, win, q_norm, kv_norm, sinks, bias, wqb, wkvb, gathered, c2, s2, l, next_gain, carry_layer=None):
    D = x.shape[1]
    p = mm_nn(h, win, 0, 640, F32)
    (cqn, ckvn, qmt, km, kmt, vm, vmt, qst, ks, kst, vs, vst, qft, kf, kft, vf, vft, ccol) = mixer_prep(
        p, c2, s2, q_norm, kv_norm, bias, wqb, wkvb, 0)
    if carry_layer is None:
        o_mla, lse_mla = attn_fwd("attn_mla", qmt, km, vmt)
        o_swa, lse_swa = attn_fwd("attn_swa", qst, ks, vst, window=WINDOW, sinks=sinks)
        o_fox, lse_fox = attn_fwd("attn_fox", qft, kf, vft, ccol=ccol)
    else:
        o_mla, lse_mla, arrays = attn_fwd("attn_mla", qmt, km, vmt, carry=("ici", carry_layer, _take(gathered, W_HEAD)))
        o_swa, lse_swa, arrays = attn_fwd("attn_swa", qst, ks, vst, window=WINDOW, sinks=sinks,
                                          carry=("pass", carry_layer, arrays))
        gathered = _put(gathered, W_HEAD, arrays)
        o_fox, lse_fox, arrays = attn_fwd("attn_fox", qft, kf, vft, ccol=ccol,
                                          carry=("ici", carry_layer, _take(gathered, W_TAIL)))
        gathered = _put(gathered, W_TAIL, arrays)
    mixed = merge_heads(o_mla, o_swa, o_fox)
    wout = gathered[6].reshape(gathered[6].shape[0], -1, D)
    y, h_next = mm_nn(mixed, wout, l, D, F32, resid=x, next_gain=next_gain)
    saved = (h, p, cqn, ckvn, qmt, km, kmt, vm, qst, ks, kst, vs, qft, kf, kft, vf, ccol, lse_mla, lse_swa, lse_fox, mixed)
    return y, h_next, saved, gathered


def _mixer_bwd(dy, x, gain, win, q_norm, kv_norm, sinks, bias, wqb, wkvb, wout, c2, s2, l, saved, dwout_so_far):
    (h, p, cqn, ckvn, qmt, km, kmt, vm, qst, ks, kst, vs, qft, kf, kft, vf, ccol, lse_mla, lse_swa, lse_fox, mixed) = saved
    S, D = x.shape
    width = mixed.shape[1]
    dmixed = mm_nt(dy, wout, l, _tile(width, 1024), BF16)
    dwout = mm_tn(mixed, dy, _tile(width, 1024), _tile(D, 1024), layer=l, layers=wout.shape[0], into=dwout_so_far)
    do_mla, do_swa, do_fox, dl_mla, dl_swa, dl_fox = split_heads(dmixed, mixed)
    dqm, dkm, dvm = attn_bwd("attn_mla_bwd", qmt, km, kmt, vm, do_mla, lse_mla, dl_mla)
    dqs, dks, dvs, dsink = attn_bwd("attn_swa_bwd", qst, ks, kst, vs, do_swa, lse_swa, dl_swa, window=WINDOW, sinks=sinks)
    dqf, dkf, dvf, dcq, dck = attn_bwd("attn_fox_bwd", qft, kf, kft, vf, do_fox, lse_fox, dl_fox, ccol=ccol)
    dc = dcq.reshape(N_HEADS, S).T + dck.reshape(N_HEADS, S).T
    dp, dq, dkv, dqn, dkvn, dbias = mixer_prep_bwd(p, c2, s2, q_norm, kv_norm, bias, wqb, wkvb, 0, dqm, dkm, dvm, dqs, dks,
                                                   dvs, dqf, dkf, dvf, dc)
    dwqb = mm_tn(cqn, dq, Q_LORA, N_HEADS * QK_MLA)
    dwkvb = mm_tn(ckvn, dkv, KV_LORA, 1024)
    dwin = mm_tn(h, dp, _tile(D, 1024), 640)
    dh = mm_nt(dp, win, 0, _tile(D, 1024), F32)
    dx, dgain = rms_bwd(dh, x, gain, dy)
    return dx, dgain, dwin, dqn, dwqb, dkvn, dwkvb, dsink[:, 0, 0], dbias[0], dwout


def _pad_in_cols(w):
    pad = jnp.zeros(w.shape[:-1] + (P_COLS - IN_COLS,), w.dtype)
    return jnp.concatenate([w[..., :IN_KR], w[..., IN_KR + ROPE:IN_COLS - N_HEADS], w[..., IN_KR:IN_KR + ROPE],
                            w[..., IN_COLS - N_HEADS:], pad], axis=-1)


def _unpad_in_cols(w):
    return jnp.concatenate([w[..., :IN_KR], w[..., P_KR:P_KR + ROPE], w[..., IN_KR:P_KR], w[..., P_F:P_F + N_HEADS]], axis=-1)


def _col_shards(w):
    R = w.shape[0]
    return w.reshape(R, N_CHIPS, -1).transpose(1, 0, 2)


def _from_col_shards(w):
    L, _, R, C = w.shape
    return w.transpose(0, 2, 1, 3).reshape(L, R, N_CHIPS * C)


def kernel(x, positions, ffn1_norm, ffn1_w_gate, ffn1_w_up, ffn1_w_down, mix_norm, w_in, mla_q_norm, mla_w_q_b, mla_kv_norm, mla_w_kv_b, swa_sinks, fox_forget_bias, w_out, ffn2_norm, ffn2_w_gate, ffn2_w_up, ffn2_w_down, final_norm, loss_target, m_ffn1_norm, m_ffn1_w_gate, m_ffn1_w_up, m_ffn1_w_down, m_mix_norm, m_w_in, m_mla_q_norm, m_mla_w_q_b, m_mla_kv_norm, m_mla_w_kv_b, m_swa_sinks, m_fox_forget_bias, m_w_out, m_ffn2_norm, m_ffn2_w_gate, m_ffn2_w_up, m_ffn2_w_down, m_final_norm, v_ffn1_norm, v_ffn1_w_gate, v_ffn1_w_up, v_ffn1_w_down, v_mix_norm, v_w_in, v_mla_q_norm, v_mla_w_q_b, v_mla_kv_norm, v_mla_w_kv_b, v_swa_sinks, v_fox_forget_bias, v_w_out, v_ffn2_norm, v_ffn2_w_gate, v_ffn2_w_up, v_ffn2_w_down, v_final_norm):
    L = ffn1_norm.shape[0]
    S, D = x.shape[1], x.shape[2]
    F = ffn1_w_down.shape[1] * N_CHIPS
    xs, target = x[0], loss_target[0]
    cx, cy, cc = lax.axis_index("x"), lax.axis_index("y"), lax.axis_index("c")
    layer_id = jnp.reshape(cc, (1,)).astype(jnp.int32)
    chip_id = jnp.reshape(2 * cx + cy, (1,)).astype(jnp.int32)

    inv_freq = ROPE_THETA ** (-jnp.arange(0, ROPE, 2, dtype=F32) / ROPE)
    ang = positions[0].astype(F32)[:, None] * inv_freq
    cos, sin = jnp.cos(ang), jnp.sin(ang)
    c2, s2 = jnp.concatenate([cos, cos], axis=-1), jnp.concatenate([-sin, sin], axis=-1)

    big = [ffn1_w_gate, ffn1_w_up, ffn1_w_down, w_in, mla_w_q_b, mla_w_kv_b, w_out, ffn2_w_gate, ffn2_w_up, ffn2_w_down]
    gathered = [place_own(w, chip_id) for w in big]
    gathered = _put(gathered, W_FFN1, gather_layer(_take(gathered, W_FFN1), 0))

    def mixer_weights(ws, l):
        return (_pad_in_cols(_from_col_shards(ws[3][l:l + 1])), _from_col_shards(ws[4][l:l + 1]),
                _from_col_shards(ws[5][l:l + 1]))

    acts, small_w = [], []
    x0, h0 = xs, rms_fwd(xs, ffn1_norm[0][None])
    for l in range(L):
        nxt = l + 1 if l + 1 < L else None
        x1, h1, s1, gathered = _ffn_fwd(x0, h0, gathered, 0, l, mix_norm[l][None],
                                        up_carry=("ici", 0, W_REST) if l == 0 else None,
                                        down_carry=("pass", 0, W_REST) if l == 0 else None)
        small_w.append(mixer_weights(gathered, l))
        x2, h2, sm, gathered = _mixer_fwd(x1, h1, small_w[l][0], mla_q_norm[l][None], mla_kv_norm[l][None], swa_sinks[l],
                                          fox_forget_bias[l][None], small_w[l][1], small_w[l][2], gathered, c2, s2, l,
                                          ffn2_norm[l][None], carry_layer=nxt)
        x3, h3, s2_, gathered = _ffn_fwd(x2, h2, gathered, 7, l, ffn1_norm[l + 1][None] if nxt else None,
                                         up_carry=("pass", nxt, [9]) if nxt else None,
                                         down_carry=("pass", nxt, [7, 8]) if nxt else None)
        acts.append((x0, x1, x2, s1, sm, s2_))
        x0, h0 = x3, h3
    loss_part, dx, d_final = loss_head(x0, final_norm[None], target)
    wg1, wu1, wd1, _, _, _, wout, wg2, wu2, wd2 = gathered
    wd1, wd2, wout = wd1.reshape(L, F, D), wd2.reshape(L, F, D), wout.reshape(L, -1, D)

    small = {k: [None] * L for k in ("ffn1_norm", "mix_norm", "q_norm", "kv_norm", "sinks", "bias", "ffn2_norm")}
    per_layer = {k: [None] * L for k in ("win", "wqb", "wkvb")}
    ffn1_grads, ffn2_grads, dwout = (None,) * 3, (None,) * 3, None
    for l in reversed(range(L)):
        x0, x1, x2, s1, sm, s2_ = acts[l]
        dx, small["ffn2_norm"][l], ffn2_grads = _ffn_bwd(dx, x2, ffn2_norm[l][None], wg2, wu2, wd2, l, s2_, ffn2_grads)
        (dx, small["mix_norm"][l], dwin, small["q_norm"][l], dwqb, small["kv_norm"][l], dwkvb, small["sinks"][l],
         small["bias"][l], dwout) = _mixer_bwd(dx, x1, mix_norm[l][None], small_w[l][0], mla_q_norm[l][None],
                                               mla_kv_norm[l][None], swa_sinks[l], fox_forget_bias[l][None], small_w[l][1],
                                               small_w[l][2], wout, c2, s2, l, sm, dwout)
        per_layer["win"][l] = _col_shards(_unpad_in_cols(dwin))
        per_layer["wqb"][l] = _col_shards(dwqb)
        per_layer["wkvb"][l] = _col_shards(dwkvb)
        dx, small["ffn1_norm"][l], ffn1_grads = _ffn_bwd(dx, x0, ffn1_norm[l][None], wg1, wu1, wd1, l, s1, ffn1_grads)
    grad_x = dx[None]

    names = ("wg1", "wu1", "wd1", "win", "wqb", "wkvb", "wout", "wg2", "wu2", "wd2")
    Fs = F // N_CHIPS
    full = [ffn1_grads[0], ffn1_grads[1], ffn1_grads[2].reshape(L, N_CHIPS, Fs, D), jnp.stack(per_layer["win"]),
            jnp.stack(per_layer["wqb"]), jnp.stack(per_layer["wkvb"]), dwout.reshape(L, N_CHIPS, -1, D),
            ffn2_grads[0], ffn2_grads[1], ffn2_grads[2].reshape(L, N_CHIPS, Fs, D)]
    flat = [g.reshape(L, -1, g.shape[-1]) for g in full]
    from_sibling = pair_exchange(flat)
    part = [sum_pair(g, o, layer_id).reshape(f.shape[1:]) for g, o, f in zip(flat, from_sibling, full)]
    from_chips = chip_scatter(part)
    mine = [sum_chips(p, r, chip_id, layer_id) for p, r in zip(part, from_chips)]
    grads_big = pair_share(mine)

    pieces = [jnp.concatenate(small["ffn1_norm"]), jnp.concatenate(small["mix_norm"]), jnp.concatenate(small["q_norm"]),
              jnp.concatenate(small["kv_norm"]), jnp.stack(small["sinks"]), jnp.stack(small["bias"]),
              jnp.concatenate(small["ffn2_norm"]), d_final, loss_part[:, 0:1]]
    sizes = [int(p.size) for p in pieces]
    packed = jnp.concatenate([p.reshape(-1) for p in pieces])
    packed = jnp.pad(packed, (0, SMALL_ROWS * 128 - packed.shape[0])).reshape(SMALL_ROWS, 128)
    summed = all_reduce_small(packed).reshape(-1)
    out_small, off = [], 0
    for p, n in zip(pieces, sizes):
        out_small.append(summed[off:off + n].reshape(p.shape))
        off += n
    g_ffn1_norm, g_mix_norm, g_q_norm, g_kv_norm, g_sinks, g_bias, g_ffn2_norm, g_final, loss = out_small
    loss = loss.reshape(())
    g_final = g_final.reshape(-1)

    gb = dict(zip(names, grads_big))
    summed_grads = [g_ffn1_norm, gb["wg1"], gb["wu1"], gb["wd1"], g_mix_norm, gb["win"], g_q_norm, gb["wqb"], g_kv_norm,
                    gb["wkvb"], g_sinks, g_bias, gb["wout"], g_ffn2_norm, gb["wg2"], gb["wu2"], gb["wd2"], g_final]
    weights = [ffn1_norm, ffn1_w_gate, ffn1_w_up, ffn1_w_down, mix_norm, w_in, mla_q_norm, mla_w_q_b, mla_kv_norm, mla_w_kv_b,
               swa_sinks, fox_forget_bias, w_out, ffn2_norm, ffn2_w_gate, ffn2_w_up, ffn2_w_down, final_norm]
    ms = [m_ffn1_norm, m_ffn1_w_gate, m_ffn1_w_up, m_ffn1_w_down, m_mix_norm, m_w_in, m_mla_q_norm, m_mla_w_q_b, m_mla_kv_norm,
          m_mla_w_kv_b, m_swa_sinks, m_fox_forget_bias, m_w_out, m_ffn2_norm, m_ffn2_w_gate, m_ffn2_w_up, m_ffn2_w_down,
          m_final_norm]
    vs = [v_ffn1_norm, v_ffn1_w_gate, v_ffn1_w_up, v_ffn1_w_down, v_mix_norm, v_w_in, v_mla_q_norm, v_mla_w_q_b, v_mla_kv_norm,
          v_mla_w_kv_b, v_swa_sinks, v_fox_forget_bias, v_w_out, v_ffn2_norm, v_ffn2_w_gate, v_ffn2_w_up, v_ffn2_w_down,
          v_final_norm]
    grads, deltas, new_m, new_v = [], [], [], []
    for w, g, m, v in zip(weights, summed_grads, ms, vs):
        three_d = w.shape if w.ndim == 3 else (1, -1, w.shape[-1])
        g_out, d, nm, nv = adamw(w.reshape(three_d), g.reshape(three_d), m.reshape(three_d), v.reshape(three_d))
        grads.append(g_out.reshape(w.shape))
        deltas.append(d.reshape(w.shape))
        new_m.append(nm.reshape(w.shape))
        new_v.append(nv.reshape(w.shape))
    return (loss, grad_x, *grads, *deltas, *new_m, *new_v)
```

```python
import jax
import jax.numpy as jnp
from jax import lax
from jax.experimental import pallas as pl
from jax.experimental.pallas import tpu as pltpu

F32, BF16 = jnp.float32, jnp.bfloat16
MESH = pl.DeviceIdType.MESH

RMS_EPS = 1e-6
ROPE_THETA = 10000.0
N_HEADS = 8
Q_LORA, KV_LORA = 512, 256
NOPE, ROPE, VDIM = 128, 64, 128
QK_MLA = NOPE + ROPE
SWA_KV, HD, WINDOW = 2, 64, 128
P_CQ, P_CKV, P_QS, P_KS, P_VS, P_QF, P_KF, P_VF, P_KR, P_F, P_COLS = (
    0, 512, 768, 1280, 1408, 1536, 2048, 2560, 3072, 3136, 3200)
IN_COLS = 3144
IN_KR = 768
ADAM_LR, ADAM_B1, ADAM_B2, ADAM_EPS, ADAM_WD, ADAM_STEP = 0.001, 0.9, 0.999, 1e-08, 0.01, 10
NEG = -1e30
LOG2E, LN2 = 1.4426950408889634, 0.6931471805599453
VMEM_LIMIT = 56 * 1024 * 1024
N_CHIPS = 4
N_DEV = 8
W_FFN1, W_REST, W_HEAD, W_TAIL = [0, 1, 2], [3, 4, 5, 6, 7, 8, 9], [0, 1, 2, 3, 4, 5, 6], [7, 8, 9]
SMALL_ROWS = 128


def _tile(n, pref):
    return pref if n % pref == 0 else n


def _cparams(*sem):
    return pltpu.CompilerParams(dimension_semantics=sem, vmem_limit_bytes=VMEM_LIMIT)


def _sigmoid(x):
    return 1.0 / (1.0 + jnp.exp(-x))


def _dot(a, b):
    return jnp.dot(a, b, preferred_element_type=F32)


def _dot_nt(a, b):
    return lax.dot_general(a, b, (((1,), (1,)), ((), ())), preferred_element_type=F32)


def _dot_tn(a, b):
    return lax.dot_general(a, b, (((0,), (0,)), ((), ())), preferred_element_type=F32)


def rms_fwd(x, gain):
    S, D = x.shape
    tm = _tile(S, 512)

    def body(x_ref, g_ref, h_ref):
        xv = x_ref[...]
        r = lax.rsqrt(jnp.mean(xv * xv, axis=-1, keepdims=True) + RMS_EPS)
        h_ref[...] = (xv * r * g_ref[...]).astype(BF16)

    return pl.pallas_call(
        body, name="rms_fwd", grid=(S // tm,),
        in_specs=[pl.BlockSpec((tm, D), lambda i: (i, 0)), pl.BlockSpec((1, D), lambda i: (0, 0))],
        out_specs=pl.BlockSpec((tm, D), lambda i: (i, 0)),
        out_shape=jax.ShapeDtypeStruct((S, D), BF16), compiler_params=_cparams("arbitrary"))(x, gain)


def rms_bwd(dh, x, gain, resid):
    S, D = x.shape
    tm = _tile(S, 512)

    def body(dh_ref, x_ref, g_ref, r_ref, dx_ref, dg_ref):
        xv, dhv = x_ref[...], dh_ref[...]
        r = lax.rsqrt(jnp.mean(xv * xv, axis=-1, keepdims=True) + RMS_EPS)
        xhat = xv * r
        dhg = dhv * g_ref[...]
        dx_ref[...] = r_ref[...] + r * (dhg - xhat * jnp.mean(dhg * xhat, axis=-1, keepdims=True))

        @pl.when(pl.program_id(0) == 0)
        def _():
            dg_ref[...] = jnp.zeros_like(dg_ref)

        dg_ref[...] += jnp.sum(dhv * xhat, axis=0, keepdims=True)

    row = pl.BlockSpec((tm, D), lambda i: (i, 0))
    vec = pl.BlockSpec((1, D), lambda i: (0, 0))
    return pl.pallas_call(
        body, name="rms_bwd", grid=(S // tm,), in_specs=[row, row, vec, row], out_specs=[row, vec],
        out_shape=[jax.ShapeDtypeStruct((S, D), F32), jax.ShapeDtypeStruct((1, D), F32)],
        compiler_params=_cparams("arbitrary"))(dh, x, gain, resid)


def loss_head(x, gain, target):
    S, D = x.shape
    tm = _tile(S, 512)

    def body(x_ref, g_ref, t_ref, loss_ref, dx_ref, dg_ref):
        xv, g = x_ref[...], g_ref[...]
        r = lax.rsqrt(jnp.mean(xv * xv, axis=-1, keepdims=True) + RMS_EPS)
        xhat = xv * r
        err = xhat * g - t_ref[...]
        dy = err * (1.0 / D)
        dyg = dy * g
        dx_ref[...] = r * (dyg - xhat * jnp.mean(dyg * xhat, axis=-1, keepdims=True))

        @pl.when(pl.program_id(0) == 0)
        def _():
            dg_ref[...] = jnp.zeros_like(dg_ref)
            loss_ref[...] = jnp.zeros_like(loss_ref)

        dg_ref[...] += jnp.sum(dy * xhat, axis=0, keepdims=True)
        loss_ref[...] += 0.5 * jnp.sum(jnp.mean(err * err, axis=-1, keepdims=True), axis=0, keepdims=True)

    row = pl.BlockSpec((tm, D), lambda i: (i, 0))
    vec = pl.BlockSpec((1, D), lambda i: (0, 0))
    return pl.pallas_call(
        body, name="loss_head", grid=(S // tm,), in_specs=[row, vec, row],
        out_specs=[pl.BlockSpec((1, 128), lambda i: (0, 0)), row, vec],
        out_shape=[jax.ShapeDtypeStruct((1, 128), F32), jax.ShapeDtypeStruct((S, D), F32),
                   jax.ShapeDtypeStruct((1, D), F32)],
        compiler_params=_cparams("arbitrary"))(x, gain, target)


def _carry_plumbing(carry, n_in, n_out):
    if not carry:
        return [], [], [], [], [], {}
    arrays = list(carry[2])
    n = len(arrays)
    sems = [pltpu.SemaphoreType.DMA((n, 3)), pltpu.SemaphoreType.DMA((n, 3))]
    return ([_ANY] * n, arrays, [_ANY] * n, [jax.ShapeDtypeStruct(a.shape, a.dtype) for a in arrays], sems,
            {n_in + t: n_out + t for t in range(n)})


def gate_up(h, wg, wu, l, carry=None):
    S, D = h.shape
    Fs = wg.shape[3]
    tm = _tile(S, 512)
    nt = S // tm
    n_carry = len(carry[2]) if carry else 0

    def body(h_ref, wg_ref, wu_ref, *rest):
        carried_in, rest = rest[:n_carry], rest[n_carry:]
        g_ref, u_ref, a_ref = rest[:3]
        carried_out, sems = rest[3:3 + n_carry], rest[3 + n_carry:]
        j, i = pl.program_id(0), pl.program_id(1)
        if carry:
            @pl.when(jnp.logical_and(j == 0, i == 0))
            def _():
                _gather_phase(carry[0], carry[1], carried_in, carried_out, sems[0], sems[1], start=True)
        hv = h_ref[...]
        g = _dot(hv, wg_ref[...])
        u = _dot(hv, wu_ref[...])
        g_ref[...] = g.astype(BF16)
        u_ref[...] = u.astype(BF16)
        a_ref[...] = (g * _sigmoid(g) * u).astype(BF16)
        if carry:
            @pl.when(jnp.logical_and(j == N_CHIPS - 1, i == nt - 1))
            def _():
                _gather_phase(carry[0], carry[1], carried_in, carried_out, sems[0], sems[1], start=False)

    w_spec = pl.BlockSpec((None, None, D, Fs), lambda j, i: (l, j, 0, 0))
    o_spec = pl.BlockSpec((tm, Fs), lambda j, i: (i, j))
    o_shape = jax.ShapeDtypeStruct((S, N_CHIPS * Fs), BF16)
    c_in, c_args, c_out, c_shape, c_scratch, aliases = _carry_plumbing(carry, 3, 3)
    out = pl.pallas_call(
        body, name="gate_up", grid=(N_CHIPS, nt),
        in_specs=[pl.BlockSpec((tm, D), lambda j, i: (i, 0)), w_spec, w_spec] + c_in,
        out_specs=[o_spec, o_spec, o_spec] + c_out, out_shape=[o_shape, o_shape, o_shape] + c_shape,
        scratch_shapes=c_scratch, input_output_aliases=aliases,
        compiler_params=_cparams("arbitrary", "arbitrary"))(h, wg, wu, *c_args)
    return (out[0], out[1], out[2], list(out[3:])) if carry else tuple(out)


def _normed(x, gain):
    r = lax.rsqrt(jnp.mean(x * x, axis=-1, keepdims=True) + RMS_EPS)
    return (x * r * gain).astype(BF16)


def down_proj(a, wd, l, x, next_gain=None, carry=None):
    S, F = a.shape
    D = wd.shape[2]
    tm, tk = _tile(S, 512), F // N_CHIPS
    nk, nt = F // tk, S // tm
    emit = next_gain is not None
    n_carry = len(carry[2]) if carry else 0
    n_out = 2 if emit else 1

    def body(a_ref, w_ref, x_ref, *rest):
        rest = list(rest)
        g_ref = rest.pop(0) if emit else None
        carried_in = [rest.pop(0) for _ in range(n_carry)]
        o_ref = rest.pop(0)
        h_ref = rest.pop(0) if emit else None
        carried_out = [rest.pop(0) for _ in range(n_carry)]
        acc_ref = rest.pop(0)
        i, k = pl.program_id(0), pl.program_id(1)
        if carry:
            @pl.when(jnp.logical_and(i == 0, k == 0))
            def _():
                _gather_phase(carry[0], carry[1], carried_in, carried_out, rest[0], rest[1], start=True)

        @pl.when(k == 0)
        def _():
            acc_ref[...] = jnp.zeros_like(acc_ref)

        acc_ref[...] += _dot(a_ref[...], w_ref[...])

        @pl.when(k == nk - 1)
        def _():
            y = x_ref[...] + 0.5 * acc_ref[...]
            o_ref[...] = y
            if emit:
                h_ref[...] = _normed(y, g_ref[...])

        if carry:
            @pl.when(jnp.logical_and(i == nt - 1, k == nk - 1))
            def _():
                _gather_phase(carry[0], carry[1], carried_in, carried_out, rest[0], rest[1], start=False)

    row = pl.BlockSpec((tm, D), lambda i, k: (i, 0))
    in_specs = [pl.BlockSpec((tm, tk), lambda i, k: (i, k)), pl.BlockSpec((None, tk, D), lambda i, k: (l, k, 0)), row]
    args, out_specs, out_shape = [a, wd, x], [row], [jax.ShapeDtypeStruct((S, D), F32)]
    if emit:
        in_specs.append(pl.BlockSpec((1, D), lambda i, k: (0, 0)))
        args.append(next_gain)
        out_specs.append(row)
        out_shape.append(jax.ShapeDtypeStruct((S, D), BF16))
    c_in, c_args, c_out, c_shape, c_scratch, aliases = _carry_plumbing(carry, len(args), n_out)
    out = pl.pallas_call(
        body, name="down_proj", grid=(nt, nk), in_specs=in_specs + c_in, out_specs=out_specs + c_out,
        out_shape=out_shape + c_shape, scratch_shapes=[pltpu.VMEM((tm, D), F32)] + c_scratch,
        input_output_aliases=aliases, compiler_params=_cparams("arbitrary", "arbitrary"))(*args, *c_args)
    result = (out[0], out[1] if emit else None)
    return result + (list(out[n_out:]),) if carry else result


def down_bwd(dy, wd, l, g, u):
    S, D = dy.shape
    F = g.shape[1]
    Fs = F // N_CHIPS
    tm = _tile(S, 512)

    def body(dy_ref, w_ref, g_ref, u_ref, dg_ref, du_ref):
        da = 0.5 * _dot_nt(dy_ref[...].astype(BF16), w_ref[...])
        gv, uv = g_ref[...].astype(F32), u_ref[...].astype(F32)
        sig = _sigmoid(gv)
        du_ref[...] = (da * (gv * sig)).astype(BF16)
        dg_ref[...] = (da * uv * (sig * (1.0 + gv * (1.0 - sig)))).astype(BF16)

    t_spec = pl.BlockSpec((tm, Fs), lambda j, i: (i, j))
    o_shape = jax.ShapeDtypeStruct((S, F), BF16)
    return pl.pallas_call(
        body, name="down_bwd", grid=(N_CHIPS, S // tm),
        in_specs=[pl.BlockSpec((tm, D), lambda j, i: (i, 0)), pl.BlockSpec((None, Fs, D), lambda j, i: (l, j, 0)),
                  t_spec, t_spec],
        out_specs=[t_spec, t_spec], out_shape=[o_shape, o_shape],
        compiler_params=_cparams("arbitrary", "arbitrary"))(dy, wd, g, u)


def gate_up_bwd(dg, du, wg, wu, l):
    S, F = dg.shape
    D, Fs = wg.shape[2], wg.shape[3]
    tm = _tile(S, 512)

    def body(dg_ref, du_ref, wg_ref, wu_ref, o_ref):
        k = pl.program_id(1)

        @pl.when(k == 0)
        def _():
            o_ref[...] = jnp.zeros_like(o_ref)

        o_ref[...] += _dot_nt(dg_ref[...], wg_ref[...]) + _dot_nt(du_ref[...], wu_ref[...])

    t_spec = pl.BlockSpec((tm, Fs), lambda i, k: (i, k))
    w_spec = pl.BlockSpec((None, None, D, Fs), lambda i, k: (l, k, 0, 0))
    return pl.pallas_call(
        body, name="gate_up_bwd", grid=(S // tm, N_CHIPS), in_specs=[t_spec, t_spec, w_spec, w_spec],
        out_specs=pl.BlockSpec((tm, D), lambda i, k: (i, 0)), out_shape=jax.ShapeDtypeStruct((S, D), F32),
        compiler_params=_cparams("arbitrary", "arbitrary"))(dg, du, wg, wu)


def mm_nn(a, b, l, tn, out_dtype, resid=None, next_gain=None):
    S, K = a.shape
    N = b.shape[2]
    tm = _tile(S, 512)
    emit = next_gain is not None
    assert not emit or tn == N

    def body(a_ref, b_ref, *rest):
        rest = list(rest)
        acc = _dot(a_ref[...].astype(BF16), b_ref[...])
        if resid is not None:
            acc = rest.pop(0)[...] + acc
        g_ref = rest.pop(0) if emit else None
        rest[0][...] = acc.astype(out_dtype)
        if emit:
            rest[1][...] = _normed(acc, g_ref[...])

    o_spec = pl.BlockSpec((tm, tn), lambda n, i: (i, n))
    in_specs = [pl.BlockSpec((tm, K), lambda n, i: (i, 0)), pl.BlockSpec((None, K, tn), lambda n, i: (l, 0, n))]
    args, out_specs, out_shape = [a, b], [o_spec], [jax.ShapeDtypeStruct((S, N), out_dtype)]
    if resid is not None:
        in_specs.append(o_spec)
        args.append(resid)
    if emit:
        in_specs.append(pl.BlockSpec((1, N), lambda n, i: (0, 0)))
        args.append(next_gain)
        out_specs.append(o_spec)
        out_shape.append(jax.ShapeDtypeStruct((S, N), BF16))
    out = pl.pallas_call(
        body, name="mm_nn", grid=(N // tn, S // tm), in_specs=in_specs, out_specs=out_specs, out_shape=out_shape,
        compiler_params=_cparams("arbitrary", "arbitrary"))(*args)
    return out if emit else out[0]


def mm_nt(a, b, l, tn, out_dtype):
    S, K = a.shape
    N = b.shape[1]
    tm = _tile(S, 512)

    def body(a_ref, b_ref, o_ref):
        o_ref[...] = _dot_nt(a_ref[...].astype(BF16), b_ref[...]).astype(out_dtype)

    return pl.pallas_call(
        body, name="mm_nt", grid=(N // tn, S // tm),
        in_specs=[pl.BlockSpec((tm, K), lambda n, i: (i, 0)), pl.BlockSpec((None, tn, K), lambda n, i: (l, n, 0))],
        out_specs=pl.BlockSpec((tm, tn), lambda n, i: (i, n)), out_shape=jax.ShapeDtypeStruct((S, N), out_dtype),
        compiler_params=_cparams("arbitrary", "arbitrary"))(a, b)


def mm_tn(a, b, tka, tnb, scale=1.0, blocked=False, layer=None, layers=None, into=None):
    S, Ka = a.shape
    Nb = b.shape[1]
    ts = _tile(S, 2048)
    ns = S // ts

    def body(a_ref, b_ref, *rest):
        o_ref, acc_ref = rest[-2:]
        s = pl.program_id(2)

        @pl.when(s == 0)
        def _():
            acc_ref[...] = jnp.zeros_like(acc_ref)

        acc_ref[...] += _dot_tn(a_ref[...].astype(BF16), b_ref[...].astype(BF16))

        @pl.when(s == ns - 1)
        def _():
            o_ref[...] = (scale * acc_ref[...]).astype(BF16)

    if blocked:
        block, shape = (None, tka, tnb), (Nb // tnb, Ka, tnb)
        index = lambda ka, nb, s: (nb, ka, 0)
    else:
        block, shape = (tka, tnb), (Ka, Nb)
        index = lambda ka, nb, s: (ka, nb)
    if layer is not None:
        block, shape = (None,) + block, (layers,) + shape
        inner = index
        index = lambda ka, nb, s: (layer,) + inner(ka, nb, s)
    in_specs = [pl.BlockSpec((ts, tka), lambda ka, nb, s: (s, ka)), pl.BlockSpec((ts, tnb), lambda ka, nb, s: (s, nb))]
    args, aliases = [a, b], {}
    if into is not None:
        in_specs.append(pl.BlockSpec(memory_space=pl.ANY))
        args.append(into)
        aliases = {2: 0}
    return pl.pallas_call(
        body, name="mm_tn", grid=(Ka // tka, Nb // tnb, ns), in_specs=in_specs,
        out_specs=pl.BlockSpec(block, index), out_shape=jax.ShapeDtypeStruct(shape, BF16),
        input_output_aliases=aliases, scratch_shapes=[pltpu.VMEM((tka, tnb), F32)],
        compiler_params=_cparams("arbitrary", "arbitrary", "arbitrary"))(*args)


def _rope(x, c2, s2):
    half = x.shape[-1] // 2
    rot = jnp.concatenate([x[:, half:], x[:, :half]], axis=-1)
    return x * c2 + rot * s2


def _tri(tm, upper):
    r = lax.broadcasted_iota(jnp.int32, (tm, tm), 0)
    c = lax.broadcasted_iota(jnp.int32, (tm, tm), 1)
    return jnp.where((c >= r) if upper else (c <= r), 1.0, 0.0).astype(F32)


def _log_sigmoid(x):
    return jnp.minimum(x, 0.0) - jnp.log(1.0 + jnp.exp(-jnp.abs(x)))


def _norm_hat(c):
    r = lax.rsqrt(jnp.mean(c * c, axis=-1, keepdims=True) + RMS_EPS)
    return c * r, r


def _tok_spec(tm, width, rev_n=None):
    if rev_n is None:
        return pl.BlockSpec((tm, width), lambda i: (i, 0))
    return pl.BlockSpec((tm, width), lambda i: (rev_n - 1 - i, 0))


def _head_spec(heads, tm, width, rev_n=None):
    if rev_n is None:
        return pl.BlockSpec((heads, tm, width), lambda i: (0, i, 0))
    return pl.BlockSpec((heads, tm, width), lambda i: (0, rev_n - 1 - i, 0))


def _lane_spec(heads, width, tm, rev_n=None):
    if rev_n is None:
        return pl.BlockSpec((heads, width, tm), lambda i: (0, 0, i))
    return pl.BlockSpec((heads, width, tm), lambda i: (0, 0, rev_n - 1 - i))


def _full_spec(shape):
    return pl.BlockSpec(shape, lambda i: (0,) * len(shape))


def mixer_prep(p, c2, s2, q_norm, kv_norm, bias, wqb, wkvb, l):
    S = p.shape[0]
    tm = _tile(S, 256)
    H = N_HEADS

    def body(p_ref, c2_ref, s2_ref, qn_ref, kvn_ref, b_ref, wqb_ref, wkvb_ref,
             cqn_ref, ckvn_ref, qmt_ref, km_ref, kmt_ref, vm_ref, vmt_ref, qst_ref, ks_ref, kst_ref, vs_ref, vst_ref,
             qft_ref, kf_ref, kft_ref, vf_ref, vft_ref, ccol_ref, carry_row):
        c2, s2 = c2_ref[...], s2_ref[...]
        cqn = (_norm_hat(p_ref[:, P_CQ:P_CQ + Q_LORA])[0] * qn_ref[...]).astype(BF16)
        ckvn = (_norm_hat(p_ref[:, P_CKV:P_CKV + KV_LORA])[0] * kvn_ref[...]).astype(BF16)
        cqn_ref[...] = cqn
        ckvn_ref[...] = ckvn
        q = _dot(cqn, wqb_ref[...])
        kv = _dot(ckvn, wkvb_ref[...])
        k_pe = _rope(p_ref[:, P_KR:P_KR + ROPE], c2, s2)
        k_pe_t = k_pe.T.astype(BF16)
        k_pe = k_pe.astype(BF16)

        def both_ways(x, tok_ref, lane_ref, h):
            tok_ref[h] = x.astype(BF16)
            lane_ref[h] = x.T.astype(BF16)

        qs_mla, qs_hd = QK_MLA ** -0.5 * LOG2E, HD ** -0.5 * LOG2E
        for h in range(H):
            qmt_ref[h, 0:NOPE, :] = (q[:, h * QK_MLA:h * QK_MLA + NOPE] * qs_mla).T.astype(BF16)
            qmt_ref[h, NOPE:QK_MLA, :] = (_rope(q[:, h * QK_MLA + NOPE:(h + 1) * QK_MLA], c2, s2) * qs_mla).T.astype(BF16)
            k_nope = kv[:, h * 256:h * 256 + NOPE]
            km_ref[h, :, 0:NOPE] = k_nope.astype(BF16)
            km_ref[h, :, NOPE:QK_MLA] = k_pe
            kmt_ref[h, 0:NOPE, :] = k_nope.T.astype(BF16)
            kmt_ref[h, NOPE:QK_MLA, :] = k_pe_t
            both_ways(kv[:, h * 256 + NOPE:(h + 1) * 256], vm_ref, vmt_ref, h)
            qst_ref[h] = (_rope(p_ref[:, P_QS + h * HD:P_QS + (h + 1) * HD], c2, s2) * qs_hd).T.astype(BF16)
            qft_ref[h] = (p_ref[:, P_QF + h * HD:P_QF + (h + 1) * HD] * qs_hd).T.astype(BF16)
            both_ways(p_ref[:, P_KF + h * HD:P_KF + (h + 1) * HD], kf_ref, kft_ref, h)
            both_ways(p_ref[:, P_VF + h * HD:P_VF + (h + 1) * HD], vf_ref, vft_ref, h)
        for h in range(SWA_KV):
            both_ways(_rope(p_ref[:, P_KS + h * HD:P_KS + (h + 1) * HD], c2, s2), ks_ref, kst_ref, h)
            both_ways(p_ref[:, P_VS + h * HD:P_VS + (h + 1) * HD], vs_ref, vst_ref, h)

        @pl.when(pl.program_id(0) == 0)
        def _():
            carry_row[...] = jnp.zeros_like(carry_row)

        log_f = _log_sigmoid(p_ref[:, P_F:P_F + H] + b_ref[...])
        c_tok = jnp.dot(_tri(tm, upper=False), log_f, preferred_element_type=F32,
                        precision=lax.Precision.HIGHEST) + carry_row[0:1, 0:H]
        for h in range(H):
            ccol_ref[h] = c_tok[:, h:h + 1] * LOG2E
        carry_row[0:1, 0:H] = c_tok[tm - 1:tm, :]

    out_shape = [jax.ShapeDtypeStruct((S, Q_LORA), BF16), jax.ShapeDtypeStruct((S, KV_LORA), BF16)]
    out_specs = [_tok_spec(tm, Q_LORA), _tok_spec(tm, KV_LORA)]

    def add(heads, d, lanes):
        out_shape.append(jax.ShapeDtypeStruct((heads, d, S) if lanes else (heads, S, d), BF16))
        out_specs.append(_lane_spec(heads, d, tm) if lanes else _head_spec(heads, tm, d))

    for heads_q, heads_kv, dqk, dv in ((H, H, QK_MLA, VDIM), (H, SWA_KV, HD, HD), (H, H, HD, HD)):
        add(heads_q, dqk, True)
        add(heads_kv, dqk, False)
        add(heads_kv, dqk, True)
        add(heads_kv, dv, False)
        add(heads_kv, dv, True)
    out_shape.append(jax.ShapeDtypeStruct((H, S, 1), F32))
    out_specs.append(_head_spec(H, tm, 1))
    in_specs = [_tok_spec(tm, P_COLS), _tok_spec(tm, ROPE), _tok_spec(tm, ROPE), _full_spec((1, Q_LORA)),
                _full_spec((1, KV_LORA)), _full_spec((1, H)),
                pl.BlockSpec((None,) + wqb.shape[1:], lambda i: (l, 0, 0)),
                pl.BlockSpec((None,) + wkvb.shape[1:], lambda i: (l, 0, 0))]
    return pl.pallas_call(
        body, name="mixer_prep", grid=(S // tm,), in_specs=in_specs, out_specs=out_specs, out_shape=out_shape,
        scratch_shapes=[pltpu.VMEM((8, 128), F32)],
        compiler_params=_cparams("arbitrary"))(p, c2, s2, q_norm, kv_norm, bias, wqb, wkvb)


def mixer_prep_bwd(p, c2, s2, q_norm, kv_norm, bias, wqb, wkvb, l, dqm, dkm, dvm, dqs, dks, dvs, dqf, dkf, dvf, dc):
    S = p.shape[0]
    tm = _tile(S, 256)
    nt = S // tm
    H, G = N_HEADS, N_HEADS // SWA_KV

    def body(p_ref, c2_ref, s2_ref, qn_ref, kvn_ref, b_ref, wqb_ref, wkvb_ref,
             dqm_ref, dkm_ref, dvm_ref, dqs_ref, dks_ref, dvs_ref, dqf_ref, dkf_ref, dvf_ref, dc_ref,
             dp_ref, dq_ref, dkv_ref, dqn_ref, dkvn_ref, db_ref, carry):
        c2, s2 = c2_ref[...], -s2_ref[...]

        @pl.when(pl.program_id(0) == 0)
        def _():
            dqn_ref[...] = jnp.zeros_like(dqn_ref)
            dkvn_ref[...] = jnp.zeros_like(dkvn_ref)
            db_ref[...] = jnp.zeros_like(db_ref)
            carry[...] = jnp.zeros_like(carry)

        sc_mla, sc_hd = QK_MLA ** -0.5, HD ** -0.5
        dk_pe_t = jnp.zeros((ROPE, tm), F32)
        for h in range(H):
            dq_ref[:, h * QK_MLA:h * QK_MLA + NOPE] = (dqm_ref[h, 0:NOPE, :] * sc_mla).T.astype(BF16)
            dq_ref[:, h * QK_MLA + NOPE:(h + 1) * QK_MLA] = _rope(
                (dqm_ref[h, NOPE:QK_MLA, :] * sc_mla).T, c2, s2).astype(BF16)
            dkv_ref[:, h * 256:h * 256 + NOPE] = (dkm_ref[h, 0:NOPE, :] * LN2).T.astype(BF16)
            dkv_ref[:, h * 256 + NOPE:(h + 1) * 256] = dvm_ref[h].T.astype(BF16)
            dk_pe_t = dk_pe_t + dkm_ref[h, NOPE:QK_MLA, :]
        dk_pe = (dk_pe_t * LN2).T

        def through_norm(dcn, c, gain, dgain_ref):
            c_hat, r = _norm_hat(c)
            dhg = dcn * gain
            dgain_ref[...] += jnp.sum(dcn * c_hat, axis=0, keepdims=True)
            return r * (dhg - c_hat * jnp.mean(dhg * c_hat, axis=-1, keepdims=True))

        dcqn = _dot_nt(dq_ref[...], wqb_ref[...])
        dckvn = _dot_nt(dkv_ref[...], wkvb_ref[...])
        dp_ref[:, P_CQ:P_CQ + Q_LORA] = through_norm(dcqn, p_ref[:, P_CQ:P_CQ + Q_LORA], qn_ref[...], dqn_ref).astype(BF16)
        dp_ref[:, P_CKV:P_CKV + KV_LORA] = through_norm(
            dckvn, p_ref[:, P_CKV:P_CKV + KV_LORA], kvn_ref[...], dkvn_ref).astype(BF16)
        for h in range(H):
            dp_ref[:, P_QS + h * HD:P_QS + (h + 1) * HD] = _rope((dqs_ref[h] * sc_hd).T, c2, s2).astype(BF16)
            dp_ref[:, P_QF + h * HD:P_QF + (h + 1) * HD] = (dqf_ref[h] * sc_hd).T.astype(BF16)
            dp_ref[:, P_KF + h * HD:P_KF + (h + 1) * HD] = (dkf_ref[h] * LN2).T.astype(BF16)
            dp_ref[:, P_VF + h * HD:P_VF + (h + 1) * HD] = dvf_ref[h].T.astype(BF16)
        for kvh in range(SWA_KV):
            dk = dks_ref[kvh * G]
            dv = dvs_ref[kvh * G]
            for g in range(1, G):
                dk = dk + dks_ref[kvh * G + g]
                dv = dv + dvs_ref[kvh * G + g]
            dp_ref[:, P_KS + kvh * HD:P_KS + (kvh + 1) * HD] = _rope((dk * LN2).T, c2, s2).astype(BF16)
            dp_ref[:, P_VS + kvh * HD:P_VS + (kvh + 1) * HD] = dv.T.astype(BF16)

        dcv = dc_ref[...]
        dlog_f = jnp.dot(_tri(tm, upper=True), dcv, preferred_element_type=F32,
                         precision=lax.Precision.HIGHEST) + carry[0:1, 0:H]
        carry[0:1, 0:H] = dlog_f[0:1, :]
        df = dlog_f * _sigmoid(-(p_ref[:, P_F:P_F + H] + b_ref[...]))
        db_ref[...] += jnp.sum(df, axis=0, keepdims=True)
        dp_ref[:, P_KR:P_COLS] = jnp.zeros((tm, P_COLS - P_KR), BF16)
        dp_ref[:, P_KR:P_KR + ROPE] = _rope(dk_pe, c2, s2).astype(BF16)
        dp_ref[:, P_F:P_F + H] = df.astype(BF16)

    rev = nt
    in_specs = [_tok_spec(tm, P_COLS, rev), _tok_spec(tm, ROPE, rev), _tok_spec(tm, ROPE, rev), _full_spec((1, Q_LORA)),
                _full_spec((1, KV_LORA)), _full_spec((1, H)),
                pl.BlockSpec((None,) + wqb.shape[1:], lambda i: (l, 0, 0)),
                pl.BlockSpec((None,) + wkvb.shape[1:], lambda i: (l, 0, 0)),
                _lane_spec(H, QK_MLA, tm, rev), _lane_spec(H, QK_MLA, tm, rev), _lane_spec(H, VDIM, tm, rev)]
    in_specs += [_lane_spec(H, HD, tm, rev)] * 6 + [_tok_spec(tm, H, rev)]
    out_specs = [_tok_spec(tm, P_COLS, rev), _tok_spec(tm, N_HEADS * QK_MLA, rev), _tok_spec(tm, N_HEADS * 256, rev),
                 _full_spec((1, Q_LORA)), _full_spec((1, KV_LORA)), _full_spec((1, H))]
    out_shape = [jax.ShapeDtypeStruct((S, P_COLS), BF16), jax.ShapeDtypeStruct((S, N_HEADS * QK_MLA), BF16),
                 jax.ShapeDtypeStruct((S, N_HEADS * 256), BF16), jax.ShapeDtypeStruct((1, Q_LORA), F32),
                 jax.ShapeDtypeStruct((1, KV_LORA), F32), jax.ShapeDtypeStruct((1, H), F32)]
    return pl.pallas_call(
        body, name="mixer_prep_bwd", grid=(nt,), in_specs=in_specs, out_specs=out_specs, out_shape=out_shape,
        scratch_shapes=[pltpu.VMEM((8, 128), F32)], compiler_params=_cparams("arbitrary"))(
            p, c2, s2, q_norm, kv_norm, bias, wqb, wkvb, dqm, dkm, dvm, dqs, dks, dvs, dqf, dkf, dvf, dc)


def merge_heads(o_mla, o_swa, o_fox):
    H, S, _ = o_mla.shape
    tm = _tile(S, 512)
    width = H * (VDIM + 2 * HD)

    def body(om_ref, os_ref, of_ref, m_ref):
        for h in range(H):
            m_ref[:, h * VDIM:(h + 1) * VDIM] = om_ref[h]
            m_ref[:, H * VDIM + h * HD:H * VDIM + (h + 1) * HD] = os_ref[h]
            m_ref[:, H * (VDIM + HD) + h * HD:H * (VDIM + HD) + (h + 1) * HD] = of_ref[h]

    return pl.pallas_call(
        body, name="merge_heads", grid=(S // tm,),
        in_specs=[_head_spec(H, tm, VDIM), _head_spec(H, tm, HD), _head_spec(H, tm, HD)],
        out_specs=_tok_spec(tm, width), out_shape=jax.ShapeDtypeStruct((S, width), BF16),
        compiler_params=_cparams("arbitrary"))(o_mla, o_swa, o_fox)


def split_heads(dmixed, mixed):
    S, width = mixed.shape
    H = N_HEADS
    tm = _tile(S, 512)

    def body(dm_ref, m_ref, dom_ref, dos_ref, dof_ref, dm_delta, ds_delta, df_delta):
        def one(h, off, d, do_ref, delta_ref):
            dv = dm_ref[:, off:off + d].astype(F32)
            do_ref[h] = dv.T.astype(BF16)
            prod = dv * m_ref[:, off:off + d].astype(F32)
            rows = lax.dot_general(jnp.ones((8, d), F32), prod, (((1,), (1,)), ((), ())),
                                   preferred_element_type=F32, precision=lax.Precision.HIGHEST)
            delta_ref[h] = rows[0:1, :]

        for h in range(H):
            one(h, h * VDIM, VDIM, dom_ref, dm_delta)
            one(h, H * VDIM + h * HD, HD, dos_ref, ds_delta)
            one(h, H * (VDIM + HD) + h * HD, HD, dof_ref, df_delta)

    row_spec = pl.BlockSpec((H, 1, tm), lambda i: (0, 0, i))
    row_shape = jax.ShapeDtypeStruct((H, 1, S), F32)
    return pl.pallas_call(
        body, name="split_heads", grid=(S // tm,),
        in_specs=[_tok_spec(tm, width), _tok_spec(tm, width)],
        out_specs=[_lane_spec(H, VDIM, tm), _lane_spec(H, HD, tm), _lane_spec(H, HD, tm), row_spec, row_spec, row_spec],
        out_shape=[jax.ShapeDtypeStruct((H, VDIM, S), BF16), jax.ShapeDtypeStruct((H, HD, S), BF16),
                   jax.ShapeDtypeStruct((H, HD, S), BF16), row_shape, row_shape, row_shape],
        compiler_params=_cparams("arbitrary"))(dmixed, mixed)


def _attn_tile(S):
    return 512 if (S % 512 == 0 and S > 512) else S // 2


def _valid(q0, k0, shape, q_axis, window):
    qpos = q0 + lax.broadcasted_iota(jnp.int32, shape, q_axis)
    kpos = k0 + lax.broadcasted_iota(jnp.int32, shape, 1 - q_axis)
    ok = kpos <= qpos
    if window is not None:
        ok = jnp.logical_and(ok, kpos > qpos - window)
    return ok


def attn_fwd(name, qt, k, vt, window=None, sinks=None, ccol=None, carry=None):
    H, dq, S = qt.shape
    Hk, dv, _ = vt.shape
    G = H // Hk
    t = _attn_tile(S)
    nq = S // t
    fox, use_sink = ccol is not None, sinks is not None
    n_carry = len(carry[2]) if carry else 0

    def body(*refs):
        refs = list(refs)
        sink_ref = refs.pop(0) if use_sink else None
        q_ref, k_ref, vt_ref = refs[:3]
        refs = refs[3:]
        ccol_ref = refs.pop(0) if fox else None
        carried_in = [refs.pop(0) for _ in range(n_carry)]
        o_ref, lse_ref = refs[:2]
        refs = refs[2:]
        carried_out = [refs.pop(0) for _ in range(n_carry)]
        m_ref, l_ref, acc_ref = refs[:3]
        h, i = pl.program_id(0), pl.program_id(1)
        if carry:
            @pl.when(jnp.logical_and(h == 0, i == 0))
            def _():
                _gather_phase(carry[0], carry[1], carried_in, carried_out, refs[3], refs[4], start=True)
        qv = q_ref[...]
        if use_sink:
            m_ref[...] = jnp.full(m_ref.shape, sink_ref[h] * LOG2E, F32)
            l_ref[...] = jnp.ones(l_ref.shape, F32)
        else:
            m_ref[...] = jnp.full(m_ref.shape, NEG, F32)
            l_ref[...] = jnp.zeros(l_ref.shape, F32)
        acc_ref[...] = jnp.zeros(acc_ref.shape, F32)

        def steps(blocks):
            offs = [pl.multiple_of(j * t, t) for j, _ in blocks]
            scores = [_dot(k_ref[pl.ds(off, t), :], qv) for off in offs]
            for (j, masked), off, st in zip(blocks, offs, scores):
                if fox:
                    st = st - ccol_ref[pl.ds(off, t), :]
                if masked:
                    st = jnp.where(_valid(i * t, j * t, (t, t), 1, window), st, NEG)
                m_prev = m_ref[...]
                m_new = jnp.maximum(m_prev, jnp.max(st, axis=0, keepdims=True))
                alpha = jnp.exp2(m_prev - m_new)
                pt = jnp.exp2(st - m_new)
                l_ref[...] = alpha * l_ref[...] + jnp.sum(pt, axis=0, keepdims=True)
                acc_ref[...] = alpha * acc_ref[...] + _dot(vt_ref[:, pl.ds(off, t)], pt.astype(BF16))
                m_ref[...] = m_new

        if window is None:
            def pair(n, carry):
                steps([(2 * n, False), (2 * n + 1, False)])
                return carry
            lax.fori_loop(0, i // 2, pair, 0)

            @pl.when(i % 2 == 1)
            def _():
                steps([(i - 1, False), (i, True)])

            @pl.when(i % 2 == 0)
            def _():
                steps([(i, True)])
        else:
            def one(j, carry):
                steps([(j, True)])
                return carry
            lax.fori_loop(jnp.maximum(i * t - (window - 1), 0) // t, i + 1, one, 0)
        l = l_ref[...]
        o_ref[...] = (acc_ref[...] / l).T.astype(BF16)
        lse_ref[...] = m_ref[...] + jnp.log2(l)
        if carry:
            @pl.when(jnp.logical_and(h == H - 1, i == nq - 1))
            def _():
                _gather_phase(carry[0], carry[1], carried_in, carried_out, refs[3], refs[4], start=False)

    in_specs, args = [], []
    if use_sink:
        in_specs.append(pl.BlockSpec(memory_space=pltpu.SMEM))
        args.append(sinks)
    in_specs += [pl.BlockSpec((None, dq, t), lambda h, i: (h, 0, i)),
                 pl.BlockSpec((None, S, dq), lambda h, i: (h // G, 0, 0)),
                 pl.BlockSpec((None, dv, S), lambda h, i: (h // G, 0, 0))]
    args += [qt, k, vt]
    if fox:
        in_specs.append(pl.BlockSpec((None, S, 1), lambda h, i: (h, 0, 0)))
        args.append(ccol)
    out_specs = [pl.BlockSpec((None, t, dv), lambda h, i: (h, i, 0)), pl.BlockSpec((None, 1, t), lambda h, i: (h, 0, i))]
    out_shape = [jax.ShapeDtypeStruct((H, S, dv), BF16), jax.ShapeDtypeStruct((H, 1, S), F32)]
    scratch = [pltpu.VMEM((1, t), F32), pltpu.VMEM((1, t), F32), pltpu.VMEM((dv, t), F32)]
    aliases = {}
    if carry:
        aliases = {len(args) + n: 2 + n for n in range(n_carry)}
        in_specs += [_ANY] * n_carry
        args += list(carry[2])
        out_specs += [_ANY] * n_carry
        out_shape += [jax.ShapeDtypeStruct(a.shape, a.dtype) for a in carry[2]]
        scratch += [pltpu.SemaphoreType.DMA((n_carry, 3)), pltpu.SemaphoreType.DMA((n_carry, 3))]
    out = pl.pallas_call(
        body, name=name, grid=(H, nq), in_specs=in_specs, out_specs=out_specs, out_shape=out_shape,
        scratch_shapes=scratch, input_output_aliases=aliases,
        compiler_params=_cparams("arbitrary", "arbitrary"))(*args)
    return (out[0], out[1], list(out[2:])) if carry else (out[0], out[1])


def attn_bwd(name, qt, k, kt, v, dot, lse, delta, window=None, sinks=None, ccol=None):
    H, dq, S = qt.shape
    Hk, _, dv = v.shape
    G = H // Hk
    t = _attn_tile(S)
    nq = S // t
    fox, use_sink = ccol is not None, sinks is not None

    def body(*refs):
        refs = list(refs)
        sink_ref = refs.pop(0) if use_sink else None
        qt_ref, k_ref, kt_ref, v_ref, dot_ref, lse_ref, delta_ref = refs[:7]
        refs = refs[7:]
        ccol_ref = refs.pop(0) if fox else None
        dqt_ref, dkt_ref, dvt_ref = refs[:3]
        refs = refs[3:]
        dc_ref = refs.pop(0) if fox else None
        dsink_ref = refs.pop(0) if use_sink else None
        dck_ref = refs.pop(0) if fox else None
        h, j = pl.program_id(0), pl.program_id(1)

        @pl.when(j == 0)
        def _():
            dqt_ref[...] = jnp.zeros(dqt_ref.shape, F32)
            if fox:
                dc_ref[...] = jnp.zeros(dc_ref.shape, F32)
            if use_sink:
                ps = jnp.exp2(sink_ref[h] * LOG2E - lse_ref[...]) * delta_ref[...]
                dsink_ref[...] = jnp.broadcast_to(-jnp.sum(ps, axis=-1, keepdims=True), dsink_ref.shape)

        dkt_ref[...] = jnp.zeros(dkt_ref.shape, F32)
        dvt_ref[...] = jnp.zeros(dvt_ref.shape, F32)
        if fox:
            dck_ref[...] = jnp.zeros(dck_ref.shape, F32)
        kv, ktv, vv = k_ref[...], kt_ref[...], v_ref[...]

        def steps(blocks):
            offs = [pl.multiple_of(i * t, t) for i, _ in blocks]
            qts = [qt_ref[:, pl.ds(off, t)] for off in offs]
            dots = [dot_ref[:, pl.ds(off, t)] for off in offs]
            scores = [_dot(kv, qti) for qti in qts]
            dprobs = [_dot(vv, doti) for doti in dots]
            for (i, masked), off, qti, doti, st, dpt in zip(blocks, offs, qts, dots, scores, dprobs):
                if fox:
                    st = st - ccol_ref[...]
                if masked:
                    st = jnp.where(_valid(i * t, j * t, (t, t), 1, window), st, NEG)
                pt = jnp.exp2(st - lse_ref[:, pl.ds(off, t)])
                dvt_ref[...] += _dot_nt(doti, pt.astype(BF16))
                dst = pt * (dpt - delta_ref[:, pl.ds(off, t)])
                if fox:
                    dc_ref[:, pl.ds(off, t)] += jnp.sum(dst, axis=0, keepdims=True)
                    dck_ref[...] += jnp.sum(dst, axis=1, keepdims=True)
                dsb = dst.astype(BF16)
                dkt_ref[...] += _dot_nt(qti, dsb)
                dqt_ref[:, pl.ds(off, t)] += _dot(ktv, dsb)

        if window is None:
            odd = (nq - 1 - j) % 2

            @pl.when(odd == 1)
            def _():
                steps([(j, True), (j + 1, False)])

            @pl.when(odd == 0)
            def _():
                steps([(j, True)])

            first = j + 1 + odd

            def pair(n, carry):
                steps([(first + 2 * n, False), (first + 2 * n + 1, False)])
                return carry
            lax.fori_loop(0, (nq - first) // 2, pair, 0)
        else:
            def one(i, carry):
                steps([(i, True)])
                return carry
            lax.fori_loop(j, jnp.minimum((j * t + t - 1 + window - 1) // t, nq - 1) + 1, one, 0)
        if fox:
            key_row = jnp.broadcast_to(dck_ref[...], (t, 128)).T[0:1, :]
            dc_ref[:, pl.ds(pl.multiple_of(j * t, t), t)] -= key_row

    in_specs, args = [], []
    if use_sink:
        in_specs.append(pl.BlockSpec(memory_space=pltpu.SMEM))
        args.append(sinks)
    whole = lambda d: pl.BlockSpec((None, d, S), lambda h, j: (h, 0, 0))
    keys = lambda d: pl.BlockSpec((None, d, t), lambda h, j: (h, 0, j))
    row = pl.BlockSpec((None, 1, S), lambda h, j: (h, 0, 0))
    in_specs += [whole(dq), pl.BlockSpec((None, t, dq), lambda h, j: (h // G, j, 0)),
                 pl.BlockSpec((None, dq, t), lambda h, j: (h // G, 0, j)),
                 pl.BlockSpec((None, t, dv), lambda h, j: (h // G, j, 0)), whole(dv), row, row]
    args += [qt, k, kt, v, dot, lse, delta]
    out_specs = [whole(dq), keys(dq), keys(dv)]
    out_shape = [jax.ShapeDtypeStruct((H, dq, S), F32), jax.ShapeDtypeStruct((H, dq, S), F32),
                 jax.ShapeDtypeStruct((H, dv, S), F32)]
    if fox:
        in_specs.append(pl.BlockSpec((None, t, 1), lambda h, j: (h, j, 0)))
        args.append(ccol)
        out_specs.append(row)
        out_shape.append(jax.ShapeDtypeStruct((H, 1, S), F32))
    if use_sink:
        out_specs.append(pl.BlockSpec((None, 1, 128), lambda h, j: (h, 0, 0)))
        out_shape.append(jax.ShapeDtypeStruct((H, 1, 128), F32))
    return pl.pallas_call(
        body, name=name, grid=(H, nq), in_specs=in_specs, out_specs=out_specs, out_shape=out_shape,
        scratch_shapes=[pltpu.VMEM((t, 1), F32)] if fox else [],
        compiler_params=_cparams("arbitrary", "arbitrary"))(*args)


def _rows_tile(rows):
    for tr in (256, 128, 64, 32, 16, 8):
        if rows % tr == 0:
            return tr
    return rows


def adamw(w, g, m, v):
    L, R, C = w.shape
    tr = _rows_tile(R)

    def body(w_ref, g_ref, m_ref, v_ref, g_out, d_ref, nm_ref, nv_ref):
        gv = g_ref[...]
        mn = ADAM_B1 * m_ref[...] + (1.0 - ADAM_B1) * gv
        vn = ADAM_B2 * v_ref[...] + (1.0 - ADAM_B2) * (gv * gv)
        m_hat = mn / (1.0 - ADAM_B1 ** ADAM_STEP)
        v_hat = vn / (1.0 - ADAM_B2 ** ADAM_STEP)
        d_ref[...] = -ADAM_LR * (m_hat / (jnp.sqrt(v_hat) + ADAM_EPS) + ADAM_WD * w_ref[...])
        nm_ref[...] = mn
        nv_ref[...] = vn
        g_out[...] = gv

    spec = pl.BlockSpec((None, tr, C), lambda l, i: (l, i, 0))
    shape = jax.ShapeDtypeStruct((L, R, C), F32)
    return pl.pallas_call(
        body, name="adamw", grid=(L, R // tr), in_specs=[spec] * 4, out_specs=[spec] * 4, out_shape=[shape] * 4,
        compiler_params=_cparams("arbitrary", "arbitrary"))(w, g, m, v)


def place_own(w, chip):
    L, R, C = w.shape
    tr = _rows_tile(R)

    def body(c_ref, w_ref, o_ref):
        o_ref[...] = w_ref[...].astype(BF16)

    return pl.pallas_call(
        body, name="place_own",
        grid_spec=pltpu.PrefetchScalarGridSpec(
            num_scalar_prefetch=1, grid=(L, R // tr),
            in_specs=[pl.BlockSpec((None, tr, C), lambda l, i, c: (l, i, 0))],
            out_specs=pl.BlockSpec((None, None, tr, C), lambda l, i, c: (l, c[0], i, 0))),
        out_shape=jax.ShapeDtypeStruct((L, N_CHIPS, R, C), BF16),
        compiler_params=_cparams("arbitrary", "arbitrary"))(chip, w)


def sum_pair(grad, other, layer):
    _, R, C = grad.shape
    tr = _rows_tile(R)

    def body(l_ref, g_ref, o_ref, out_ref):
        out_ref[...] = (g_ref[...].astype(F32) + o_ref[...].astype(F32)).astype(BF16)

    return pl.pallas_call(
        body, name="sum_pair",
        grid_spec=pltpu.PrefetchScalarGridSpec(
            num_scalar_prefetch=1, grid=(R // tr,),
            in_specs=[pl.BlockSpec((None, tr, C), lambda i, l: (l[0], i, 0)), pl.BlockSpec((tr, C), lambda i, l: (i, 0))],
            out_specs=pl.BlockSpec((tr, C), lambda i, l: (i, 0))),
        out_shape=jax.ShapeDtypeStruct((R, C), BF16), compiler_params=_cparams("arbitrary"))(layer, grad, other)


def sum_chips(part, recv, chip, layer):
    _, R, C = part.shape
    tr = _rows_tile(R)

    def body(c_ref, l_ref, p_ref, r_ref, out_ref):
        acc = p_ref[...].astype(F32)
        for k in range(N_CHIPS - 1):
            acc = acc + r_ref[k].astype(F32)
        out_ref[...] = acc

    return pl.pallas_call(
        body, name="sum_chips",
        grid_spec=pltpu.PrefetchScalarGridSpec(
            num_scalar_prefetch=2, grid=(R // tr,),
            in_specs=[pl.BlockSpec((None, tr, C), lambda i, c, l: (c[0], i, 0)),
                      pl.BlockSpec((N_CHIPS - 1, tr, C), lambda i, c, l: (0, i, 0))],
            out_specs=pl.BlockSpec((None, tr, C), lambda i, c, l: (l[0], i, 0))),
        out_shape=jax.ShapeDtypeStruct((2, R, C), F32), compiler_params=_cparams("arbitrary"))(chip, layer, part, recv)


_ANY = pl.BlockSpec(memory_space=pl.ANY)


def _place():
    x, y, c = lax.axis_index("x"), lax.axis_index("y"), lax.axis_index("c")
    chips = [(1 - x, y), (x, 1 - y), (1 - x, 1 - y)]
    return x, y, c, chips


def _gather_phase(phase, layer, w, o, send, recv, start):
    x, y, c, chips = _place()
    me, sib = 2 * x + y, (x, y, 1 - c)
    n = len(w)
    works = c == layer

    def copy(t, k, shard, to, src=None):
        blk = o[t].at[layer, shard]
        return pltpu.make_async_remote_copy(src_ref=blk if src is None else src, dst_ref=blk, send_sem=send.at[t, k],
                                            recv_sem=recv.at[t, k], device_id=to, device_id_type=MESH)

    def outgoing():
        if phase == "ici":
            return [copy(t, k, me, (*chip, c), src=w[t].at[layer, me]) for t in range(n) for k, chip in enumerate(chips)]
        return [copy(t, k, 2 * chip[0] + chip[1], sib) for t in range(n) for k, chip in enumerate(chips)]

    def incoming():
        return [copy(t, k, 2 * chip[0] + chip[1], (x, y, c)) for t in range(n) for k, chip in enumerate(chips)]

    if start:
        @pl.when(works)
        def _():
            for cp in outgoing():
                cp.start()
    else:
        @pl.when(works)
        def _():
            if phase == "ici":
                for cp in incoming():
                    cp.wait_recv()
            for cp in outgoing():
                cp.wait_send()

        if phase == "pass":
            @pl.when(jnp.logical_not(works))
            def _():
                for cp in incoming():
                    cp.wait_recv()


def gather_layer(ws, layer):
    n = len(ws)

    def body(*refs):
        w, o = refs[:n], refs[n:2 * n]
        ici_send, ici_recv, pass_send, pass_recv = refs[2 * n:]
        _gather_phase("ici", layer, w, o, ici_send, ici_recv, start=True)
        _gather_phase("ici", layer, w, o, ici_send, ici_recv, start=False)
        _gather_phase("pass", layer, w, o, pass_send, pass_recv, start=True)
        _gather_phase("pass", layer, w, o, pass_send, pass_recv, start=False)

    sems = [pltpu.SemaphoreType.DMA((n, 3))] * 4
    return pl.pallas_call(
        body, name="gather_layer", in_specs=[_ANY] * n, out_specs=[_ANY] * n,
        out_shape=[jax.ShapeDtypeStruct(w.shape, w.dtype) for w in ws], input_output_aliases={t: t for t in range(n)},
        scratch_shapes=sems, compiler_params=pltpu.CompilerParams(has_side_effects=True))(*ws)


def pair_exchange(gs):
    n = len(gs)

    def body(*refs):
        g, o = refs[:n], refs[n:2 * n]
        send, recv = refs[2 * n:]
        x, y, c, _ = _place()
        cps = [pltpu.make_async_remote_copy(src_ref=g[t].at[1 - c], dst_ref=o[t], send_sem=send.at[t], recv_sem=recv.at[t],
                                            device_id=(x, y, 1 - c), device_id_type=MESH) for t in range(n)]
        for cp in cps:
            cp.start()
        for cp in cps:
            cp.wait()

    return pl.pallas_call(
        body, name="pair_exchange", in_specs=[_ANY] * n, out_specs=[_ANY] * n,
        out_shape=[jax.ShapeDtypeStruct(g.shape[1:], g.dtype) for g in gs],
        scratch_shapes=[pltpu.SemaphoreType.DMA((n,)), pltpu.SemaphoreType.DMA((n,))],
        compiler_params=pltpu.CompilerParams(has_side_effects=True))(*gs)


def chip_scatter(ps):
    n = len(ps)

    def body(*refs):
        p, o = refs[:n], refs[n:2 * n]
        send, recv = refs[2 * n:]
        x, y, c, chips = _place()
        cps = [pltpu.make_async_remote_copy(src_ref=p[t].at[2 * chip[0] + chip[1]], dst_ref=o[t].at[k],
                                            send_sem=send.at[t, k], recv_sem=recv.at[t, k], device_id=(*chip, c),
                                            device_id_type=MESH)
               for t in range(n) for k, chip in enumerate(chips)]
        for cp in cps:
            cp.start()
        for cp in cps:
            cp.wait()

    return pl.pallas_call(
        body, name="chip_scatter", in_specs=[_ANY] * n, out_specs=[_ANY] * n,
        out_shape=[jax.ShapeDtypeStruct((N_CHIPS - 1,) + p.shape[1:], p.dtype) for p in ps],
        scratch_shapes=[pltpu.SemaphoreType.DMA((n, 3)), pltpu.SemaphoreType.DMA((n, 3))],
        compiler_params=pltpu.CompilerParams(has_side_effects=True))(*ps)


def pair_share(rs):
    n = len(rs)

    def body(*refs):
        r, o = refs[:n], refs[n:2 * n]
        send, recv = refs[2 * n:]
        x, y, c, _ = _place()

        def remote(t, layer):
            return pltpu.make_async_remote_copy(src_ref=r[t].at[layer], dst_ref=o[t].at[layer], send_sem=send.at[t],
                                                recv_sem=recv.at[t], device_id=(x, y, 1 - c), device_id_type=MESH)

        for t in range(n):
            remote(t, c).start()
        for t in range(n):
            remote(t, 1 - c).wait_recv()
            remote(t, c).wait_send()

    return pl.pallas_call(
        body, name="pair_share", in_specs=[_ANY] * n, out_specs=[_ANY] * n,
        out_shape=[jax.ShapeDtypeStruct(r.shape, r.dtype) for r in rs], input_output_aliases={t: t for t in range(n)},
        scratch_shapes=[pltpu.SemaphoreType.DMA((n,)), pltpu.SemaphoreType.DMA((n,))],
        compiler_params=pltpu.CompilerParams(has_side_effects=True))(*rs)


def all_reduce_small(buf):
    def body(x_ref, o_ref, land, send, recv):
        x, y, c, _ = _place()
        me = 4 * x + 2 * y + c
        land[me] = x_ref[...]
        cps = []
        for mask in range(1, N_DEV):
            px = 1 - x if mask & 4 else x
            py = 1 - y if mask & 2 else y
            pc = 1 - c if mask & 1 else c
            cps.append(pltpu.make_async_remote_copy(src_ref=x_ref, dst_ref=land.at[me], send_sem=send.at[mask - 1],
                                                    recv_sem=recv.at[mask - 1], device_id=(px, py, pc), device_id_type=MESH))
            cps[-1].start()
        for mask in range(1, N_DEV):
            px = 1 - x if mask & 4 else x
            py = 1 - y if mask & 2 else y
            pc = 1 - c if mask & 1 else c
            pltpu.make_async_remote_copy(src_ref=x_ref, dst_ref=land.at[4 * px + 2 * py + pc], send_sem=send.at[mask - 1],
                                         recv_sem=recv.at[mask - 1], device_id=(px, py, pc), device_id_type=MESH).wait_recv()
        for cp in cps:
            cp.wait_send()
        acc = land[0]
        for d in range(1, N_DEV):
            acc = acc + land[d]
        o_ref[...] = acc

    vm = pl.BlockSpec(memory_space=pltpu.VMEM)
    return pl.pallas_call(
        body, name="all_reduce_small", in_specs=[vm], out_specs=vm, out_shape=jax.ShapeDtypeStruct(buf.shape, F32),
        scratch_shapes=[pltpu.VMEM((N_DEV,) + buf.shape, F32), pltpu.SemaphoreType.DMA((N_DEV - 1,)),
                        pltpu.SemaphoreType.DMA((N_DEV - 1,))])(buf)


def _take(ws, idx):
    return [ws[t] for t in idx]


def _put(ws, idx, new):
    ws = list(ws)
    for t, a in zip(idx, new):
        ws[t] = a
    return ws


def _ffn_fwd(x, h, gathered, first, l, next_gain, up_carry=None, down_carry=None):
    D = x.shape[1]
    if up_carry:
        phase, layer, idx = up_carry
        g, u, a, arrays = gate_up(h, gathered[first], gathered[first + 1], l, carry=(phase, layer, _take(gathered, idx)))
        gathered = _put(gathered, idx, arrays)
    else:
        g, u, a = gate_up(h, gathered[first], gathered[first + 1], l)
    wd = gathered[first + 2].reshape(gathered[first + 2].shape[0], -1, D)
    if down_carry:
        phase, layer, idx = down_carry
        y, h_next, arrays = down_proj(a, wd, l, x, next_gain, carry=(phase, layer, _take(gathered, idx)))
        gathered = _put(gathered, idx, arrays)
    else:
        y, h_next = down_proj(a, wd, l, x, next_gain)
    return y, h_next, (h, g, u, a), gathered


def _ffn_bwd(dy, x, gain, wg, wu, wd, l, saved, grads):
    h, g, u, a = saved
    D = x.shape[1]
    L = wg.shape[0]
    Fs = g.shape[1] // N_CHIPS
    dg, du = down_bwd(dy, wd, l, g, u)
    dwg = mm_tn(h, dg, _tile(D, 1024), Fs, blocked=True, layer=l, layers=L, into=grads[0])
    dwu = mm_tn(h, du, _tile(D, 1024), Fs, blocked=True, layer=l, layers=L, into=grads[1])
    dwd = mm_tn(a, dy, Fs, _tile(D, 1024), scale=0.5, layer=l, layers=L, into=grads[2])
    dh = gate_up_bwd(dg, du, wg, wu, l)
    dx, dgain = rms_bwd(dh, x, gain, dy)
    return dx, dgain, (dwg, dwu, dwd)


def _mixer_fwd(x, h, win, q_norm, kv_norm, sinks, bias, wqb, wkvb, gathered, c2, s2, l, next_gain, carry_layer=None):
    D = x.shape[1]
    p = mm_nn(h, win, 0, 640, F32)
    (cqn, ckvn, qmt, km, kmt, vm, vmt, qst, ks, kst, vs, vst, qft, kf, kft, vf, vft, ccol) = mixer_prep(
        p, c2, s2, q_norm, kv_norm, bias, wqb, wkvb, 0)
    if carry_layer is None:
        o_mla, lse_mla = attn_fwd("attn_mla", qmt, km, vmt)
        o_swa, lse_swa = attn_fwd("attn_swa", qst, ks, vst, window=WINDOW, sinks=sinks)
        o_fox, lse_fox = attn_fwd("attn_fox", qft, kf, vft, ccol=ccol)
    else:
        o_mla, lse_mla, arrays = attn_fwd("attn_mla", qmt, km, vmt, carry=("ici", carry_layer, _take(gathered, W_HEAD)))
        o_swa, lse_swa, arrays = attn_fwd("attn_swa", qst, ks, vst, window=WINDOW, sinks=sinks,
                                          carry=("pass", carry_layer, arrays))
        gathered = _put(gathered, W_HEAD, arrays)
        o_fox, lse_fox, arrays = attn_fwd("attn_fox", qft, kf, vft, ccol=ccol,
                                          carry=("ici", carry_layer, _take(gathered, W_TAIL)))
        gathered = _put(gathered, W_TAIL, arrays)
    mixed = merge_heads(o_mla, o_swa, o_fox)
    wout = gathered[6].reshape(gathered[6].shape[0], -1, D)
    y, h_next = mm_nn(mixed, wout, l, D, F32, resid=x, next_gain=next_gain)
    saved = (h, p, cqn, ckvn, qmt, km, kmt, vm, qst, ks, kst, vs, qft, kf, kft, vf, ccol, lse_mla, lse_swa, lse_fox, mixed)
    return y, h_next, saved, gathered


def _mixer_bwd(dy, x, gain, win, q_norm, kv_norm, sinks, bias, wqb, wkvb, wout, c2, s2, l, saved, dwout_so_far):
    (h, p, cqn, ckvn, qmt, km, kmt, vm, qst, ks, kst, vs, qft, kf, kft, vf, ccol, lse_mla, lse_swa, lse_fox, mixed) = saved
    S, D = x.shape
    width = mixed.shape[1]
    dmixed = mm_nt(dy, wout, l, _tile(width, 1024), BF16)
    dwout = mm_tn(mixed, dy, _tile(width, 1024), _tile(D, 1024), layer=l, layers=wout.shape[0], into=dwout_so_far)
    do_mla, do_swa, do_fox, dl_mla, dl_swa, dl_fox = split_heads(dmixed, mixed)
    dqm, dkm, dvm = attn_bwd("attn_mla_bwd", qmt, km, kmt, vm, do_mla, lse_mla, dl_mla)
    dqs, dks, dvs, dsink = attn_bwd("attn_swa_bwd", qst, ks, kst, vs, do_swa, lse_swa, dl_swa, window=WINDOW, sinks=sinks)
    dqf, dkf, dvf, dc_rows = attn_bwd("attn_fox_bwd", qft, kf, kft, vf, do_fox, lse_fox, dl_fox, ccol=ccol)
    dc = dc_rows.reshape(N_HEADS, S).T
    dp, dq, dkv, dqn, dkvn, dbias = mixer_prep_bwd(p, c2, s2, q_norm, kv_norm, bias, wqb, wkvb, 0, dqm, dkm, dvm, dqs, dks,
                                                   dvs, dqf, dkf, dvf, dc)
    dwqb = mm_tn(cqn, dq, Q_LORA, N_HEADS * QK_MLA)
    dwkvb = mm_tn(ckvn, dkv, KV_LORA, 1024)
    dwin = mm_tn(h, dp, _tile(D, 1024), 640)
    dh = mm_nt(dp, win, 0, _tile(D, 1024), F32)
    dx, dgain = rms_bwd(dh, x, gain, dy)
    return dx, dgain, dwin, dqn, dwqb, dkvn, dwkvb, dsink[:, 0, 0], dbias[0], dwout


def _pad_in_cols(w):
    pad = jnp.zeros(w.shape[:-1] + (P_COLS - IN_COLS,), w.dtype)
    return jnp.concatenate([w[..., :IN_KR], w[..., IN_KR + ROPE:IN_COLS - N_HEADS], w[..., IN_KR:IN_KR + ROPE],
                            w[..., IN_COLS - N_HEADS:], pad], axis=-1)


def _unpad_in_cols(w):
    return jnp.concatenate([w[..., :IN_KR], w[..., P_KR:P_KR + ROPE], w[..., IN_KR:P_KR], w[..., P_F:P_F + N_HEADS]], axis=-1)


def _col_shards(w):
    R = w.shape[0]
    return w.reshape(R, N_CHIPS, -1).transpose(1, 0, 2)


def _from_col_shards(w):
    L, _, R, C = w.shape
    return w.transpose(0, 2, 1, 3).reshape(L, R, N_CHIPS * C)


def kernel(x, positions, ffn1_norm, ffn1_w_gate, ffn1_w_up, ffn1_w_down, mix_norm, w_in, mla_q_norm, mla_w_q_b, mla_kv_norm, mla_w_kv_b, swa_sinks, fox_forget_bias, w_out, ffn2_norm, ffn2_w_gate, ffn2_w_up, ffn2_w_down, final_norm, loss_target, m_ffn1_norm, m_ffn1_w_gate, m_ffn1_w_up, m_ffn1_w_down, m_mix_norm, m_w_in, m_mla_q_norm, m_mla_w_q_b, m_mla_kv_norm, m_mla_w_kv_b, m_swa_sinks, m_fox_forget_bias, m_w_out, m_ffn2_norm, m_ffn2_w_gate, m_ffn2_w_up, m_ffn2_w_down, m_final_norm, v_ffn1_norm, v_ffn1_w_gate, v_ffn1_w_up, v_ffn1_w_down, v_mix_norm, v_w_in, v_mla_q_norm, v_mla_w_q_b, v_mla_kv_norm, v_mla_w_kv_b, v_swa_sinks, v_fox_forget_bias, v_w_out, v_ffn2_norm, v_ffn2_w_gate, v_ffn2_w_up, v_ffn2_w_down, v_final_norm):
    L = ffn1_norm.shape[0]
    S, D = x.shape[1], x.shape[2]
    F = ffn1_w_down.shape[1] * N_CHIPS
    xs, target = x[0], loss_target[0]
    cx, cy, cc = lax.axis_index("x"), lax.axis_index("y"), lax.axis_index("c")
    layer_id = jnp.reshape(cc, (1,)).astype(jnp.int32)
    chip_id = jnp.reshape(2 * cx + cy, (1,)).astype(jnp.int32)

    inv_freq = ROPE_THETA ** (-jnp.arange(0, ROPE, 2, dtype=F32) / ROPE)
    ang = positions[0].astype(F32)[:, None] * inv_freq
    cos, sin = jnp.cos(ang), jnp.sin(ang)
    c2, s2 = jnp.concatenate([cos, cos], axis=-1), jnp.concatenate([-sin, sin], axis=-1)

    big = [ffn1_w_gate, ffn1_w_up, ffn1_w_down, w_in, mla_w_q_b, mla_w_kv_b, w_out, ffn2_w_gate, ffn2_w_up, ffn2_w_down]
    gathered = [place_own(w, chip_id) for w in big]
    gathered = _put(gathered, W_FFN1, gather_layer(_take(gathered, W_FFN1), 0))

    def mixer_weights(ws, l):
        return (_pad_in_cols(_from_col_shards(ws[3][l:l + 1])), _from_col_shards(ws[4][l:l + 1]),
                _from_col_shards(ws[5][l:l + 1]))

    acts, small_w = [], []
    x0, h0 = xs, rms_fwd(xs, ffn1_norm[0][None])
    for l in range(L):
        nxt = l + 1 if l + 1 < L else None
        x1, h1, s1, gathered = _ffn_fwd(x0, h0, gathered, 0, l, mix_norm[l][None],
                                        up_carry=("ici", 0, W_REST) if l == 0 else None,
                                        down_carry=("pass", 0, W_REST) if l == 0 else None)
        small_w.append(mixer_weights(gathered, l))
        x2, h2, sm, gathered = _mixer_fwd(x1, h1, small_w[l][0], mla_q_norm[l][None], mla_kv_norm[l][None], swa_sinks[l],
                                          fox_forget_bias[l][None], small_w[l][1], small_w[l][2], gathered, c2, s2, l,
                                          ffn2_norm[l][None], carry_layer=nxt)
        x3, h3, s2_, gathered = _ffn_fwd(x2, h2, gathered, 7, l, ffn1_norm[l + 1][None] if nxt else None,
                                         up_carry=("pass", nxt, [9]) if nxt else None,
                                         down_carry=("pass", nxt, [7, 8]) if nxt else None)
        acts.append((x0, x1, x2, s1, sm, s2_))
        x0, h0 = x3, h3
    loss_part, dx, d_final = loss_head(x0, final_norm[None], target)
    wg1, wu1, wd1, _, _, _, wout, wg2, wu2, wd2 = gathered
    wd1, wd2, wout = wd1.reshape(L, F, D), wd2.reshape(L, F, D), wout.reshape(L, -1, D)

    small = {k: [None] * L for k in ("ffn1_norm", "mix_norm", "q_norm", "kv_norm", "sinks", "bias", "ffn2_norm")}
    per_layer = {k: [None] * L for k in ("win", "wqb", "wkvb")}
    ffn1_grads, ffn2_grads, dwout = (None,) * 3, (None,) * 3, None
    for l in reversed(range(L)):
        x0, x1, x2, s1, sm, s2_ = acts[l]
        dx, small["ffn2_norm"][l], ffn2_grads = _ffn_bwd(dx, x2, ffn2_norm[l][None], wg2, wu2, wd2, l, s2_, ffn2_grads)
        (dx, small["mix_norm"][l], dwin, small["q_norm"][l], dwqb, small["kv_norm"][l], dwkvb, small["sinks"][l],
         small["bias"][l], dwout) = _mixer_bwd(dx, x1, mix_norm[l][None], small_w[l][0], mla_q_norm[l][None],
                                               mla_kv_norm[l][None], swa_sinks[l], fox_forget_bias[l][None], small_w[l][1],
                                               small_w[l][2], wout, c2, s2, l, sm, dwout)
        per_layer["win"][l] = _col_shards(_unpad_in_cols(dwin))
        per_layer["wqb"][l] = _col_shards(dwqb)
        per_layer["wkvb"][l] = _col_shards(dwkvb)
        dx, small["ffn1_norm"][l], ffn1_grads = _ffn_bwd(dx, x0, ffn1_norm[l][None], wg1, wu1, wd1, l, s1, ffn1_grads)
    grad_x = dx[None]

    names = ("wg1", "wu1", "wd1", "win", "wqb", "wkvb", "wout", "wg2", "wu2", "wd2")
    Fs = F // N_CHIPS
    full = [ffn1_grads[0], ffn1_grads[1], ffn1_grads[2].reshape(L, N_CHIPS, Fs, D), jnp.stack(per_layer["win"]),
            jnp.stack(per_layer["wqb"]), jnp.stack(per_layer["wkvb"]), dwout.reshape(L, N_CHIPS, -1, D),
            ffn2_grads[0], ffn2_grads[1], ffn2_grads[2].reshape(L, N_CHIPS, Fs, D)]
    flat = [g.reshape(L, -1, g.shape[-1]) for g in full]
    from_sibling = pair_exchange(flat)
    part = [sum_pair(g, o, layer_id).reshape(f.shape[1:]) for g, o, f in zip(flat, from_sibling, full)]
    from_chips = chip_scatter(part)
    mine = [sum_chips(p, r, chip_id, layer_id) for p, r in zip(part, from_chips)]
    grads_big = pair_share(mine)

    pieces = [jnp.concatenate(small["ffn1_norm"]), jnp.concatenate(small["mix_norm"]), jnp.concatenate(small["q_norm"]),
              jnp.concatenate(small["kv_norm"]), jnp.stack(small["sinks"]), jnp.stack(small["bias"]),
              jnp.concatenate(small["ffn2_norm"]), d_final, loss_part[:, 0:1]]
    sizes = [int(p.size) for p in pieces]
    packed = jnp.concatenate([p.reshape(-1) for p in pieces])
    packed = jnp.pad(packed, (0, SMALL_ROWS * 128 - packed.shape[0])).reshape(SMALL_ROWS, 128)
    summed = all_reduce_small(packed).reshape(-1)
    out_small, off = [], 0
    for p, n in zip(pieces, sizes):
        out_small.append(summed[off:off + n].reshape(p.shape))
        off += n
    g_ffn1_norm, g_mix_norm, g_q_norm, g_kv_norm, g_sinks, g_bias, g_ffn2_norm, g_final, loss = out_small
    loss = loss.reshape(())
    g_final = g_final.reshape(-1)

    gb = dict(zip(names, grads_big))
    summed_grads = [g_ffn1_norm, gb["wg1"], gb["wu1"], gb["wd1"], g_mix_norm, gb["win"], g_q_norm, gb["wqb"], g_kv_norm,
                    gb["wkvb"], g_sinks, g_bias, gb["wout"], g_ffn2_norm, gb["wg2"], gb["wu2"], gb["wd2"], g_final]
    weights = [ffn1_norm, ffn1_w_gate, ffn1_w_up, ffn1_w_down, mix_norm, w_in, mla_q_norm, mla_w_q_b, mla_kv_norm, mla_w_kv_b,
               swa_sinks, fox_forget_bias, w_out, ffn2_norm, ffn2_w_gate, ffn2_w_up, ffn2_w_down, final_norm]
    ms = [m_ffn1_norm, m_ffn1_w_gate, m_ffn1_w_up, m_ffn1_w_down, m_mix_norm, m_w_in, m_mla_q_norm, m_mla_w_q_b, m_mla_kv_norm,
          m_mla_w_kv_b, m_swa_sinks, m_fox_forget_bias, m_w_out, m_ffn2_norm, m_ffn2_w_gate, m_ffn2_w_up, m_ffn2_w_down,
          m_final_norm]
    vs = [v_ffn1_norm, v_ffn1_w_gate, v_ffn1_w_up, v_ffn1_w_down, v_mix_norm, v_w_in, v_mla_q_norm, v_mla_w_q_b, v_mla_kv_norm,
          v_mla_w_kv_b, v_swa_sinks, v_fox_forget_bias, v_w_out, v_ffn2_norm, v_ffn2_w_gate, v_ffn2_w_up, v_ffn2_w_down,
          v_final_norm]
    grads, deltas, new_m, new_v = [], [], [], []
    for w, g, m, v in zip(weights, summed_grads, ms, vs):
        three_d = w.shape if w.ndim == 3 else (1, -1, w.shape[-1])
        g_out, d, nm, nv = adamw(w.reshape(three_d), g.reshape(three_d), m.reshape(three_d), v.reshape(three_d))
        grads.append(g_out.reshape(w.shape))
        deltas.append(d.reshape(w.shape))
        new_m.append(nm.reshape(w.shape))
        new_v.append(nv.reshape(w.shape))
    return (loss, grad_x, *grads, *deltas, *new_m, *new_v)
```

```python
import jax
import jax.numpy as jnp
from jax import lax
from jax.experimental import pallas as pl
from jax.experimental.pallas import tpu as pltpu

F32, BF16 = jnp.float32, jnp.bfloat16
MESH = pl.DeviceIdType.MESH

RMS_EPS = 1e-6
ROPE_THETA = 10000.0
N_HEADS = 8
Q_LORA, KV_LORA = 512, 256
NOPE, ROPE, VDIM = 128, 64, 128
QK_MLA = NOPE + ROPE
SWA_KV, HD, WINDOW = 2, 64, 128
P_CQ, P_CKV, P_QS, P_KS, P_VS, P_QF, P_KF, P_VF, P_KR, P_F, P_COLS = (
    0, 512, 768, 1280, 1408, 1536, 2048, 2560, 3072, 3136, 3200)
IN_COLS = 3144
IN_KR = 768
ADAM_LR, ADAM_B1, ADAM_B2, ADAM_EPS, ADAM_WD, ADAM_STEP = 0.001, 0.9, 0.999, 1e-08, 0.01, 10
NEG = -1e30
LOG2E, LN2 = 1.4426950408889634, 0.6931471805599453
VMEM_LIMIT = 56 * 1024 * 1024
N_CHIPS = 4
N_DEV = 8
W_FFN1, W_REST, W_HEAD, W_TAIL = [0, 1, 2], [3, 4, 5, 6, 7, 8, 9], [0, 1, 2, 3, 4, 5, 6], [7, 8, 9]
SMALL_ROWS = 128


def _tile(n, pref):
    return pref if n % pref == 0 else n


def _cparams(*sem):
    return pltpu.CompilerParams(dimension_semantics=sem, vmem_limit_bytes=VMEM_LIMIT)


def _sigmoid(x):
    return 1.0 / (1.0 + jnp.exp(-x))


def _dot(a, b):
    return jnp.dot(a, b, preferred_element_type=F32)


def _dot_nt(a, b):
    return lax.dot_general(a, b, (((1,), (1,)), ((), ())), preferred_element_type=F32)


def _dot_tn(a, b):
    return lax.dot_general(a, b, (((0,), (0,)), ((), ())), preferred_element_type=F32)


def rms_fwd(x, gain):
    S, D = x.shape
    tm = _tile(S, 512)

    def body(x_ref, g_ref, h_ref):
        xv = x_ref[...]
        r = lax.rsqrt(jnp.mean(xv * xv, axis=-1, keepdims=True) + RMS_EPS)
        h_ref[...] = (xv * r * g_ref[...]).astype(BF16)

    return pl.pallas_call(
        body, name="rms_fwd", grid=(S // tm,),
        in_specs=[pl.BlockSpec((tm, D), lambda i: (i, 0)), pl.BlockSpec((1, D), lambda i: (0, 0))],
        out_specs=pl.BlockSpec((tm, D), lambda i: (i, 0)),
        out_shape=jax.ShapeDtypeStruct((S, D), BF16), compiler_params=_cparams("arbitrary"))(x, gain)


def rms_bwd(dh, x, gain, resid):
    S, D = x.shape
    tm = _tile(S, 512)

    def body(dh_ref, x_ref, g_ref, r_ref, dx_ref, dxb_ref, dg_ref):
        xv, dhv = x_ref[...], dh_ref[...]
        r = lax.rsqrt(jnp.mean(xv * xv, axis=-1, keepdims=True) + RMS_EPS)
        xhat = xv * r
        dhg = dhv * g_ref[...]
        dx = r_ref[...] + r * (dhg - xhat * jnp.mean(dhg * xhat, axis=-1, keepdims=True))
        dx_ref[...] = dx
        dxb_ref[...] = dx.astype(BF16)

        @pl.when(pl.program_id(0) == 0)
        def _():
            dg_ref[...] = jnp.zeros_like(dg_ref)

        dg_ref[...] += jnp.sum(dhv * xhat, axis=0, keepdims=True)

    row = pl.BlockSpec((tm, D), lambda i: (i, 0))
    vec = pl.BlockSpec((1, D), lambda i: (0, 0))
    return pl.pallas_call(
        body, name="rms_bwd", grid=(S // tm,), in_specs=[row, row, vec, row], out_specs=[row, row, vec],
        out_shape=[jax.ShapeDtypeStruct((S, D), F32), jax.ShapeDtypeStruct((S, D), BF16), jax.ShapeDtypeStruct((1, D), F32)],
        compiler_params=_cparams("arbitrary"))(dh, x, gain, resid)


def loss_head(x, gain, target):
    S, D = x.shape
    tm = _tile(S, 512)

    def body(x_ref, g_ref, t_ref, loss_ref, dx_ref, dxb_ref, dg_ref):
        xv, g = x_ref[...], g_ref[...]
        r = lax.rsqrt(jnp.mean(xv * xv, axis=-1, keepdims=True) + RMS_EPS)
        xhat = xv * r
        err = xhat * g - t_ref[...]
        dy = err * (1.0 / D)
        dyg = dy * g
        dx = r * (dyg - xhat * jnp.mean(dyg * xhat, axis=-1, keepdims=True))
        dx_ref[...] = dx
        dxb_ref[...] = dx.astype(BF16)

        @pl.when(pl.program_id(0) == 0)
        def _():
            dg_ref[...] = jnp.zeros_like(dg_ref)
            loss_ref[...] = jnp.zeros_like(loss_ref)

        dg_ref[...] += jnp.sum(dy * xhat, axis=0, keepdims=True)
        loss_ref[...] += 0.5 * jnp.sum(jnp.mean(err * err, axis=-1, keepdims=True), axis=0, keepdims=True)

    row = pl.BlockSpec((tm, D), lambda i: (i, 0))
    vec = pl.BlockSpec((1, D), lambda i: (0, 0))
    return pl.pallas_call(
        body, name="loss_head", grid=(S // tm,), in_specs=[row, vec, row],
        out_specs=[pl.BlockSpec((1, 128), lambda i: (0, 0)), row, row, vec],
        out_shape=[jax.ShapeDtypeStruct((1, 128), F32), jax.ShapeDtypeStruct((S, D), F32),
                   jax.ShapeDtypeStruct((S, D), BF16), jax.ShapeDtypeStruct((1, D), F32)],
        compiler_params=_cparams("arbitrary"))(x, gain, target)


def _carry_plumbing(carry, n_in, n_out):
    if not carry:
        return [], [], [], [], [], {}
    arrays = list(carry[2])
    n = len(arrays)
    sems = [pltpu.SemaphoreType.DMA((n, 3)), pltpu.SemaphoreType.DMA((n, 3))]
    return ([_ANY] * n, arrays, [_ANY] * n, [jax.ShapeDtypeStruct(a.shape, a.dtype) for a in arrays], sems,
            {n_in + t: n_out + t for t in range(n)})


def gate_up(h, wg, wu, l, carry=None):
    S, D = h.shape
    Fs = wg.shape[3]
    tm = _tile(S, 512)
    nt = S // tm
    n_carry = len(carry[2]) if carry else 0

    def body(h_ref, wg_ref, wu_ref, *rest):
        carried_in, rest = rest[:n_carry], rest[n_carry:]
        g_ref, u_ref, a_ref = rest[:3]
        carried_out, sems = rest[3:3 + n_carry], rest[3 + n_carry:]
        j, i = pl.program_id(0), pl.program_id(1)
        if carry:
            @pl.when(jnp.logical_and(j == 0, i == 0))
            def _():
                _gather_phase(carry[0], carry[1], carried_in, carried_out, sems[0], sems[1], start=True)
        hv = h_ref[...]
        g = _dot(hv, wg_ref[...])
        u = _dot(hv, wu_ref[...])
        g_ref[...] = g.astype(BF16)
        u_ref[...] = u.astype(BF16)
        a_ref[...] = (g * _sigmoid(g) * u).astype(BF16)
        if carry:
            @pl.when(jnp.logical_and(j == N_CHIPS - 1, i == nt - 1))
            def _():
                _gather_phase(carry[0], carry[1], carried_in, carried_out, sems[0], sems[1], start=False)

    w_spec = pl.BlockSpec((None, None, D, Fs), lambda j, i: (l, j, 0, 0))
    o_spec = pl.BlockSpec((tm, Fs), lambda j, i: (i, j))
    o_shape = jax.ShapeDtypeStruct((S, N_CHIPS * Fs), BF16)
    c_in, c_args, c_out, c_shape, c_scratch, aliases = _carry_plumbing(carry, 3, 3)
    out = pl.pallas_call(
        body, name="gate_up", grid=(N_CHIPS, nt),
        in_specs=[pl.BlockSpec((tm, D), lambda j, i: (i, 0)), w_spec, w_spec] + c_in,
        out_specs=[o_spec, o_spec, o_spec] + c_out, out_shape=[o_shape, o_shape, o_shape] + c_shape,
        scratch_shapes=c_scratch, input_output_aliases=aliases,
        compiler_params=_cparams("arbitrary", "arbitrary"))(h, wg, wu, *c_args)
    return (out[0], out[1], out[2], list(out[3:])) if carry else tuple(out)


def _normed(x, gain):
    r = lax.rsqrt(jnp.mean(x * x, axis=-1, keepdims=True) + RMS_EPS)
    return (x * r * gain).astype(BF16)


def down_proj(a, wd, l, x, next_gain=None, carry=None):
    S, F = a.shape
    D = wd.shape[2]
    tm, tk = _tile(S, 512), F // N_CHIPS
    nk, nt = F // tk, S // tm
    emit = next_gain is not None
    n_carry = len(carry[2]) if carry else 0
    n_out = 2 if emit else 1

    def body(a_ref, w_ref, x_ref, *rest):
        rest = list(rest)
        g_ref = rest.pop(0) if emit else None
        carried_in = [rest.pop(0) for _ in range(n_carry)]
        o_ref = rest.pop(0)
        h_ref = rest.pop(0) if emit else None
        carried_out = [rest.pop(0) for _ in range(n_carry)]
        acc_ref = rest.pop(0)
        i, k = pl.program_id(0), pl.program_id(1)
        if carry:
            @pl.when(jnp.logical_and(i == 0, k == 0))
            def _():
                _gather_phase(carry[0], carry[1], carried_in, carried_out, rest[0], rest[1], start=True)

        @pl.when(k == 0)
        def _():
            acc_ref[...] = jnp.zeros_like(acc_ref)

        acc_ref[...] += _dot(a_ref[...], w_ref[...])

        @pl.when(k == nk - 1)
        def _():
            y = x_ref[...] + 0.5 * acc_ref[...]
            o_ref[...] = y
            if emit:
                h_ref[...] = _normed(y, g_ref[...])

        if carry:
            @pl.when(jnp.logical_and(i == nt - 1, k == nk - 1))
            def _():
                _gather_phase(carry[0], carry[1], carried_in, carried_out, rest[0], rest[1], start=False)

    row = pl.BlockSpec((tm, D), lambda i, k: (i, 0))
    in_specs = [pl.BlockSpec((tm, tk), lambda i, k: (i, k)), pl.BlockSpec((None, tk, D), lambda i, k: (l, k, 0)), row]
    args, out_specs, out_shape = [a, wd, x], [row], [jax.ShapeDtypeStruct((S, D), F32)]
    if emit:
        in_specs.append(pl.BlockSpec((1, D), lambda i, k: (0, 0)))
        args.append(next_gain)
        out_specs.append(row)
        out_shape.append(jax.ShapeDtypeStruct((S, D), BF16))
    c_in, c_args, c_out, c_shape, c_scratch, aliases = _carry_plumbing(carry, len(args), n_out)
    out = pl.pallas_call(
        body, name="down_proj", grid=(nt, nk), in_specs=in_specs + c_in, out_specs=out_specs + c_out,
        out_shape=out_shape + c_shape, scratch_shapes=[pltpu.VMEM((tm, D), F32)] + c_scratch,
        input_output_aliases=aliases, compiler_params=_cparams("arbitrary", "arbitrary"))(*args, *c_args)
    result = (out[0], out[1] if emit else None)
    return result + (list(out[n_out:]),) if carry else result


def down_bwd(dy, wd, l, g, u):
    S, D = dy.shape
    F = g.shape[1]
    Fs = F // N_CHIPS
    tm = _tile(S, 512)

    def body(dy_ref, w_ref, g_ref, u_ref, dg_ref, du_ref):
        da = 0.5 * _dot_nt(dy_ref[...].astype(BF16), w_ref[...])
        gv, uv = g_ref[...].astype(F32), u_ref[...].astype(F32)
        sig = _sigmoid(gv)
        du_ref[...] = (da * (gv * sig)).astype(BF16)
        dg_ref[...] = (da * uv * (sig * (1.0 + gv * (1.0 - sig)))).astype(BF16)

    t_spec = pl.BlockSpec((tm, Fs), lambda j, i: (i, j))
    o_shape = jax.ShapeDtypeStruct((S, F), BF16)
    return pl.pallas_call(
        body, name="down_bwd", grid=(N_CHIPS, S // tm),
        in_specs=[pl.BlockSpec((tm, D), lambda j, i: (i, 0)), pl.BlockSpec((None, Fs, D), lambda j, i: (l, j, 0)),
                  t_spec, t_spec],
        out_specs=[t_spec, t_spec], out_shape=[o_shape, o_shape],
        compiler_params=_cparams("arbitrary", "arbitrary"))(dy, wd, g, u)


def gate_up_bwd(dg, du, wg, wu, l):
    S, F = dg.shape
    D, Fs = wg.shape[2], wg.shape[3]
    tm = _tile(S, 512)

    def body(dg_ref, du_ref, wg_ref, wu_ref, o_ref):
        k = pl.program_id(1)

        @pl.when(k == 0)
        def _():
            o_ref[...] = jnp.zeros_like(o_ref)

        o_ref[...] += _dot_nt(dg_ref[...], wg_ref[...]) + _dot_nt(du_ref[...], wu_ref[...])

    t_spec = pl.BlockSpec((tm, Fs), lambda i, k: (i, k))
    w_spec = pl.BlockSpec((None, None, D, Fs), lambda i, k: (l, k, 0, 0))
    return pl.pallas_call(
        body, name="gate_up_bwd", grid=(S // tm, N_CHIPS), in_specs=[t_spec, t_spec, w_spec, w_spec],
        out_specs=pl.BlockSpec((tm, D), lambda i, k: (i, 0)), out_shape=jax.ShapeDtypeStruct((S, D), F32),
        compiler_params=_cparams("arbitrary", "arbitrary"))(dg, du, wg, wu)


def mm_nn(a, b, l, tn, out_dtype, resid=None, next_gain=None):
    S, K = a.shape
    N = b.shape[2]
    tm = _tile(S, 512)
    emit = next_gain is not None
    assert not emit or tn == N

    def body(a_ref, b_ref, *rest):
        rest = list(rest)
        acc = _dot(a_ref[...].astype(BF16), b_ref[...])
        if resid is not None:
            acc = rest.pop(0)[...] + acc
        g_ref = rest.pop(0) if emit else None
        rest[0][...] = acc.astype(out_dtype)
        if emit:
            rest[1][...] = _normed(acc, g_ref[...])

    o_spec = pl.BlockSpec((tm, tn), lambda n, i: (i, n))
    in_specs = [pl.BlockSpec((tm, K), lambda n, i: (i, 0)), pl.BlockSpec((None, K, tn), lambda n, i: (l, 0, n))]
    args, out_specs, out_shape = [a, b], [o_spec], [jax.ShapeDtypeStruct((S, N), out_dtype)]
    if resid is not None:
        in_specs.append(o_spec)
        args.append(resid)
    if emit:
        in_specs.append(pl.BlockSpec((1, N), lambda n, i: (0, 0)))
        args.append(next_gain)
        out_specs.append(o_spec)
        out_shape.append(jax.ShapeDtypeStruct((S, N), BF16))
    out = pl.pallas_call(
        body, name="mm_nn", grid=(N // tn, S // tm), in_specs=in_specs, out_specs=out_specs, out_shape=out_shape,
        compiler_params=_cparams("arbitrary", "arbitrary"))(*args)
    return out if emit else out[0]


def mm_nt(a, b, l, tn, out_dtype):
    S, K = a.shape
    N = b.shape[1]
    tm = _tile(S, 512)

    def body(a_ref, b_ref, o_ref):
        o_ref[...] = _dot_nt(a_ref[...].astype(BF16), b_ref[...]).astype(out_dtype)

    return pl.pallas_call(
        body, name="mm_nt", grid=(N // tn, S // tm),
        in_specs=[pl.BlockSpec((tm, K), lambda n, i: (i, 0)), pl.BlockSpec((None, tn, K), lambda n, i: (l, n, 0))],
        out_specs=pl.BlockSpec((tm, tn), lambda n, i: (i, n)), out_shape=jax.ShapeDtypeStruct((S, N), out_dtype),
        compiler_params=_cparams("arbitrary", "arbitrary"))(a, b)


def mm_tn(a, b, tka, tnb, scale=1.0, blocked=False, layer=None, layers=None, into=None):
    S, Ka = a.shape
    Nb = b.shape[1]
    ts = _tile(S, 2048)
    ns = S // ts

    def body(a_ref, b_ref, *rest):
        o_ref, acc_ref = rest[-2:]
        s = pl.program_id(2)

        @pl.when(s == 0)
        def _():
            acc_ref[...] = jnp.zeros_like(acc_ref)

        acc_ref[...] += _dot_tn(a_ref[...].astype(BF16), b_ref[...].astype(BF16))

        @pl.when(s == ns - 1)
        def _():
            o_ref[...] = (scale * acc_ref[...]).astype(BF16)

    if blocked:
        block, shape = (None, tka, tnb), (Nb // tnb, Ka, tnb)
        index = lambda ka, nb, s: (nb, ka, 0)
    else:
        block, shape = (tka, tnb), (Ka, Nb)
        index = lambda ka, nb, s: (ka, nb)
    if layer is not None:
        block, shape = (None,) + block, (layers,) + shape
        inner = index
        index = lambda ka, nb, s: (layer,) + inner(ka, nb, s)
    in_specs = [pl.BlockSpec((ts, tka), lambda ka, nb, s: (s, ka)), pl.BlockSpec((ts, tnb), lambda ka, nb, s: (s, nb))]
    args, aliases = [a, b], {}
    if into is not None:
        in_specs.append(pl.BlockSpec(memory_space=pl.ANY))
        args.append(into)
        aliases = {2: 0}
    return pl.pallas_call(
        body, name="mm_tn", grid=(Ka // tka, Nb // tnb, ns), in_specs=in_specs,
        out_specs=pl.BlockSpec(block, index), out_shape=jax.ShapeDtypeStruct(shape, BF16),
        input_output_aliases=aliases, scratch_shapes=[pltpu.VMEM((tka, tnb), F32)],
        compiler_params=_cparams("arbitrary", "arbitrary", "arbitrary"))(*args)


def _rope(x, c2, s2):
    half = x.shape[-1] // 2
    rot = jnp.concatenate([x[:, half:], x[:, :half]], axis=-1)
    return x * c2 + rot * s2


def _tri(tm, upper):
    r = lax.broadcasted_iota(jnp.int32, (tm, tm), 0)
    c = lax.broadcasted_iota(jnp.int32, (tm, tm), 1)
    return jnp.where((c >= r) if upper else (c <= r), 1.0, 0.0).astype(F32)


def _log_sigmoid(x):
    return jnp.minimum(x, 0.0) - jnp.log(1.0 + jnp.exp(-jnp.abs(x)))


def _norm_hat(c):
    r = lax.rsqrt(jnp.mean(c * c, axis=-1, keepdims=True) + RMS_EPS)
    return c * r, r


def _tok_spec(tm, width, rev_n=None):
    if rev_n is None:
        return pl.BlockSpec((tm, width), lambda i: (i, 0))
    return pl.BlockSpec((tm, width), lambda i: (rev_n - 1 - i, 0))


def _head_spec(heads, tm, width, rev_n=None):
    if rev_n is None:
        return pl.BlockSpec((heads, tm, width), lambda i: (0, i, 0))
    return pl.BlockSpec((heads, tm, width), lambda i: (0, rev_n - 1 - i, 0))


def _lane_spec(heads, width, tm, rev_n=None):
    if rev_n is None:
        return pl.BlockSpec((heads, width, tm), lambda i: (0, 0, i))
    return pl.BlockSpec((heads, width, tm), lambda i: (0, 0, rev_n - 1 - i))


def _full_spec(shape):
    return pl.BlockSpec(shape, lambda i: (0,) * len(shape))


def mixer_prep(p, c2, s2, q_norm, kv_norm, bias, wqb, wkvb, l):
    S = p.shape[0]
    tm = _tile(S, 256)
    H = N_HEADS

    def body(p_ref, c2_ref, s2_ref, qn_ref, kvn_ref, b_ref, wqb_ref, wkvb_ref,
             cqn_ref, ckvn_ref, qmt_ref, km_ref, kmt_ref, vm_ref, vmt_ref, qst_ref, ks_ref, kst_ref, vs_ref, vst_ref,
             qft_ref, kf_ref, kft_ref, vf_ref, vft_ref, ccol_ref, carry_row):
        c2, s2 = c2_ref[...], s2_ref[...]
        cqn = (_norm_hat(p_ref[:, P_CQ:P_CQ + Q_LORA])[0] * qn_ref[...]).astype(BF16)
        ckvn = (_norm_hat(p_ref[:, P_CKV:P_CKV + KV_LORA])[0] * kvn_ref[...]).astype(BF16)
        cqn_ref[...] = cqn
        ckvn_ref[...] = ckvn
        q = _dot(cqn, wqb_ref[...])
        kv = _dot(ckvn, wkvb_ref[...])
        k_pe = _rope(p_ref[:, P_KR:P_KR + ROPE], c2, s2)
        k_pe_t = k_pe.T.astype(BF16)
        k_pe = k_pe.astype(BF16)

        def both_ways(x, tok_ref, lane_ref, h):
            tok_ref[h] = x.astype(BF16)
            lane_ref[h] = x.T.astype(BF16)

        qs_mla, qs_hd = QK_MLA ** -0.5 * LOG2E, HD ** -0.5 * LOG2E
        for h in range(H):
            qmt_ref[h, 0:NOPE, :] = (q[:, h * QK_MLA:h * QK_MLA + NOPE] * qs_mla).T.astype(BF16)
            qmt_ref[h, NOPE:QK_MLA, :] = (_rope(q[:, h * QK_MLA + NOPE:(h + 1) * QK_MLA], c2, s2) * qs_mla).T.astype(BF16)
            k_nope = kv[:, h * 256:h * 256 + NOPE]
            km_ref[h, :, 0:NOPE] = k_nope.astype(BF16)
            km_ref[h, :, NOPE:QK_MLA] = k_pe
            kmt_ref[h, 0:NOPE, :] = k_nope.T.astype(BF16)
            kmt_ref[h, NOPE:QK_MLA, :] = k_pe_t
            both_ways(kv[:, h * 256 + NOPE:(h + 1) * 256], vm_ref, vmt_ref, h)
            qst_ref[h] = (_rope(p_ref[:, P_QS + h * HD:P_QS + (h + 1) * HD], c2, s2) * qs_hd).T.astype(BF16)
            qft_ref[h] = (p_ref[:, P_QF + h * HD:P_QF + (h + 1) * HD] * qs_hd).T.astype(BF16)
            both_ways(p_ref[:, P_KF + h * HD:P_KF + (h + 1) * HD], kf_ref, kft_ref, h)
            both_ways(p_ref[:, P_VF + h * HD:P_VF + (h + 1) * HD], vf_ref, vft_ref, h)
        for h in range(SWA_KV):
            both_ways(_rope(p_ref[:, P_KS + h * HD:P_KS + (h + 1) * HD], c2, s2), ks_ref, kst_ref, h)
            both_ways(p_ref[:, P_VS + h * HD:P_VS + (h + 1) * HD], vs_ref, vst_ref, h)

        @pl.when(pl.program_id(0) == 0)
        def _():
            carry_row[...] = jnp.zeros_like(carry_row)

        log_f = _log_sigmoid(p_ref[:, P_F:P_F + H] + b_ref[...])
        c_tok = jnp.dot(_tri(tm, upper=False), log_f, preferred_element_type=F32,
                        precision=lax.Precision.HIGHEST) + carry_row[0:1, 0:H]
        for h in range(H):
            ccol_ref[h] = c_tok[:, h:h + 1] * LOG2E
        carry_row[0:1, 0:H] = c_tok[tm - 1:tm, :]

    out_shape = [jax.ShapeDtypeStruct((S, Q_LORA), BF16), jax.ShapeDtypeStruct((S, KV_LORA), BF16)]
    out_specs = [_tok_spec(tm, Q_LORA), _tok_spec(tm, KV_LORA)]

    def add(heads, d, lanes):
        out_shape.append(jax.ShapeDtypeStruct((heads, d, S) if lanes else (heads, S, d), BF16))
        out_specs.append(_lane_spec(heads, d, tm) if lanes else _head_spec(heads, tm, d))

    for heads_q, heads_kv, dqk, dv in ((H, H, QK_MLA, VDIM), (H, SWA_KV, HD, HD), (H, H, HD, HD)):
        add(heads_q, dqk, True)
        add(heads_kv, dqk, False)
        add(heads_kv, dqk, True)
        add(heads_kv, dv, False)
        add(heads_kv, dv, True)
    out_shape.append(jax.ShapeDtypeStruct((H, S, 1), F32))
    out_specs.append(_head_spec(H, tm, 1))
    in_specs = [_tok_spec(tm, P_COLS), _tok_spec(tm, ROPE), _tok_spec(tm, ROPE), _full_spec((1, Q_LORA)),
                _full_spec((1, KV_LORA)), _full_spec((1, H)),
                pl.BlockSpec((None,) + wqb.shape[1:], lambda i: (l, 0, 0)),
                pl.BlockSpec((None,) + wkvb.shape[1:], lambda i: (l, 0, 0))]
    return pl.pallas_call(
        body, name="mixer_prep", grid=(S // tm,), in_specs=in_specs, out_specs=out_specs, out_shape=out_shape,
        scratch_shapes=[pltpu.VMEM((8, 128), F32)],
        compiler_params=_cparams("arbitrary"))(p, c2, s2, q_norm, kv_norm, bias, wqb, wkvb)


def mixer_prep_bwd(p, c2, s2, q_norm, kv_norm, bias, wqb, wkvb, l, dqm, dkm, dvm, dqs, dks, dvs, dqf, dkf, dvf, dc):
    S = p.shape[0]
    tm = _tile(S, 256)
    nt = S // tm
    H, G = N_HEADS, N_HEADS // SWA_KV

    def body(p_ref, c2_ref, s2_ref, qn_ref, kvn_ref, b_ref, wqb_ref, wkvb_ref,
             dqm_ref, dkm_ref, dvm_ref, dqs_ref, dks_ref, dvs_ref, dqf_ref, dkf_ref, dvf_ref, dc_ref,
             dp_ref, dq_ref, dkv_ref, dqn_ref, dkvn_ref, db_ref, carry):
        c2, s2 = c2_ref[...], -s2_ref[...]

        @pl.when(pl.program_id(0) == 0)
        def _():
            dqn_ref[...] = jnp.zeros_like(dqn_ref)
            dkvn_ref[...] = jnp.zeros_like(dkvn_ref)
            db_ref[...] = jnp.zeros_like(db_ref)
            carry[...] = jnp.zeros_like(carry)

        sc_mla, sc_hd = QK_MLA ** -0.5, HD ** -0.5
        dk_pe_t = jnp.zeros((ROPE, tm), F32)
        for h in range(H):
            dq_ref[:, h * QK_MLA:h * QK_MLA + NOPE] = (dqm_ref[h, 0:NOPE, :] * sc_mla).T.astype(BF16)
            dq_ref[:, h * QK_MLA + NOPE:(h + 1) * QK_MLA] = _rope(
                (dqm_ref[h, NOPE:QK_MLA, :] * sc_mla).T, c2, s2).astype(BF16)
            dkv_ref[:, h * 256:h * 256 + NOPE] = (dkm_ref[h, 0:NOPE, :] * LN2).T.astype(BF16)
            dkv_ref[:, h * 256 + NOPE:(h + 1) * 256] = dvm_ref[h].T.astype(BF16)
            dk_pe_t = dk_pe_t + dkm_ref[h, NOPE:QK_MLA, :]
        dk_pe = (dk_pe_t * LN2).T

        def through_norm(dcn, c, gain, dgain_ref):
            c_hat, r = _norm_hat(c)
            dhg = dcn * gain
            dgain_ref[...] += jnp.sum(dcn * c_hat, axis=0, keepdims=True)
            return r * (dhg - c_hat * jnp.mean(dhg * c_hat, axis=-1, keepdims=True))

        dcqn = _dot_nt(dq_ref[...], wqb_ref[...])
        dckvn = _dot_nt(dkv_ref[...], wkvb_ref[...])
        dp_ref[:, P_CQ:P_CQ + Q_LORA] = through_norm(dcqn, p_ref[:, P_CQ:P_CQ + Q_LORA], qn_ref[...], dqn_ref).astype(BF16)
        dp_ref[:, P_CKV:P_CKV + KV_LORA] = through_norm(
            dckvn, p_ref[:, P_CKV:P_CKV + KV_LORA], kvn_ref[...], dkvn_ref).astype(BF16)
        for h in range(H):
            dp_ref[:, P_QS + h * HD:P_QS + (h + 1) * HD] = _rope((dqs_ref[h] * sc_hd).T, c2, s2).astype(BF16)
            dp_ref[:, P_QF + h * HD:P_QF + (h + 1) * HD] = (dqf_ref[h] * sc_hd).T.astype(BF16)
            dp_ref[:, P_KF + h * HD:P_KF + (h + 1) * HD] = (dkf_ref[h] * LN2).T.astype(BF16)
            dp_ref[:, P_VF + h * HD:P_VF + (h + 1) * HD] = dvf_ref[h].T.astype(BF16)
        for kvh in range(SWA_KV):
            dk = dks_ref[kvh * G]
            dv = dvs_ref[kvh * G]
            for g in range(1, G):
                dk = dk + dks_ref[kvh * G + g]
                dv = dv + dvs_ref[kvh * G + g]
            dp_ref[:, P_KS + kvh * HD:P_KS + (kvh + 1) * HD] = _rope((dk * LN2).T, c2, s2).astype(BF16)
            dp_ref[:, P_VS + kvh * HD:P_VS + (kvh + 1) * HD] = dv.T.astype(BF16)

        dcv = dc_ref[...]
        dlog_f = jnp.dot(_tri(tm, upper=True), dcv, preferred_element_type=F32,
                         precision=lax.Precision.HIGHEST) + carry[0:1, 0:H]
        carry[0:1, 0:H] = dlog_f[0:1, :]
        df = dlog_f * _sigmoid(-(p_ref[:, P_F:P_F + H] + b_ref[...]))
        db_ref[...] += jnp.sum(df, axis=0, keepdims=True)
        dp_ref[:, P_KR:P_COLS] = jnp.zeros((tm, P_COLS - P_KR), BF16)
        dp_ref[:, P_KR:P_KR + ROPE] = _rope(dk_pe, c2, s2).astype(BF16)
        dp_ref[:, P_F:P_F + H] = df.astype(BF16)

    rev = nt
    in_specs = [_tok_spec(tm, P_COLS, rev), _tok_spec(tm, ROPE, rev), _tok_spec(tm, ROPE, rev), _full_spec((1, Q_LORA)),
                _full_spec((1, KV_LORA)), _full_spec((1, H)),
                pl.BlockSpec((None,) + wqb.shape[1:], lambda i: (l, 0, 0)),
                pl.BlockSpec((None,) + wkvb.shape[1:], lambda i: (l, 0, 0)),
                _lane_spec(H, QK_MLA, tm, rev), _lane_spec(H, QK_MLA, tm, rev), _lane_spec(H, VDIM, tm, rev)]
    in_specs += [_lane_spec(H, HD, tm, rev)] * 6 + [_tok_spec(tm, H, rev)]
    out_specs = [_tok_spec(tm, P_COLS, rev), _tok_spec(tm, N_HEADS * QK_MLA, rev), _tok_spec(tm, N_HEADS * 256, rev),
                 _full_spec((1, Q_LORA)), _full_spec((1, KV_LORA)), _full_spec((1, H))]
    out_shape = [jax.ShapeDtypeStruct((S, P_COLS), BF16), jax.ShapeDtypeStruct((S, N_HEADS * QK_MLA), BF16),
                 jax.ShapeDtypeStruct((S, N_HEADS * 256), BF16), jax.ShapeDtypeStruct((1, Q_LORA), F32),
                 jax.ShapeDtypeStruct((1, KV_LORA), F32), jax.ShapeDtypeStruct((1, H), F32)]
    return pl.pallas_call(
        body, name="mixer_prep_bwd", grid=(nt,), in_specs=in_specs, out_specs=out_specs, out_shape=out_shape,
        scratch_shapes=[pltpu.VMEM((8, 128), F32)], compiler_params=_cparams("arbitrary"))(
            p, c2, s2, q_norm, kv_norm, bias, wqb, wkvb, dqm, dkm, dvm, dqs, dks, dvs, dqf, dkf, dvf, dc)


def merge_heads(o_mla, o_swa, o_fox):
    H, S, _ = o_mla.shape
    tm = _tile(S, 512)
    width = H * (VDIM + 2 * HD)

    def body(om_ref, os_ref, of_ref, m_ref):
        for h in range(H):
            m_ref[:, h * VDIM:(h + 1) * VDIM] = om_ref[h]
            m_ref[:, H * VDIM + h * HD:H * VDIM + (h + 1) * HD] = os_ref[h]
            m_ref[:, H * (VDIM + HD) + h * HD:H * (VDIM + HD) + (h + 1) * HD] = of_ref[h]

    return pl.pallas_call(
        body, name="merge_heads", grid=(S // tm,),
        in_specs=[_head_spec(H, tm, VDIM), _head_spec(H, tm, HD), _head_spec(H, tm, HD)],
        out_specs=_tok_spec(tm, width), out_shape=jax.ShapeDtypeStruct((S, width), BF16),
        compiler_params=_cparams("arbitrary"))(o_mla, o_swa, o_fox)


def split_heads(dmixed, mixed):
    S, width = mixed.shape
    H = N_HEADS
    tm = _tile(S, 512)

    def body(dm_ref, m_ref, dom_ref, dos_ref, dof_ref, dm_delta, ds_delta, df_delta):
        def one(h, off, d, do_ref, delta_ref):
            dv = dm_ref[:, off:off + d].astype(F32)
            do_ref[h] = dv.T.astype(BF16)
            prod = dv * m_ref[:, off:off + d].astype(F32)
            rows = lax.dot_general(jnp.ones((8, d), F32), prod, (((1,), (1,)), ((), ())),
                                   preferred_element_type=F32, precision=lax.Precision.HIGHEST)
            delta_ref[h] = rows[0:1, :]

        for h in range(H):
            one(h, h * VDIM, VDIM, dom_ref, dm_delta)
            one(h, H * VDIM + h * HD, HD, dos_ref, ds_delta)
            one(h, H * (VDIM + HD) + h * HD, HD, dof_ref, df_delta)

    row_spec = pl.BlockSpec((H, 1, tm), lambda i: (0, 0, i))
    row_shape = jax.ShapeDtypeStruct((H, 1, S), F32)
    return pl.pallas_call(
        body, name="split_heads", grid=(S // tm,),
        in_specs=[_tok_spec(tm, width), _tok_spec(tm, width)],
        out_specs=[_lane_spec(H, VDIM, tm), _lane_spec(H, HD, tm), _lane_spec(H, HD, tm), row_spec, row_spec, row_spec],
        out_shape=[jax.ShapeDtypeStruct((H, VDIM, S), BF16), jax.ShapeDtypeStruct((H, HD, S), BF16),
                   jax.ShapeDtypeStruct((H, HD, S), BF16), row_shape, row_shape, row_shape],
        compiler_params=_cparams("arbitrary"))(dmixed, mixed)


def _attn_tile(S):
    return 512 if (S % 512 == 0 and S > 512) else S // 2


def _valid(q0, k0, shape, q_axis, window):
    qpos = q0 + lax.broadcasted_iota(jnp.int32, shape, q_axis)
    kpos = k0 + lax.broadcasted_iota(jnp.int32, shape, 1 - q_axis)
    ok = kpos <= qpos
    if window is not None:
        ok = jnp.logical_and(ok, kpos > qpos - window)
    return ok


def attn_fwd(name, qt, k, vt, window=None, sinks=None, ccol=None, carry=None):
    H, dq, S = qt.shape
    Hk, dv, _ = vt.shape
    G = H // Hk
    t = _attn_tile(S)
    nq = S // t
    fox, use_sink = ccol is not None, sinks is not None
    n_carry = len(carry[2]) if carry else 0

    def body(*refs):
        refs = list(refs)
        sink_ref = refs.pop(0) if use_sink else None
        q_ref, k_ref, vt_ref = refs[:3]
        refs = refs[3:]
        ccol_ref = refs.pop(0) if fox else None
        carried_in = [refs.pop(0) for _ in range(n_carry)]
        o_ref, lse_ref = refs[:2]
        refs = refs[2:]
        carried_out = [refs.pop(0) for _ in range(n_carry)]
        m_ref, l_ref, acc_ref = refs[:3]
        h, i = pl.program_id(0), pl.program_id(1)
        if carry:
            @pl.when(jnp.logical_and(h == 0, i == 0))
            def _():
                _gather_phase(carry[0], carry[1], carried_in, carried_out, refs[3], refs[4], start=True)
        qv = q_ref[...]
        if use_sink:
            m_ref[...] = jnp.full(m_ref.shape, sink_ref[h] * LOG2E, F32)
            l_ref[...] = jnp.ones(l_ref.shape, F32)
        else:
            m_ref[...] = jnp.full(m_ref.shape, NEG, F32)
            l_ref[...] = jnp.zeros(l_ref.shape, F32)
        acc_ref[...] = jnp.zeros(acc_ref.shape, F32)

        def steps(blocks):
            offs = [pl.multiple_of(j * t, t) for j, _ in blocks]
            scores = [_dot(k_ref[pl.ds(off, t), :], qv) for off in offs]
            for (j, masked), off, st in zip(blocks, offs, scores):
                if fox:
                    st = st - ccol_ref[pl.ds(off, t), :]
                if masked:
                    st = jnp.where(_valid(i * t, j * t, (t, t), 1, window), st, NEG)
                m_prev = m_ref[...]
                m_new = jnp.maximum(m_prev, jnp.max(st, axis=0, keepdims=True))
                alpha = jnp.exp2(m_prev - m_new)
                pt = jnp.exp2(st - m_new)
                l_ref[...] = alpha * l_ref[...] + jnp.sum(pt, axis=0, keepdims=True)
                acc_ref[...] = alpha * acc_ref[...] + _dot(vt_ref[:, pl.ds(off, t)], pt.astype(BF16))
                m_ref[...] = m_new

        if window is None:
            def pair(n, carry):
                steps([(2 * n, False), (2 * n + 1, False)])
                return carry
            lax.fori_loop(0, i // 2, pair, 0)

            @pl.when(i % 2 == 1)
            def _():
                steps([(i - 1, False), (i, True)])

            @pl.when(i % 2 == 0)
            def _():
                steps([(i, True)])
        else:
            def one(j, carry):
                steps([(j, True)])
                return carry
            lax.fori_loop(jnp.maximum(i * t - (window - 1), 0) // t, i + 1, one, 0)
        l = l_ref[...]
        o_ref[...] = (acc_ref[...] / l).T.astype(BF16)
        lse_ref[...] = m_ref[...] + jnp.log2(l)
        if carry:
            @pl.when(jnp.logical_and(h == H - 1, i == nq - 1))
            def _():
                _gather_phase(carry[0], carry[1], carried_in, carried_out, refs[3], refs[4], start=False)

    in_specs, args = [], []
    if use_sink:
        in_specs.append(pl.BlockSpec(memory_space=pltpu.SMEM))
        args.append(sinks)
    in_specs += [pl.BlockSpec((None, dq, t), lambda h, i: (h, 0, i)),
                 pl.BlockSpec((None, S, dq), lambda h, i: (h // G, 0, 0)),
                 pl.BlockSpec((None, dv, S), lambda h, i: (h // G, 0, 0))]
    args += [qt, k, vt]
    if fox:
        in_specs.append(pl.BlockSpec((None, S, 1), lambda h, i: (h, 0, 0)))
        args.append(ccol)
    out_specs = [pl.BlockSpec((None, t, dv), lambda h, i: (h, i, 0)), pl.BlockSpec((None, 1, t), lambda h, i: (h, 0, i))]
    out_shape = [jax.ShapeDtypeStruct((H, S, dv), BF16), jax.ShapeDtypeStruct((H, 1, S), F32)]
    scratch = [pltpu.VMEM((1, t), F32), pltpu.VMEM((1, t), F32), pltpu.VMEM((dv, t), F32)]
    aliases = {}
    if carry:
        aliases = {len(args) + n: 2 + n for n in range(n_carry)}
        in_specs += [_ANY] * n_carry
        args += list(carry[2])
        out_specs += [_ANY] * n_carry
        out_shape += [jax.ShapeDtypeStruct(a.shape, a.dtype) for a in carry[2]]
        scratch += [pltpu.SemaphoreType.DMA((n_carry, 3)), pltpu.SemaphoreType.DMA((n_carry, 3))]
    out = pl.pallas_call(
        body, name=name, grid=(H, nq), in_specs=in_specs, out_specs=out_specs, out_shape=out_shape,
        scratch_shapes=scratch, input_output_aliases=aliases,
        compiler_params=_cparams("arbitrary", "arbitrary"))(*args)
    return (out[0], out[1], list(out[2:])) if carry else (out[0], out[1])


def attn_bwd(name, qt, k, kt, v, dot, lse, delta, window=None, sinks=None, ccol=None):
    H, dq, S = qt.shape
    Hk, _, dv = v.shape
    G = H // Hk
    t = _attn_tile(S)
    nq = S // t
    fox, use_sink = ccol is not None, sinks is not None

    def body(*refs):
        refs = list(refs)
        sink_ref = refs.pop(0) if use_sink else None
        qt_ref, k_ref, kt_ref, v_ref, dot_ref, lse_ref, delta_ref = refs[:7]
        refs = refs[7:]
        ccol_ref = refs.pop(0) if fox else None
        dqt_ref, dkt_ref, dvt_ref = refs[:3]
        refs = refs[3:]
        dc_ref = refs.pop(0) if fox else None
        dsink_ref = refs.pop(0) if use_sink else None
        dck_ref = refs.pop(0) if fox else None
        h, j = pl.program_id(0), pl.program_id(1)

        @pl.when(j == 0)
        def _():
            dqt_ref[...] = jnp.zeros(dqt_ref.shape, F32)
            if fox:
                dc_ref[...] = jnp.zeros(dc_ref.shape, F32)
            if use_sink:
                ps = jnp.exp2(sink_ref[h] * LOG2E - lse_ref[...]) * delta_ref[...]
                dsink_ref[...] = jnp.broadcast_to(-jnp.sum(ps, axis=-1, keepdims=True), dsink_ref.shape)

        dkt_ref[...] = jnp.zeros(dkt_ref.shape, F32)
        dvt_ref[...] = jnp.zeros(dvt_ref.shape, F32)
        if fox:
            dck_ref[...] = jnp.zeros(dck_ref.shape, F32)
        kv, ktv, vv = k_ref[...], kt_ref[...], v_ref[...]

        def steps(blocks):
            offs = [pl.multiple_of(i * t, t) for i, _ in blocks]
            qts = [qt_ref[:, pl.ds(off, t)] for off in offs]
            dots = [dot_ref[:, pl.ds(off, t)] for off in offs]
            scores = [_dot(kv, qti) for qti in qts]
            dprobs = [_dot(vv, doti) for doti in dots]
            for (i, masked), off, qti, doti, st, dpt in zip(blocks, offs, qts, dots, scores, dprobs):
                if fox:
                    st = st - ccol_ref[...]
                if masked:
                    st = jnp.where(_valid(i * t, j * t, (t, t), 1, window), st, NEG)
                pt = jnp.exp2(st - lse_ref[:, pl.ds(off, t)])
                dvt_ref[...] += _dot_nt(doti, pt.astype(BF16))
                dst = pt * (dpt - delta_ref[:, pl.ds(off, t)])
                if fox:
                    dc_ref[:, pl.ds(off, t)] += jnp.sum(dst, axis=0, keepdims=True)
                    dck_ref[...] += jnp.sum(dst, axis=1, keepdims=True)
                dsb = dst.astype(BF16)
                dkt_ref[...] += _dot_nt(qti, dsb)
                dqt_ref[:, pl.ds(off, t)] += _dot(ktv, dsb)

        if window is None:
            odd = (nq - 1 - j) % 2

            @pl.when(odd == 1)
            def _():
                steps([(j, True), (j + 1, False)])

            @pl.when(odd == 0)
            def _():
                steps([(j, True)])

            first = j + 1 + odd

            def pair(n, carry):
                steps([(first + 2 * n, False), (first + 2 * n + 1, False)])
                return carry
            lax.fori_loop(0, (nq - first) // 2, pair, 0)
        else:
            def one(i, carry):
                steps([(i, True)])
                return carry
            lax.fori_loop(j, jnp.minimum((j * t + t - 1 + window - 1) // t, nq - 1) + 1, one, 0)
        if fox:
            key_row = jnp.broadcast_to(dck_ref[...], (t, 128)).T[0:1, :]
            dc_ref[:, pl.ds(pl.multiple_of(j * t, t), t)] -= key_row

    in_specs, args = [], []
    if use_sink:
        in_specs.append(pl.BlockSpec(memory_space=pltpu.SMEM))
        args.append(sinks)
    whole = lambda d: pl.BlockSpec((None, d, S), lambda h, j: (h, 0, 0))
    keys = lambda d: pl.BlockSpec((None, d, t), lambda h, j: (h, 0, j))
    row = pl.BlockSpec((None, 1, S), lambda h, j: (h, 0, 0))
    in_specs += [whole(dq), pl.BlockSpec((None, t, dq), lambda h, j: (h // G, j, 0)),
                 pl.BlockSpec((None, dq, t), lambda h, j: (h // G, 0, j)),
                 pl.BlockSpec((None, t, dv), lambda h, j: (h // G, j, 0)), whole(dv), row, row]
    args += [qt, k, kt, v, dot, lse, delta]
    out_specs = [whole(dq), keys(dq), keys(dv)]
    out_shape = [jax.ShapeDtypeStruct((H, dq, S), F32), jax.ShapeDtypeStruct((H, dq, S), F32),
                 jax.ShapeDtypeStruct((H, dv, S), F32)]
    if fox:
        in_specs.append(pl.BlockSpec((None, t, 1), lambda h, j: (h, j, 0)))
        args.append(ccol)
        out_specs.append(row)
        out_shape.append(jax.ShapeDtypeStruct((H, 1, S), F32))
    if use_sink:
        out_specs.append(pl.BlockSpec((None, 1, 128), lambda h, j: (h, 0, 0)))
        out_shape.append(jax.ShapeDtypeStruct((H, 1, 128), F32))
    return pl.pallas_call(
        body, name=name, grid=(H, nq), in_specs=in_specs, out_specs=out_specs, out_shape=out_shape,
        scratch_shapes=[pltpu.VMEM((t, 1), F32)] if fox else [],
        compiler_params=_cparams("arbitrary", "arbitrary"))(*args)


def _rows_tile(rows):
    for tr in (256, 128, 64, 32, 16, 8):
        if rows % tr == 0:
            return tr
    return rows


def adamw(w, g, m, v):
    L, R, C = w.shape
    tr = _rows_tile(R)

    def body(w_ref, g_ref, m_ref, v_ref, g_out, d_ref, nm_ref, nv_ref):
        gv = g_ref[...]
        mn = ADAM_B1 * m_ref[...] + (1.0 - ADAM_B1) * gv
        vn = ADAM_B2 * v_ref[...] + (1.0 - ADAM_B2) * (gv * gv)
        m_hat = mn / (1.0 - ADAM_B1 ** ADAM_STEP)
        v_hat = vn / (1.0 - ADAM_B2 ** ADAM_STEP)
        d_ref[...] = -ADAM_LR * (m_hat / (jnp.sqrt(v_hat) + ADAM_EPS) + ADAM_WD * w_ref[...])
        nm_ref[...] = mn
        nv_ref[...] = vn
        g_out[...] = gv

    spec = pl.BlockSpec((None, tr, C), lambda l, i: (l, i, 0))
    shape = jax.ShapeDtypeStruct((L, R, C), F32)
    return pl.pallas_call(
        body, name="adamw", grid=(L, R // tr), in_specs=[spec] * 4, out_specs=[spec] * 4, out_shape=[shape] * 4,
        compiler_params=_cparams("arbitrary", "arbitrary"))(w, g, m, v)


def place_own(w, chip):
    L, R, C = w.shape
    tr = _rows_tile(R)

    def body(c_ref, w_ref, o_ref):
        o_ref[...] = w_ref[...].astype(BF16)

    return pl.pallas_call(
        body, name="place_own",
        grid_spec=pltpu.PrefetchScalarGridSpec(
            num_scalar_prefetch=1, grid=(L, R // tr),
            in_specs=[pl.BlockSpec((None, tr, C), lambda l, i, c: (l, i, 0))],
            out_specs=pl.BlockSpec((None, None, tr, C), lambda l, i, c: (l, c[0], i, 0))),
        out_shape=jax.ShapeDtypeStruct((L, N_CHIPS, R, C), BF16),
        compiler_params=_cparams("arbitrary", "arbitrary"))(chip, w)


def sum_pair(grad, other, layer):
    _, R, C = grad.shape
    tr = _rows_tile(R)

    def body(l_ref, g_ref, o_ref, out_ref):
        out_ref[...] = (g_ref[...].astype(F32) + o_ref[...].astype(F32)).astype(BF16)

    return pl.pallas_call(
        body, name="sum_pair",
        grid_spec=pltpu.PrefetchScalarGridSpec(
            num_scalar_prefetch=1, grid=(R // tr,),
            in_specs=[pl.BlockSpec((None, tr, C), lambda i, l: (l[0], i, 0)), pl.BlockSpec((tr, C), lambda i, l: (i, 0))],
            out_specs=pl.BlockSpec((tr, C), lambda i, l: (i, 0))),
        out_shape=jax.ShapeDtypeStruct((R, C), BF16), compiler_params=_cparams("arbitrary"))(layer, grad, other)


def sum_chips(part, recv, chip, layer):
    _, R, C = part.shape
    tr = _rows_tile(R)

    def body(c_ref, l_ref, p_ref, r_ref, out_ref):
        acc = p_ref[...].astype(F32)
        for k in range(N_CHIPS - 1):
            acc = acc + r_ref[k].astype(F32)
        out_ref[...] = acc

    return pl.pallas_call(
        body, name="sum_chips",
        grid_spec=pltpu.PrefetchScalarGridSpec(
            num_scalar_prefetch=2, grid=(R // tr,),
            in_specs=[pl.BlockSpec((None, tr, C), lambda i, c, l: (c[0], i, 0)),
                      pl.BlockSpec((N_CHIPS - 1, tr, C), lambda i, c, l: (0, i, 0))],
            out_specs=pl.BlockSpec((None, tr, C), lambda i, c, l: (l[0], i, 0))),
        out_shape=jax.ShapeDtypeStruct((2, R, C), F32), compiler_params=_cparams("arbitrary"))(chip, layer, part, recv)


_ANY = pl.BlockSpec(memory_space=pl.ANY)


def _place():
    x, y, c = lax.axis_index("x"), lax.axis_index("y"), lax.axis_index("c")
    chips = [(1 - x, y), (x, 1 - y), (1 - x, 1 - y)]
    return x, y, c, chips


def _gather_phase(phase, layer, w, o, send, recv, start):
    x, y, c, chips = _place()
    me, sib = 2 * x + y, (x, y, 1 - c)
    n = len(w)
    works = c == layer

    def copy(t, k, shard, to, src=None):
        blk = o[t].at[layer, shard]
        return pltpu.make_async_remote_copy(src_ref=blk if src is None else src, dst_ref=blk, send_sem=send.at[t, k],
                                            recv_sem=recv.at[t, k], device_id=to, device_id_type=MESH)

    def outgoing():
        if phase == "ici":
            return [copy(t, k, me, (*chip, c), src=w[t].at[layer, me]) for t in range(n) for k, chip in enumerate(chips)]
        return [copy(t, k, 2 * chip[0] + chip[1], sib) for t in range(n) for k, chip in enumerate(chips)]

    def incoming():
        return [copy(t, k, 2 * chip[0] + chip[1], (x, y, c)) for t in range(n) for k, chip in enumerate(chips)]

    if start:
        @pl.when(works)
        def _():
            for cp in outgoing():
                cp.start()
    else:
        @pl.when(works)
        def _():
            if phase == "ici":
                for cp in incoming():
                    cp.wait_recv()
            for cp in outgoing():
                cp.wait_send()

        if phase == "pass":
            @pl.when(jnp.logical_not(works))
            def _():
                for cp in incoming():
                    cp.wait_recv()


def gather_layer(ws, layer):
    n = len(ws)

    def body(*refs):
        w, o = refs[:n], refs[n:2 * n]
        ici_send, ici_recv, pass_send, pass_recv = refs[2 * n:]
        _gather_phase("ici", layer, w, o, ici_send, ici_recv, start=True)
        _gather_phase("ici", layer, w, o, ici_send, ici_recv, start=False)
        _gather_phase("pass", layer, w, o, pass_send, pass_recv, start=True)
        _gather_phase("pass", layer, w, o, pass_send, pass_recv, start=False)

    sems = [pltpu.SemaphoreType.DMA((n, 3))] * 4
    return pl.pallas_call(
        body, name="gather_layer", in_specs=[_ANY] * n, out_specs=[_ANY] * n,
        out_shape=[jax.ShapeDtypeStruct(w.shape, w.dtype) for w in ws], input_output_aliases={t: t for t in range(n)},
        scratch_shapes=sems, compiler_params=pltpu.CompilerParams(has_side_effects=True))(*ws)


def pair_exchange(gs):
    n = len(gs)

    def body(*refs):
        g, o = refs[:n], refs[n:2 * n]
        send, recv = refs[2 * n:]
        x, y, c, _ = _place()
        cps = [pltpu.make_async_remote_copy(src_ref=g[t].at[1 - c], dst_ref=o[t], send_sem=send.at[t], recv_sem=recv.at[t],
                                            device_id=(x, y, 1 - c), device_id_type=MESH) for t in range(n)]
        for cp in cps:
            cp.start()
        for cp in cps:
            cp.wait()

    return pl.pallas_call(
        body, name="pair_exchange", in_specs=[_ANY] * n, out_specs=[_ANY] * n,
        out_shape=[jax.ShapeDtypeStruct(g.shape[1:], g.dtype) for g in gs],
        scratch_shapes=[pltpu.SemaphoreType.DMA((n,)), pltpu.SemaphoreType.DMA((n,))],
        compiler_params=pltpu.CompilerParams(has_side_effects=True))(*gs)


def chip_scatter(ps):
    n = len(ps)

    def body(*refs):
        p, o = refs[:n], refs[n:2 * n]
        send, recv = refs[2 * n:]
        x, y, c, chips = _place()
        cps = [pltpu.make_async_remote_copy(src_ref=p[t].at[2 * chip[0] + chip[1]], dst_ref=o[t].at[k],
                                            send_sem=send.at[t, k], recv_sem=recv.at[t, k], device_id=(*chip, c),
                                            device_id_type=MESH)
               for t in range(n) for k, chip in enumerate(chips)]
        for cp in cps:
            cp.start()
        for cp in cps:
            cp.wait()

    return pl.pallas_call(
        body, name="chip_scatter", in_specs=[_ANY] * n, out_specs=[_ANY] * n,
        out_shape=[jax.ShapeDtypeStruct((N_CHIPS - 1,) + p.shape[1:], p.dtype) for p in ps],
        scratch_shapes=[pltpu.SemaphoreType.DMA((n, 3)), pltpu.SemaphoreType.DMA((n, 3))],
        compiler_params=pltpu.CompilerParams(has_side_effects=True))(*ps)


def pair_share(rs):
    n = len(rs)

    def body(*refs):
        r, o = refs[:n], refs[n:2 * n]
        send, recv = refs[2 * n:]
        x, y, c, _ = _place()

        def remote(t, layer):
            return pltpu.make_async_remote_copy(src_ref=r[t].at[layer], dst_ref=o[t].at[layer], send_sem=send.at[t],
                                                recv_sem=recv.at[t], device_id=(x, y, 1 - c), device_id_type=MESH)

        for t in range(n):
            remote(t, c).start()
        for t in range(n):
            remote(t, 1 - c).wait_recv()
            remote(t, c).wait_send()

    return pl.pallas_call(
        body, name="pair_share", in_specs=[_ANY] * n, out_specs=[_ANY] * n,
        out_shape=[jax.ShapeDtypeStruct(r.shape, r.dtype) for r in rs], input_output_aliases={t: t for t in range(n)},
        scratch_shapes=[pltpu.SemaphoreType.DMA((n,)), pltpu.SemaphoreType.DMA((n,))],
        compiler_params=pltpu.CompilerParams(has_side_effects=True))(*rs)


def all_reduce_small(buf):
    def body(x_ref, o_ref, land, send, recv):
        x, y, c, _ = _place()
        me = 4 * x + 2 * y + c
        land[me] = x_ref[...]
        cps = []
        for mask in range(1, N_DEV):
            px = 1 - x if mask & 4 else x
            py = 1 - y if mask & 2 else y
            pc = 1 - c if mask & 1 else c
            cps.append(pltpu.make_async_remote_copy(src_ref=x_ref, dst_ref=land.at[me], send_sem=send.at[mask - 1],
                                                    recv_sem=recv.at[mask - 1], device_id=(px, py, pc), device_id_type=MESH))
            cps[-1].start()
        for mask in range(1, N_DEV):
            px = 1 - x if mask & 4 else x
            py = 1 - y if mask & 2 else y
            pc = 1 - c if mask & 1 else c
            pltpu.make_async_remote_copy(src_ref=x_ref, dst_ref=land.at[4 * px + 2 * py + pc], send_sem=send.at[mask - 1],
                                         recv_sem=recv.at[mask - 1], device_id=(px, py, pc), device_id_type=MESH).wait_recv()
        for cp in cps:
            cp.wait_send()
        acc = land[0]
        for d in range(1, N_DEV):
            acc = acc + land[d]
        o_ref[...] = acc

    vm = pl.BlockSpec(memory_space=pltpu.VMEM)
    return pl.pallas_call(
        body, name="all_reduce_small", in_specs=[vm], out_specs=vm, out_shape=jax.ShapeDtypeStruct(buf.shape, F32),
        scratch_shapes=[pltpu.VMEM((N_DEV,) + buf.shape, F32), pltpu.SemaphoreType.DMA((N_DEV - 1,)),
                        pltpu.SemaphoreType.DMA((N_DEV - 1,))])(buf)


def _take(ws, idx):
    return [ws[t] for t in idx]


def _put(ws, idx, new):
    ws = list(ws)
    for t, a in zip(idx, new):
        ws[t] = a
    return ws


def _ffn_fwd(x, h, gathered, first, l, next_gain, up_carry=None, down_carry=None):
    D = x.shape[1]
    if up_carry:
        phase, layer, idx = up_carry
        g, u, a, arrays = gate_up(h, gathered[first], gathered[first + 1], l, carry=(phase, layer, _take(gathered, idx)))
        gathered = _put(gathered, idx, arrays)
    else:
        g, u, a = gate_up(h, gathered[first], gathered[first + 1], l)
    wd = gathered[first + 2].reshape(gathered[first + 2].shape[0], -1, D)
    if down_carry:
        phase, layer, idx = down_carry
        y, h_next, arrays = down_proj(a, wd, l, x, next_gain, carry=(phase, layer, _take(gathered, idx)))
        gathered = _put(gathered, idx, arrays)
    else:
        y, h_next = down_proj(a, wd, l, x, next_gain)
    return y, h_next, (h, g, u, a), gathered


def _ffn_bwd(dy, x, gain, wg, wu, wd, l, saved, grads):
    h, g, u, a = saved
    dy, dyb = dy
    D = x.shape[1]
    L = wg.shape[0]
    Fs = g.shape[1] // N_CHIPS
    dg, du = down_bwd(dyb, wd, l, g, u)
    dwg = mm_tn(h, dg, _tile(D, 1024), Fs, blocked=True, layer=l, layers=L, into=grads[0])
    dwu = mm_tn(h, du, _tile(D, 1024), Fs, blocked=True, layer=l, layers=L, into=grads[1])
    dwd = mm_tn(a, dyb, Fs, _tile(D, 1024), scale=0.5, layer=l, layers=L, into=grads[2])
    dh = gate_up_bwd(dg, du, wg, wu, l)
    dx, dxb, dgain = rms_bwd(dh, x, gain, dy)
    return (dx, dxb), dgain, (dwg, dwu, dwd)


def _mixer_fwd(x, h, win, q_norm, kv_norm, sinks, bias, wqb, wkvb, gathered, c2, s2, l, next_gain, carry_layer=None):
    D = x.shape[1]
    p = mm_nn(h, win, 0, 640, F32)
    (cqn, ckvn, qmt, km, kmt, vm, vmt, qst, ks, kst, vs, vst, qft, kf, kft, vf, vft, ccol) = mixer_prep(
        p, c2, s2, q_norm, kv_norm, bias, wqb, wkvb, 0)
    if carry_layer is None:
        o_mla, lse_mla = attn_fwd("attn_mla", qmt, km, vmt)
        o_swa, lse_swa = attn_fwd("attn_swa", qst, ks, vst, window=WINDOW, sinks=sinks)
        o_fox, lse_fox = attn_fwd("attn_fox", qft, kf, vft, ccol=ccol)
    else:
        o_mla, lse_mla, arrays = attn_fwd("attn_mla", qmt, km, vmt, carry=("ici", carry_layer, _take(gathered, W_HEAD)))
        o_swa, lse_swa, arrays = attn_fwd("attn_swa", qst, ks, vst, window=WINDOW, sinks=sinks,
                                          carry=("pass", carry_layer, arrays))
        gathered = _put(gathered, W_HEAD, arrays)
        o_fox, lse_fox, arrays = attn_fwd("attn_fox", qft, kf, vft, ccol=ccol,
                                          carry=("ici", carry_layer, _take(gathered, W_TAIL)))
        gathered = _put(gathered, W_TAIL, arrays)
    mixed = merge_heads(o_mla, o_swa, o_fox)
    wout = gathered[6].reshape(gathered[6].shape[0], -1, D)
    y, h_next = mm_nn(mixed, wout, l, D, F32, resid=x, next_gain=next_gain)
    saved = (h, p, cqn, ckvn, qmt, km, kmt, vm, qst, ks, kst, vs, qft, kf, kft, vf, ccol, lse_mla, lse_swa, lse_fox, mixed)
    return y, h_next, saved, gathered


def _mixer_bwd(dy, x, gain, win, q_norm, kv_norm, sinks, bias, wqb, wkvb, wout, c2, s2, l, saved, dwout_so_far):
    (h, p, cqn, ckvn, qmt, km, kmt, vm, qst, ks, kst, vs, qft, kf, kft, vf, ccol, lse_mla, lse_swa, lse_fox, mixed) = saved
    S, D = x.shape
    width = mixed.shape[1]
    dy, dyb = dy
    dmixed = mm_nt(dyb, wout, l, _tile(width, 1024), BF16)
    dwout = mm_tn(mixed, dyb, _tile(width, 1024), _tile(D, 1024), layer=l, layers=wout.shape[0], into=dwout_so_far)
    do_mla, do_swa, do_fox, dl_mla, dl_swa, dl_fox = split_heads(dmixed, mixed)
    dqm, dkm, dvm = attn_bwd("attn_mla_bwd", qmt, km, kmt, vm, do_mla, lse_mla, dl_mla)
    dqs, dks, dvs, dsink = attn_bwd("attn_swa_bwd", qst, ks, kst, vs, do_swa, lse_swa, dl_swa, window=WINDOW, sinks=sinks)
    dqf, dkf, dvf, dc_rows = attn_bwd("attn_fox_bwd", qft, kf, kft, vf, do_fox, lse_fox, dl_fox, ccol=ccol)
    dc = dc_rows.reshape(N_HEADS, S).T
    dp, dq, dkv, dqn, dkvn, dbias = mixer_prep_bwd(p, c2, s2, q_norm, kv_norm, bias, wqb, wkvb, 0, dqm, dkm, dvm, dqs, dks,
                                                   dvs, dqf, dkf, dvf, dc)
    dwqb = mm_tn(cqn, dq, Q_LORA, N_HEADS * QK_MLA)
    dwkvb = mm_tn(ckvn, dkv, KV_LORA, 1024)
    dwin = mm_tn(h, dp, _tile(D, 1024), 640)
    dh = mm_nt(dp, win, 0, _tile(D, 1024), F32)
    dx, dxb, dgain = rms_bwd(dh, x, gain, dy)
    return (dx, dxb), dgain, dwin, dqn, dwqb, dkvn, dwkvb, dsink[:, 0, 0], dbias[0], dwout


def _pad_in_cols(w):
    pad = jnp.zeros(w.shape[:-1] + (P_COLS - IN_COLS,), w.dtype)
    return jnp.concatenate([w[..., :IN_KR], w[..., IN_KR + ROPE:IN_COLS - N_HEADS], w[..., IN_KR:IN_KR + ROPE],
                            w[..., IN_COLS - N_HEADS:], pad], axis=-1)


def _unpad_in_cols(w):
    return jnp.concatenate([w[..., :IN_KR], w[..., P_KR:P_KR + ROPE], w[..., IN_KR:P_KR], w[..., P_F:P_F + N_HEADS]], axis=-1)


def _col_shards(w):
    R = w.shape[0]
    return w.reshape(R, N_CHIPS, -1).transpose(1, 0, 2)


def _from_col_shards(w):
    L, _, R, C = w.shape
    return w.transpose(0, 2, 1, 3).reshape(L, R, N_CHIPS * C)


def kernel(x, positions, ffn1_norm, ffn1_w_gate, ffn1_w_up, ffn1_w_down, mix_norm, w_in, mla_q_norm, mla_w_q_b, mla_kv_norm, mla_w_kv_b, swa_sinks, fox_forget_bias, w_out, ffn2_norm, ffn2_w_gate, ffn2_w_up, ffn2_w_down, final_norm, loss_target, m_ffn1_norm, m_ffn1_w_gate, m_ffn1_w_up, m_ffn1_w_down, m_mix_norm, m_w_in, m_mla_q_norm, m_mla_w_q_b, m_mla_kv_norm, m_mla_w_kv_b, m_swa_sinks, m_fox_forget_bias, m_w_out, m_ffn2_norm, m_ffn2_w_gate, m_ffn2_w_up, m_ffn2_w_down, m_final_norm, v_ffn1_norm, v_ffn1_w_gate, v_ffn1_w_up, v_ffn1_w_down, v_mix_norm, v_w_in, v_mla_q_norm, v_mla_w_q_b, v_mla_kv_norm, v_mla_w_kv_b, v_swa_sinks, v_fox_forget_bias, v_w_out, v_ffn2_norm, v_ffn2_w_gate, v_ffn2_w_up, v_ffn2_w_down, v_final_norm):
    L = ffn1_norm.shape[0]
    S, D = x.shape[1], x.shape[2]
    F = ffn1_w_down.shape[1] * N_CHIPS
    xs, target = x[0], loss_target[0]
    cx, cy, cc = lax.axis_index("x"), lax.axis_index("y"), lax.axis_index("c")
    layer_id = jnp.reshape(cc, (1,)).astype(jnp.int32)
    chip_id = jnp.reshape(2 * cx + cy, (1,)).astype(jnp.int32)

    inv_freq = ROPE_THETA ** (-jnp.arange(0, ROPE, 2, dtype=F32) / ROPE)
    ang = positions[0].astype(F32)[:, None] * inv_freq
    cos, sin = jnp.cos(ang), jnp.sin(ang)
    c2, s2 = jnp.concatenate([cos, cos], axis=-1), jnp.concatenate([-sin, sin], axis=-1)

    big = [ffn1_w_gate, ffn1_w_up, ffn1_w_down, w_in, mla_w_q_b, mla_w_kv_b, w_out, ffn2_w_gate, ffn2_w_up, ffn2_w_down]
    gathered = [place_own(w, chip_id) for w in big]
    gathered = _put(gathered, W_FFN1, gather_layer(_take(gathered, W_FFN1), 0))

    def mixer_weights(ws, l):
        return (_pad_in_cols(_from_col_shards(ws[3][l:l + 1])), _from_col_shards(ws[4][l:l + 1]),
                _from_col_shards(ws[5][l:l + 1]))

    acts, small_w = [], []
    x0, h0 = xs, rms_fwd(xs, ffn1_norm[0][None])
    for l in range(L):
        nxt = l + 1 if l + 1 < L else None
        x1, h1, s1, gathered = _ffn_fwd(x0, h0, gathered, 0, l, mix_norm[l][None],
                                        up_carry=("ici", 0, W_REST) if l == 0 else None,
                                        down_carry=("pass", 0, W_REST) if l == 0 else None)
        small_w.append(mixer_weights(gathered, l))
        x2, h2, sm, gathered = _mixer_fwd(x1, h1, small_w[l][0], mla_q_norm[l][None], mla_kv_norm[l][None], swa_sinks[l],
                                          fox_forget_bias[l][None], small_w[l][1], small_w[l][2], gathered, c2, s2, l,
                                          ffn2_norm[l][None], carry_layer=nxt)
        x3, h3, s2_, gathered = _ffn_fwd(x2, h2, gathered, 7, l, ffn1_norm[l + 1][None] if nxt else None,
                                         up_carry=("pass", nxt, [9]) if nxt else None,
                                         down_carry=("pass", nxt, [7, 8]) if nxt else None)
        acts.append((x0, x1, x2, s1, sm, s2_))
        x0, h0 = x3, h3
    loss_part, dx_f32, dx_bf16, d_final = loss_head(x0, final_norm[None], target)
    dx = (dx_f32, dx_bf16)
    wg1, wu1, wd1, _, _, _, wout, wg2, wu2, wd2 = gathered
    wd1, wd2, wout = wd1.reshape(L, F, D), wd2.reshape(L, F, D), wout.reshape(L, -1, D)

    small = {k: [None] * L for k in ("ffn1_norm", "mix_norm", "q_norm", "kv_norm", "sinks", "bias", "ffn2_norm")}
    per_layer = {k: [None] * L for k in ("win", "wqb", "wkvb")}
    ffn1_grads, ffn2_grads, dwout = (None,) * 3, (None,) * 3, None
    for l in reversed(range(L)):
        x0, x1, x2, s1, sm, s2_ = acts[l]
        dx, small["ffn2_norm"][l], ffn2_grads = _ffn_bwd(dx, x2, ffn2_norm[l][None], wg2, wu2, wd2, l, s2_, ffn2_grads)
        (dx, small["mix_norm"][l], dwin, small["q_norm"][l], dwqb, small["kv_norm"][l], dwkvb, small["sinks"][l],
         small["bias"][l], dwout) = _mixer_bwd(dx, x1, mix_norm[l][None], small_w[l][0], mla_q_norm[l][None],
                                               mla_kv_norm[l][None], swa_sinks[l], fox_forget_bias[l][None], small_w[l][1],
                                               small_w[l][2], wout, c2, s2, l, sm, dwout)
        per_layer["win"][l] = _col_shards(_unpad_in_cols(dwin))
        per_layer["wqb"][l] = _col_shards(dwqb)
        per_layer["wkvb"][l] = _col_shards(dwkvb)
        dx, small["ffn1_norm"][l], ffn1_grads = _ffn_bwd(dx, x0, ffn1_norm[l][None], wg1, wu1, wd1, l, s1, ffn1_grads)
    grad_x = dx[0][None]

    names = ("wg1", "wu1", "wd1", "win", "wqb", "wkvb", "wout", "wg2", "wu2", "wd2")
    Fs = F // N_CHIPS
    full = [ffn1_grads[0], ffn1_grads[1], ffn1_grads[2].reshape(L, N_CHIPS, Fs, D), jnp.stack(per_layer["win"]),
            jnp.stack(per_layer["wqb"]), jnp.stack(per_layer["wkvb"]), dwout.reshape(L, N_CHIPS, -1, D),
            ffn2_grads[0], ffn2_grads[1], ffn2_grads[2].reshape(L, N_CHIPS, Fs, D)]
    flat = [g.reshape(L, -1, g.shape[-1]) for g in full]
    from_sibling = pair_exchange(flat)
    part = [sum_pair(g, o, layer_id).reshape(f.shape[1:]) for g, o, f in zip(flat, from_sibling, full)]
    from_chips = chip_scatter(part)
    mine = [sum_chips(p, r, chip_id, layer_id) for p, r in zip(part, from_chips)]
    grads_big = pair_share(mine)

    pieces = [jnp.concatenate(small["ffn1_norm"]), jnp.concatenate(small["mix_norm"]), jnp.concatenate(small["q_norm"]),
              jnp.concatenate(small["kv_norm"]), jnp.stack(small["sinks"]), jnp.stack(small["bias"]),
              jnp.concatenate(small["ffn2_norm"]), d_final, loss_part[:, 0:1]]
    sizes = [int(p.size) for p in pieces]
    packed = jnp.concatenate([p.reshape(-1) for p in pieces])
    packed = jnp.pad(packed, (0, SMALL_ROWS * 128 - packed.shape[0])).reshape(SMALL_ROWS, 128)
    summed = all_reduce_small(packed).reshape(-1)
    out_small, off = [], 0
    for p, n in zip(pieces, sizes):
        out_small.append(summed[off:off + n].reshape(p.shape))
        off += n
    g_ffn1_norm, g_mix_norm, g_q_norm, g_kv_norm, g_sinks, g_bias, g_ffn2_norm, g_final, loss = out_small
    loss = loss.reshape(())
    g_final = g_final.reshape(-1)

    gb = dict(zip(names, grads_big))
    summed_grads = [g_ffn1_norm, gb["wg1"], gb["wu1"], gb["wd1"], g_mix_norm, gb["win"], g_q_norm, gb["wqb"], g_kv_norm,
                    gb["wkvb"], g_sinks, g_bias, gb["wout"], g_ffn2_norm, gb["wg2"], gb["wu2"], gb["wd2"], g_final]
    weights = [ffn1_norm, ffn1_w_gate, ffn1_w_up, ffn1_w_down, mix_norm, w_in, mla_q_norm, mla_w_q_b, mla_kv_norm, mla_w_kv_b,
               swa_sinks, fox_forget_bias, w_out, ffn2_norm, ffn2_w_gate, ffn2_w_up, ffn2_w_down, final_norm]
    ms = [m_ffn1_norm, m_ffn1_w_gate, m_ffn1_w_up, m_ffn1_w_down, m_mix_norm, m_w_in, m_mla_q_norm, m_mla_w_q_b, m_mla_kv_norm,
          m_mla_w_kv_b, m_swa_sinks, m_fox_forget_bias, m_w_out, m_ffn2_norm, m_ffn2_w_gate, m_ffn2_w_up, m_ffn2_w_down,
          m_final_norm]
    vs = [v_ffn1_norm, v_ffn1_w_gate, v_ffn1_w_up, v_ffn1_w_down, v_mix_norm, v_w_in, v_mla_q_norm, v_mla_w_q_b, v_mla_kv_norm,
          v_mla_w_kv_b, v_swa_sinks, v_fox_forget_bias, v_w_out, v_ffn2_norm, v_ffn2_w_gate, v_ffn2_w_up, v_ffn2_w_down,
          v_final_norm]
    grads, deltas, new_m, new_v = [], [], [], []
    for w, g, m, v in zip(weights, summed_grads, ms, vs):
        three_d = w.shape if w.ndim == 3 else (1, -1, w.shape[-1])
        g_out, d, nm, nv = adamw(w.reshape(three_d), g.reshape(three_d), m.reshape(three_d), v.reshape(three_d))
        grads.append(g_out.reshape(w.shape))
        deltas.append(d.reshape(w.shape))
        new_m.append(nm.reshape(w.shape))
        new_v.append(nv.reshape(w.shape))
    return (loss, grad_x, *grads, *deltas, *new_m, *new_v)
```

```python
import jax
import jax.numpy as jnp
from jax import lax
from jax.experimental import pallas as pl
from jax.experimental.pallas import tpu as pltpu

F32, BF16 = jnp.float32, jnp.bfloat16
MESH = pl.DeviceIdType.MESH

RMS_EPS = 1e-6
ROPE_THETA = 10000.0
N_HEADS = 8
Q_LORA, KV_LORA = 512, 256
NOPE, ROPE, VDIM = 128, 64, 128
QK_MLA = NOPE + ROPE
SWA_KV, HD, WINDOW = 2, 64, 128
P_CQ, P_CKV, P_QS, P_KS, P_VS, P_QF, P_KF, P_VF, P_KR, P_F, P_COLS = (
    0, 512, 768, 1280, 1408, 1536, 2048, 2560, 3072, 3136, 3200)
IN_COLS = 3144
IN_KR = 768
ADAM_LR, ADAM_B1, ADAM_B2, ADAM_EPS, ADAM_WD, ADAM_STEP = 0.001, 0.9, 0.999, 1e-08, 0.01, 10
NEG = -1e30
LOG2E, LN2 = 1.4426950408889634, 0.6931471805599453
VMEM_LIMIT = 56 * 1024 * 1024
N_CHIPS = 4
N_DEV = 8
W_FFN1, W_REST, W_HEAD, W_TAIL = [0, 1, 2], [3, 4, 5, 6, 7, 8, 9], [0, 1, 2, 3, 4, 5, 6], [7, 8, 9]
SMALL_ROWS = 128


def _tile(n, pref):
    return pref if n % pref == 0 else n


def _cparams(*sem):
    return pltpu.CompilerParams(dimension_semantics=sem, vmem_limit_bytes=VMEM_LIMIT)


def _sigmoid(x):
    return 1.0 / (1.0 + jnp.exp(-x))


def _dot(a, b):
    return jnp.dot(a, b, preferred_element_type=F32)


def _dot_nt(a, b):
    return lax.dot_general(a, b, (((1,), (1,)), ((), ())), preferred_element_type=F32)


def _dot_tn(a, b):
    return lax.dot_general(a, b, (((0,), (0,)), ((), ())), preferred_element_type=F32)


def rms_fwd(x, gain):
    S, D = x.shape
    tm = _tile(S, 512)

    def body(x_ref, g_ref, h_ref):
        xv = x_ref[...]
        r = lax.rsqrt(jnp.mean(xv * xv, axis=-1, keepdims=True) + RMS_EPS)
        h_ref[...] = (xv * r * g_ref[...]).astype(BF16)

    return pl.pallas_call(
        body, name="rms_fwd", grid=(S // tm,),
        in_specs=[pl.BlockSpec((tm, D), lambda i: (i, 0)), pl.BlockSpec((1, D), lambda i: (0, 0))],
        out_specs=pl.BlockSpec((tm, D), lambda i: (i, 0)),
        out_shape=jax.ShapeDtypeStruct((S, D), BF16), compiler_params=_cparams("arbitrary"))(x, gain)


def rms_bwd(dh, x, gain, resid):
    S, D = x.shape
    tm = _tile(S, 512)

    def body(dh_ref, x_ref, g_ref, r_ref, dx_ref, dg_ref):
        xv, dhv = x_ref[...], dh_ref[...]
        r = lax.rsqrt(jnp.mean(xv * xv, axis=-1, keepdims=True) + RMS_EPS)
        xhat = xv * r
        dhg = dhv * g_ref[...]
        dx_ref[...] = r_ref[...] + r * (dhg - xhat * jnp.mean(dhg * xhat, axis=-1, keepdims=True))

        @pl.when(pl.program_id(0) == 0)
        def _():
            dg_ref[...] = jnp.zeros_like(dg_ref)

        dg_ref[...] += jnp.sum(dhv * xhat, axis=0, keepdims=True)

    row = pl.BlockSpec((tm, D), lambda i: (i, 0))
    vec = pl.BlockSpec((1, D), lambda i: (0, 0))
    return pl.pallas_call(
        body, name="rms_bwd", grid=(S // tm,), in_specs=[row, row, vec, row], out_specs=[row, vec],
        out_shape=[jax.ShapeDtypeStruct((S, D), F32), jax.ShapeDtypeStruct((1, D), F32)],
        compiler_params=_cparams("arbitrary"))(dh, x, gain, resid)


def loss_head(x, gain, target):
    S, D = x.shape
    tm = _tile(S, 512)

    def body(x_ref, g_ref, t_ref, loss_ref, dx_ref, dg_ref):
        xv, g = x_ref[...], g_ref[...]
        r = lax.rsqrt(jnp.mean(xv * xv, axis=-1, keepdims=True) + RMS_EPS)
        xhat = xv * r
        err = xhat * g - t_ref[...]
        dy = err * (1.0 / D)
        dyg = dy * g
        dx_ref[...] = r * (dyg - xhat * jnp.mean(dyg * xhat, axis=-1, keepdims=True))

        @pl.when(pl.program_id(0) == 0)
        def _():
            dg_ref[...] = jnp.zeros_like(dg_ref)
            loss_ref[...] = jnp.zeros_like(loss_ref)

        dg_ref[...] += jnp.sum(dy * xhat, axis=0, keepdims=True)
        loss_ref[...] += 0.5 * jnp.sum(jnp.mean(err * err, axis=-1, keepdims=True), axis=0, keepdims=True)

    row = pl.BlockSpec((tm, D), lambda i: (i, 0))
    vec = pl.BlockSpec((1, D), lambda i: (0, 0))
    return pl.pallas_call(
        body, name="loss_head", grid=(S // tm,), in_specs=[row, vec, row],
        out_specs=[pl.BlockSpec((1, 128), lambda i: (0, 0)), row, vec],
        out_shape=[jax.ShapeDtypeStruct((1, 128), F32), jax.ShapeDtypeStruct((S, D), F32),
                   jax.ShapeDtypeStruct((1, D), F32)],
        compiler_params=_cparams("arbitrary"))(x, gain, target)


def _carry_plumbing(carry, n_in, n_out):
    if not carry:
        return [], [], [], [], [], {}
    arrays = list(carry[2])
    n = len(arrays)
    sems = [pltpu.SemaphoreType.DMA((n, 3)), pltpu.SemaphoreType.DMA((n, 3))]
    return ([_ANY] * n, arrays, [_ANY] * n, [jax.ShapeDtypeStruct(a.shape, a.dtype) for a in arrays], sems,
            {n_in + t: n_out + t for t in range(n)})


def gate_up(h, wg, wu, l, carry=None):
    S, D = h.shape
    Fs = wg.shape[3]
    tm = _tile(S, 512)
    nt = S // tm
    n_carry = len(carry[2]) if carry else 0

    def body(h_ref, wg_ref, wu_ref, *rest):
        carried_in, rest = rest[:n_carry], rest[n_carry:]
        g_ref, u_ref, a_ref = rest[:3]
        carried_out, sems = rest[3:3 + n_carry], rest[3 + n_carry:]
        j, i = pl.program_id(0), pl.program_id(1)
        if carry:
            @pl.when(jnp.logical_and(j == 0, i == 0))
            def _():
                _gather_phase(carry[0], carry[1], carried_in, carried_out, sems[0], sems[1], start=True)
        hv = h_ref[...]
        g = _dot(hv, wg_ref[...])
        u = _dot(hv, wu_ref[...])
        g_ref[...] = g.astype(BF16)
        u_ref[...] = u.astype(BF16)
        a_ref[...] = (g * _sigmoid(g) * u).astype(BF16)
        if carry:
            @pl.when(jnp.logical_and(j == N_CHIPS - 1, i == nt - 1))
            def _():
                _gather_phase(carry[0], carry[1], carried_in, carried_out, sems[0], sems[1], start=False)

    w_spec = pl.BlockSpec((None, None, D, Fs), lambda j, i: (l, j, 0, 0))
    o_spec = pl.BlockSpec((tm, Fs), lambda j, i: (i, j))
    o_shape = jax.ShapeDtypeStruct((S, N_CHIPS * Fs), BF16)
    c_in, c_args, c_out, c_shape, c_scratch, aliases = _carry_plumbing(carry, 3, 3)
    out = pl.pallas_call(
        body, name="gate_up", grid=(N_CHIPS, nt),
        in_specs=[pl.BlockSpec((tm, D), lambda j, i: (i, 0)), w_spec, w_spec] + c_in,
        out_specs=[o_spec, o_spec, o_spec] + c_out, out_shape=[o_shape, o_shape, o_shape] + c_shape,
        scratch_shapes=c_scratch, input_output_aliases=aliases,
        compiler_params=_cparams("arbitrary", "arbitrary"))(h, wg, wu, *c_args)
    return (out[0], out[1], out[2], list(out[3:])) if carry else tuple(out)


def _normed(x, gain):
    r = lax.rsqrt(jnp.mean(x * x, axis=-1, keepdims=True) + RMS_EPS)
    return (x * r * gain).astype(BF16)


def down_proj(a, wd, l, x, next_gain=None, carry=None):
    S, F = a.shape
    D = wd.shape[2]
    tm, tk = _tile(S, 512), F // N_CHIPS
    nk, nt = F // tk, S // tm
    emit = next_gain is not None
    n_carry = len(carry[2]) if carry else 0
    n_out = 2 if emit else 1

    def body(a_ref, w_ref, x_ref, *rest):
        rest = list(rest)
        g_ref = rest.pop(0) if emit else None
        carried_in = [rest.pop(0) for _ in range(n_carry)]
        o_ref = rest.pop(0)
        h_ref = rest.pop(0) if emit else None
        carried_out = [rest.pop(0) for _ in range(n_carry)]
        acc_ref = rest.pop(0)
        i, k = pl.program_id(0), pl.program_id(1)
        if carry:
            @pl.when(jnp.logical_and(i == 0, k == 0))
            def _():
                _gather_phase(carry[0], carry[1], carried_in, carried_out, rest[0], rest[1], start=True)

        @pl.when(k == 0)
        def _():
            acc_ref[...] = jnp.zeros_like(acc_ref)

        acc_ref[...] += _dot(a_ref[...], w_ref[...])

        @pl.when(k == nk - 1)
        def _():
            y = x_ref[...] + 0.5 * acc_ref[...]
            o_ref[...] = y
            if emit:
                h_ref[...] = _normed(y, g_ref[...])

        if carry:
            @pl.when(jnp.logical_and(i == nt - 1, k == nk - 1))
            def _():
                _gather_phase(carry[0], carry[1], carried_in, carried_out, rest[0], rest[1], start=False)

    row = pl.BlockSpec((tm, D), lambda i, k: (i, 0))
    in_specs = [pl.BlockSpec((tm, tk), lambda i, k: (i, k)), pl.BlockSpec((None, tk, D), lambda i, k: (l, k, 0)), row]
    args, out_specs, out_shape = [a, wd, x], [row], [jax.ShapeDtypeStruct((S, D), F32)]
    if emit:
        in_specs.append(pl.BlockSpec((1, D), lambda i, k: (0, 0)))
        args.append(next_gain)
        out_specs.append(row)
        out_shape.append(jax.ShapeDtypeStruct((S, D), BF16))
    c_in, c_args, c_out, c_shape, c_scratch, aliases = _carry_plumbing(carry, len(args), n_out)
    out = pl.pallas_call(
        body, name="down_proj", grid=(nt, nk), in_specs=in_specs + c_in, out_specs=out_specs + c_out,
        out_shape=out_shape + c_shape, scratch_shapes=[pltpu.VMEM((tm, D), F32)] + c_scratch,
        input_output_aliases=aliases, compiler_params=_cparams("arbitrary", "arbitrary"))(*args, *c_args)
    result = (out[0], out[1] if emit else None)
    return result + (list(out[n_out:]),) if carry else result


def down_bwd(dy, wd, l, g, u):
    S, D = dy.shape
    F = g.shape[1]
    Fs = F // N_CHIPS
    tm = _tile(S, 512)

    def body(dy_ref, w_ref, g_ref, u_ref, dg_ref, du_ref):
        da = 0.5 * _dot_nt(dy_ref[...].astype(BF16), w_ref[...])
        gv, uv = g_ref[...].astype(F32), u_ref[...].astype(F32)
        sig = _sigmoid(gv)
        du_ref[...] = (da * (gv * sig)).astype(BF16)
        dg_ref[...] = (da * uv * (sig * (1.0 + gv * (1.0 - sig)))).astype(BF16)

    t_spec = pl.BlockSpec((tm, Fs), lambda j, i: (i, j))
    o_shape = jax.ShapeDtypeStruct((S, F), BF16)
    return pl.pallas_call(
        body, name="down_bwd", grid=(N_CHIPS, S // tm),
        in_specs=[pl.BlockSpec((tm, D), lambda j, i: (i, 0)), pl.BlockSpec((None, Fs, D), lambda j, i: (l, j, 0)),
                  t_spec, t_spec],
        out_specs=[t_spec, t_spec], out_shape=[o_shape, o_shape],
        compiler_params=_cparams("arbitrary", "arbitrary"))(dy, wd, g, u)


def gate_up_bwd(dg, du, wg, wu, l):
    S, F = dg.shape
    D, Fs = wg.shape[2], wg.shape[3]
    tm = _tile(S, 512)

    def body(dg_ref, du_ref, wg_ref, wu_ref, o_ref):
        k = pl.program_id(1)

        @pl.when(k == 0)
        def _():
            o_ref[...] = jnp.zeros_like(o_ref)

        o_ref[...] += _dot_nt(dg_ref[...], wg_ref[...]) + _dot_nt(du_ref[...], wu_ref[...])

    t_spec = pl.BlockSpec((tm, Fs), lambda i, k: (i, k))
    w_spec = pl.BlockSpec((None, None, D, Fs), lambda i, k: (l, k, 0, 0))
    return pl.pallas_call(
        body, name="gate_up_bwd", grid=(S // tm, N_CHIPS), in_specs=[t_spec, t_spec, w_spec, w_spec],
        out_specs=pl.BlockSpec((tm, D), lambda i, k: (i, 0)), out_shape=jax.ShapeDtypeStruct((S, D), F32),
        compiler_params=_cparams("arbitrary", "arbitrary"))(dg, du, wg, wu)


def mm_nn(a, b, l, tn, out_dtype, resid=None, next_gain=None):
    S, K = a.shape
    N = b.shape[2]
    tm = _tile(S, 512)
    emit = next_gain is not None
    assert not emit or tn == N

    def body(a_ref, b_ref, *rest):
        rest = list(rest)
        acc = _dot(a_ref[...].astype(BF16), b_ref[...])
        if resid is not None:
            acc = rest.pop(0)[...] + acc
        g_ref = rest.pop(0) if emit else None
        rest[0][...] = acc.astype(out_dtype)
        if emit:
            rest[1][...] = _normed(acc, g_ref[...])

    o_spec = pl.BlockSpec((tm, tn), lambda n, i: (i, n))
    in_specs = [pl.BlockSpec((tm, K), lambda n, i: (i, 0)), pl.BlockSpec((None, K, tn), lambda n, i: (l, 0, n))]
    args, out_specs, out_shape = [a, b], [o_spec], [jax.ShapeDtypeStruct((S, N), out_dtype)]
    if resid is not None:
        in_specs.append(o_spec)
        args.append(resid)
    if emit:
        in_specs.append(pl.BlockSpec((1, N), lambda n, i: (0, 0)))
        args.append(next_gain)
        out_specs.append(o_spec)
        out_shape.append(jax.ShapeDtypeStruct((S, N), BF16))
    out = pl.pallas_call(
        body, name="mm_nn", grid=(N // tn, S // tm), in_specs=in_specs, out_specs=out_specs, out_shape=out_shape,
        compiler_params=_cparams("arbitrary", "arbitrary"))(*args)
    return out if emit else out[0]


def mm_nt(a, b, l, tn, out_dtype):
    S, K = a.shape
    N = b.shape[1]
    tm = _tile(S, 512)

    def body(a_ref, b_ref, o_ref):
        o_ref[...] = _dot_nt(a_ref[...].astype(BF16), b_ref[...]).astype(out_dtype)

    return pl.pallas_call(
        body, name="mm_nt", grid=(N // tn, S // tm),
        in_specs=[pl.BlockSpec((tm, K), lambda n, i: (i, 0)), pl.BlockSpec((None, tn, K), lambda n, i: (l, n, 0))],
        out_specs=pl.BlockSpec((tm, tn), lambda n, i: (i, n)), out_shape=jax.ShapeDtypeStruct((S, N), out_dtype),
        compiler_params=_cparams("arbitrary", "arbitrary"))(a, b)


def mm_tn(a, b, tka, tnb, scale=1.0, blocked=False, layer=None, layers=None, into=None):
    S, Ka = a.shape
    Nb = b.shape[1]
    ts = _tile(S, 2048)
    ns = S // ts

    def body(a_ref, b_ref, *rest):
        o_ref, acc_ref = rest[-2:]
        s = pl.program_id(2)

        @pl.when(s == 0)
        def _():
            acc_ref[...] = jnp.zeros_like(acc_ref)

        acc_ref[...] += _dot_tn(a_ref[...].astype(BF16), b_ref[...].astype(BF16))

        @pl.when(s == ns - 1)
        def _():
            o_ref[...] = (scale * acc_ref[...]).astype(BF16)

    if blocked:
        block, shape = (None, tka, tnb), (Nb // tnb, Ka, tnb)
        index = lambda ka, nb, s: (nb, ka, 0)
    else:
        block, shape = (tka, tnb), (Ka, Nb)
        index = lambda ka, nb, s: (ka, nb)
    if layer is not None:
        block, shape = (None,) + block, (layers,) + shape
        inner = index
        index = lambda ka, nb, s: (layer,) + inner(ka, nb, s)
    in_specs = [pl.BlockSpec((ts, tka), lambda ka, nb, s: (s, ka)), pl.BlockSpec((ts, tnb), lambda ka, nb, s: (s, nb))]
    args, aliases = [a, b], {}
    if into is not None:
        in_specs.append(pl.BlockSpec(memory_space=pl.ANY))
        args.append(into)
        aliases = {2: 0}
    return pl.pallas_call(
        body, name="mm_tn", grid=(Ka // tka, Nb // tnb, ns), in_specs=in_specs,
        out_specs=pl.BlockSpec(block, index), out_shape=jax.ShapeDtypeStruct(shape, BF16),
        input_output_aliases=aliases, scratch_shapes=[pltpu.VMEM((tka, tnb), F32)],
        compiler_params=_cparams("arbitrary", "arbitrary", "arbitrary"))(*args)


def _rope(x, c2, s2):
    half = x.shape[-1] // 2
    rot = jnp.concatenate([x[:, half:], x[:, :half]], axis=-1)
    return x * c2 + rot * s2


def _tri(tm, upper):
    r = lax.broadcasted_iota(jnp.int32, (tm, tm), 0)
    c = lax.broadcasted_iota(jnp.int32, (tm, tm), 1)
    return jnp.where((c >= r) if upper else (c <= r), 1.0, 0.0).astype(F32)


def _log_sigmoid(x):
    return jnp.minimum(x, 0.0) - jnp.log(1.0 + jnp.exp(-jnp.abs(x)))


def _norm_hat(c):
    r = lax.rsqrt(jnp.mean(c * c, axis=-1, keepdims=True) + RMS_EPS)
    return c * r, r


def _tok_spec(tm, width, rev_n=None):
    if rev_n is None:
        return pl.BlockSpec((tm, width), lambda i: (i, 0))
    return pl.BlockSpec((tm, width), lambda i: (rev_n - 1 - i, 0))


def _head_spec(heads, tm, width, rev_n=None):
    if rev_n is None:
        return pl.BlockSpec((heads, tm, width), lambda i: (0, i, 0))
    return pl.BlockSpec((heads, tm, width), lambda i: (0, rev_n - 1 - i, 0))


def _lane_spec(heads, width, tm, rev_n=None):
    if rev_n is None:
        return pl.BlockSpec((heads, width, tm), lambda i: (0, 0, i))
    return pl.BlockSpec((heads, width, tm), lambda i: (0, 0, rev_n - 1 - i))


def _full_spec(shape):
    return pl.BlockSpec(shape, lambda i: (0,) * len(shape))


def mixer_prep(p, c2, s2, q_norm, kv_norm, bias, wqb, wkvb, l):
    S = p.shape[0]
    tm = _tile(S, 256)
    H = N_HEADS

    def body(p_ref, c2_ref, s2_ref, qn_ref, kvn_ref, b_ref, wqb_ref, wkvb_ref,
             cqn_ref, ckvn_ref, qmt_ref, km_ref, kmt_ref, vm_ref, vmt_ref, qst_ref, ks_ref, kst_ref, vs_ref, vst_ref,
             qft_ref, kf_ref, kft_ref, vf_ref, vft_ref, ccol_ref, carry_row):
        c2, s2 = c2_ref[...], s2_ref[...]
        cqn = (_norm_hat(p_ref[:, P_CQ:P_CQ + Q_LORA])[0] * qn_ref[...]).astype(BF16)
        ckvn = (_norm_hat(p_ref[:, P_CKV:P_CKV + KV_LORA])[0] * kvn_ref[...]).astype(BF16)
        cqn_ref[...] = cqn
        ckvn_ref[...] = ckvn
        q = _dot(cqn, wqb_ref[...])
        kv = _dot(ckvn, wkvb_ref[...])
        k_pe = _rope(p_ref[:, P_KR:P_KR + ROPE], c2, s2)
        k_pe_t = k_pe.T.astype(BF16)
        k_pe = k_pe.astype(BF16)

        def both_ways(x, tok_ref, lane_ref, h):
            tok_ref[h] = x.astype(BF16)
            lane_ref[h] = x.T.astype(BF16)

        qs_mla, qs_hd = QK_MLA ** -0.5 * LOG2E, HD ** -0.5 * LOG2E
        for h in range(H):
            qmt_ref[h, 0:NOPE, :] = (q[:, h * QK_MLA:h * QK_MLA + NOPE] * qs_mla).T.astype(BF16)
            qmt_ref[h, NOPE:QK_MLA, :] = (_rope(q[:, h * QK_MLA + NOPE:(h + 1) * QK_MLA], c2, s2) * qs_mla).T.astype(BF16)
            k_nope = kv[:, h * 256:h * 256 + NOPE]
            km_ref[h, :, 0:NOPE] = k_nope.astype(BF16)
            km_ref[h, :, NOPE:QK_MLA] = k_pe
            kmt_ref[h, 0:NOPE, :] = k_nope.T.astype(BF16)
            kmt_ref[h, NOPE:QK_MLA, :] = k_pe_t
            both_ways(kv[:, h * 256 + NOPE:(h + 1) * 256], vm_ref, vmt_ref, h)
            qst_ref[h] = (_rope(p_ref[:, P_QS + h * HD:P_QS + (h + 1) * HD], c2, s2) * qs_hd).T.astype(BF16)
            qft_ref[h] = (p_ref[:, P_QF + h * HD:P_QF + (h + 1) * HD] * qs_hd).T.astype(BF16)
            both_ways(p_ref[:, P_KF + h * HD:P_KF + (h + 1) * HD], kf_ref, kft_ref, h)
            both_ways(p_ref[:, P_VF + h * HD:P_VF + (h + 1) * HD], vf_ref, vft_ref, h)
        for h in range(SWA_KV):
            both_ways(_rope(p_ref[:, P_KS + h * HD:P_KS + (h + 1) * HD], c2, s2), ks_ref, kst_ref, h)
            both_ways(p_ref[:, P_VS + h * HD:P_VS + (h + 1) * HD], vs_ref, vst_ref, h)

        @pl.when(pl.program_id(0) == 0)
        def _():
            carry_row[...] = jnp.zeros_like(carry_row)

        log_f = _log_sigmoid(p_ref[:, P_F:P_F + H] + b_ref[...])
        c_tok = jnp.dot(_tri(tm, upper=False), log_f, preferred_element_type=F32,
                        precision=lax.Precision.HIGHEST) + carry_row[0:1, 0:H]
        for h in range(H):
            ccol_ref[h] = c_tok[:, h:h + 1] * LOG2E
        carry_row[0:1, 0:H] = c_tok[tm - 1:tm, :]

    out_shape = [jax.ShapeDtypeStruct((S, Q_LORA), BF16), jax.ShapeDtypeStruct((S, KV_LORA), BF16)]
    out_specs = [_tok_spec(tm, Q_LORA), _tok_spec(tm, KV_LORA)]

    def add(heads, d, lanes):
        out_shape.append(jax.ShapeDtypeStruct((heads, d, S) if lanes else (heads, S, d), BF16))
        out_specs.append(_lane_spec(heads, d, tm) if lanes else _head_spec(heads, tm, d))

    for heads_q, heads_kv, dqk, dv in ((H, H, QK_MLA, VDIM), (H, SWA_KV, HD, HD), (H, H, HD, HD)):
        add(heads_q, dqk, True)
        add(heads_kv, dqk, False)
        add(heads_kv, dqk, True)
        add(heads_kv, dv, False)
        add(heads_kv, dv, True)
    out_shape.append(jax.ShapeDtypeStruct((H, S, 1), F32))
    out_specs.append(_head_spec(H, tm, 1))
    in_specs = [_tok_spec(tm, P_COLS), _tok_spec(tm, ROPE), _tok_spec(tm, ROPE), _full_spec((1, Q_LORA)),
                _full_spec((1, KV_LORA)), _full_spec((1, H)),
                pl.BlockSpec((None,) + wqb.shape[1:], lambda i: (l, 0, 0)),
                pl.BlockSpec((None,) + wkvb.shape[1:], lambda i: (l, 0, 0))]
    return pl.pallas_call(
        body, name="mixer_prep", grid=(S // tm,), in_specs=in_specs, out_specs=out_specs, out_shape=out_shape,
        scratch_shapes=[pltpu.VMEM((8, 128), F32)],
        compiler_params=_cparams("arbitrary"))(p, c2, s2, q_norm, kv_norm, bias, wqb, wkvb)


def mixer_prep_bwd(p, c2, s2, q_norm, kv_norm, bias, wqb, wkvb, l, dqm, dkm, dvm, dqs, dks, dvs, dqf, dkf, dvf, dc):
    S = p.shape[0]
    tm = _tile(S, 256)
    nt = S // tm
    H, G = N_HEADS, N_HEADS // SWA_KV

    def body(p_ref, c2_ref, s2_ref, qn_ref, kvn_ref, b_ref, wqb_ref, wkvb_ref,
             dqm_ref, dkm_ref, dvm_ref, dqs_ref, dks_ref, dvs_ref, dqf_ref, dkf_ref, dvf_ref, dc_ref,
             dp_ref, dq_ref, dkv_ref, dqn_ref, dkvn_ref, db_ref, carry):
        c2, s2 = c2_ref[...], -s2_ref[...]

        @pl.when(pl.program_id(0) == 0)
        def _():
            dqn_ref[...] = jnp.zeros_like(dqn_ref)
            dkvn_ref[...] = jnp.zeros_like(dkvn_ref)
            db_ref[...] = jnp.zeros_like(db_ref)
            carry[...] = jnp.zeros_like(carry)

        sc_mla, sc_hd = QK_MLA ** -0.5, HD ** -0.5
        dk_pe_t = jnp.zeros((ROPE, tm), F32)
        for h in range(H):
            dq_ref[:, h * QK_MLA:h * QK_MLA + NOPE] = (dqm_ref[h, 0:NOPE, :] * sc_mla).T.astype(BF16)
            dq_ref[:, h * QK_MLA + NOPE:(h + 1) * QK_MLA] = _rope(
                (dqm_ref[h, NOPE:QK_MLA, :] * sc_mla).T, c2, s2).astype(BF16)
            dkv_ref[:, h * 256:h * 256 + NOPE] = (dkm_ref[h, 0:NOPE, :] * LN2).T.astype(BF16)
            dkv_ref[:, h * 256 + NOPE:(h + 1) * 256] = dvm_ref[h].T.astype(BF16)
            dk_pe_t = dk_pe_t + dkm_ref[h, NOPE:QK_MLA, :]
        dk_pe = (dk_pe_t * LN2).T

        def through_norm(dcn, c, gain, dgain_ref):
            c_hat, r = _norm_hat(c)
            dhg = dcn * gain
            dgain_ref[...] += jnp.sum(dcn * c_hat, axis=0, keepdims=True)
            return r * (dhg - c_hat * jnp.mean(dhg * c_hat, axis=-1, keepdims=True))

        dcqn = _dot_nt(dq_ref[...], wqb_ref[...])
        dckvn = _dot_nt(dkv_ref[...], wkvb_ref[...])
        dp_ref[:, P_CQ:P_CQ + Q_LORA] = through_norm(dcqn, p_ref[:, P_CQ:P_CQ + Q_LORA], qn_ref[...], dqn_ref).astype(BF16)
        dp_ref[:, P_CKV:P_CKV + KV_LORA] = through_norm(
            dckvn, p_ref[:, P_CKV:P_CKV + KV_LORA], kvn_ref[...], dkvn_ref).astype(BF16)
        for h in range(H):
            dp_ref[:, P_QS + h * HD:P_QS + (h + 1) * HD] = _rope((dqs_ref[h] * sc_hd).T, c2, s2).astype(BF16)
            dp_ref[:, P_QF + h * HD:P_QF + (h + 1) * HD] = (dqf_ref[h] * sc_hd).T.astype(BF16)
            dp_ref[:, P_KF + h * HD:P_KF + (h + 1) * HD] = (dkf_ref[h] * LN2).T.astype(BF16)
            dp_ref[:, P_VF + h * HD:P_VF + (h + 1) * HD] = dvf_ref[h].T.astype(BF16)
        for kvh in range(SWA_KV):
            dk = dks_ref[kvh * G]
            dv = dvs_ref[kvh * G]
            for g in range(1, G):
                dk = dk + dks_ref[kvh * G + g]
                dv = dv + dvs_ref[kvh * G + g]
            dp_ref[:, P_KS + kvh * HD:P_KS + (kvh + 1) * HD] = _rope((dk * LN2).T, c2, s2).astype(BF16)
            dp_ref[:, P_VS + kvh * HD:P_VS + (kvh + 1) * HD] = dv.T.astype(BF16)

        dcv = dc_ref[...]
        dlog_f = jnp.dot(_tri(tm, upper=True), dcv, preferred_element_type=F32,
                         precision=lax.Precision.HIGHEST) + carry[0:1, 0:H]
        carry[0:1, 0:H] = dlog_f[0:1, :]
        df = dlog_f * _sigmoid(-(p_ref[:, P_F:P_F + H] + b_ref[...]))
        db_ref[...] += jnp.sum(df, axis=0, keepdims=True)
        dp_ref[:, P_KR:P_COLS] = jnp.zeros((tm, P_COLS - P_KR), BF16)
        dp_ref[:, P_KR:P_KR + ROPE] = _rope(dk_pe, c2, s2).astype(BF16)
        dp_ref[:, P_F:P_F + H] = df.astype(BF16)

    rev = nt
    in_specs = [_tok_spec(tm, P_COLS, rev), _tok_spec(tm, ROPE, rev), _tok_spec(tm, ROPE, rev), _full_spec((1, Q_LORA)),
                _full_spec((1, KV_LORA)), _full_spec((1, H)),
                pl.BlockSpec((None,) + wqb.shape[1:], lambda i: (l, 0, 0)),
                pl.BlockSpec((None,) + wkvb.shape[1:], lambda i: (l, 0, 0)),
                _lane_spec(H, QK_MLA, tm, rev), _lane_spec(H, QK_MLA, tm, rev), _lane_spec(H, VDIM, tm, rev)]
    in_specs += [_lane_spec(H, HD, tm, rev)] * 6 + [_tok_spec(tm, H, rev)]
    out_specs = [_tok_spec(tm, P_COLS, rev), _tok_spec(tm, N_HEADS * QK_MLA, rev), _tok_spec(tm, N_HEADS * 256, rev),
                 _full_spec((1, Q_LORA)), _full_spec((1, KV_LORA)), _full_spec((1, H))]
    out_shape = [jax.ShapeDtypeStruct((S, P_COLS), BF16), jax.ShapeDtypeStruct((S, N_HEADS * QK_MLA), BF16),
                 jax.ShapeDtypeStruct((S, N_HEADS * 256), BF16), jax.ShapeDtypeStruct((1, Q_LORA), F32),
                 jax.ShapeDtypeStruct((1, KV_LORA), F32), jax.ShapeDtypeStruct((1, H), F32)]
    return pl.pallas_call(
        body, name="mixer_prep_bwd", grid=(nt,), in_specs=in_specs, out_specs=out_specs, out_shape=out_shape,
        scratch_shapes=[pltpu.VMEM((8, 128), F32)], compiler_params=_cparams("arbitrary"))(
            p, c2, s2, q_norm, kv_norm, bias, wqb, wkvb, dqm, dkm, dvm, dqs, dks, dvs, dqf, dkf, dvf, dc)


def merge_heads(o_mla, o_swa, o_fox):
    H, S, _ = o_mla.shape
    tm = _tile(S, 512)
    width = H * (VDIM + 2 * HD)

    def body(om_ref, os_ref, of_ref, m_ref):
        for h in range(H):
            m_ref[:, h * VDIM:(h + 1) * VDIM] = om_ref[h]
            m_ref[:, H * VDIM + h * HD:H * VDIM + (h + 1) * HD] = os_ref[h]
            m_ref[:, H * (VDIM + HD) + h * HD:H * (VDIM + HD) + (h + 1) * HD] = of_ref[h]

    return pl.pallas_call(
        body, name="merge_heads", grid=(S // tm,),
        in_specs=[_head_spec(H, tm, VDIM), _head_spec(H, tm, HD), _head_spec(H, tm, HD)],
        out_specs=_tok_spec(tm, width), out_shape=jax.ShapeDtypeStruct((S, width), BF16),
        compiler_params=_cparams("arbitrary"))(o_mla, o_swa, o_fox)


def split_heads(dmixed, mixed):
    S, width = mixed.shape
    H = N_HEADS
    tm = _tile(S, 512)

    def body(dm_ref, m_ref, dom_ref, dos_ref, dof_ref, dm_delta, ds_delta, df_delta):
        def one(h, off, d, do_ref, delta_ref):
            dv = dm_ref[:, off:off + d].astype(F32)
            do_ref[h] = dv.T.astype(BF16)
            prod = dv * m_ref[:, off:off + d].astype(F32)
            rows = lax.dot_general(jnp.ones((8, d), F32), prod, (((1,), (1,)), ((), ())),
                                   preferred_element_type=F32, precision=lax.Precision.HIGHEST)
            delta_ref[h] = rows[0:1, :]

        for h in range(H):
            one(h, h * VDIM, VDIM, dom_ref, dm_delta)
            one(h, H * VDIM + h * HD, HD, dos_ref, ds_delta)
            one(h, H * (VDIM + HD) + h * HD, HD, dof_ref, df_delta)

    row_spec = pl.BlockSpec((H, 1, tm), lambda i: (0, 0, i))
    row_shape = jax.ShapeDtypeStruct((H, 1, S), F32)
    return pl.pallas_call(
        body, name="split_heads", grid=(S // tm,),
        in_specs=[_tok_spec(tm, width), _tok_spec(tm, width)],
        out_specs=[_lane_spec(H, VDIM, tm), _lane_spec(H, HD, tm), _lane_spec(H, HD, tm), row_spec, row_spec, row_spec],
        out_shape=[jax.ShapeDtypeStruct((H, VDIM, S), BF16), jax.ShapeDtypeStruct((H, HD, S), BF16),
                   jax.ShapeDtypeStruct((H, HD, S), BF16), row_shape, row_shape, row_shape],
        compiler_params=_cparams("arbitrary"))(dmixed, mixed)


def _attn_tile(S):
    return 512 if (S % 512 == 0 and S > 512) else S // 2


def _valid(q0, k0, shape, q_axis, window):
    qpos = q0 + lax.broadcasted_iota(jnp.int32, shape, q_axis)
    kpos = k0 + lax.broadcasted_iota(jnp.int32, shape, 1 - q_axis)
    ok = kpos <= qpos
    if window is not None:
        ok = jnp.logical_and(ok, kpos > qpos - window)
    return ok


def attn_fwd(name, qt, k, vt, window=None, sinks=None, ccol=None, carry=None):
    H, dq, S = qt.shape
    Hk, dv, _ = vt.shape
    G = H // Hk
    t = _attn_tile(S)
    nq = S // t
    fox, use_sink = ccol is not None, sinks is not None
    n_carry = len(carry[2]) if carry else 0

    def body(*refs):
        refs = list(refs)
        sink_ref = refs.pop(0) if use_sink else None
        q_ref, k_ref, vt_ref = refs[:3]
        refs = refs[3:]
        ccol_ref = refs.pop(0) if fox else None
        carried_in = [refs.pop(0) for _ in range(n_carry)]
        o_ref, lse_ref = refs[:2]
        refs = refs[2:]
        carried_out = [refs.pop(0) for _ in range(n_carry)]
        m_ref, l_ref, acc_ref = refs[:3]
        h, i = pl.program_id(0), pl.program_id(1)
        if carry:
            @pl.when(jnp.logical_and(h == 0, i == 0))
            def _():
                _gather_phase(carry[0], carry[1], carried_in, carried_out, refs[3], refs[4], start=True)
        qv = q_ref[...]
        if use_sink:
            m_ref[...] = jnp.full(m_ref.shape, sink_ref[h] * LOG2E, F32)
            l_ref[...] = jnp.ones(l_ref.shape, F32)
        else:
            m_ref[...] = jnp.full(m_ref.shape, NEG, F32)
            l_ref[...] = jnp.zeros(l_ref.shape, F32)
        acc_ref[...] = jnp.zeros(acc_ref.shape, F32)

        def steps(blocks):
            offs = [pl.multiple_of(j * t, t) for j, _ in blocks]
            scores = [_dot(k_ref[pl.ds(off, t), :], qv) for off in offs]
            for (j, masked), off, st in zip(blocks, offs, scores):
                if fox:
                    st = st - ccol_ref[pl.ds(off, t), :]
                if masked:
                    st = jnp.where(_valid(i * t, j * t, (t, t), 1, window), st, NEG)
                m_prev = m_ref[...]
                m_new = jnp.maximum(m_prev, jnp.max(st, axis=0, keepdims=True))
                alpha = jnp.exp2(m_prev - m_new)
                pt = jnp.exp2(st - m_new)
                l_ref[...] = alpha * l_ref[...] + jnp.sum(pt, axis=0, keepdims=True)
                acc_ref[...] = alpha * acc_ref[...] + _dot(vt_ref[:, pl.ds(off, t)], pt.astype(BF16))
                m_ref[...] = m_new

        if window is None:
            def pair(n, carry):
                steps([(2 * n, False), (2 * n + 1, False)])
                return carry
            lax.fori_loop(0, i // 2, pair, 0)

            @pl.when(i % 2 == 1)
            def _():
                steps([(i - 1, False), (i, True)])

            @pl.when(i % 2 == 0)
            def _():
                steps([(i, True)])
        else:
            def one(j, carry):
                steps([(j, True)])
                return carry
            lax.fori_loop(jnp.maximum(i * t - (window - 1), 0) // t, i + 1, one, 0)
        l = l_ref[...]
        o_ref[...] = (acc_ref[...] / l).T.astype(BF16)
        lse_ref[...] = m_ref[...] + jnp.log2(l)
        if carry:
            @pl.when(jnp.logical_and(h == H - 1, i == nq - 1))
            def _():
                _gather_phase(carry[0], carry[1], carried_in, carried_out, refs[3], refs[4], start=False)

    in_specs, args = [], []
    if use_sink:
        in_specs.append(pl.BlockSpec(memory_space=pltpu.SMEM))
        args.append(sinks)
    in_specs += [pl.BlockSpec((None, dq, t), lambda h, i: (h, 0, i)),
                 pl.BlockSpec((None, S, dq), lambda h, i: (h // G, 0, 0)),
                 pl.BlockSpec((None, dv, S), lambda h, i: (h // G, 0, 0))]
    args += [qt, k, vt]
    if fox:
        in_specs.append(pl.BlockSpec((None, S, 1), lambda h, i: (h, 0, 0)))
        args.append(ccol)
    out_specs = [pl.BlockSpec((None, t, dv), lambda h, i: (h, i, 0)), pl.BlockSpec((None, 1, t), lambda h, i: (h, 0, i))]
    out_shape = [jax.ShapeDtypeStruct((H, S, dv), BF16), jax.ShapeDtypeStruct((H, 1, S), F32)]
    scratch = [pltpu.VMEM((1, t), F32), pltpu.VMEM((1, t), F32), pltpu.VMEM((dv, t), F32)]
    aliases = {}
    if carry:
        aliases = {len(args) + n: 2 + n for n in range(n_carry)}
        in_specs += [_ANY] * n_carry
        args += list(carry[2])
        out_specs += [_ANY] * n_carry
        out_shape += [jax.ShapeDtypeStruct(a.shape, a.dtype) for a in carry[2]]
        scratch += [pltpu.SemaphoreType.DMA((n_carry, 3)), pltpu.SemaphoreType.DMA((n_carry, 3))]
    out = pl.pallas_call(
        body, name=name, grid=(H, nq), in_specs=in_specs, out_specs=out_specs, out_shape=out_shape,
        scratch_shapes=scratch, input_output_aliases=aliases,
        compiler_params=_cparams("arbitrary", "arbitrary"))(*args)
    return (out[0], out[1], list(out[2:])) if carry else (out[0], out[1])


def attn_bwd(name, qt, k, kt, v, dot, lse, delta, window=None, sinks=None, ccol=None):
    H, dq, S = qt.shape
    Hk, _, dv = v.shape
    G = H // Hk
    t = _attn_tile(S)
    nq = S // t
    fox, use_sink = ccol is not None, sinks is not None

    def body(*refs):
        refs = list(refs)
        sink_ref = refs.pop(0) if use_sink else None
        qt_ref, k_ref, kt_ref, v_ref, dot_ref, lse_ref, delta_ref = refs[:7]
        refs = refs[7:]
        ccol_ref = refs.pop(0) if fox else None
        dqt_ref, dkt_ref, dvt_ref = refs[:3]
        refs = refs[3:]
        dc_ref = refs.pop(0) if fox else None
        dsink_ref = refs.pop(0) if use_sink else None
        dck_ref = refs.pop(0) if fox else None
        h, j = pl.program_id(0), pl.program_id(1)

        @pl.when(j == 0)
        def _():
            dqt_ref[...] = jnp.zeros(dqt_ref.shape, F32)
            if fox:
                dc_ref[...] = jnp.zeros(dc_ref.shape, F32)
            if use_sink:
                ps = jnp.exp2(sink_ref[h] * LOG2E - lse_ref[...]) * delta_ref[...]
                dsink_ref[...] = jnp.broadcast_to(-jnp.sum(ps, axis=-1, keepdims=True), dsink_ref.shape)

        dkt_ref[...] = jnp.zeros(dkt_ref.shape, F32)
        dvt_ref[...] = jnp.zeros(dvt_ref.shape, F32)
        if fox:
            dck_ref[...] = jnp.zeros(dck_ref.shape, F32)
        kv, ktv, vv = k_ref[...], kt_ref[...], v_ref[...]

        def steps(blocks):
            offs = [pl.multiple_of(i * t, t) for i, _ in blocks]
            qts = [qt_ref[:, pl.ds(off, t)] for off in offs]
            dots = [dot_ref[:, pl.ds(off, t)] for off in offs]
            scores = [_dot(kv, qti) for qti in qts]
            dprobs = [_dot(vv, doti) for doti in dots]
            for (i, masked), off, qti, doti, st, dpt in zip(blocks, offs, qts, dots, scores, dprobs):
                if fox:
                    st = st - ccol_ref[...]
                if masked:
                    st = jnp.where(_valid(i * t, j * t, (t, t), 1, window), st, NEG)
                pt = jnp.exp2(st - lse_ref[:, pl.ds(off, t)])
                dvt_ref[...] += _dot_nt(doti, pt.astype(BF16))
                dst = pt * (dpt - delta_ref[:, pl.ds(off, t)])
                if fox:
                    dc_ref[:, pl.ds(off, t)] += jnp.sum(dst, axis=0, keepdims=True)
                    dck_ref[...] += jnp.sum(dst, axis=1, keepdims=True)
                dsb = dst.astype(BF16)
                dkt_ref[...] += _dot_nt(qti, dsb)
                dqt_ref[:, pl.ds(off, t)] += _dot(ktv, dsb)

        if window is None:
            odd = (nq - 1 - j) % 2

            @pl.when(odd == 1)
            def _():
                steps([(j, True), (j + 1, False)])

            @pl.when(odd == 0)
            def _():
                steps([(j, True)])

            first = j + 1 + odd

            def pair(n, carry):
                steps([(first + 2 * n, False), (first + 2 * n + 1, False)])
                return carry
            lax.fori_loop(0, (nq - first) // 2, pair, 0)
        else:
            def one(i, carry):
                steps([(i, True)])
                return carry
            lax.fori_loop(j, jnp.minimum((j * t + t - 1 + window - 1) // t, nq - 1) + 1, one, 0)
        if fox:
            key_row = jnp.broadcast_to(dck_ref[...], (t, 128)).T[0:1, :]
            dc_ref[:, pl.ds(pl.multiple_of(j * t, t), t)] -= key_row

    in_specs, args = [], []
    if use_sink:
        in_specs.append(pl.BlockSpec(memory_space=pltpu.SMEM))
        args.append(sinks)
    whole = lambda d: pl.BlockSpec((None, d, S), lambda h, j: (h, 0, 0))
    keys = lambda d: pl.BlockSpec((None, d, t), lambda h, j: (h, 0, j))
    row = pl.BlockSpec((None, 1, S), lambda h, j: (h, 0, 0))
    in_specs += [whole(dq), pl.BlockSpec((None, t, dq), lambda h, j: (h // G, j, 0)),
                 pl.BlockSpec((None, dq, t), lambda h, j: (h // G, 0, j)),
                 pl.BlockSpec((None, t, dv), lambda h, j: (h // G, j, 0)), whole(dv), row, row]
    args += [qt, k, kt, v, dot, lse, delta]
    out_specs = [whole(dq), keys(dq), keys(dv)]
    out_shape = [jax.ShapeDtypeStruct((H, dq, S), F32), jax.ShapeDtypeStruct((H, dq, S), F32),
                 jax.ShapeDtypeStruct((H, dv, S), F32)]
    if fox:
        in_specs.append(pl.BlockSpec((None, t, 1), lambda h, j: (h, j, 0)))
        args.append(ccol)
        out_specs.append(row)
        out_shape.append(jax.ShapeDtypeStruct((H, 1, S), F32))
    if use_sink:
        out_specs.append(pl.BlockSpec((None, 1, 128), lambda h, j: (h, 0, 0)))
        out_shape.append(jax.ShapeDtypeStruct((H, 1, 128), F32))
    return pl.pallas_call(
        body, name=name, grid=(H, nq), in_specs=in_specs, out_specs=out_specs, out_shape=out_shape,
        scratch_shapes=[pltpu.VMEM((t, 1), F32)] if fox else [],
        compiler_params=_cparams("arbitrary", "arbitrary"))(*args)


def _rows_tile(rows):
    for tr in (256, 128, 64, 32, 16, 8):
        if rows % tr == 0:
            return tr
    return rows


def adamw(w, g, m, v):
    L, R, C = w.shape
    tr = _rows_tile(R)

    def body(w_ref, g_ref, m_ref, v_ref, g_out, d_ref, nm_ref, nv_ref):
        gv = g_ref[...]
        mn = ADAM_B1 * m_ref[...] + (1.0 - ADAM_B1) * gv
        vn = ADAM_B2 * v_ref[...] + (1.0 - ADAM_B2) * (gv * gv)
        m_hat = mn / (1.0 - ADAM_B1 ** ADAM_STEP)
        v_hat = vn / (1.0 - ADAM_B2 ** ADAM_STEP)
        d_ref[...] = -ADAM_LR * (m_hat / (jnp.sqrt(v_hat) + ADAM_EPS) + ADAM_WD * w_ref[...])
        nm_ref[...] = mn
        nv_ref[...] = vn
        g_out[...] = gv

    spec = pl.BlockSpec((None, tr, C), lambda l, i: (l, i, 0))
    shape = jax.ShapeDtypeStruct((L, R, C), F32)
    return pl.pallas_call(
        body, name="adamw", grid=(L, R // tr), in_specs=[spec] * 4, out_specs=[spec] * 4, out_shape=[shape] * 4,
        compiler_params=_cparams("arbitrary", "arbitrary"))(w, g, m, v)


def place_own(w, chip):
    L, R, C = w.shape
    tr = _rows_tile(R)

    def body(c_ref, w_ref, o_ref):
        o_ref[...] = w_ref[...].astype(BF16)

    return pl.pallas_call(
        body, name="place_own",
        grid_spec=pltpu.PrefetchScalarGridSpec(
            num_scalar_prefetch=1, grid=(L, R // tr),
            in_specs=[pl.BlockSpec((None, tr, C), lambda l, i, c: (l, i, 0))],
            out_specs=pl.BlockSpec((None, None, tr, C), lambda l, i, c: (l, c[0], i, 0))),
        out_shape=jax.ShapeDtypeStruct((L, N_CHIPS, R, C), BF16),
        compiler_params=_cparams("arbitrary", "arbitrary"))(chip, w)


def sum_pair(grad, other, layer):
    _, R, C = grad.shape
    tr = _rows_tile(R)

    def body(l_ref, g_ref, o_ref, out_ref):
        out_ref[...] = (g_ref[...].astype(F32) + o_ref[...].astype(F32)).astype(BF16)

    return pl.pallas_call(
        body, name="sum_pair",
        grid_spec=pltpu.PrefetchScalarGridSpec(
            num_scalar_prefetch=1, grid=(R // tr,),
            in_specs=[pl.BlockSpec((None, tr, C), lambda i, l: (l[0], i, 0)), pl.BlockSpec((tr, C), lambda i, l: (i, 0))],
            out_specs=pl.BlockSpec((tr, C), lambda i, l: (i, 0))),
        out_shape=jax.ShapeDtypeStruct((R, C), BF16), compiler_params=_cparams("arbitrary"))(layer, grad, other)


def sum_chips(part, recv, chip, layer):
    _, R, C = part.shape
    tr = _rows_tile(R)

    def body(c_ref, l_ref, p_ref, r_ref, out_ref):
        acc = p_ref[...].astype(F32)
        for k in range(N_CHIPS - 1):
            acc = acc + r_ref[k].astype(F32)
        out_ref[...] = acc

    return pl.pallas_call(
        body, name="sum_chips",
        grid_spec=pltpu.PrefetchScalarGridSpec(
            num_scalar_prefetch=2, grid=(R // tr,),
            in_specs=[pl.BlockSpec((None, tr, C), lambda i, c, l: (c[0], i, 0)),
                      pl.BlockSpec((N_CHIPS - 1, tr, C), lambda i, c, l: (0, i, 0))],
            out_specs=pl.BlockSpec((None, tr, C), lambda i, c, l: (l[0], i, 0))),
        out_shape=jax.ShapeDtypeStruct((2, R, C), F32), compiler_params=_cparams("arbitrary"))(chip, layer, part, recv)


_ANY = pl.BlockSpec(memory_space=pl.ANY)


def _place():
    x, y, c = lax.axis_index("x"), lax.axis_index("y"), lax.axis_index("c")
    chips = [(1 - x, y), (x, 1 - y), (1 - x, 1 - y)]
    return x, y, c, chips


def _gather_phase(phase, layer, w, o, send, recv, start):
    x, y, c, chips = _place()
    me, sib = 2 * x + y, (x, y, 1 - c)
    n = len(w)
    works = c == layer

    def copy(t, k, shard, to, src=None):
        blk = o[t].at[layer, shard]
        return pltpu.make_async_remote_copy(src_ref=blk if src is None else src, dst_ref=blk, send_sem=send.at[t, k],
                                            recv_sem=recv.at[t, k], device_id=to, device_id_type=MESH)

    def outgoing():
        if phase == "ici":
            return [copy(t, k, me, (*chip, c), src=w[t].at[layer, me]) for t in range(n) for k, chip in enumerate(chips)]
        return [copy(t, k, 2 * chip[0] + chip[1], sib) for t in range(n) for k, chip in enumerate(chips)]

    def incoming():
        return [copy(t, k, 2 * chip[0] + chip[1], (x, y, c)) for t in range(n) for k, chip in enumerate(chips)]

    if start:
        @pl.when(works)
        def _():
            for cp in outgoing():
                cp.start()
    else:
        @pl.when(works)
        def _():
            if phase == "ici":
                for cp in incoming():
                    cp.wait_recv()
            for cp in outgoing():
                cp.wait_send()

        if phase == "pass":
            @pl.when(jnp.logical_not(works))
            def _():
                for cp in incoming():
                    cp.wait_recv()


def gather_layer(ws, layer):
    n = len(ws)

    def body(*refs):
        w, o = refs[:n], refs[n:2 * n]
        ici_send, ici_recv, pass_send, pass_recv = refs[2 * n:]
        x, y, c, chips = _place()
        _gather_phase("ici", layer, w, o, ici_send, ici_recv, start=True)

        @pl.when(c == layer)
        def _():
            for t in range(n):
                for k, chip in enumerate(chips):
                    blk = o[t].at[layer, 2 * chip[0] + chip[1]]
                    pltpu.make_async_remote_copy(src_ref=blk, dst_ref=blk, send_sem=ici_send.at[t, k], recv_sem=ici_recv.at[t, k],
                                                 device_id=(x, y, c), device_id_type=MESH).wait_recv()
                    pltpu.make_async_remote_copy(src_ref=blk, dst_ref=blk, send_sem=pass_send.at[t, k],
                                                 recv_sem=pass_recv.at[t, k], device_id=(x, y, 1 - c),
                                                 device_id_type=MESH).start()
            for t in range(n):
                for k, chip in enumerate(chips):
                    blk = o[t].at[layer, 2 * x + y]
                    pltpu.make_async_remote_copy(src_ref=w[t].at[layer, 2 * x + y], dst_ref=blk, send_sem=ici_send.at[t, k],
                                                 recv_sem=ici_recv.at[t, k], device_id=(*chip, c),
                                                 device_id_type=MESH).wait_send()

        _gather_phase("pass", layer, w, o, pass_send, pass_recv, start=False)

    sems = [pltpu.SemaphoreType.DMA((n, 3))] * 4
    return pl.pallas_call(
        body, name="gather_layer", in_specs=[_ANY] * n, out_specs=[_ANY] * n,
        out_shape=[jax.ShapeDtypeStruct(w.shape, w.dtype) for w in ws], input_output_aliases={t: t for t in range(n)},
        scratch_shapes=sems, compiler_params=pltpu.CompilerParams(has_side_effects=True))(*ws)


def pair_exchange(gs):
    n = len(gs)

    def body(*refs):
        g, o = refs[:n], refs[n:2 * n]
        send, recv = refs[2 * n:]
        x, y, c, _ = _place()
        cps = [pltpu.make_async_remote_copy(src_ref=g[t].at[1 - c], dst_ref=o[t], send_sem=send.at[t], recv_sem=recv.at[t],
                                            device_id=(x, y, 1 - c), device_id_type=MESH) for t in range(n)]
        for cp in cps:
            cp.start()
        for cp in cps:
            cp.wait()

    return pl.pallas_call(
        body, name="pair_exchange", in_specs=[_ANY] * n, out_specs=[_ANY] * n,
        out_shape=[jax.ShapeDtypeStruct(g.shape[1:], g.dtype) for g in gs],
        scratch_shapes=[pltpu.SemaphoreType.DMA((n,)), pltpu.SemaphoreType.DMA((n,))],
        compiler_params=pltpu.CompilerParams(has_side_effects=True))(*gs)


def chip_scatter(ps):
    n = len(ps)

    def body(*refs):
        p, o = refs[:n], refs[n:2 * n]
        send, recv = refs[2 * n:]
        x, y, c, chips = _place()
        cps = [pltpu.make_async_remote_copy(src_ref=p[t].at[2 * chip[0] + chip[1]], dst_ref=o[t].at[k],
                                            send_sem=send.at[t, k], recv_sem=recv.at[t, k], device_id=(*chip, c),
                                            device_id_type=MESH)
               for t in range(n) for k, chip in enumerate(chips)]
        for cp in cps:
            cp.start()
        for cp in cps:
            cp.wait()

    return pl.pallas_call(
        body, name="chip_scatter", in_specs=[_ANY] * n, out_specs=[_ANY] * n,
        out_shape=[jax.ShapeDtypeStruct((N_CHIPS - 1,) + p.shape[1:], p.dtype) for p in ps],
        scratch_shapes=[pltpu.SemaphoreType.DMA((n, 3)), pltpu.SemaphoreType.DMA((n, 3))],
        compiler_params=pltpu.CompilerParams(has_side_effects=True))(*ps)


def pair_share(rs):
    n = len(rs)

    def body(*refs):
        r, o = refs[:n], refs[n:2 * n]
        send, recv = refs[2 * n:]
        x, y, c, _ = _place()

        def remote(t, layer):
            return pltpu.make_async_remote_copy(src_ref=r[t].at[layer], dst_ref=o[t].at[layer], send_sem=send.at[t],
                                                recv_sem=recv.at[t], device_id=(x, y, 1 - c), device_id_type=MESH)

        for t in range(n):
            remote(t, c).start()
        for t in range(n):
            remote(t, 1 - c).wait_recv()
            remote(t, c).wait_send()

    return pl.pallas_call(
        body, name="pair_share", in_specs=[_ANY] * n, out_specs=[_ANY] * n,
        out_shape=[jax.ShapeDtypeStruct(r.shape, r.dtype) for r in rs], input_output_aliases={t: t for t in range(n)},
        scratch_shapes=[pltpu.SemaphoreType.DMA((n,)), pltpu.SemaphoreType.DMA((n,))],
        compiler_params=pltpu.CompilerParams(has_side_effects=True))(*rs)


def all_reduce_small(buf):
    def body(x_ref, o_ref, land, send, recv):
        x, y, c, _ = _place()
        me = 4 * x + 2 * y + c
        land[me] = x_ref[...]
        cps = []
        for mask in range(1, N_DEV):
            px = 1 - x if mask & 4 else x
            py = 1 - y if mask & 2 else y
            pc = 1 - c if mask & 1 else c
            cps.append(pltpu.make_async_remote_copy(src_ref=x_ref, dst_ref=land.at[me], send_sem=send.at[mask - 1],
                                                    recv_sem=recv.at[mask - 1], device_id=(px, py, pc), device_id_type=MESH))
            cps[-1].start()
        for mask in range(1, N_DEV):
            px = 1 - x if mask & 4 else x
            py = 1 - y if mask & 2 else y
            pc = 1 - c if mask & 1 else c
            pltpu.make_async_remote_copy(src_ref=x_ref, dst_ref=land.at[4 * px + 2 * py + pc], send_sem=send.at[mask - 1],
                                         recv_sem=recv.at[mask - 1], device_id=(px, py, pc), device_id_type=MESH).wait_recv()
        for cp in cps:
            cp.wait_send()
        acc = land[0]
        for d in range(1, N_DEV):
            acc = acc + land[d]
        o_ref[...] = acc

    vm = pl.BlockSpec(memory_space=pltpu.VMEM)
    return pl.pallas_call(
        body, name="all_reduce_small", in_specs=[vm], out_specs=vm, out_shape=jax.ShapeDtypeStruct(buf.shape, F32),
        scratch_shapes=[pltpu.VMEM((N_DEV,) + buf.shape, F32), pltpu.SemaphoreType.DMA((N_DEV - 1,)),
                        pltpu.SemaphoreType.DMA((N_DEV - 1,))])(buf)


def _take(ws, idx):
    return [ws[t] for t in idx]


def _put(ws, idx, new):
    ws = list(ws)
    for t, a in zip(idx, new):
        ws[t] = a
    return ws


def _ffn_fwd(x, h, gathered, first, l, next_gain, up_carry=None, down_carry=None):
    D = x.shape[1]
    if up_carry:
        phase, layer, idx = up_carry
        g, u, a, arrays = gate_up(h, gathered[first], gathered[first + 1], l, carry=(phase, layer, _take(gathered, idx)))
        gathered = _put(gathered, idx, arrays)
    else:
        g, u, a = gate_up(h, gathered[first], gathered[first + 1], l)
    wd = gathered[first + 2].reshape(gathered[first + 2].shape[0], -1, D)
    if down_carry:
        phase, layer, idx = down_carry
        y, h_next, arrays = down_proj(a, wd, l, x, next_gain, carry=(phase, layer, _take(gathered, idx)))
        gathered = _put(gathered, idx, arrays)
    else:
        y, h_next = down_proj(a, wd, l, x, next_gain)
    return y, h_next, (h, g, u, a), gathered


def _ffn_bwd(dy, x, gain, wg, wu, wd, l, saved, grads):
    h, g, u, a = saved
    D = x.shape[1]
    L = wg.shape[0]
    Fs = g.shape[1] // N_CHIPS
    dg, du = down_bwd(dy, wd, l, g, u)
    dwg = mm_tn(h, dg, _tile(D, 1024), Fs, blocked=True, layer=l, layers=L, into=grads[0])
    dwu = mm_tn(h, du, _tile(D, 1024), Fs, blocked=True, layer=l, layers=L, into=grads[1])
    dwd = mm_tn(a, dy, Fs, _tile(D, 1024), scale=0.5, layer=l, layers=L, into=grads[2])
    dh = gate_up_bwd(dg, du, wg, wu, l)
    dx, dgain = rms_bwd(dh, x, gain, dy)
    return dx, dgain, (dwg, dwu, dwd)


def _mixer_fwd(x, h, win, q_norm, kv_norm, sinks, bias, wqb, wkvb, gathered, c2, s2, l, next_gain, carry_layer=None):
    D = x.shape[1]
    p = mm_nn(h, win, 0, 640, F32)
    (cqn, ckvn, qmt, km, kmt, vm, vmt, qst, ks, kst, vs, vst, qft, kf, kft, vf, vft, ccol) = mixer_prep(
        p, c2, s2, q_norm, kv_norm, bias, wqb, wkvb, 0)
    if carry_layer is None:
        o_mla, lse_mla = attn_fwd("attn_mla", qmt, km, vmt)
        o_swa, lse_swa = attn_fwd("attn_swa", qst, ks, vst, window=WINDOW, sinks=sinks)
        o_fox, lse_fox = attn_fwd("attn_fox", qft, kf, vft, ccol=ccol)
    else:
        o_mla, lse_mla, arrays = attn_fwd("attn_mla", qmt, km, vmt, carry=("ici", carry_layer, _take(gathered, W_HEAD)))
        o_swa, lse_swa, arrays = attn_fwd("attn_swa", qst, ks, vst, window=WINDOW, sinks=sinks,
                                          carry=("pass", carry_layer, arrays))
        gathered = _put(gathered, W_HEAD, arrays)
        o_fox, lse_fox, arrays = attn_fwd("attn_fox", qft, kf, vft, ccol=ccol,
                                          carry=("ici", carry_layer, _take(gathered, W_TAIL)))
        gathered = _put(gathered, W_TAIL, arrays)
    mixed = merge_heads(o_mla, o_swa, o_fox)
    wout = gathered[6].reshape(gathered[6].shape[0], -1, D)
    y, h_next = mm_nn(mixed, wout, l, D, F32, resid=x, next_gain=next_gain)
    saved = (h, p, cqn, ckvn, qmt, km, kmt, vm, qst, ks, kst, vs, qft, kf, kft, vf, ccol, lse_mla, lse_swa, lse_fox, mixed)
    return y, h_next, saved, gathered


def _mixer_bwd(dy, x, gain, win, q_norm, kv_norm, sinks, bias, wqb, wkvb, wout, c2, s2, l, saved, dwout_so_far):
    (h, p, cqn, ckvn, qmt, km, kmt, vm, qst, ks, kst, vs, qft, kf, kft, vf, ccol, lse_mla, lse_swa, lse_fox, mixed) = saved
    S, D = x.shape
    width = mixed.shape[1]
    dmixed = mm_nt(dy, wout, l, _tile(width, 1024), BF16)
    dwout = mm_tn(mixed, dy, _tile(width, 1024), _tile(D, 1024), layer=l, layers=wout.shape[0], into=dwout_so_far)
    do_mla, do_swa, do_fox, dl_mla, dl_swa, dl_fox = split_heads(dmixed, mixed)
    dqm, dkm, dvm = attn_bwd("attn_mla_bwd", qmt, km, kmt, vm, do_mla, lse_mla, dl_mla)
    dqs, dks, dvs, dsink = attn_bwd("attn_swa_bwd", qst, ks, kst, vs, do_swa, lse_swa, dl_swa, window=WINDOW, sinks=sinks)
    dqf, dkf, dvf, dc_rows = attn_bwd("attn_fox_bwd", qft, kf, kft, vf, do_fox, lse_fox, dl_fox, ccol=ccol)
    dc = dc_rows.reshape(N_HEADS, S).T
    dp, dq, dkv, dqn, dkvn, dbias = mixer_prep_bwd(p, c2, s2, q_norm, kv_norm, bias, wqb, wkvb, 0, dqm, dkm, dvm, dqs, dks,
                                                   dvs, dqf, dkf, dvf, dc)
    dwqb = mm_tn(cqn, dq, Q_LORA, N_HEADS * QK_MLA)
    dwkvb = mm_tn(ckvn, dkv, KV_LORA, 1024)
    dwin = mm_tn(h, dp, _tile(D, 1024), 640)
    dh = mm_nt(dp, win, 0, _tile(D, 1024), F32)
    dx, dgain = rms_bwd(dh, x, gain, dy)
    return dx, dgain, dwin, dqn, dwqb, dkvn, dwkvb, dsink[:, 0, 0], dbias[0], dwout


def _pad_in_cols(w):
    pad = jnp.zeros(w.shape[:-1] + (P_COLS - IN_COLS,), w.dtype)
    return jnp.concatenate([w[..., :IN_KR], w[..., IN_KR + ROPE:IN_COLS - N_HEADS], w[..., IN_KR:IN_KR + ROPE],
                            w[..., IN_COLS - N_HEADS:], pad], axis=-1)


def _unpad_in_cols(w):
    return jnp.concatenate([w[..., :IN_KR], w[..., P_KR:P_KR + ROPE], w[..., IN_KR:P_KR], w[..., P_F:P_F + N_HEADS]], axis=-1)


def _col_shards(w):
    R = w.shape[0]
    return w.reshape(R, N_CHIPS, -1).transpose(1, 0, 2)


def _from_col_shards(w):
    L, _, R, C = w.shape
    return w.transpose(0, 2, 1, 3).reshape(L, R, N_CHIPS * C)


def kernel(x, positions, ffn1_norm, ffn1_w_gate, ffn1_w_up, ffn1_w_down, mix_norm, w_in, mla_q_norm, mla_w_q_b, mla_kv_norm, mla_w_kv_b, swa_sinks, fox_forget_bias, w_out, ffn2_norm, ffn2_w_gate, ffn2_w_up, ffn2_w_down, final_norm, loss_target, m_ffn1_norm, m_ffn1_w_gate, m_ffn1_w_up, m_ffn1_w_down, m_mix_norm, m_w_in, m_mla_q_norm, m_mla_w_q_b, m_mla_kv_norm, m_mla_w_kv_b, m_swa_sinks, m_fox_forget_bias, m_w_out, m_ffn2_norm, m_ffn2_w_gate, m_ffn2_w_up, m_ffn2_w_down, m_final_norm, v_ffn1_norm, v_ffn1_w_gate, v_ffn1_w_up, v_ffn1_w_down, v_mix_norm, v_w_in, v_mla_q_norm, v_mla_w_q_b, v_mla_kv_norm, v_mla_w_kv_b, v_swa_sinks, v_fox_forget_bias, v_w_out, v_ffn2_norm, v_ffn2_w_gate, v_ffn2_w_up, v_ffn2_w_down, v_final_norm):
    L = ffn1_norm.shape[0]
    S, D = x.shape[1], x.shape[2]
    F = ffn1_w_down.shape[1] * N_CHIPS
    xs, target = x[0], loss_target[0]
    cx, cy, cc = lax.axis_index("x"), lax.axis_index("y"), lax.axis_index("c")
    layer_id = jnp.reshape(cc, (1,)).astype(jnp.int32)
    chip_id = jnp.reshape(2 * cx + cy, (1,)).astype(jnp.int32)

    inv_freq = ROPE_THETA ** (-jnp.arange(0, ROPE, 2, dtype=F32) / ROPE)
    ang = positions[0].astype(F32)[:, None] * inv_freq
    cos, sin = jnp.cos(ang), jnp.sin(ang)
    c2, s2 = jnp.concatenate([cos, cos], axis=-1), jnp.concatenate([-sin, sin], axis=-1)

    big = [ffn1_w_gate, ffn1_w_up, ffn1_w_down, w_in, mla_w_q_b, mla_w_kv_b, w_out, ffn2_w_gate, ffn2_w_up, ffn2_w_down]
    gathered = [place_own(w, chip_id) for w in big]
    gathered = _put(gathered, W_FFN1, gather_layer(_take(gathered, W_FFN1), 0))

    def mixer_weights(ws, l):
        return (_pad_in_cols(_from_col_shards(ws[3][l:l + 1])), _from_col_shards(ws[4][l:l + 1]),
                _from_col_shards(ws[5][l:l + 1]))

    acts, small_w = [], []
    x0, h0 = xs, rms_fwd(xs, ffn1_norm[0][None])
    for l in range(L):
        nxt = l + 1 if l + 1 < L else None
        x1, h1, s1, gathered = _ffn_fwd(x0, h0, gathered, 0, l, mix_norm[l][None],
                                        up_carry=("ici", 0, W_REST) if l == 0 else None,
                                        down_carry=("pass", 0, W_REST) if l == 0 else None)
        small_w.append(mixer_weights(gathered, l))
        x2, h2, sm, gathered = _mixer_fwd(x1, h1, small_w[l][0], mla_q_norm[l][None], mla_kv_norm[l][None], swa_sinks[l],
                                          fox_forget_bias[l][None], small_w[l][1], small_w[l][2], gathered, c2, s2, l,
                                          ffn2_norm[l][None], carry_layer=nxt)
        x3, h3, s2_, gathered = _ffn_fwd(x2, h2, gathered, 7, l, ffn1_norm[l + 1][None] if nxt else None,
                                         up_carry=("pass", nxt, [9]) if nxt else None,
                                         down_carry=("pass", nxt, [7, 8]) if nxt else None)
        acts.append((x0, x1, x2, s1, sm, s2_))
        x0, h0 = x3, h3
    loss_part, dx, d_final = loss_head(x0, final_norm[None], target)
    wg1, wu1, wd1, _, _, _, wout, wg2, wu2, wd2 = gathered
    wd1, wd2, wout = wd1.reshape(L, F, D), wd2.reshape(L, F, D), wout.reshape(L, -1, D)

    small = {k: [None] * L for k in ("ffn1_norm", "mix_norm", "q_norm", "kv_norm", "sinks", "bias", "ffn2_norm")}
    per_layer = {k: [None] * L for k in ("win", "wqb", "wkvb")}
    ffn1_grads, ffn2_grads, dwout = (None,) * 3, (None,) * 3, None
    for l in reversed(range(L)):
        x0, x1, x2, s1, sm, s2_ = acts[l]
        dx, small["ffn2_norm"][l], ffn2_grads = _ffn_bwd(dx, x2, ffn2_norm[l][None], wg2, wu2, wd2, l, s2_, ffn2_grads)
        (dx, small["mix_norm"][l], dwin, small["q_norm"][l], dwqb, small["kv_norm"][l], dwkvb, small["sinks"][l],
         small["bias"][l], dwout) = _mixer_bwd(dx, x1, mix_norm[l][None], small_w[l][0], mla_q_norm[l][None],
                                               mla_kv_norm[l][None], swa_sinks[l], fox_forget_bias[l][None], small_w[l][1],
                                               small_w[l][2], wout, c2, s2, l, sm, dwout)
        per_layer["win"][l] = _col_shards(_unpad_in_cols(dwin))
        per_layer["wqb"][l] = _col_shards(dwqb)
        per_layer["wkvb"][l] = _col_shards(dwkvb)
        dx, small["ffn1_norm"][l], ffn1_grads = _ffn_bwd(dx, x0, ffn1_norm[l][None], wg1, wu1, wd1, l, s1, ffn1_grads)
    grad_x = dx[None]

    names = ("wg1", "wu1", "wd1", "win", "wqb", "wkvb", "wout", "wg2", "wu2", "wd2")
    Fs = F // N_CHIPS
    full = [ffn1_grads[0], ffn1_grads[1], ffn1_grads[2].reshape(L, N_CHIPS, Fs, D), jnp.stack(per_layer["win"]),
            jnp.stack(per_layer["wqb"]), jnp.stack(per_layer["wkvb"]), dwout.reshape(L, N_CHIPS, -1, D),
            ffn2_grads[0], ffn2_grads[1], ffn2_grads[2].reshape(L, N_CHIPS, Fs, D)]
    flat = [g.reshape(L, -1, g.shape[-1]) for g in full]
    from_sibling = pair_exchange(flat)
    part = [sum_pair(g, o, layer_id).reshape(f.shape[1:]) for g, o, f in zip(flat, from_sibling, full)]
    from_chips = chip_scatter(part)
    mine = [sum_chips(p, r, chip_id, layer_id) for p, r in zip(part, from_chips)]
    grads_big = pair_share(mine)

    pieces = [jnp.concatenate(small["ffn1_norm"]), jnp.concatenate(small["mix_norm"]), jnp.concatenate(small["q_norm"]),
              jnp.concatenate(small["kv_norm"]), jnp.stack(small["sinks"]), jnp.stack(small["bias"]),
              jnp.concatenate(small["ffn2_norm"]), d_final, loss_part[:, 0:1]]
    sizes = [int(p.size) for p in pieces]
    packed = jnp.concatenate([p.reshape(-1) for p in pieces])
    packed = jnp.pad(packed, (0, SMALL_ROWS * 128 - packed.shape[0])).reshape(SMALL_ROWS, 128)
    summed = all_reduce_small(packed).reshape(-1)
    out_small, off = [], 0
    for p, n in zip(pieces, sizes):
        out_small.append(summed[off:off + n].reshape(p.shape))
        off += n
    g_ffn1_norm, g_mix_norm, g_q_norm, g_kv_norm, g_sinks, g_bias, g_ffn2_norm, g_final, loss = out_small
    loss = loss.reshape(())
    g_final = g_final.reshape(-1)

    gb = dict(zip(names, grads_big))
    summed_grads = [g_ffn1_norm, gb["wg1"], gb["wu1"], gb["wd1"], g_mix_norm, gb["win"], g_q_norm, gb["wqb"], g_kv_norm,
                    gb["wkvb"], g_sinks, g_bias, gb["wout"], g_ffn2_norm, gb["wg2"], gb["wu2"], gb["wd2"], g_final]
    weights = [ffn1_norm, ffn1_w_gate, ffn1_w_up, ffn1_w_down, mix_norm, w_in, mla_q_norm, mla_w_q_b, mla_kv_norm, mla_w_kv_b,
               swa_sinks, fox_forget_bias, w_out, ffn2_norm, ffn2_w_gate, ffn2_w_up, ffn2_w_down, final_norm]
    ms = [m_ffn1_norm, m_ffn1_w_gate, m_ffn1_w_up, m_ffn1_w_down, m_mix_norm, m_w_in, m_mla_q_norm, m_mla_w_q_b, m_mla_kv_norm,
          m_mla_w_kv_b, m_swa_sinks, m_fox_forget_bias, m_w_out, m_ffn2_norm, m_ffn2_w_gate, m_ffn2_w_up, m_ffn2_w_down,
          m_final_norm]
    vs = [v_ffn1_norm, v_ffn1_w_gate, v_ffn1_w_up, v_ffn1_w_down, v_mix_norm, v_w_in, v_mla_q_norm, v_mla_w_q_b, v_mla_kv_norm,
          v_mla_w_kv_b, v_swa_sinks, v_fox_forget_bias, v_w_out, v_ffn2_norm, v_ffn2_w_gate, v_ffn2_w_up, v_ffn2_w_down,
          v_final_norm]
    grads, deltas, new_m, new_v = [], [], [], []
    for w, g, m, v in zip(weights, summed_grads, ms, vs):
        three_d = w.shape if w.ndim == 3 else (1, -1, w.shape[-1])
        g_out, d, nm, nv = adamw(w.reshape(three_d), g.reshape(three_d), m.reshape(three_d), v.reshape(three_d))
        grads.append(g_out.reshape(w.shape))
        deltas.append(d.reshape(w.shape))
        new_m.append(nm.reshape(w.shape))
        new_v.append(nv.reshape(w.shape))
    return (loss, grad_x, *grads, *deltas, *new_m, *new_v)
```
